```python
import math
import jax, jax.numpy as jnp
from jax import lax
import numpy as np

D_MODEL = 1024
BATCH = 8
SEQ = 2048
DEPTH = 1

CHUNK = 64
Q_BLOCK = 128
N_MEM = 256
HEAD_DIM = 64
DIFF_WIDTH = D_MODEL // 2
SB_WIDTH = D_MODEL - DIFF_WIDTH
N_DIFF_HEADS = DIFF_WIDTH // (2 * HEAD_DIM)
DIFF_V_DIM = 2 * HEAD_DIM
N_SB_HEADS = SB_WIDTH // HEAD_DIM
N_XHEADS = 4
XHEAD_DIM = D_MODEL // N_XHEADS
D_FF = -(-8 * D_MODEL // (3 * 256)) * 256
ALPHA = (2.0 * DEPTH) ** 0.25
BETA = (8.0 * DEPTH) ** -0.25
LN_EPS = 1e-5
RMS_EPS = 1e-5
NEG_INF = -1e30
IN_WIDTHS = (N_DIFF_HEADS * 2 * HEAD_DIM, N_DIFF_HEADS * 2 * HEAD_DIM, N_DIFF_HEADS * DIFF_V_DIM,
             N_SB_HEADS * HEAD_DIM, N_SB_HEADS * HEAD_DIM, N_SB_HEADS * HEAD_DIM)
D_IN = sum(IN_WIDTHS)

kernel_name = "hybrid_diffattn_stickbreak_deepnorm_encoder"


def _layer_norm(x, g, b):
    xf = x.astype(jnp.float32)
    mu = jnp.mean(xf, axis=-1, keepdims=True)
    var = jnp.mean(jnp.square(xf - mu), axis=-1, keepdims=True)
    return ((xf - mu) * lax.rsqrt(var + LN_EPS) * g.astype(jnp.float32) + b.astype(jnp.float32)).astype(x.dtype)


def _rms_norm(x, g):
    xf = x.astype(jnp.float32)
    ms = jnp.mean(jnp.square(xf), axis=-1, keepdims=True)
    return (xf * lax.rsqrt(ms + RMS_EPS) * g.astype(jnp.float32)).astype(x.dtype)


def _blocks_to_seq(out):
    nb, b, h, qb, dh = out.shape
    return out.transpose(1, 0, 3, 2, 4).reshape(b, nb * qb, h, dh)


def _diff_attention(q, k, v, lam, subln_g, lambda_init):
    b, s = q.shape[:2]
    qh = q.transpose(0, 2, 3, 1, 4)
    kh = k.transpose(0, 2, 3, 1, 4)
    vh = v.transpose(0, 2, 1, 3)
    slopes = jnp.exp2(-8.0 * jnp.arange(1, N_DIFF_HEADS + 1, dtype=jnp.float32) / N_DIFF_HEADS)
    s_pos = jnp.arange(s)
    scale = HEAD_DIM ** -0.5

    def block(start):
        qb = lax.dynamic_slice_in_dim(qh, start, Q_BLOCK, axis=3)
        t_pos = start + jnp.arange(Q_BLOCK)
        sc = jnp.einsum('bhmqd,bhmkd->bhmqk', qb, kh).astype(jnp.float32) * scale
        dist = jnp.abs(t_pos[:, None] - s_pos[None, :]).astype(jnp.float32)
        allowed = (s_pos[None, :] // CHUNK) <= (t_pos[:, None] // CHUNK)
        sc = sc - slopes[None, :, None, None, None] * dist
        sc = jnp.where(allowed, sc, NEG_INF)
        p = jax.nn.softmax(sc, axis=-1)
        a = p[:, :, 0] - lam * p[:, :, 1]
        return jnp.einsum('bhqk,bhkd->bhqd', a.astype(vh.dtype), vh)

    starts = jnp.arange(s // Q_BLOCK) * Q_BLOCK
    out = _blocks_to_seq(lax.map(block, starts))
    out = _rms_norm(out, subln_g) * (1.0 - lambda_init)
    return out.reshape(b, s, N_DIFF_HEADS * DIFF_V_DIM)


def _stick_breaking_attention(q, k, v, norm_g):
    b, s = q.shape[:2]
    qh = q.transpose(0, 2, 1, 3)
    kh = k.transpose(0, 2, 1, 3)
    vh = v.transpose(0, 2, 1, 3)
    s_pos = jnp.arange(s)
    scale = HEAD_DIM ** -0.5

    def block(start):
        qb = lax.dynamic_slice_in_dim(qh, start, Q_BLOCK, axis=2)
        t_pos = start + jnp.arange(Q_BLOCK)
        z = jnp.einsum('bhqd,bhkd->bhqk', qb, kh).astype(jnp.float32) * scale
        strict = s_pos[None, :] < t_pos[:, None]
        log_fail = jnp.where(strict, jax.nn.log_sigmoid(-z), 0.0)
        tail = lax.cumsum(log_fail, axis=3, reverse=True) - log_fail
        log_a = jax.nn.log_sigmoid(z) + tail
        a = jnp.where(strict, jnp.exp(log_a), 0.0)
        return jnp.einsum('bhqk,bhkd->bhqd', a.astype(vh.dtype), vh)

    starts = jnp.arange(s // Q_BLOCK) * Q_BLOCK
    out = _blocks_to_seq(lax.map(block, starts))
    out = _rms_norm(out, norm_g)
    return out.reshape(b, s, N_SB_HEADS * HEAD_DIM)


def _cross_attention(x, mem, w_q, w_kv, w_o):
    b, s, _ = x.shape
    q = (x @ w_q).reshape(b, s, N_XHEADS, XHEAD_DIM)
    kv = mem @ w_kv
    k, v = jnp.split(kv, 2, axis=-1)
    k = k.reshape(b, mem.shape[1], N_XHEADS, XHEAD_DIM)
    v = v.reshape(b, mem.shape[1], N_XHEADS, XHEAD_DIM)
    sc = jnp.einsum('bqhd,bkhd->bhqk', q, k).astype(jnp.float32) * (XHEAD_DIM ** -0.5)
    p = jax.nn.softmax(sc, axis=-1).astype(v.dtype)
    o = jnp.einsum('bhqk,bkhd->bqhd', p, v).reshape(b, s, D_MODEL)
    return o @ w_o


def _swiglu(x, w_gate, w_up, w_down):
    return (jax.nn.silu(x @ w_gate) * (x @ w_up)) @ w_down


def setup_inputs(seed: int = 0) -> dict:
    key = jax.random.key(seed)
    ks = jax.random.split(key, 24)

    def w(k, shape, fan_in, scale=1.0):
        return jax.random.normal(k, shape, jnp.float32) * (fan_in ** -0.5) * scale

    def gain(k, shape):
        return 1.0 + 0.02 * jax.random.normal(k, shape, jnp.float32)

    def bias(k, shape):
        return 0.02 * jax.random.normal(k, shape, jnp.float32)

    x = jax.random.normal(ks[0], (BATCH, SEQ, D_MODEL), jnp.float32)
    mem = jax.random.normal(ks[1], (BATCH, N_MEM, D_MODEL), jnp.float32)
    col_scale = jnp.concatenate([
        jnp.ones((IN_WIDTHS[0] + IN_WIDTHS[1],), jnp.float32),
        jnp.full((IN_WIDTHS[2],), BETA, jnp.float32),
        jnp.ones((IN_WIDTHS[3] + IN_WIDTHS[4],), jnp.float32),
        jnp.full((IN_WIDTHS[5],), BETA, jnp.float32)])
    w_in = w(ks[2], (DEPTH, D_MODEL, D_IN), D_MODEL) * col_scale
    kv_scale = jnp.concatenate([jnp.ones((D_MODEL,), jnp.float32), jnp.full((D_MODEL,), BETA, jnp.float32)])
    return {
        "x": x,
        "mem": mem,
        "w_in": w_in,
        "diff_lambda_q1": 0.1 * jax.random.normal(ks[3], (DEPTH, HEAD_DIM), jnp.float32),
        "diff_lambda_k1": 0.1 * jax.random.normal(ks[4], (DEPTH, HEAD_DIM), jnp.float32),
        "diff_lambda_q2": 0.1 * jax.random.normal(ks[5], (DEPTH, HEAD_DIM), jnp.float32),
        "diff_lambda_k2": 0.1 * jax.random.normal(ks[6], (DEPTH, HEAD_DIM), jnp.float32),
        "diff_subln_g": gain(ks[7], (DEPTH, DIFF_V_DIM)),
        "sb_norm_g": gain(ks[8], (DEPTH, N_SB_HEADS, HEAD_DIM)),
        "w_o": w(ks[9], (DEPTH, DIFF_WIDTH + SB_WIDTH, D_MODEL), DIFF_WIDTH + SB_WIDTH, BETA),
        "ln1_g": gain(ks[10], (DEPTH, D_MODEL)),
        "ln1_b": bias(ks[11], (DEPTH, D_MODEL)),
        "w_xq": w(ks[12], (DEPTH, D_MODEL, D_MODEL), D_MODEL),
        "w_xkv": w(ks[13], (DEPTH, D_MODEL, 2 * D_MODEL), D_MODEL) * kv_scale,
        "w_xo": w(ks[14], (DEPTH, D_MODEL, D_MODEL), D_MODEL, BETA),
        "ln2_g": gain(ks[15], (DEPTH, D_MODEL)),
        "ln2_b": bias(ks[16], (DEPTH, D_MODEL)),
        "w_gate": w(ks[17], (DEPTH, D_MODEL, D_FF), D_MODEL),
        "w_up": w(ks[18], (DEPTH, D_MODEL, D_FF), D_MODEL, BETA),
        "w_down": w(ks[19], (DEPTH, D_FF, D_MODEL), D_FF, BETA),
        "ln3_g": gain(ks[20], (DEPTH, D_MODEL)),
        "ln3_b": bias(ks[21], (DEPTH, D_MODEL)),
    }


def reference(x, mem, w_in, diff_lambda_q1, diff_lambda_k1, diff_lambda_q2, diff_lambda_k2,
              diff_subln_g, sb_norm_g, w_o, ln1_g, ln1_b, w_xq, w_xkv, w_xo, ln2_g, ln2_b,
              w_gate, w_up, w_down, ln3_g, ln3_b):
    b, s, _ = x.shape
    split_at = np.cumsum(IN_WIDTHS)[:-1].tolist()
    for l in range(DEPTH):
        lambda_init = 0.8 - 0.6 * math.exp(-0.3 * l)
        proj = x @ w_in[l]
        q_d, k_d, v_d, q_s, k_s, v_s = jnp.split(proj, split_at, axis=-1)
        q_d = q_d.reshape(b, s, N_DIFF_HEADS, 2, HEAD_DIM)
        k_d = k_d.reshape(b, s, N_DIFF_HEADS, 2, HEAD_DIM)
        v_d = v_d.reshape(b, s, N_DIFF_HEADS, DIFF_V_DIM)
        lam = (jnp.exp(jnp.sum(diff_lambda_q1[l].astype(jnp.float32) * diff_lambda_k1[l].astype(jnp.float32)))
               - jnp.exp(jnp.sum(diff_lambda_q2[l].astype(jnp.float32) * diff_lambda_k2[l].astype(jnp.float32)))
               + lambda_init)
        y_diff = _diff_attention(q_d, k_d, v_d, lam, diff_subln_g[l], lambda_init)
        y_sb = _stick_breaking_attention(q_s.reshape(b, s, N_SB_HEADS, HEAD_DIM),
                                         k_s.reshape(b, s, N_SB_HEADS, HEAD_DIM),
                                         v_s.reshape(b, s, N_SB_HEADS, HEAD_DIM),
                                         sb_norm_g[l])
        mix = jnp.concatenate([y_diff, y_sb], axis=-1) @ w_o[l]
        x = _layer_norm(ALPHA * x + mix, ln1_g[l], ln1_b[l])
        x = _layer_norm(ALPHA * x + _cross_attention(x, mem, w_xq[l], w_xkv[l], w_xo[l]), ln2_g[l], ln2_b[l])
        x = _layer_norm(ALPHA * x + _swiglu(x, w_gate[l], w_up[l], w_down[l]), ln3_g[l], ln3_b[l])
    return x
```

```python
import functools
import math

import jax
import jax.numpy as jnp
from jax import lax
from jax.experimental import pallas as pl
from jax.experimental.pallas import tpu as pltpu

D_MODEL = 1024
DEPTH = 1
CHUNK = 64
N_MEM = 256
HEAD_DIM = 64
N_DIFF_HEADS = 4
DIFF_V_DIM = 2 * HEAD_DIM
N_SB_HEADS = 8
N_XHEADS = 4
XHEAD_DIM = D_MODEL // N_XHEADS
D_FF = 2816
ALPHA = (2.0 * DEPTH) ** 0.25
LN_EPS = 1e-5
RMS_EPS = 1e-5
NEG_INF = -1e30
LAMBDA_INIT = 0.8 - 0.6 * math.exp(-0.3 * 0)

LANES = 128
QD_BLK, KD_BLK, VD_BLK, QS_BLK, KS_BLK, VS_BLK = 0, 4, 8, 12, 16, 20
D_IN = 24 * LANES

TM_PROJ = 512
TM_ROW = 256
TQ = 256
TK = 256
SB_W = 128
VMEM_LIMIT = 56 * 1024 * 1024

BF16 = jnp.bfloat16
F32 = jnp.float32


def _dot(a, b):
    return jnp.dot(a, b, preferred_element_type=F32)


def _dot_nt(a, b):
    return lax.dot_general(a, b, (((1,), (1,)), ((), ())), preferred_element_type=F32)


def _layer_norm(v, g, b):
    mu = jnp.mean(v, axis=-1, keepdims=True)
    d = v - mu
    var = jnp.mean(d * d, axis=-1, keepdims=True)
    return d * lax.rsqrt(var + LN_EPS) * g + b


def _in_proj_kernel(x_ref, w_ref, o_ref):
    o_ref[...] = _dot(x_ref[...].astype(BF16), w_ref[...]).astype(o_ref.dtype)


def _in_proj(x2d, w):
    t, d = x2d.shape
    n = w.shape[1]
    return pl.pallas_call(
        _in_proj_kernel,
        grid=(t // TM_PROJ,),
        in_specs=[pl.BlockSpec((TM_PROJ, d), lambda i: (i, 0)),
                  pl.BlockSpec((d, n), lambda i: (0, 0))],
        out_specs=pl.BlockSpec((TM_PROJ, n), lambda i: (i, 0)),
        out_shape=jax.ShapeDtypeStruct((t, n), BF16),
        compiler_params=pltpu.CompilerParams(dimension_semantics=("arbitrary",),
                                             vmem_limit_bytes=VMEM_LIMIT),
        name="in_proj",
    )(x2d, w)


def _diff_attn_kernel(slopes_ref, lq1_ref, lk1_ref, lq2_ref, lk2_ref, q_ref, k_ref, v_ref,
                      offb_ref, diagb_ref, g_ref, o_ref, acc1_ref, acc2_ref):
    h = pl.program_id(1)
    qi = pl.program_id(2)
    slope = slopes_ref[h]

    lane = lax.broadcasted_iota(jnp.int32, (TQ, LANES), 1)
    qf = q_ref[0].astype(F32) * (HEAD_DIM ** -0.5)
    q1 = jnp.where(lane < HEAD_DIM, qf, 0.0).astype(BF16)
    q2 = jnp.where(lane >= HEAD_DIM, qf, 0.0).astype(BF16)

    acc1_ref[...] = jnp.zeros_like(acc1_ref)
    acc2_ref[...] = jnp.zeros_like(acc2_ref)

    def update(qm, kb, vb, bias, shift, m, l, acc_ref):
        s = _dot_nt(qm, kb) + bias
        mb = jnp.max(s, axis=-1, keepdims=True) + shift
        m_new = jnp.maximum(m, mb)
        alpha = jnp.exp(m - m_new)
        p = jnp.exp(s - (m_new - shift))
        l_new = alpha * l + jnp.sum(p, axis=-1, keepdims=True)
        acc_ref[...] = alpha * acc_ref[...] + _dot(p.astype(BF16), vb)
        return m_new, l_new

    def body(j, carry):
        m1, l1, m2, l2 = carry
        start = pl.multiple_of(j * TK, TK)
        kb = k_ref[0, pl.ds(start, TK), :]
        vb = v_ref[0, pl.ds(start, TK), :]
        bias = offb_ref[0]
        shift = -slope * ((qi - j) * TQ).astype(F32)
        m1, l1 = update(q1, kb, vb, bias, shift, m1, l1, acc1_ref)
        m2, l2 = update(q2, kb, vb, bias, shift, m2, l2, acc2_ref)
        return m1, l1, m2, l2

    init = (jnp.full((TQ, 1), NEG_INF, F32), jnp.zeros((TQ, 1), F32),
            jnp.full((TQ, 1), NEG_INF, F32), jnp.zeros((TQ, 1), F32))
    m1, l1, m2, l2 = lax.fori_loop(0, qi, body, init)

    start = pl.multiple_of(qi * TK, TK)
    kb = k_ref[0, pl.ds(start, TK), :]
    vb = v_ref[0, pl.ds(start, TK), :]
    dbias = diagb_ref[0]
    zero = jnp.float32(0.0)
    m1, l1 = update(q1, kb, vb, dbias, zero, m1, l1, acc1_ref)
    m2, l2 = update(q2, kb, vb, dbias, zero, m2, l2, acc2_ref)

    lam = (jnp.exp(jnp.sum(lq1_ref[...] * lk1_ref[...]))
           - jnp.exp(jnp.sum(lq2_ref[...] * lk2_ref[...])) + LAMBDA_INIT)
    out = acc1_ref[...] * (1.0 / l1) - lam * (acc2_ref[...] * (1.0 / l2))
    ms = jnp.mean(out * out, axis=-1, keepdims=True)
    out = out * lax.rsqrt(ms + RMS_EPS) * g_ref[...] * (1.0 - LAMBDA_INIT)
    o_ref[0] = out.astype(o_ref.dtype)


def _diff_attn(proj, slopes, lq1, lk1, lq2, lk2, offb, diagb, g):
    b, s, _ = proj.shape
    nq = s // TQ
    lam_spec = pl.BlockSpec((1, HEAD_DIM), lambda bi, hi, qi: (0, 0))
    return pl.pallas_call(
        _diff_attn_kernel,
        grid=(b, N_DIFF_HEADS, nq),
        in_specs=[
            pl.BlockSpec(memory_space=pltpu.SMEM),
            lam_spec, lam_spec, lam_spec, lam_spec,
            pl.BlockSpec((1, TQ, LANES), lambda bi, hi, qi: (bi, qi, QD_BLK + hi)),
            pl.BlockSpec((1, s, LANES), lambda bi, hi, qi: (bi, 0, KD_BLK + hi)),
            pl.BlockSpec((1, s, LANES), lambda bi, hi, qi: (bi, 0, VD_BLK + hi)),
            pl.BlockSpec((1, TQ, TK), lambda bi, hi, qi: (hi, 0, 0)),
            pl.BlockSpec((1, TQ, TK), lambda bi, hi, qi: (hi, 0, 0)),
            pl.BlockSpec((1, DIFF_V_DIM), lambda bi, hi, qi: (0, 0)),
        ],
        out_specs=pl.BlockSpec((1, TQ, LANES), lambda bi, hi, qi: (bi, qi, hi)),
        out_shape=jax.ShapeDtypeStruct((b, s, N_DIFF_HEADS * DIFF_V_DIM), BF16),
        scratch_shapes=[pltpu.VMEM((TQ, LANES), F32), pltpu.VMEM((TQ, LANES), F32)],
        compiler_params=pltpu.CompilerParams(
            dimension_semantics=("arbitrary", "arbitrary", "arbitrary"),
            vmem_limit_bytes=VMEM_LIMIT),
        name="diff_attn",
    )(slopes, lq1, lk1, lq2, lk2, proj, proj, proj, offb, diagb, g)


def _sb_attn_kernel(q_ref, k_ref, v_ref, tri_ref, g_ref, o_ref, acc_ref, carry_ref):
    qi = pl.program_id(2)
    lane = lax.broadcasted_iota(jnp.int32, (TQ, LANES), 1)
    lo_half = lane < HEAD_DIM
    qf = q_ref[0].astype(F32) * (HEAD_DIM ** -0.5)
    qm = (jnp.where(lo_half, qf, 0.0).astype(BF16),
          jnp.where(lo_half, 0.0, qf).astype(BF16))
    tri = tri_ref[...]

    acc_ref[...] = jnp.zeros_like(acc_ref)
    carry_ref[...] = jnp.zeros_like(carry_ref)

    def sub_block(start, strict):
        kb = k_ref[0, pl.ds(start, SB_W), :]
        vb = v_ref[0, pl.ds(start, SB_W), :]
        for hh in range(2):
            z = _dot_nt(qm[hh], kb)
            sp = jnp.maximum(z, 0.0) + jnp.log(1.0 + jnp.exp(-jnp.abs(z)))
            if strict is not None:
                sp = jnp.where(strict, sp, 0.0)
            hi = sp.astype(BF16)
            lo = (sp - hi.astype(F32)).astype(BF16)
            r = _dot(jnp.concatenate([hi, lo], axis=1), tri)
            a = jnp.exp(z - carry_ref[hh] - r[:, :SB_W])
            if strict is not None:
                a = jnp.where(strict, a, 0.0)
            acc_ref[hh] += _dot(a.astype(BF16), vb)
            carry_ref[hh] += r[:, SB_W:]

    row = lax.broadcasted_iota(jnp.int32, (TQ, SB_W), 0)
    col = lax.broadcasted_iota(jnp.int32, (TQ, SB_W), 1)
    for d in reversed(range(TQ // SB_W)):
        start = pl.multiple_of(qi * TQ + d * SB_W, SB_W)
        sub_block(start, (col + d * SB_W) < row)

    def body(it, c):
        j = qi * (TQ // SB_W) - 1 - it
        sub_block(pl.multiple_of(j * SB_W, SB_W), None)
        return c

    lax.fori_loop(0, qi * (TQ // SB_W), body, 0)

    out = jnp.where(lo_half, acc_ref[0], acc_ref[1])
    sq = out * out
    ss_lo = jnp.sum(jnp.where(lo_half, sq, 0.0), axis=-1, keepdims=True)
    ss_hi = jnp.sum(jnp.where(lo_half, 0.0, sq), axis=-1, keepdims=True)
    ms = jnp.where(lo_half, ss_lo, ss_hi) * (1.0 / HEAD_DIM)
    o_ref[0] = (out * lax.rsqrt(ms + RMS_EPS) * g_ref[0]).astype(o_ref.dtype)


def _sb_attn(proj, tri, g):
    b, s, _ = proj.shape
    nq = s // TQ
    npair = N_SB_HEADS // 2
    return pl.pallas_call(
        _sb_attn_kernel,
        grid=(b, npair, nq),
        in_specs=[
            pl.BlockSpec((1, TQ, LANES), lambda bi, pi, qi: (bi, qi, QS_BLK + pi)),
            pl.BlockSpec((1, s, LANES), lambda bi, pi, qi: (bi, 0, KS_BLK + pi)),
            pl.BlockSpec((1, s, LANES), lambda bi, pi, qi: (bi, 0, VS_BLK + pi)),
            pl.BlockSpec((2 * SB_W, 2 * SB_W), lambda bi, pi, qi: (0, 0)),
            pl.BlockSpec((1, 1, LANES), lambda bi, pi, qi: (pi, 0, 0)),
        ],
        out_specs=pl.BlockSpec((1, TQ, LANES), lambda bi, pi, qi: (bi, qi, pi)),
        out_shape=jax.ShapeDtypeStruct((b, s, N_SB_HEADS * HEAD_DIM), BF16),
        scratch_shapes=[pltpu.VMEM((2, TQ, LANES), F32), pltpu.VMEM((2, TQ, SB_W), F32)],
        compiler_params=pltpu.CompilerParams(
            dimension_semantics=("arbitrary", "arbitrary", "arbitrary"),
            vmem_limit_bytes=VMEM_LIMIT),
        name="sb_attn",
    )(proj, proj, proj, tri, g)


def _mix_ln1_kernel(yd_ref, ys_ref, x_ref, wd_ref, ws_ref, g_ref, b_ref, o_ref):
    mix = _dot(yd_ref[...], wd_ref[...]) + _dot(ys_ref[...], ws_ref[...])
    o_ref[...] = _layer_norm(ALPHA * x_ref[...] + mix, g_ref[...], b_ref[...])


def _mix_ln1(yd, ys, x2d, w_od, w_os, g, bvec):
    t, d = x2d.shape
    half = yd.shape[1]
    row = lambda i: (i, 0)
    const = lambda i: (0, 0)
    return pl.pallas_call(
        _mix_ln1_kernel,
        grid=(t // TM_ROW,),
        in_specs=[pl.BlockSpec((TM_ROW, half), row), pl.BlockSpec((TM_ROW, half), row),
                  pl.BlockSpec((TM_ROW, d), row),
                  pl.BlockSpec((half, d), const), pl.BlockSpec((half, d), const),
                  pl.BlockSpec((1, d), const), pl.BlockSpec((1, d), const)],
        out_specs=pl.BlockSpec((TM_ROW, d), row),
        out_shape=jax.ShapeDtypeStruct((t, d), F32),
        compiler_params=pltpu.CompilerParams(dimension_semantics=("arbitrary",),
                                             vmem_limit_bytes=VMEM_LIMIT),
        name="mix_ln1",
    )(yd, ys, x2d, w_od, w_os, g, bvec)


def _xkv_kernel(mem_ref, w_ref, o_ref):
    o_ref[0] = _dot(mem_ref[0].astype(BF16), w_ref[...]).astype(o_ref.dtype)


def _xkv_proj(mem, w):
    b, n, d = mem.shape
    return pl.pallas_call(
        _xkv_kernel,
        grid=(b,),
        in_specs=[pl.BlockSpec((1, n, d), lambda i: (i, 0, 0)),
                  pl.BlockSpec((d, 2 * d), lambda i: (0, 0))],
        out_specs=pl.BlockSpec((1, n, 2 * d), lambda i: (i, 0, 0)),
        out_shape=jax.ShapeDtypeStruct((b, n, 2 * d), BF16),
        compiler_params=pltpu.CompilerParams(dimension_semantics=("arbitrary",),
                                             vmem_limit_bytes=VMEM_LIMIT),
        name="xkv_proj",
    )(mem, w)


def _xattn_ln2_kernel(x_ref, kv_ref, wq_ref, wo_ref, g_ref, b_ref, o_ref):
    x1 = x_ref[0]
    q = (_dot(x1.astype(BF16), wq_ref[...]) * (XHEAD_DIM ** -0.5)).astype(BF16)
    heads = []
    for h in range(N_XHEADS):
        qh = q[:, h * XHEAD_DIM:(h + 1) * XHEAD_DIM]
        kh = kv_ref[0, :, h * XHEAD_DIM:(h + 1) * XHEAD_DIM]
        vh = kv_ref[0, :, D_MODEL + h * XHEAD_DIM:D_MODEL + (h + 1) * XHEAD_DIM]
        s = _dot_nt(qh, kh)
        p = jnp.exp(s - jnp.max(s, axis=-1, keepdims=True))
        inv = 1.0 / jnp.sum(p, axis=-1, keepdims=True)
        heads.append((_dot(p.astype(BF16), vh) * inv).astype(BF16))
    o = jnp.concatenate(heads, axis=1)
    o_ref[0] = _layer_norm(ALPHA * x1 + _dot(o, wo_ref[...]), g_ref[...], b_ref[...])


def _xattn_ln2(x1, kv, wq, wo, g, bvec):
    b, s, d = x1.shape
    n = kv.shape[1]
    const = lambda bi, i: (0, 0)
    return pl.pallas_call(
        _xattn_ln2_kernel,
        grid=(b, s // TM_ROW),
        in_specs=[pl.BlockSpec((1, TM_ROW, d), lambda bi, i: (bi, i, 0)),
                  pl.BlockSpec((1, n, 2 * d), lambda bi, i: (bi, 0, 0)),
                  pl.BlockSpec((d, d), const), pl.BlockSpec((d, d), const),
                  pl.BlockSpec((1, d), const), pl.BlockSpec((1, d), const)],
        out_specs=pl.BlockSpec((1, TM_ROW, d), lambda bi, i: (bi, i, 0)),
        out_shape=jax.ShapeDtypeStruct((b, s, d), F32),
        compiler_params=pltpu.CompilerParams(dimension_semantics=("arbitrary", "arbitrary"),
                                             vmem_limit_bytes=VMEM_LIMIT),
        name="xattn_ln2",
    )(x1, kv, wq, wo, g, bvec)


def _swiglu_ln3_kernel(x_ref, wg_ref, wu_ref, wd_ref, g_ref, b_ref, o_ref):
    x2 = x_ref[...]
    xb = x2.astype(BF16)
    gate = _dot(xb, wg_ref[...])
    up = _dot(xb, wu_ref[...])
    hidden = (gate * jax.nn.sigmoid(gate) * up).astype(BF16)
    o_ref[...] = _layer_norm(ALPHA * x2 + _dot(hidden, wd_ref[...]), g_ref[...], b_ref[...])


def _swiglu_ln3(x2d, wg, wu, wd, g, bvec):
    t, d = x2d.shape
    f = wg.shape[1]
    row = lambda i: (i, 0)
    const = lambda i: (0, 0)
    return pl.pallas_call(
        _swiglu_ln3_kernel,
        grid=(t // TM_ROW,),
        in_specs=[pl.BlockSpec((TM_ROW, d), row),
                  pl.BlockSpec((d, f), const), pl.BlockSpec((d, f), const),
                  pl.BlockSpec((f, d), const),
                  pl.BlockSpec((1, d), const), pl.BlockSpec((1, d), const)],
        out_specs=pl.BlockSpec((TM_ROW, d), row),
        out_shape=jax.ShapeDtypeStruct((t, d), F32),
        compiler_params=pltpu.CompilerParams(dimension_semantics=("arbitrary",),
                                             vmem_limit_bytes=VMEM_LIMIT),
        name="swiglu_ln3",
    )(x2d, wg, wu, wd, g, bvec)


def _alibi_tables(slopes):
    i = jnp.arange(TQ, dtype=jnp.int32)[:, None]
    j = jnp.arange(TK, dtype=jnp.int32)[None, :]
    rel = (i - j).astype(F32)
    sl = slopes[:, None, None]
    off = -sl * rel[None]
    allowed = (j // CHUNK) <= (i // CHUNK)
    diag = jnp.where(allowed[None], -sl * jnp.abs(rel)[None], NEG_INF)
    return off, diag


def _suffix_sum_matrix():
    j = jnp.arange(2 * SB_W, dtype=jnp.int32)[:, None] % SB_W
    c = jnp.arange(2 * SB_W, dtype=jnp.int32)[None, :]
    return jnp.where((c >= SB_W) | (j >= c), 1.0, 0.0).astype(BF16)


def kernel(x, mem, w_in, diff_lambda_q1, diff_lambda_k1, diff_lambda_q2, diff_lambda_k2,
           diff_subln_g, sb_norm_g, w_o, ln1_g, ln1_b, w_xq, w_xkv, w_xo, ln2_g, ln2_b,
           w_gate, w_up, w_down, ln3_g, ln3_b):
    b, s, d = x.shape
    assert (b, s, d) == (8, 2048, D_MODEL) and w_in.shape == (DEPTH, D_MODEL, D_IN)
    x2d = x.reshape(b * s, d)
    slopes = jnp.exp2(-8.0 * jnp.arange(1, N_DIFF_HEADS + 1, dtype=F32) / N_DIFF_HEADS)
    offb, diagb = _alibi_tables(slopes)
    tri = _suffix_sum_matrix()
    vec = lambda a: a[0].reshape(1, -1)

    proj = _in_proj(x2d, w_in[0].astype(BF16)).reshape(b, s, D_IN)
    y_diff = _diff_attn(proj, slopes, vec(diff_lambda_q1), vec(diff_lambda_k1),
                        vec(diff_lambda_q2), vec(diff_lambda_k2), offb, diagb, vec(diff_subln_g))
    y_sb = _sb_attn(proj, tri, sb_norm_g[0].reshape(N_SB_HEADS // 2, 1, LANES))

    w_o_b = w_o[0].astype(BF16)
    half = N_DIFF_HEADS * DIFF_V_DIM
    x1 = _mix_ln1(y_diff.reshape(b * s, half), y_sb.reshape(b * s, -1), x2d,
                  w_o_b[:half], w_o_b[half:], vec(ln1_g), vec(ln1_b))

    kv = _xkv_proj(mem, w_xkv[0].astype(BF16))
    x2 = _xattn_ln2(x1.reshape(b, s, d), kv, w_xq[0].astype(BF16), w_xo[0].astype(BF16),
                    vec(ln2_g), vec(ln2_b))

    out = _swiglu_ln3(x2.reshape(b * s, d), w_gate[0].astype(BF16), w_up[0].astype(BF16),
                      w_down[0].astype(BF16), vec(ln3_g), vec(ln3_b))
    return out.reshape(b, s, d)
```

```python
import math

import jax
import jax.numpy as jnp
from jax import lax
from jax.experimental import pallas as pl
from jax.experimental.pallas import tpu as pltpu

D_MODEL = 1024
DEPTH = 1
CHUNK = 64
N_MEM = 256
HEAD_DIM = 64
N_DIFF_HEADS = 4
DIFF_V_DIM = 2 * HEAD_DIM
N_SB_HEADS = 8
N_XHEADS = 4
XHEAD_DIM = D_MODEL // N_XHEADS
D_FF = 2816
ALPHA = (2.0 * DEPTH) ** 0.25
LN_EPS = 1e-5
RMS_EPS = 1e-5
NEG_INF = -1e30
LAMBDA_INIT = 0.8 - 0.6 * math.exp(-0.3 * 0)
LOG2E = math.log2(math.e)

LANES = 128
QD_BLK, KD_BLK, VD_BLK, QS_BLK, KS_BLK, VS_BLK = 0, 4, 8, 12, 16, 20
D_IN = 24 * LANES

TM_PROJ = 512
TM_ROW = 256
TQ = 512
TK = 512
HALF = TQ // 2
SB_W = 128
SB_U = TQ // SB_W
VMEM_LIMIT = 56 * 1024 * 1024

BF16 = jnp.bfloat16
F32 = jnp.float32


def _dot(a, b):
    return jnp.dot(a, b, preferred_element_type=F32)


def _dot_nt(a, b):
    return lax.dot_general(a, b, (((1,), (1,)), ((), ())), preferred_element_type=F32)


def _layer_norm(v, g, b):
    mu = jnp.mean(v, axis=-1, keepdims=True)
    d = v - mu
    var = jnp.mean(d * d, axis=-1, keepdims=True)
    return d * lax.rsqrt(var + LN_EPS) * g + b


def _in_proj_kernel(x_ref, w_ref, o_ref):
    o_ref[...] = _dot(x_ref[...].astype(BF16), w_ref[...]).astype(o_ref.dtype)


def _in_proj(x2d, w):
    t, d = x2d.shape
    n = w.shape[1]
    return pl.pallas_call(
        _in_proj_kernel,
        grid=(t // TM_PROJ,),
        in_specs=[pl.BlockSpec((TM_PROJ, d), lambda i: (i, 0)),
                  pl.BlockSpec((d, n), lambda i: (0, 0))],
        out_specs=pl.BlockSpec((TM_PROJ, n), lambda i: (i, 0)),
        out_shape=jax.ShapeDtypeStruct((t, n), BF16),
        compiler_params=pltpu.CompilerParams(dimension_semantics=("arbitrary",),
                                             vmem_limit_bytes=VMEM_LIMIT),
        name="in_proj",
    )(x2d, w)


def _diff_attn_kernel(slopes_ref, lq1_ref, lk1_ref, lq2_ref, lk2_ref, q_ref, k_ref, v_ref,
                      diagb_ref, g_ref, o_ref, m_ref, l_ref, acc_ref):
    h = pl.program_id(1)
    qi = pl.program_id(2)
    slope2 = slopes_ref[h] * LOG2E

    lane = lax.broadcasted_iota(jnp.int32, (TQ, LANES), 1)
    qf = q_ref[0].astype(F32) * (HEAD_DIM ** -0.5 * LOG2E)
    qm = (jnp.where(lane < HEAD_DIM, qf, 0.0).astype(BF16),
          jnp.where(lane >= HEAD_DIM, qf, 0.0).astype(BF16))

    m_ref[...] = jnp.full_like(m_ref, NEG_INF)
    l_ref[...] = jnp.zeros_like(l_ref)
    acc_ref[...] = jnp.zeros_like(acc_ref)

    def update(mp, r0, nr, kstart, nk, col_bias, row_shift, tile_bias):
        rows = slice(r0, r0 + nr)
        kb = k_ref[0, pl.ds(kstart, nk), :]
        vb = v_ref[0, pl.ds(kstart, nk), :]
        s = _dot_nt(qm[mp][rows], kb)
        if col_bias is not None:
            s = s + col_bias
        if tile_bias is not None:
            s = s + tile_bias
        m_old = m_ref[mp, rows]
        mb = jnp.max(s, axis=-1, keepdims=True) + row_shift
        m_new = jnp.maximum(m_old, mb)
        alpha = jnp.exp2(m_old - m_new)
        mrel = m_new - row_shift
        ps = [jnp.exp2(s[:, c * LANES:(c + 1) * LANES] - mrel) for c in range(nk // LANES)]
        psum = ps[0]
        for pc in ps[1:]:
            psum = psum + pc
        l_ref[mp, rows] = alpha * l_ref[mp, rows] + jnp.sum(psum, axis=-1, keepdims=True)
        p = jnp.concatenate([pc.astype(BF16) for pc in ps], axis=1)
        acc_ref[mp, rows] = alpha * acc_ref[mp, rows] + _dot(p, vb)
        m_ref[mp, rows] = m_new

    col = lax.broadcasted_iota(jnp.int32, (1, TK), 1).astype(F32)
    row = lax.broadcasted_iota(jnp.int32, (TQ, LANES), 0).astype(F32)
    col_bias = slope2 * col

    def body(j, c):
        kstart = pl.multiple_of(j * TK, TK)
        row_shift = -slope2 * (row + ((qi - j) * TQ).astype(F32))
        for mp in range(2):
            update(mp, 0, TQ, kstart, TK, col_bias, row_shift, None)
        return c

    lax.fori_loop(0, qi, body, 0)

    t0 = pl.multiple_of(qi * TQ, TQ)
    dbias = diagb_ref[0]
    zero = jnp.zeros((HALF, LANES), F32)
    lo_shift = -slope2 * (row[:HALF] + float(HALF))
    for mp in range(2):
        update(mp, 0, HALF, t0, HALF, None, zero, dbias)
        update(mp, HALF, HALF, t0, HALF, col_bias[:, :HALF], lo_shift, None)
        update(mp, HALF, HALF, t0 + HALF, HALF, None, zero, dbias)

    lam = (jnp.exp(jnp.sum(lq1_ref[...] * lk1_ref[...]))
           - jnp.exp(jnp.sum(lq2_ref[...] * lk2_ref[...])) + LAMBDA_INIT)
    out = acc_ref[0] * (1.0 / l_ref[0]) - lam * (acc_ref[1] * (1.0 / l_ref[1]))
    ms = jnp.mean(out * out, axis=-1, keepdims=True)
    out = out * lax.rsqrt(ms + RMS_EPS) * g_ref[...] * (1.0 - LAMBDA_INIT)
    o_ref[0] = out.astype(o_ref.dtype)


def _diff_attn(proj, slopes, lq1, lk1, lq2, lk2, diagb, g):
    b, s, _ = proj.shape
    nq = s // TQ
    lam_spec = pl.BlockSpec((1, HEAD_DIM), lambda bi, hi, qi: (0, 0))
    return pl.pallas_call(
        _diff_attn_kernel,
        grid=(b, N_DIFF_HEADS, nq),
        in_specs=[
            pl.BlockSpec(memory_space=pltpu.SMEM),
            lam_spec, lam_spec, lam_spec, lam_spec,
            pl.BlockSpec((1, TQ, LANES), lambda bi, hi, qi: (bi, qi, QD_BLK + hi)),
            pl.BlockSpec((1, s, LANES), lambda bi, hi, qi: (bi, 0, KD_BLK + hi)),
            pl.BlockSpec((1, s, LANES), lambda bi, hi, qi: (bi, 0, VD_BLK + hi)),
            pl.BlockSpec((1, HALF, HALF), lambda bi, hi, qi: (hi, 0, 0)),
            pl.BlockSpec((1, DIFF_V_DIM), lambda bi, hi, qi: (0, 0)),
        ],
        out_specs=pl.BlockSpec((1, TQ, LANES), lambda bi, hi, qi: (bi, qi, hi)),
        out_shape=jax.ShapeDtypeStruct((b, s, N_DIFF_HEADS * DIFF_V_DIM), BF16),
        scratch_shapes=[pltpu.VMEM((2, TQ, LANES), F32), pltpu.VMEM((2, TQ, LANES), F32),
                        pltpu.VMEM((2, TQ, LANES), F32)],
        compiler_params=pltpu.CompilerParams(
            dimension_semantics=("arbitrary", "arbitrary", "arbitrary"),
            vmem_limit_bytes=VMEM_LIMIT),
        name="diff_attn",
    )(slopes, lq1, lk1, lq2, lk2, proj, proj, proj, diagb, g)


def _sb_attn_kernel(q_ref, k_ref, v_ref, tri_ref, g_ref, o_ref, acc_ref, carry_ref):
    qi = pl.program_id(2)
    lane = lax.broadcasted_iota(jnp.int32, (TQ, LANES), 1)
    lo_half = lane < HEAD_DIM
    qf = q_ref[0].astype(F32) * (HEAD_DIM ** -0.5 * LOG2E)
    qm = (jnp.where(lo_half, qf, 0.0).astype(BF16),
          jnp.where(lo_half, 0.0, qf).astype(BF16))
    tri = tri_ref[...]

    acc_ref[...] = jnp.zeros_like(acc_ref)
    carry_ref[...] = jnp.zeros_like(carry_ref)

    def chain(hh, r0, kstart, strict):
        rows = slice(r0, TQ)
        kb = k_ref[0, pl.ds(kstart, SB_W), :]
        vb = v_ref[0, pl.ds(kstart, SB_W), :]
        z = _dot_nt(qm[hh][rows], kb)
        sp = jnp.maximum(z, 0.0) + jnp.log(1.0 + jnp.exp2(-jnp.abs(z))) * LOG2E
        if strict is not None:
            sp = jnp.where(strict, sp, 0.0)
        tot = jnp.sum(sp, axis=-1, keepdims=True)
        hi = sp.astype(BF16)
        lo = (sp - hi.astype(F32)).astype(BF16)
        incl = _dot(jnp.concatenate([hi, lo], axis=1), tri)
        a = jnp.exp2(z - incl - carry_ref[hh, rows])
        if strict is not None:
            a = jnp.where(strict, a, 0.0)
        acc_ref[hh, rows] += _dot(a.astype(BF16), vb)
        carry_ref[hh, rows] += tot

    t0 = pl.multiple_of(qi * TQ, TQ)
    for d in reversed(range(SB_U)):
        r0 = d * SB_W
        row = lax.broadcasted_iota(jnp.int32, (TQ - r0, SB_W), 0)
        colk = lax.broadcasted_iota(jnp.int32, (TQ - r0, SB_W), 1)
        strict = colk < row
        for hh in range(2):
            chain(hh, r0, t0 + r0, strict)

    def body(it, c):
        base = (qi - 1 - it) * TQ
        for u in reversed(range(SB_U)):
            kstart = pl.multiple_of(base + u * SB_W, SB_W)
            for hh in range(2):
                chain(hh, 0, kstart, None)
        return c

    lax.fori_loop(0, qi, body, 0)

    out = jnp.where(lo_half, acc_ref[0], acc_ref[1])
    sq = out * out
    ss_lo = jnp.sum(jnp.where(lo_half, sq, 0.0), axis=-1, keepdims=True)
    ss_hi = jnp.sum(jnp.where(lo_half, 0.0, sq), axis=-1, keepdims=True)
    ms = jnp.where(lo_half, ss_lo, ss_hi) * (1.0 / HEAD_DIM)
    o_ref[0] = (out * lax.rsqrt(ms + RMS_EPS) * g_ref[0]).astype(o_ref.dtype)


def _sb_attn(proj, tri, g):
    b, s, _ = proj.shape
    nq = s // TQ
    npair = N_SB_HEADS // 2
    return pl.pallas_call(
        _sb_attn_kernel,
        grid=(b, npair, nq),
        in_specs=[
            pl.BlockSpec((1, TQ, LANES), lambda bi, pi, qi: (bi, qi, QS_BLK + pi)),
            pl.BlockSpec((1, s, LANES), lambda bi, pi, qi: (bi, 0, KS_BLK + pi)),
            pl.BlockSpec((1, s, LANES), lambda bi, pi, qi: (bi, 0, VS_BLK + pi)),
            pl.BlockSpec((2 * SB_W, SB_W), lambda bi, pi, qi: (0, 0)),
            pl.BlockSpec((1, 1, LANES), lambda bi, pi, qi: (pi, 0, 0)),
        ],
        out_specs=pl.BlockSpec((1, TQ, LANES), lambda bi, pi, qi: (bi, qi, pi)),
        out_shape=jax.ShapeDtypeStruct((b, s, N_SB_HEADS * HEAD_DIM), BF16),
        scratch_shapes=[pltpu.VMEM((2, TQ, LANES), F32), pltpu.VMEM((2, TQ, LANES), F32)],
        compiler_params=pltpu.CompilerParams(
            dimension_semantics=("arbitrary", "arbitrary", "arbitrary"),
            vmem_limit_bytes=VMEM_LIMIT),
        name="sb_attn",
    )(proj, proj, proj, tri, g)


def _mix_ln1_kernel(yd_ref, ys_ref, x_ref, wd_ref, ws_ref, g_ref, b_ref, o_ref):
    mix = _dot(yd_ref[...], wd_ref[...]) + _dot(ys_ref[...], ws_ref[...])
    o_ref[...] = _layer_norm(ALPHA * x_ref[...] + mix, g_ref[...], b_ref[...])


def _mix_ln1(yd, ys, x2d, w_od, w_os, g, bvec):
    t, d = x2d.shape
    half = yd.shape[1]
    row = lambda i: (i, 0)
    const = lambda i: (0, 0)
    return pl.pallas_call(
        _mix_ln1_kernel,
        grid=(t // TM_ROW,),
        in_specs=[pl.BlockSpec((TM_ROW, half), row), pl.BlockSpec((TM_ROW, half), row),
                  pl.BlockSpec((TM_ROW, d), row),
                  pl.BlockSpec((half, d), const), pl.BlockSpec((half, d), const),
                  pl.BlockSpec((1, d), const), pl.BlockSpec((1, d), const)],
        out_specs=pl.BlockSpec((TM_ROW, d), row),
        out_shape=jax.ShapeDtypeStruct((t, d), F32),
        compiler_params=pltpu.CompilerParams(dimension_semantics=("arbitrary",),
                                             vmem_limit_bytes=VMEM_LIMIT),
        name="mix_ln1",
    )(yd, ys, x2d, w_od, w_os, g, bvec)


def _xkv_kernel(mem_ref, w_ref, o_ref):
    o_ref[0] = _dot(mem_ref[0].astype(BF16), w_ref[...]).astype(o_ref.dtype)


def _xkv_proj(mem, w):
    b, n, d = mem.shape
    return pl.pallas_call(
        _xkv_kernel,
        grid=(b,),
        in_specs=[pl.BlockSpec((1, n, d), lambda i: (i, 0, 0)),
                  pl.BlockSpec((d, 2 * d), lambda i: (0, 0))],
        out_specs=pl.BlockSpec((1, n, 2 * d), lambda i: (i, 0, 0)),
        out_shape=jax.ShapeDtypeStruct((b, n, 2 * d), BF16),
        compiler_params=pltpu.CompilerParams(dimension_semantics=("arbitrary",),
                                             vmem_limit_bytes=VMEM_LIMIT),
        name="xkv_proj",
    )(mem, w)


def _xattn_ln2_kernel(x_ref, kv_ref, wq_ref, wo_ref, g_ref, b_ref, o_ref):
    x1 = x_ref[0]
    q = (_dot(x1.astype(BF16), wq_ref[...]) * (XHEAD_DIM ** -0.5)).astype(BF16)
    heads = []
    for h in range(N_XHEADS):
        qh = q[:, h * XHEAD_DIM:(h + 1) * XHEAD_DIM]
        kh = kv_ref[0, :, h * XHEAD_DIM:(h + 1) * XHEAD_DIM]
        vh = kv_ref[0, :, D_MODEL + h * XHEAD_DIM:D_MODEL + (h + 1) * XHEAD_DIM]
        s = _dot_nt(qh, kh)
        p = jnp.exp(s - jnp.max(s, axis=-1, keepdims=True))
        inv = 1.0 / jnp.sum(p, axis=-1, keepdims=True)
        heads.append((_dot(p.astype(BF16), vh) * inv).astype(BF16))
    o = jnp.concatenate(heads, axis=1)
    o_ref[0] = _layer_norm(ALPHA * x1 + _dot(o, wo_ref[...]), g_ref[...], b_ref[...])


def _xattn_ln2(x1, kv, wq, wo, g, bvec):
    b, s, d = x1.shape
    n = kv.shape[1]
    const = lambda bi, i: (0, 0)
    return pl.pallas_call(
        _xattn_ln2_kernel,
        grid=(b, s // TM_ROW),
        in_specs=[pl.BlockSpec((1, TM_ROW, d), lambda bi, i: (bi, i, 0)),
                  pl.BlockSpec((1, n, 2 * d), lambda bi, i: (bi, 0, 0)),
                  pl.BlockSpec((d, d), const), pl.BlockSpec((d, d), const),
                  pl.BlockSpec((1, d), const), pl.BlockSpec((1, d), const)],
        out_specs=pl.BlockSpec((1, TM_ROW, d), lambda bi, i: (bi, i, 0)),
        out_shape=jax.ShapeDtypeStruct((b, s, d), F32),
        compiler_params=pltpu.CompilerParams(dimension_semantics=("arbitrary", "arbitrary"),
                                             vmem_limit_bytes=VMEM_LIMIT),
        name="xattn_ln2",
    )(x1, kv, wq, wo, g, bvec)


def _swiglu_ln3_kernel(x_ref, wg_ref, wu_ref, wd_ref, g_ref, b_ref, o_ref):
    x2 = x_ref[...]
    xb = x2.astype(BF16)
    gate = _dot(xb, wg_ref[...])
    up = _dot(xb, wu_ref[...])
    hidden = (gate * jax.nn.sigmoid(gate) * up).astype(BF16)
    o_ref[...] = _layer_norm(ALPHA * x2 + _dot(hidden, wd_ref[...]), g_ref[...], b_ref[...])


def _swiglu_ln3(x2d, wg, wu, wd, g, bvec):
    t, d = x2d.shape
    f = wg.shape[1]
    row = lambda i: (i, 0)
    const = lambda i: (0, 0)
    return pl.pallas_call(
        _swiglu_ln3_kernel,
        grid=(t // TM_ROW,),
        in_specs=[pl.BlockSpec((TM_ROW, d), row),
                  pl.BlockSpec((d, f), const), pl.BlockSpec((d, f), const),
                  pl.BlockSpec((f, d), const),
                  pl.BlockSpec((1, d), const), pl.BlockSpec((1, d), const)],
        out_specs=pl.BlockSpec((TM_ROW, d), row),
        out_shape=jax.ShapeDtypeStruct((t, d), F32),
        compiler_params=pltpu.CompilerParams(dimension_semantics=("arbitrary",),
                                             vmem_limit_bytes=VMEM_LIMIT),
        name="swiglu_ln3",
    )(x2d, wg, wu, wd, g, bvec)


def _alibi_diag_table(slopes):
    i = jnp.arange(HALF, dtype=jnp.int32)[:, None]
    j = jnp.arange(HALF, dtype=jnp.int32)[None, :]
    dist = jnp.abs(i - j).astype(F32)
    allowed = (j // CHUNK) <= (i // CHUNK)
    return jnp.where(allowed[None], -(slopes * LOG2E)[:, None, None] * dist[None], NEG_INF)


def _suffix_sum_matrix():
    j = jnp.arange(2 * SB_W, dtype=jnp.int32)[:, None] % SB_W
    c = jnp.arange(SB_W, dtype=jnp.int32)[None, :]
    return jnp.where(j >= c, 1.0, 0.0).astype(BF16)


def kernel(x, mem, w_in, diff_lambda_q1, diff_lambda_k1, diff_lambda_q2, diff_lambda_k2,
           diff_subln_g, sb_norm_g, w_o, ln1_g, ln1_b, w_xq, w_xkv, w_xo, ln2_g, ln2_b,
           w_gate, w_up, w_down, ln3_g, ln3_b):
    b, s, d = x.shape
    assert (b, s, d) == (8, 2048, D_MODEL) and w_in.shape == (DEPTH, D_MODEL, D_IN)
    x2d = x.reshape(b * s, d)
    slopes = jnp.exp2(-8.0 * jnp.arange(1, N_DIFF_HEADS + 1, dtype=F32) / N_DIFF_HEADS)
    diagb = _alibi_diag_table(slopes)
    tri = _suffix_sum_matrix()
    vec = lambda a: a[0].reshape(1, -1)

    proj = _in_proj(x2d, w_in[0].astype(BF16)).reshape(b, s, D_IN)
    y_diff = _diff_attn(proj, slopes, vec(diff_lambda_q1), vec(diff_lambda_k1),
                        vec(diff_lambda_q2), vec(diff_lambda_k2), diagb, vec(diff_subln_g))
    y_sb = _sb_attn(proj, tri, sb_norm_g[0].reshape(N_SB_HEADS // 2, 1, LANES))

    w_o_b = w_o[0].astype(BF16)
    half = N_DIFF_HEADS * DIFF_V_DIM
    x1 = _mix_ln1(y_diff.reshape(b * s, half), y_sb.reshape(b * s, -1), x2d,
                  w_o_b[:half], w_o_b[half:], vec(ln1_g), vec(ln1_b))

    kv = _xkv_proj(mem, w_xkv[0].astype(BF16))
    x2 = _xattn_ln2(x1.reshape(b, s, d), kv, w_xq[0].astype(BF16), w_xo[0].astype(BF16),
                    vec(ln2_g), vec(ln2_b))

    out = _swiglu_ln3(x2.reshape(b * s, d), w_gate[0].astype(BF16), w_up[0].astype(BF16),
                      w_down[0].astype(BF16), vec(ln3_g), vec(ln3_b))
    return out.reshape(b, s, d)
```

```python
import math

import jax
import jax.numpy as jnp
from jax import lax
from jax.experimental import pallas as pl
from jax.experimental.pallas import tpu as pltpu

D_MODEL = 1024
DEPTH = 1
CHUNK = 64
N_MEM = 256
HEAD_DIM = 64
N_DIFF_HEADS = 4
DIFF_V_DIM = 2 * HEAD_DIM
N_SB_HEADS = 8
N_XHEADS = 4
XHEAD_DIM = D_MODEL // N_XHEADS
D_FF = 2816
ALPHA = (2.0 * DEPTH) ** 0.25
LN_EPS = 1e-5
RMS_EPS = 1e-5
NEG_INF = -1e30
LAMBDA_INIT = 0.8 - 0.6 * math.exp(-0.3 * 0)
LOG2E = math.log2(math.e)

LANES = 128
QD_BLK, KD_BLK, VD_BLK, QS_BLK, KS_BLK, VS_BLK = 0, 4, 8, 12, 16, 20
D_IN = 24 * LANES

TM_PROJ = 512
TM_ROW = 256
TQ = 512
SEQ = 2048
NQ = SEQ // TQ
SB_W = 128
SB_U = TQ // SB_W
VMEM_LIMIT = 56 * 1024 * 1024

BF16 = jnp.bfloat16
F32 = jnp.float32


def _dot(a, b):
    return jnp.dot(a, b, preferred_element_type=F32)


def _dot_nt(a, b):
    return lax.dot_general(a, b, (((1,), (1,)), ((), ())), preferred_element_type=F32)


def _layer_norm(v, g, b):
    mu = jnp.mean(v, axis=-1, keepdims=True)
    d = v - mu
    var = jnp.mean(d * d, axis=-1, keepdims=True)
    return d * lax.rsqrt(var + LN_EPS) * g + b


def _in_proj_kernel(x_ref, w_ref, o_ref):
    o_ref[...] = _dot(x_ref[...].astype(BF16), w_ref[...]).astype(o_ref.dtype)


def _in_proj(x2d, w):
    t, d = x2d.shape
    n = w.shape[1]
    return pl.pallas_call(
        _in_proj_kernel,
        grid=(t // TM_PROJ,),
        in_specs=[pl.BlockSpec((TM_PROJ, d), lambda i: (i, 0)),
                  pl.BlockSpec((d, n), lambda i: (0, 0))],
        out_specs=pl.BlockSpec((TM_PROJ, n), lambda i: (i, 0)),
        out_shape=jax.ShapeDtypeStruct((t, n), BF16),
        compiler_params=pltpu.CompilerParams(dimension_semantics=("arbitrary",),
                                             vmem_limit_bytes=VMEM_LIMIT),
        name="in_proj",
    )(x2d, w)


def _diff_attn_kernel(slopes_ref, lq1_ref, lk1_ref, lq2_ref, lk2_ref, q_ref, k_ref, v_ref,
                      diagb_ref, g_ref, o_ref, m_ref, l_ref, acc_ref, s_ref, mb_ref):
    h = pl.program_id(1)
    qi = pl.program_id(2)
    slope2 = slopes_ref[h] * LOG2E

    lane = lax.broadcasted_iota(jnp.int32, (TQ, LANES), 1)
    qf = q_ref[0].astype(F32) * (HEAD_DIM ** -0.5 * LOG2E)
    qm = (jnp.where(lane < HEAD_DIM, qf, 0.0).astype(BF16),
          jnp.where(lane >= HEAD_DIM, qf, 0.0).astype(BF16))

    m_ref[...] = jnp.full_like(m_ref, NEG_INF)
    l_ref[...] = jnp.zeros_like(l_ref)
    acc_ref[...] = jnp.zeros_like(acc_ref)

    col = lax.broadcasted_iota(jnp.int32, (1, TQ), 1).astype(F32)
    row = lax.broadcasted_iota(jnp.int32, (TQ, LANES), 0).astype(F32)
    col_bias = slope2 * col

    def scores(slot, kstart, bias):
        kb = k_ref[0, kstart:kstart + TQ, :]
        for mp in range(2):
            s = _dot_nt(qm[mp], kb) + bias
            s_ref[slot, mp] = s
            mb_ref[slot, mp] = jnp.broadcast_to(jnp.max(s, axis=-1, keepdims=True), (TQ, LANES))

    def softmax_pv(slot, kstart, row_shift):
        vb = v_ref[0, kstart:kstart + TQ, :]
        for mp in range(2):
            m_old = m_ref[mp]
            m_new = jnp.maximum(m_old, mb_ref[slot, mp] + row_shift)
            alpha = jnp.exp2(m_old - m_new)
            mrel = m_new - row_shift
            ps = [jnp.exp2(s_ref[slot, mp, :, c * LANES:(c + 1) * LANES] - mrel)
                  for c in range(TQ // LANES)]
            psum = ps[0]
            for pc in ps[1:]:
                psum = psum + pc
            l_ref[mp] = alpha * l_ref[mp] + psum
            p = jnp.concatenate([pc.astype(BF16) for pc in ps], axis=1)
            acc_ref[mp] = alpha * acc_ref[mp] + _dot(p, vb)
            m_ref[mp] = m_new

    def query_block(nb):
        def issue(i):
            if i < nb:
                scores(i % 2, i * TQ, col_bias)
            else:
                scores(i % 2, nb * TQ, diagb_ref[0])

        issue(0)
        for i in range(nb + 1):
            if i < nb:
                issue(i + 1)
            row_shift = -slope2 * (row + float((nb - i) * TQ)) if i < nb else jnp.zeros_like(row)
            softmax_pv(i % 2, i * TQ, row_shift)

    for nb in range(NQ):
        pl.when(qi == nb)(lambda nb=nb: query_block(nb))

    lam = (jnp.exp(jnp.sum(lq1_ref[...] * lk1_ref[...]))
           - jnp.exp(jnp.sum(lq2_ref[...] * lk2_ref[...])) + LAMBDA_INIT)
    l1 = jnp.sum(l_ref[0], axis=-1, keepdims=True)
    l2 = jnp.sum(l_ref[1], axis=-1, keepdims=True)
    out = acc_ref[0] * (1.0 / l1) - lam * (acc_ref[1] * (1.0 / l2))
    ms = jnp.mean(out * out, axis=-1, keepdims=True)
    out = out * lax.rsqrt(ms + RMS_EPS) * g_ref[...] * (1.0 - LAMBDA_INIT)
    o_ref[0] = out.astype(o_ref.dtype)


def _diff_attn(proj, slopes, lq1, lk1, lq2, lk2, diagb, g):
    b, s, _ = proj.shape
    nq = s // TQ
    lam_spec = pl.BlockSpec((1, HEAD_DIM), lambda bi, hi, qi: (0, 0))
    return pl.pallas_call(
        _diff_attn_kernel,
        grid=(b, N_DIFF_HEADS, nq),
        in_specs=[
            pl.BlockSpec(memory_space=pltpu.SMEM),
            lam_spec, lam_spec, lam_spec, lam_spec,
            pl.BlockSpec((1, TQ, LANES), lambda bi, hi, qi: (bi, qi, QD_BLK + hi)),
            pl.BlockSpec((1, s, LANES), lambda bi, hi, qi: (bi, 0, KD_BLK + hi)),
            pl.BlockSpec((1, s, LANES), lambda bi, hi, qi: (bi, 0, VD_BLK + hi)),
            pl.BlockSpec((1, TQ, TQ), lambda bi, hi, qi: (hi, 0, 0)),
            pl.BlockSpec((1, DIFF_V_DIM), lambda bi, hi, qi: (0, 0)),
        ],
        out_specs=pl.BlockSpec((1, TQ, LANES), lambda bi, hi, qi: (bi, qi, hi)),
        out_shape=jax.ShapeDtypeStruct((b, s, N_DIFF_HEADS * DIFF_V_DIM), BF16),
        scratch_shapes=[pltpu.VMEM((2, TQ, LANES), F32), pltpu.VMEM((2, TQ, LANES), F32),
                        pltpu.VMEM((2, TQ, LANES), F32), pltpu.VMEM((2, 2, TQ, TQ), F32),
                        pltpu.VMEM((2, 2, TQ, LANES), F32)],
        compiler_params=pltpu.CompilerParams(
            dimension_semantics=("arbitrary", "arbitrary", "arbitrary"),
            vmem_limit_bytes=VMEM_LIMIT),
        name="diff_attn",
    )(slopes, lq1, lk1, lq2, lk2, proj, proj, proj, diagb, g)


def _sb_attn_kernel(q_ref, k_ref, v_ref, tri_ref, g_ref, o_ref, acc_ref):
    qi = pl.program_id(2)
    lane = lax.broadcasted_iota(jnp.int32, (TQ, LANES), 1)
    lo_half = lane < HEAD_DIM
    qf = q_ref[0].astype(F32) * (HEAD_DIM ** -0.5 * LOG2E)
    qm = (jnp.where(lo_half, qf, 0.0).astype(BF16),
          jnp.where(lo_half, 0.0, qf).astype(BF16))
    tri = tri_ref[...]

    row = lax.broadcasted_iota(jnp.int32, (TQ, SB_W), 0)
    colk = lax.broadcasted_iota(jnp.int32, (TQ, SB_W), 1)
    order = [(hh, u) for hh in range(2) for u in reversed(range(SB_U))]

    def query_block(nb):
        starts = [nb * TQ] + [j * TQ for j in reversed(range(nb))]
        strict = {u: (colk + u * SB_W) < row for u in range(SB_U)}

        def scores(i):
            kb = k_ref[0, starts[i]:starts[i] + TQ, :]
            return [_dot_nt(qm[hh], kb) for hh in range(2)]

        def softplus_split(i, zs):
            z, x = {}, {}
            for key in order:
                hh, u = key
                zc = zs[hh][:, u * SB_W:(u + 1) * SB_W]
                sp = jnp.maximum(zc, 0.0) + jnp.log(1.0 + jnp.exp2(-jnp.abs(zc))) * LOG2E
                if i == 0:
                    sp = jnp.where(strict[u], sp, 0.0)
                hi = sp.astype(BF16)
                lo = (sp - hi.astype(F32)).astype(BF16)
                z[key] = zc
                x[key] = jnp.concatenate([hi, lo], axis=1)
            return z, x

        def suffix_sums(x):
            return {key: _dot(x[key], tri) for key in order}

        def weights_pv(i, z, r, carry, acc):
            vb = v_ref[0, starts[i]:starts[i] + TQ, :]
            for hh in range(2):
                a = {}
                for u in reversed(range(SB_U)):
                    key = (hh, u)
                    av = jnp.exp2(z[key] - r[key][:, :SB_W] - carry[hh])
                    if i == 0:
                        av = jnp.where(strict[u], av, 0.0)
                    a[u] = av.astype(BF16)
                    carry[hh] = carry[hh] + r[key][:, SB_W:]
                acc[hh] = acc[hh] + _dot(jnp.concatenate([a[u] for u in range(SB_U)], axis=1), vb)

        carry = [jnp.zeros((TQ, SB_W), F32), jnp.zeros((TQ, SB_W), F32)]
        acc = [jnp.zeros((TQ, LANES), F32), jnp.zeros((TQ, LANES), F32)]
        z, x = softplus_split(0, scores(0))
        for i in range(nb + 1):
            if i < nb:
                zs_next = scores(i + 1)
            r = suffix_sums(x)
            if i < nb:
                z_next, x_next = softplus_split(i + 1, zs_next)
            weights_pv(i, z, r, carry, acc)
            if i < nb:
                z, x = z_next, x_next
        acc_ref[0] = acc[0]
        acc_ref[1] = acc[1]

    for nb in range(NQ):
        pl.when(qi == nb)(lambda nb=nb: query_block(nb))

    out = jnp.where(lo_half, acc_ref[0], acc_ref[1])
    sq = out * out
    ss_lo = jnp.sum(jnp.where(lo_half, sq, 0.0), axis=-1, keepdims=True)
    ss_hi = jnp.sum(jnp.where(lo_half, 0.0, sq), axis=-1, keepdims=True)
    ms = jnp.where(lo_half, ss_lo, ss_hi) * (1.0 / HEAD_DIM)
    o_ref[0] = (out * lax.rsqrt(ms + RMS_EPS) * g_ref[0]).astype(o_ref.dtype)


def _sb_attn(proj, tri, g):
    b, s, _ = proj.shape
    nq = s // TQ
    npair = N_SB_HEADS // 2
    return pl.pallas_call(
        _sb_attn_kernel,
        grid=(b, npair, nq),
        in_specs=[
            pl.BlockSpec((1, TQ, LANES), lambda bi, pi, qi: (bi, qi, QS_BLK + pi)),
            pl.BlockSpec((1, s, LANES), lambda bi, pi, qi: (bi, 0, KS_BLK + pi)),
            pl.BlockSpec((1, s, LANES), lambda bi, pi, qi: (bi, 0, VS_BLK + pi)),
            pl.BlockSpec((2 * SB_W, 2 * SB_W), lambda bi, pi, qi: (0, 0)),
            pl.BlockSpec((1, 1, LANES), lambda bi, pi, qi: (pi, 0, 0)),
        ],
        out_specs=pl.BlockSpec((1, TQ, LANES), lambda bi, pi, qi: (bi, qi, pi)),
        out_shape=jax.ShapeDtypeStruct((b, s, N_SB_HEADS * HEAD_DIM), BF16),
        scratch_shapes=[pltpu.VMEM((2, TQ, LANES), F32)],
        compiler_params=pltpu.CompilerParams(
            dimension_semantics=("arbitrary", "arbitrary", "arbitrary"),
            vmem_limit_bytes=VMEM_LIMIT),
        name="sb_attn",
    )(proj, proj, proj, tri, g)


def _mix_ln1_kernel(yd_ref, ys_ref, x_ref, wd_ref, ws_ref, g_ref, b_ref, o_ref):
    mix = _dot(yd_ref[...], wd_ref[...]) + _dot(ys_ref[...], ws_ref[...])
    o_ref[...] = _layer_norm(ALPHA * x_ref[...] + mix, g_ref[...], b_ref[...])


def _mix_ln1(yd, ys, x2d, w_od, w_os, g, bvec):
    t, d = x2d.shape
    half = yd.shape[1]
    row = lambda i: (i, 0)
    const = lambda i: (0, 0)
    return pl.pallas_call(
        _mix_ln1_kernel,
        grid=(t // TM_ROW,),
        in_specs=[pl.BlockSpec((TM_ROW, half), row), pl.BlockSpec((TM_ROW, half), row),
                  pl.BlockSpec((TM_ROW, d), row),
                  pl.BlockSpec((half, d), const), pl.BlockSpec((half, d), const),
                  pl.BlockSpec((1, d), const), pl.BlockSpec((1, d), const)],
        out_specs=pl.BlockSpec((TM_ROW, d), row),
        out_shape=jax.ShapeDtypeStruct((t, d), F32),
        compiler_params=pltpu.CompilerParams(dimension_semantics=("arbitrary",),
                                             vmem_limit_bytes=VMEM_LIMIT),
        name="mix_ln1",
    )(yd, ys, x2d, w_od, w_os, g, bvec)


def _xkv_kernel(mem_ref, w_ref, o_ref):
    o_ref[0] = _dot(mem_ref[0].astype(BF16), w_ref[...]).astype(o_ref.dtype)


def _xkv_proj(mem, w):
    b, n, d = mem.shape
    return pl.pallas_call(
        _xkv_kernel,
        grid=(b,),
        in_specs=[pl.BlockSpec((1, n, d), lambda i: (i, 0, 0)),
                  pl.BlockSpec((d, 2 * d), lambda i: (0, 0))],
        out_specs=pl.BlockSpec((1, n, 2 * d), lambda i: (i, 0, 0)),
        out_shape=jax.ShapeDtypeStruct((b, n, 2 * d), BF16),
        compiler_params=pltpu.CompilerParams(dimension_semantics=("arbitrary",),
                                             vmem_limit_bytes=VMEM_LIMIT),
        name="xkv_proj",
    )(mem, w)


def _xattn_ln2_kernel(x_ref, kv_ref, wq_ref, wo_ref, g_ref, b_ref, o_ref):
    x1 = x_ref[0]
    q = (_dot(x1.astype(BF16), wq_ref[...]) * (XHEAD_DIM ** -0.5)).astype(BF16)
    heads = []
    for h in range(N_XHEADS):
        qh = q[:, h * XHEAD_DIM:(h + 1) * XHEAD_DIM]
        kh = kv_ref[0, :, h * XHEAD_DIM:(h + 1) * XHEAD_DIM]
        vh = kv_ref[0, :, D_MODEL + h * XHEAD_DIM:D_MODEL + (h + 1) * XHEAD_DIM]
        s = _dot_nt(qh, kh)
        p = jnp.exp(s - jnp.max(s, axis=-1, keepdims=True))
        inv = 1.0 / jnp.sum(p, axis=-1, keepdims=True)
        heads.append((_dot(p.astype(BF16), vh) * inv).astype(BF16))
    o = jnp.concatenate(heads, axis=1)
    o_ref[0] = _layer_norm(ALPHA * x1 + _dot(o, wo_ref[...]), g_ref[...], b_ref[...])


def _xattn_ln2(x1, kv, wq, wo, g, bvec):
    b, s, d = x1.shape
    n = kv.shape[1]
    const = lambda bi, i: (0, 0)
    return pl.pallas_call(
        _xattn_ln2_kernel,
        grid=(b, s // TM_ROW),
        in_specs=[pl.BlockSpec((1, TM_ROW, d), lambda bi, i: (bi, i, 0)),
                  pl.BlockSpec((1, n, 2 * d), lambda bi, i: (bi, 0, 0)),
                  pl.BlockSpec((d, d), const), pl.BlockSpec((d, d), const),
                  pl.BlockSpec((1, d), const), pl.BlockSpec((1, d), const)],
        out_specs=pl.BlockSpec((1, TM_ROW, d), lambda bi, i: (bi, i, 0)),
        out_shape=jax.ShapeDtypeStruct((b, s, d), F32),
        compiler_params=pltpu.CompilerParams(dimension_semantics=("arbitrary", "arbitrary"),
                                             vmem_limit_bytes=VMEM_LIMIT),
        name="xattn_ln2",
    )(x1, kv, wq, wo, g, bvec)


def _swiglu_ln3_kernel(x_ref, wg_ref, wu_ref, wd_ref, g_ref, b_ref, o_ref):
    x2 = x_ref[...]
    xb = x2.astype(BF16)
    gate = _dot(xb, wg_ref[...])
    up = _dot(xb, wu_ref[...])
    hidden = (gate * jax.nn.sigmoid(gate) * up).astype(BF16)
    o_ref[...] = _layer_norm(ALPHA * x2 + _dot(hidden, wd_ref[...]), g_ref[...], b_ref[...])


def _swiglu_ln3(x2d, wg, wu, wd, g, bvec):
    t, d = x2d.shape
    f = wg.shape[1]
    row = lambda i: (i, 0)
    const = lambda i: (0, 0)
    return pl.pallas_call(
        _swiglu_ln3_kernel,
        grid=(t // TM_ROW,),
        in_specs=[pl.BlockSpec((TM_ROW, d), row),
                  pl.BlockSpec((d, f), const), pl.BlockSpec((d, f), const),
                  pl.BlockSpec((f, d), const),
                  pl.BlockSpec((1, d), const), pl.BlockSpec((1, d), const)],
        out_specs=pl.BlockSpec((TM_ROW, d), row),
        out_shape=jax.ShapeDtypeStruct((t, d), F32),
        compiler_params=pltpu.CompilerParams(dimension_semantics=("arbitrary",),
                                             vmem_limit_bytes=VMEM_LIMIT),
        name="swiglu_ln3",
    )(x2d, wg, wu, wd, g, bvec)


def _alibi_diag_table(slopes):
    i = jnp.arange(TQ, dtype=jnp.int32)[:, None]
    j = jnp.arange(TQ, dtype=jnp.int32)[None, :]
    dist = jnp.abs(i - j).astype(F32)
    allowed = (j // CHUNK) <= (i // CHUNK)
    return jnp.where(allowed[None], -(slopes * LOG2E)[:, None, None] * dist[None], NEG_INF)


def _suffix_sum_matrix():
    j = jnp.arange(2 * SB_W, dtype=jnp.int32)[:, None] % SB_W
    c = jnp.arange(2 * SB_W, dtype=jnp.int32)[None, :]
    return jnp.where((c >= SB_W) | (j >= c), 1.0, 0.0).astype(BF16)


def kernel(x, mem, w_in, diff_lambda_q1, diff_lambda_k1, diff_lambda_q2, diff_lambda_k2,
           diff_subln_g, sb_norm_g, w_o, ln1_g, ln1_b, w_xq, w_xkv, w_xo, ln2_g, ln2_b,
           w_gate, w_up, w_down, ln3_g, ln3_b):
    b, s, d = x.shape
    assert (b, s, d) == (8, SEQ, D_MODEL) and w_in.shape == (DEPTH, D_MODEL, D_IN)
    x2d = x.reshape(b * s, d)
    slopes = jnp.exp2(-8.0 * jnp.arange(1, N_DIFF_HEADS + 1, dtype=F32) / N_DIFF_HEADS)
    diagb = _alibi_diag_table(slopes)
    tri = _suffix_sum_matrix()
    vec = lambda a: a[0].reshape(1, -1)

    proj = _in_proj(x2d, w_in[0].astype(BF16)).reshape(b, s, D_IN)
    y_diff = _diff_attn(proj, slopes, vec(diff_lambda_q1), vec(diff_lambda_k1),
                        vec(diff_lambda_q2), vec(diff_lambda_k2), diagb, vec(diff_subln_g))
    y_sb = _sb_attn(proj, tri, sb_norm_g[0].reshape(N_SB_HEADS // 2, 1, LANES))

    w_o_b = w_o[0].astype(BF16)
    half = N_DIFF_HEADS * DIFF_V_DIM
    x1 = _mix_ln1(y_diff.reshape(b * s, half), y_sb.reshape(b * s, -1), x2d,
                  w_o_b[:half], w_o_b[half:], vec(ln1_g), vec(ln1_b))

    kv = _xkv_proj(mem, w_xkv[0].astype(BF16))
    x2 = _xattn_ln2(x1.reshape(b, s, d), kv, w_xq[0].astype(BF16), w_xo[0].astype(BF16),
                    vec(ln2_g), vec(ln2_b))

    out = _swiglu_ln3(x2.reshape(b * s, d), w_gate[0].astype(BF16), w_up[0].astype(BF16),
                      w_down[0].astype(BF16), vec(ln3_g), vec(ln3_b))
    return out.reshape(b, s, d)
```

```python
import math

import jax
import jax.numpy as jnp
from jax import lax
from jax.experimental import pallas as pl
from jax.experimental.pallas import tpu as pltpu

D_MODEL = 1024
DEPTH = 1
CHUNK = 64
N_MEM = 256
HEAD_DIM = 64
N_DIFF_HEADS = 4
DIFF_V_DIM = 2 * HEAD_DIM
N_SB_HEADS = 8
N_XHEADS = 4
XHEAD_DIM = D_MODEL // N_XHEADS
D_FF = 2816
ALPHA = (2.0 * DEPTH) ** 0.25
LN_EPS = 1e-5
RMS_EPS = 1e-5
NEG_INF = -1e30
LAMBDA_INIT = 0.8 - 0.6 * math.exp(-0.3 * 0)
LOG2E = math.log2(math.e)

LANES = 128
QD_BLK, KD_BLK, VD_BLK, QS_BLK, KS_BLK, VS_BLK = 0, 4, 8, 12, 16, 20
D_IN = 24 * LANES

TM_PROJ = 512
TM_ROW = 256
N_SUB = 2
TQ = 512
SEQ = 2048
NQ = SEQ // TQ
SB_W = 128
SB_U = TQ // SB_W
VMEM_LIMIT = 56 * 1024 * 1024

BF16 = jnp.bfloat16
F32 = jnp.float32


def _dot(a, b):
    return jnp.dot(a, b, preferred_element_type=F32)


def _dot_nt(a, b):
    return lax.dot_general(a, b, (((1,), (1,)), ((), ())), preferred_element_type=F32)


def _layer_norm(v, g, b):
    mu = jnp.mean(v, axis=-1, keepdims=True)
    d = v - mu
    var = jnp.mean(d * d, axis=-1, keepdims=True)
    return d * lax.rsqrt(var + LN_EPS) * g + b


def _in_proj_kernel(x_ref, w_ref, o_ref):
    o_ref[...] = _dot(x_ref[...].astype(BF16), w_ref[...]).astype(o_ref.dtype)


def _in_proj(x2d, w):
    t, d = x2d.shape
    n = w.shape[1]
    return pl.pallas_call(
        _in_proj_kernel,
        grid=(t // TM_PROJ,),
        in_specs=[pl.BlockSpec((TM_PROJ, d), lambda i: (i, 0)),
                  pl.BlockSpec((d, n), lambda i: (0, 0))],
        out_specs=pl.BlockSpec((TM_PROJ, n), lambda i: (i, 0)),
        out_shape=jax.ShapeDtypeStruct((t, n), BF16),
        compiler_params=pltpu.CompilerParams(dimension_semantics=("arbitrary",),
                                             vmem_limit_bytes=VMEM_LIMIT),
        name="in_proj",
    )(x2d, w)


def _diff_attn_kernel(slopes_ref, lq1_ref, lk1_ref, lq2_ref, lk2_ref, q_ref, k_ref, v_ref,
                      diagb_ref, g_ref, o_ref, m_ref, l_ref, acc_ref, s_ref, mb_ref):
    h = pl.program_id(1)
    qi = pl.program_id(2)
    slope2 = slopes_ref[h] * LOG2E

    lane = lax.broadcasted_iota(jnp.int32, (TQ, LANES), 1)
    qf = q_ref[0].astype(F32) * (HEAD_DIM ** -0.5 * LOG2E)
    qm = (jnp.where(lane < HEAD_DIM, qf, 0.0).astype(BF16),
          jnp.where(lane >= HEAD_DIM, qf, 0.0).astype(BF16))

    m_ref[...] = jnp.full_like(m_ref, NEG_INF)
    l_ref[...] = jnp.zeros_like(l_ref)
    acc_ref[...] = jnp.zeros_like(acc_ref)

    col = lax.broadcasted_iota(jnp.int32, (1, TQ), 1).astype(F32)
    row = lax.broadcasted_iota(jnp.int32, (TQ, LANES), 0).astype(F32)
    col_bias = slope2 * col

    def scores(slot, kstart, bias):
        kb = k_ref[0, kstart:kstart + TQ, :]
        for mp in range(2):
            s = _dot_nt(qm[mp], kb) + bias
            s_ref[slot, mp] = s
            mb_ref[slot, mp] = jnp.broadcast_to(jnp.max(s, axis=-1, keepdims=True), (TQ, LANES))

    def softmax_pv(slot, kstart, row_shift):
        vb = v_ref[0, kstart:kstart + TQ, :]
        for mp in range(2):
            m_old = m_ref[mp]
            m_new = jnp.maximum(m_old, mb_ref[slot, mp] + row_shift)
            alpha = jnp.exp2(m_old - m_new)
            mrel = m_new - row_shift
            ps = [jnp.exp2(s_ref[slot, mp, :, c * LANES:(c + 1) * LANES] - mrel)
                  for c in range(TQ // LANES)]
            psum = ps[0]
            for pc in ps[1:]:
                psum = psum + pc
            l_ref[mp] = alpha * l_ref[mp] + psum
            p = jnp.concatenate([pc.astype(BF16) for pc in ps], axis=1)
            acc_ref[mp] = alpha * acc_ref[mp] + _dot(p, vb)
            m_ref[mp] = m_new

    def query_block(nb):
        def issue(i):
            if i < nb:
                scores(i % 2, i * TQ, col_bias)
            else:
                scores(i % 2, nb * TQ, diagb_ref[0])

        issue(0)
        for i in range(nb + 1):
            if i < nb:
                issue(i + 1)
            row_shift = -slope2 * (row + float((nb - i) * TQ)) if i < nb else jnp.zeros_like(row)
            softmax_pv(i % 2, i * TQ, row_shift)

    for nb in range(NQ):
        pl.when(qi == nb)(lambda nb=nb: query_block(nb))

    lam = (jnp.exp(jnp.sum(lq1_ref[...] * lk1_ref[...]))
           - jnp.exp(jnp.sum(lq2_ref[...] * lk2_ref[...])) + LAMBDA_INIT)
    l1 = jnp.sum(l_ref[0], axis=-1, keepdims=True)
    l2 = jnp.sum(l_ref[1], axis=-1, keepdims=True)
    out = acc_ref[0] * (1.0 / l1) - lam * (acc_ref[1] * (1.0 / l2))
    ms = jnp.mean(out * out, axis=-1, keepdims=True)
    out = out * lax.rsqrt(ms + RMS_EPS) * g_ref[...] * (1.0 - LAMBDA_INIT)
    o_ref[0] = out.astype(o_ref.dtype)


def _diff_attn(proj, slopes, lq1, lk1, lq2, lk2, diagb, g):
    b, s, _ = proj.shape
    nq = s // TQ
    lam_spec = pl.BlockSpec((1, HEAD_DIM), lambda bi, hi, qi: (0, 0))
    return pl.pallas_call(
        _diff_attn_kernel,
        grid=(b, N_DIFF_HEADS, nq),
        in_specs=[
            pl.BlockSpec(memory_space=pltpu.SMEM),
            lam_spec, lam_spec, lam_spec, lam_spec,
            pl.BlockSpec((1, TQ, LANES), lambda bi, hi, qi: (bi, qi, QD_BLK + hi)),
            pl.BlockSpec((1, s, LANES), lambda bi, hi, qi: (bi, 0, KD_BLK + hi)),
            pl.BlockSpec((1, s, LANES), lambda bi, hi, qi: (bi, 0, VD_BLK + hi)),
            pl.BlockSpec((1, TQ, TQ), lambda bi, hi, qi: (hi, 0, 0)),
            pl.BlockSpec((1, DIFF_V_DIM), lambda bi, hi, qi: (0, 0)),
        ],
        out_specs=pl.BlockSpec((1, TQ, LANES), lambda bi, hi, qi: (bi, qi, hi)),
        out_shape=jax.ShapeDtypeStruct((b, s, N_DIFF_HEADS * DIFF_V_DIM), BF16),
        scratch_shapes=[pltpu.VMEM((2, TQ, LANES), F32), pltpu.VMEM((2, TQ, LANES), F32),
                        pltpu.VMEM((2, TQ, LANES), F32), pltpu.VMEM((2, 2, TQ, TQ), F32),
                        pltpu.VMEM((2, 2, TQ, LANES), F32)],
        compiler_params=pltpu.CompilerParams(
            dimension_semantics=("arbitrary", "arbitrary", "arbitrary"),
            vmem_limit_bytes=VMEM_LIMIT),
        name="diff_attn",
    )(slopes, lq1, lk1, lq2, lk2, proj, proj, proj, diagb, g)


def _sb_attn_kernel(q_ref, k_ref, v_ref, tri_ref, g_ref, o_ref, acc_ref):
    qi = pl.program_id(2)
    lane = lax.broadcasted_iota(jnp.int32, (TQ, LANES), 1)
    lo_half = lane < HEAD_DIM
    qf = q_ref[0].astype(F32) * (HEAD_DIM ** -0.5 * LOG2E)
    qm = (jnp.where(lo_half, qf, 0.0).astype(BF16),
          jnp.where(lo_half, 0.0, qf).astype(BF16))
    tri = tri_ref[...]

    row = lax.broadcasted_iota(jnp.int32, (TQ, SB_W), 0)
    colk = lax.broadcasted_iota(jnp.int32, (TQ, SB_W), 1)
    order = [(hh, u) for hh in range(2) for u in reversed(range(SB_U))]

    def query_block(nb):
        starts = [nb * TQ] + [j * TQ for j in reversed(range(nb))]
        strict = {u: (colk + u * SB_W) < row for u in range(SB_U)}

        def scores(i):
            kb = k_ref[0, starts[i]:starts[i] + TQ, :]
            return [_dot_nt(qm[hh], kb) for hh in range(2)]

        def softplus_split(i, zs):
            z, x = {}, {}
            for key in order:
                hh, u = key
                zc = zs[hh][:, u * SB_W:(u + 1) * SB_W]
                sp = jnp.maximum(zc, 0.0) + jnp.log(1.0 + jnp.exp2(-jnp.abs(zc))) * LOG2E
                if i == 0:
                    sp = jnp.where(strict[u], sp, 0.0)
                hi = sp.astype(BF16)
                lo = (sp - hi.astype(F32)).astype(BF16)
                z[key] = zc
                x[key] = jnp.concatenate([hi, lo], axis=1)
            return z, x

        def suffix_sums(x):
            return {key: _dot(x[key], tri) for key in order}

        def weights_pv(i, z, r, carry, acc):
            vb = v_ref[0, starts[i]:starts[i] + TQ, :]
            for hh in range(2):
                a = {}
                for u in reversed(range(SB_U)):
                    key = (hh, u)
                    av = jnp.exp2(z[key] - r[key][:, :SB_W] - carry[hh])
                    if i == 0:
                        av = jnp.where(strict[u], av, 0.0)
                    a[u] = av.astype(BF16)
                    carry[hh] = carry[hh] + r[key][:, SB_W:]
                acc[hh] = acc[hh] + _dot(jnp.concatenate([a[u] for u in range(SB_U)], axis=1), vb)

        carry = [jnp.zeros((TQ, SB_W), F32), jnp.zeros((TQ, SB_W), F32)]
        acc = [jnp.zeros((TQ, LANES), F32), jnp.zeros((TQ, LANES), F32)]
        z, x = softplus_split(0, scores(0))
        for i in range(nb + 1):
            if i < nb:
                zs_next = scores(i + 1)
            r = suffix_sums(x)
            if i < nb:
                z_next, x_next = softplus_split(i + 1, zs_next)
            weights_pv(i, z, r, carry, acc)
            if i < nb:
                z, x = z_next, x_next
        acc_ref[0] = acc[0]
        acc_ref[1] = acc[1]

    for nb in range(NQ):
        pl.when(qi == nb)(lambda nb=nb: query_block(nb))

    out = jnp.where(lo_half, acc_ref[0], acc_ref[1])
    sq = out * out
    ss_lo = jnp.sum(jnp.where(lo_half, sq, 0.0), axis=-1, keepdims=True)
    ss_hi = jnp.sum(jnp.where(lo_half, 0.0, sq), axis=-1, keepdims=True)
    ms = jnp.where(lo_half, ss_lo, ss_hi) * (1.0 / HEAD_DIM)
    o_ref[0] = (out * lax.rsqrt(ms + RMS_EPS) * g_ref[0]).astype(o_ref.dtype)


def _sb_attn(proj, tri, g):
    b, s, _ = proj.shape
    nq = s // TQ
    npair = N_SB_HEADS // 2
    return pl.pallas_call(
        _sb_attn_kernel,
        grid=(b, npair, nq),
        in_specs=[
            pl.BlockSpec((1, TQ, LANES), lambda bi, pi, qi: (bi, qi, QS_BLK + pi)),
            pl.BlockSpec((1, s, LANES), lambda bi, pi, qi: (bi, 0, KS_BLK + pi)),
            pl.BlockSpec((1, s, LANES), lambda bi, pi, qi: (bi, 0, VS_BLK + pi)),
            pl.BlockSpec((2 * SB_W, 2 * SB_W), lambda bi, pi, qi: (0, 0)),
            pl.BlockSpec((1, 1, LANES), lambda bi, pi, qi: (pi, 0, 0)),
        ],
        out_specs=pl.BlockSpec((1, TQ, LANES), lambda bi, pi, qi: (bi, qi, pi)),
        out_shape=jax.ShapeDtypeStruct((b, s, N_SB_HEADS * HEAD_DIM), BF16),
        scratch_shapes=[pltpu.VMEM((2, TQ, LANES), F32)],
        compiler_params=pltpu.CompilerParams(
            dimension_semantics=("arbitrary", "arbitrary", "arbitrary"),
            vmem_limit_bytes=VMEM_LIMIT),
        name="sb_attn",
    )(proj, proj, proj, tri, g)


def _xkv_kernel(mem_ref, w_ref, o_ref):
    o_ref[0] = _dot(mem_ref[0].astype(BF16), w_ref[...]).astype(o_ref.dtype)


def _xkv_proj(mem, w):
    b, n, d = mem.shape
    return pl.pallas_call(
        _xkv_kernel,
        grid=(b,),
        in_specs=[pl.BlockSpec((1, n, d), lambda i: (i, 0, 0)),
                  pl.BlockSpec((d, 2 * d), lambda i: (0, 0))],
        out_specs=pl.BlockSpec((1, n, 2 * d), lambda i: (i, 0, 0)),
        out_shape=jax.ShapeDtypeStruct((b, n, 2 * d), BF16),
        compiler_params=pltpu.CompilerParams(dimension_semantics=("arbitrary",),
                                             vmem_limit_bytes=VMEM_LIMIT),
        name="xkv_proj",
    )(mem, w)


def _tail_kernel(yd_ref, ys_ref, x_ref, kv_ref, wod_ref, wos_ref, g1_ref, b1_ref, wq_ref, wxo_ref,
                 g2_ref, b2_ref, wg_ref, wu_ref, wd_ref, g3_ref, b3_ref, o_ref):
    subs = [slice(t * TM_ROW, (t + 1) * TM_ROW) for t in range(N_SUB)]
    hcols = [slice(h * XHEAD_DIM, (h + 1) * XHEAD_DIM) for h in range(N_XHEADS)]

    mix = [_dot(yd_ref[r], wod_ref[...]) + _dot(ys_ref[r], wos_ref[...]) for r in subs]
    x1 = [_layer_norm(ALPHA * x_ref[r] + m, g1_ref[...], b1_ref[...]) for r, m in zip(subs, mix)]

    q = [(_dot(v.astype(BF16), wq_ref[...]) * (XHEAD_DIM ** -0.5)).astype(BF16) for v in x1]
    sc = [[_dot_nt(qt[:, hc], kv_ref[0, :, hc]) for hc in hcols] for qt in q]
    heads = []
    for st in sc:
        ps = [jnp.exp(sh - jnp.max(sh, axis=-1, keepdims=True)) for sh in st]
        inv = [1.0 / jnp.sum(p, axis=-1, keepdims=True) for p in ps]
        pv = [_dot(p.astype(BF16), kv_ref[0, :, D_MODEL + h * XHEAD_DIM:D_MODEL + (h + 1) * XHEAD_DIM])
              for h, p in enumerate(ps)]
        heads.append(jnp.concatenate([(o * i).astype(BF16) for o, i in zip(pv, inv)], axis=1))
    x2 = [_layer_norm(ALPHA * v + _dot(o, wxo_ref[...]), g2_ref[...], b2_ref[...])
          for v, o in zip(x1, heads)]

    x2b = [v.astype(BF16) for v in x2]
    gate = [_dot(v, wg_ref[...]) for v in x2b]
    up = [_dot(v, wu_ref[...]) for v in x2b]
    hidden = [(g * jax.nn.sigmoid(g) * u).astype(BF16) for g, u in zip(gate, up)]
    for r, v, hd in zip(subs, x2, hidden):
        o_ref[r] = _layer_norm(ALPHA * v + _dot(hd, wd_ref[...]), g3_ref[...], b3_ref[...])


def _tail(yd, ys, x2d, kv, w_od, w_os, g1, b1, wq, wxo, g2, b2, wg, wu, wd, g3, b3, seq):
    t, d = x2d.shape
    half = yd.shape[1]
    n = kv.shape[1]
    f = wg.shape[1]
    tm = N_SUB * TM_ROW
    per_batch = seq // tm
    row = lambda i: (i, 0)
    const = lambda i: (0, 0)
    resident = lambda shape: pl.BlockSpec(shape, const, pipeline_mode=pl.Buffered(1))
    return pl.pallas_call(
        _tail_kernel,
        grid=(t // tm,),
        in_specs=[pl.BlockSpec((tm, half), row), pl.BlockSpec((tm, half), row),
                  pl.BlockSpec((tm, d), row),
                  pl.BlockSpec((1, n, 2 * d), lambda i: (i // per_batch, 0, 0)),
                  resident((half, d)), resident((half, d)), resident((1, d)), resident((1, d)),
                  resident((d, d)), resident((d, d)), resident((1, d)), resident((1, d)),
                  resident((d, f)), resident((d, f)), resident((f, d)),
                  resident((1, d)), resident((1, d))],
        out_specs=pl.BlockSpec((tm, d), row),
        out_shape=jax.ShapeDtypeStruct((t, d), F32),
        compiler_params=pltpu.CompilerParams(dimension_semantics=("arbitrary",),
                                             vmem_limit_bytes=VMEM_LIMIT),
        name="tail",
    )(yd, ys, x2d, kv, w_od, w_os, g1, b1, wq, wxo, g2, b2, wg, wu, wd, g3, b3)


def _alibi_diag_table(slopes):
    i = jnp.arange(TQ, dtype=jnp.int32)[:, None]
    j = jnp.arange(TQ, dtype=jnp.int32)[None, :]
    dist = jnp.abs(i - j).astype(F32)
    allowed = (j // CHUNK) <= (i // CHUNK)
    return jnp.where(allowed[None], -(slopes * LOG2E)[:, None, None] * dist[None], NEG_INF)


def _suffix_sum_matrix():
    j = jnp.arange(2 * SB_W, dtype=jnp.int32)[:, None] % SB_W
    c = jnp.arange(2 * SB_W, dtype=jnp.int32)[None, :]
    return jnp.where((c >= SB_W) | (j >= c), 1.0, 0.0).astype(BF16)


def kernel(x, mem, w_in, diff_lambda_q1, diff_lambda_k1, diff_lambda_q2, diff_lambda_k2,
           diff_subln_g, sb_norm_g, w_o, ln1_g, ln1_b, w_xq, w_xkv, w_xo, ln2_g, ln2_b,
           w_gate, w_up, w_down, ln3_g, ln3_b):
    b, s, d = x.shape
    assert (b, s, d) == (8, SEQ, D_MODEL) and w_in.shape == (DEPTH, D_MODEL, D_IN)
    x2d = x.reshape(b * s, d)
    slopes = jnp.exp2(-8.0 * jnp.arange(1, N_DIFF_HEADS + 1, dtype=F32) / N_DIFF_HEADS)
    diagb = _alibi_diag_table(slopes)
    tri = _suffix_sum_matrix()
    vec = lambda a: a[0].reshape(1, -1)

    proj = _in_proj(x2d, w_in[0].astype(BF16)).reshape(b, s, D_IN)
    y_diff = _diff_attn(proj, slopes, vec(diff_lambda_q1), vec(diff_lambda_k1),
                        vec(diff_lambda_q2), vec(diff_lambda_k2), diagb, vec(diff_subln_g))
    y_sb = _sb_attn(proj, tri, sb_norm_g[0].reshape(N_SB_HEADS // 2, 1, LANES))

    w_o_b = w_o[0].astype(BF16)
    half = N_DIFF_HEADS * DIFF_V_DIM
    kv = _xkv_proj(mem, w_xkv[0].astype(BF16))
    out = _tail(y_diff.reshape(b * s, half), y_sb.reshape(b * s, -1), x2d, kv,
                w_o_b[:half], w_o_b[half:], vec(ln1_g), vec(ln1_b),
                w_xq[0].astype(BF16), w_xo[0].astype(BF16), vec(ln2_g), vec(ln2_b),
                w_gate[0].astype(BF16), w_up[0].astype(BF16), w_down[0].astype(BF16),
                vec(ln3_g), vec(ln3_b), s)
    return out.reshape(b, s, d)
```

```python
import math

import jax
import jax.numpy as jnp
from jax import lax
from jax.experimental import pallas as pl
from jax.experimental.pallas import tpu as pltpu

D_MODEL = 1024
DEPTH = 1
CHUNK = 64
N_MEM = 256
HEAD_DIM = 64
N_DIFF_HEADS = 4
DIFF_V_DIM = 2 * HEAD_DIM
N_SB_HEADS = 8
N_XHEADS = 4
XHEAD_DIM = D_MODEL // N_XHEADS
D_FF = 2816
ALPHA = (2.0 * DEPTH) ** 0.25
LN_EPS = 1e-5
RMS_EPS = 1e-5
NEG_INF = -1e30
LAMBDA_INIT = 0.8 - 0.6 * math.exp(-0.3 * 0)
LOG2E = math.log2(math.e)

LANES = 128
QD_BLK, KD_BLK, VD_BLK, QS_BLK, KS_BLK, VS_BLK = 0, 4, 8, 12, 16, 20
D_IN = 24 * LANES

TM_PROJ = 512
TM_ROW = 256
N_SUB = 2
TQ = 512
SEQ = 2048
NQ = SEQ // TQ
SB_W = 128
SB_U = TQ // SB_W
VMEM_LIMIT = 56 * 1024 * 1024

BF16 = jnp.bfloat16
F32 = jnp.float32


def _dot(a, b):
    return jnp.dot(a, b, preferred_element_type=F32)


def _dot_nt(a, b):
    return lax.dot_general(a, b, (((1,), (1,)), ((), ())), preferred_element_type=F32)


def _layer_norm(v, g, b):
    mu = jnp.mean(v, axis=-1, keepdims=True)
    d = v - mu
    var = jnp.mean(d * d, axis=-1, keepdims=True)
    return d * lax.rsqrt(var + LN_EPS) * g + b


def _in_proj_kernel(x_ref, w_ref, o_ref):
    o_ref[...] = _dot(x_ref[...], w_ref[...]).astype(o_ref.dtype)


def _in_proj(x2d, w):
    t, d = x2d.shape
    n = w.shape[1]
    return pl.pallas_call(
        _in_proj_kernel,
        grid=(t // TM_PROJ,),
        in_specs=[pl.BlockSpec((TM_PROJ, d), lambda i: (i, 0)),
                  pl.BlockSpec((d, n), lambda i: (0, 0), pipeline_mode=pl.Buffered(1))],
        out_specs=pl.BlockSpec((TM_PROJ, n), lambda i: (i, 0)),
        out_shape=jax.ShapeDtypeStruct((t, n), BF16),
        compiler_params=pltpu.CompilerParams(dimension_semantics=("arbitrary",),
                                             vmem_limit_bytes=VMEM_LIMIT),
        name="in_proj",
    )(x2d, w)


def _diff_attn_kernel(slopes_ref, lq1_ref, lk1_ref, lq2_ref, lk2_ref, q_ref, k_ref, v_ref,
                      diagb_ref, g_ref, o_ref, m_ref, l_ref, acc_ref, s_ref, mb_ref):
    h = pl.program_id(1)
    qi = pl.program_id(2)
    slope2 = slopes_ref[h] * LOG2E

    lane = lax.broadcasted_iota(jnp.int32, (TQ, LANES), 1)
    qf = q_ref[0].astype(F32) * (HEAD_DIM ** -0.5 * LOG2E)
    qm = (jnp.where(lane < HEAD_DIM, qf, 0.0).astype(BF16),
          jnp.where(lane >= HEAD_DIM, qf, 0.0).astype(BF16))

    m_ref[...] = jnp.full_like(m_ref, NEG_INF)
    l_ref[...] = jnp.zeros_like(l_ref)
    acc_ref[...] = jnp.zeros_like(acc_ref)

    col = lax.broadcasted_iota(jnp.int32, (1, TQ), 1).astype(F32)
    row = lax.broadcasted_iota(jnp.int32, (TQ, LANES), 0).astype(F32)
    col_bias = slope2 * col

    def scores(slot, kstart, bias):
        kb = k_ref[0, kstart:kstart + TQ, :]
        for mp in range(2):
            s = _dot_nt(qm[mp], kb) + bias
            s_ref[slot, mp] = s
            mb_ref[slot, mp] = jnp.broadcast_to(jnp.max(s, axis=-1, keepdims=True), (TQ, LANES))

    def softmax_pv(slot, kstart, row_shift):
        vb = v_ref[0, kstart:kstart + TQ, :]
        for mp in range(2):
            m_old = m_ref[mp]
            m_new = jnp.maximum(m_old, mb_ref[slot, mp] + row_shift)
            alpha = jnp.exp2(m_old - m_new)
            mrel = m_new - row_shift
            ps = [jnp.exp2(s_ref[slot, mp, :, c * LANES:(c + 1) * LANES] - mrel)
                  for c in range(TQ // LANES)]
            psum = ps[0]
            for pc in ps[1:]:
                psum = psum + pc
            l_ref[mp] = alpha * l_ref[mp] + psum
            p = jnp.concatenate([pc.astype(BF16) for pc in ps], axis=1)
            acc_ref[mp] = alpha * acc_ref[mp] + _dot(p, vb)
            m_ref[mp] = m_new

    def query_block(nb):
        def issue(i):
            if i < nb:
                scores(i % 2, i * TQ, col_bias)
            else:
                scores(i % 2, nb * TQ, diagb_ref[0])

        issue(0)
        for i in range(nb + 1):
            if i < nb:
                issue(i + 1)
            row_shift = -slope2 * (row + float((nb - i) * TQ)) if i < nb else jnp.zeros_like(row)
            softmax_pv(i % 2, i * TQ, row_shift)

    for nb in range(NQ):
        pl.when(qi == nb)(lambda nb=nb: query_block(nb))

    lam = (jnp.exp(jnp.sum(lq1_ref[...] * lk1_ref[...]))
           - jnp.exp(jnp.sum(lq2_ref[...] * lk2_ref[...])) + LAMBDA_INIT)
    l1 = jnp.sum(l_ref[0], axis=-1, keepdims=True)
    l2 = jnp.sum(l_ref[1], axis=-1, keepdims=True)
    out = acc_ref[0] * (1.0 / l1) - lam * (acc_ref[1] * (1.0 / l2))
    ms = jnp.mean(out * out, axis=-1, keepdims=True)
    out = out * lax.rsqrt(ms + RMS_EPS) * g_ref[...] * (1.0 - LAMBDA_INIT)
    o_ref[0] = out.astype(o_ref.dtype)


def _diff_attn(proj, slopes, lq1, lk1, lq2, lk2, diagb, g):
    b, s, _ = proj.shape
    nq = s // TQ
    lam_spec = pl.BlockSpec((1, HEAD_DIM), lambda bi, hi, qi: (0, 0))
    return pl.pallas_call(
        _diff_attn_kernel,
        grid=(b, N_DIFF_HEADS, nq),
        in_specs=[
            pl.BlockSpec(memory_space=pltpu.SMEM),
            lam_spec, lam_spec, lam_spec, lam_spec,
            pl.BlockSpec((1, TQ, LANES), lambda bi, hi, qi: (bi, qi, QD_BLK + hi)),
            pl.BlockSpec((1, s, LANES), lambda bi, hi, qi: (bi, 0, KD_BLK + hi)),
            pl.BlockSpec((1, s, LANES), lambda bi, hi, qi: (bi, 0, VD_BLK + hi)),
            pl.BlockSpec((1, TQ, TQ), lambda bi, hi, qi: (hi, 0, 0)),
            pl.BlockSpec((1, DIFF_V_DIM), lambda bi, hi, qi: (0, 0)),
        ],
        out_specs=pl.BlockSpec((1, TQ, LANES), lambda bi, hi, qi: (bi, qi, hi)),
        out_shape=jax.ShapeDtypeStruct((b, s, N_DIFF_HEADS * DIFF_V_DIM), BF16),
        scratch_shapes=[pltpu.VMEM((2, TQ, LANES), F32), pltpu.VMEM((2, TQ, LANES), F32),
                        pltpu.VMEM((2, TQ, LANES), F32), pltpu.VMEM((2, 2, TQ, TQ), F32),
                        pltpu.VMEM((2, 2, TQ, LANES), F32)],
        compiler_params=pltpu.CompilerParams(
            dimension_semantics=("arbitrary", "arbitrary", "arbitrary"),
            vmem_limit_bytes=VMEM_LIMIT),
        name="diff_attn",
    )(slopes, lq1, lk1, lq2, lk2, proj, proj, proj, diagb, g)


def _sb_attn_kernel(q_ref, k_ref, v_ref, tri_ref, g_ref, o_ref, acc_ref):
    qi = pl.program_id(2)
    lane = lax.broadcasted_iota(jnp.int32, (TQ, LANES), 1)
    lo_half = lane < HEAD_DIM
    qf = q_ref[0].astype(F32) * (HEAD_DIM ** -0.5 * LOG2E)
    qm = (jnp.where(lo_half, qf, 0.0).astype(BF16),
          jnp.where(lo_half, 0.0, qf).astype(BF16))
    tri = tri_ref[...]

    row = lax.broadcasted_iota(jnp.int32, (TQ, SB_W), 0)
    colk = lax.broadcasted_iota(jnp.int32, (TQ, SB_W), 1)
    order = [(hh, u) for hh in range(2) for u in reversed(range(SB_U))]

    def query_block(nb):
        starts = [nb * TQ] + [j * TQ for j in reversed(range(nb))]
        strict = colk < row

        def scores(i):
            kb = k_ref[0, starts[i]:starts[i] + TQ, :]
            return [_dot_nt(qm[hh], kb) for hh in range(2)]

        def softplus_split(i, zs):
            z, x = {}, {}
            for key in order:
                hh, u = key
                r0 = u * SB_W if i == 0 else 0
                zc = zs[hh][r0:, u * SB_W:(u + 1) * SB_W]
                sp = jnp.maximum(zc, 0.0) + jnp.log(1.0 + jnp.exp2(-jnp.abs(zc))) * LOG2E
                if i == 0:
                    sp = jnp.where(strict[:TQ - r0], sp, 0.0)
                hi = sp.astype(BF16)
                lo = (sp - hi.astype(F32)).astype(BF16)
                z[key] = zc
                x[key] = jnp.concatenate([hi, lo], axis=1)
            return z, x

        def suffix_sums(x):
            return {key: _dot(x[key], tri) for key in order}

        def weights_pv(i, z, r, carry, acc):
            vb = v_ref[0, starts[i]:starts[i] + TQ, :]
            for hh in range(2):
                a = {}
                for u in reversed(range(SB_U)):
                    key = (hh, u)
                    r0 = u * SB_W if i == 0 else 0
                    av = jnp.exp2(z[key] - r[key][:, :SB_W] - carry[hh][r0:])
                    if i == 0:
                        av = jnp.where(strict[:TQ - r0], av, 0.0)
                    avb, tot = av.astype(BF16), r[key][:, SB_W:]
                    if r0:
                        a[u] = jnp.concatenate([jnp.zeros((r0, SB_W), BF16), avb], axis=0)
                        carry[hh] = jnp.concatenate([carry[hh][:r0], carry[hh][r0:] + tot], axis=0)
                    else:
                        a[u], carry[hh] = avb, carry[hh] + tot
                acc[hh] = acc[hh] + _dot(jnp.concatenate([a[u] for u in range(SB_U)], axis=1), vb)

        carry = [jnp.zeros((TQ, SB_W), F32), jnp.zeros((TQ, SB_W), F32)]
        acc = [jnp.zeros((TQ, LANES), F32), jnp.zeros((TQ, LANES), F32)]
        z, x = softplus_split(0, scores(0))
        for i in range(nb + 1):
            if i < nb:
                zs_next = scores(i + 1)
            r = suffix_sums(x)
            if i < nb:
                z_next, x_next = softplus_split(i + 1, zs_next)
            weights_pv(i, z, r, carry, acc)
            if i < nb:
                z, x = z_next, x_next
        acc_ref[0] = acc[0]
        acc_ref[1] = acc[1]

    for nb in range(NQ):
        pl.when(qi == nb)(lambda nb=nb: query_block(nb))

    out = jnp.where(lo_half, acc_ref[0], acc_ref[1])
    sq = out * out
    ss_lo = jnp.sum(jnp.where(lo_half, sq, 0.0), axis=-1, keepdims=True)
    ss_hi = jnp.sum(jnp.where(lo_half, 0.0, sq), axis=-1, keepdims=True)
    ms = jnp.where(lo_half, ss_lo, ss_hi) * (1.0 / HEAD_DIM)
    o_ref[0] = (out * lax.rsqrt(ms + RMS_EPS) * g_ref[0]).astype(o_ref.dtype)


def _sb_attn(proj, tri, g):
    b, s, _ = proj.shape
    nq = s // TQ
    npair = N_SB_HEADS // 2
    return pl.pallas_call(
        _sb_attn_kernel,
        grid=(b, npair, nq),
        in_specs=[
            pl.BlockSpec((1, TQ, LANES), lambda bi, pi, qi: (bi, qi, QS_BLK + pi)),
            pl.BlockSpec((1, s, LANES), lambda bi, pi, qi: (bi, 0, KS_BLK + pi)),
            pl.BlockSpec((1, s, LANES), lambda bi, pi, qi: (bi, 0, VS_BLK + pi)),
            pl.BlockSpec((2 * SB_W, 2 * SB_W), lambda bi, pi, qi: (0, 0)),
            pl.BlockSpec((1, 1, LANES), lambda bi, pi, qi: (pi, 0, 0)),
        ],
        out_specs=pl.BlockSpec((1, TQ, LANES), lambda bi, pi, qi: (bi, qi, pi)),
        out_shape=jax.ShapeDtypeStruct((b, s, N_SB_HEADS * HEAD_DIM), BF16),
        scratch_shapes=[pltpu.VMEM((2, TQ, LANES), F32)],
        compiler_params=pltpu.CompilerParams(
            dimension_semantics=("arbitrary", "arbitrary", "arbitrary"),
            vmem_limit_bytes=VMEM_LIMIT),
        name="sb_attn",
    )(proj, proj, proj, tri, g)


def _xkv_kernel(mem_ref, w_ref, o_ref):
    o_ref[0] = _dot(mem_ref[0], w_ref[...]).astype(o_ref.dtype)


def _xkv_proj(mem, w):
    b, n, d = mem.shape
    return pl.pallas_call(
        _xkv_kernel,
        grid=(b,),
        in_specs=[pl.BlockSpec((1, n, d), lambda i: (i, 0, 0)),
                  pl.BlockSpec((d, 2 * d), lambda i: (0, 0), pipeline_mode=pl.Buffered(1))],
        out_specs=pl.BlockSpec((1, n, 2 * d), lambda i: (i, 0, 0)),
        out_shape=jax.ShapeDtypeStruct((b, n, 2 * d), BF16),
        compiler_params=pltpu.CompilerParams(dimension_semantics=("arbitrary",),
                                             vmem_limit_bytes=VMEM_LIMIT),
        name="xkv_proj",
    )(mem, w)


def _tail_kernel(yd_ref, ys_ref, x_ref, kv_ref, wod_ref, wos_ref, g1_ref, b1_ref, wq_ref, wxo_ref,
                 g2_ref, b2_ref, wg_ref, wu_ref, wd_ref, g3_ref, b3_ref, o_ref):
    subs = [slice(t * TM_ROW, (t + 1) * TM_ROW) for t in range(N_SUB)]
    hcols = [slice(h * XHEAD_DIM, (h + 1) * XHEAD_DIM) for h in range(N_XHEADS)]

    mix = [_dot(yd_ref[r], wod_ref[...]) + _dot(ys_ref[r], wos_ref[...]) for r in subs]
    x1 = [_layer_norm(ALPHA * x_ref[r] + m, g1_ref[...], b1_ref[...]) for r, m in zip(subs, mix)]

    q = [(_dot(v, wq_ref[...]) * (XHEAD_DIM ** -0.5)).astype(BF16) for v in x1]
    sc = [[_dot_nt(qt[:, hc], kv_ref[0, :, hc]) for hc in hcols] for qt in q]
    heads = []
    for st in sc:
        ps = [jnp.exp(sh - jnp.max(sh, axis=-1, keepdims=True)) for sh in st]
        inv = [1.0 / jnp.sum(p, axis=-1, keepdims=True) for p in ps]
        pv = [_dot(p.astype(BF16), kv_ref[0, :, D_MODEL + h * XHEAD_DIM:D_MODEL + (h + 1) * XHEAD_DIM])
              for h, p in enumerate(ps)]
        heads.append(jnp.concatenate([o * i for o, i in zip(pv, inv)], axis=1))
    x2 = [_layer_norm(ALPHA * v + _dot(o, wxo_ref[...]), g2_ref[...], b2_ref[...])
          for v, o in zip(x1, heads)]

    x2b = [v.astype(BF16) for v in x2]
    gate = [_dot(v, wg_ref[...]) for v in x2b]
    up = [_dot(v, wu_ref[...]) for v in x2b]
    hidden = [(g * jax.nn.sigmoid(g) * u).astype(BF16) for g, u in zip(gate, up)]
    for r, v, hd in zip(subs, x2, hidden):
        o_ref[r] = _layer_norm(ALPHA * v + _dot(hd, wd_ref[...]), g3_ref[...], b3_ref[...])


def _tail(yd, ys, x2d, kv, w_od, w_os, g1, b1, wq, wxo, g2, b2, wg, wu, wd, g3, b3, seq):
    t, d = x2d.shape
    half = yd.shape[1]
    n = kv.shape[1]
    f = wg.shape[1]
    tm = N_SUB * TM_ROW
    per_batch = seq // tm
    row = lambda i: (i, 0)
    const = lambda i: (0, 0)
    resident = lambda shape: pl.BlockSpec(shape, const, pipeline_mode=pl.Buffered(1))
    return pl.pallas_call(
        _tail_kernel,
        grid=(t // tm,),
        in_specs=[pl.BlockSpec((tm, half), row), pl.BlockSpec((tm, half), row),
                  pl.BlockSpec((tm, d), row),
                  pl.BlockSpec((1, n, 2 * d), lambda i: (i // per_batch, 0, 0)),
                  resident((half, d)), resident((half, d)), resident((1, d)), resident((1, d)),
                  resident((d, d)), resident((d, d)), resident((1, d)), resident((1, d)),
                  resident((d, f)), resident((d, f)), resident((f, d)),
                  resident((1, d)), resident((1, d))],
        out_specs=pl.BlockSpec((tm, d), row),
        out_shape=jax.ShapeDtypeStruct((t, d), F32),
        compiler_params=pltpu.CompilerParams(dimension_semantics=("arbitrary",),
                                             vmem_limit_bytes=VMEM_LIMIT),
        name="tail",
    )(yd, ys, x2d, kv, w_od, w_os, g1, b1, wq, wxo, g2, b2, wg, wu, wd, g3, b3)


def _alibi_diag_table(slopes):
    i = jnp.arange(TQ, dtype=jnp.int32)[:, None]
    j = jnp.arange(TQ, dtype=jnp.int32)[None, :]
    dist = jnp.abs(i - j).astype(F32)
    allowed = (j // CHUNK) <= (i // CHUNK)
    return jnp.where(allowed[None], -(slopes * LOG2E)[:, None, None] * dist[None], NEG_INF)


def _suffix_sum_matrix():
    j = jnp.arange(2 * SB_W, dtype=jnp.int32)[:, None] % SB_W
    c = jnp.arange(2 * SB_W, dtype=jnp.int32)[None, :]
    return jnp.where((c >= SB_W) | (j >= c), 1.0, 0.0).astype(BF16)


def kernel(x, mem, w_in, diff_lambda_q1, diff_lambda_k1, diff_lambda_q2, diff_lambda_k2,
           diff_subln_g, sb_norm_g, w_o, ln1_g, ln1_b, w_xq, w_xkv, w_xo, ln2_g, ln2_b,
           w_gate, w_up, w_down, ln3_g, ln3_b):
    b, s, d = x.shape
    assert (b, s, d) == (8, SEQ, D_MODEL) and w_in.shape == (DEPTH, D_MODEL, D_IN)
    x2d = x.reshape(b * s, d)
    slopes = jnp.exp2(-8.0 * jnp.arange(1, N_DIFF_HEADS + 1, dtype=F32) / N_DIFF_HEADS)
    diagb = _alibi_diag_table(slopes)
    tri = _suffix_sum_matrix()
    vec = lambda a: a[0].reshape(1, -1)

    proj = _in_proj(x2d, w_in[0]).reshape(b, s, D_IN)
    y_diff = _diff_attn(proj, slopes, vec(diff_lambda_q1), vec(diff_lambda_k1),
                        vec(diff_lambda_q2), vec(diff_lambda_k2), diagb, vec(diff_subln_g))
    y_sb = _sb_attn(proj, tri, sb_norm_g[0].reshape(N_SB_HEADS // 2, 1, LANES))

    half = N_DIFF_HEADS * DIFF_V_DIM
    kv = _xkv_proj(mem, w_xkv[0])
    out = _tail(y_diff.reshape(b * s, half), y_sb.reshape(b * s, -1), x2d, kv,
                w_o[0, :half], w_o[0, half:], vec(ln1_g), vec(ln1_b),
                w_xq[0], w_xo[0], vec(ln2_g), vec(ln2_b),
                w_gate[0].astype(BF16), w_up[0].astype(BF16), w_down[0].astype(BF16),
                vec(ln3_g), vec(ln3_b), s)
    return out.reshape(b, s, d)
```

```python
import math

import jax
import jax.numpy as jnp
from jax import lax
from jax.experimental import pallas as pl
from jax.experimental.pallas import tpu as pltpu

D_MODEL = 1024
DEPTH = 1
CHUNK = 64
N_MEM = 256
HEAD_DIM = 64
N_DIFF_HEADS = 4
DIFF_V_DIM = 2 * HEAD_DIM
N_SB_HEADS = 8
N_XHEADS = 4
XHEAD_DIM = D_MODEL // N_XHEADS
D_FF = 2816
ALPHA = (2.0 * DEPTH) ** 0.25
LN_EPS = 1e-5
RMS_EPS = 1e-5
NEG_INF = -1e30
LAMBDA_INIT = 0.8 - 0.6 * math.exp(-0.3 * 0)
LOG2E = math.log2(math.e)

LANES = 128
QD_BLK, KD_BLK, VD_BLK, QS_BLK, KS_BLK, VS_BLK = 0, 4, 8, 12, 16, 20
D_IN = 24 * LANES

TM_PROJ = 512
TM_ROW = 256
N_SUB = 2
TQ = 512
SEQ = 2048
NQ = SEQ // TQ
SB_W = 128
SB_U = TQ // SB_W
SB_NEAR = 256
SB_EXIT = 150.0
VMEM_LIMIT = 56 * 1024 * 1024

BF16 = jnp.bfloat16
F32 = jnp.float32


def _dot(a, b):
    return jnp.dot(a, b, preferred_element_type=F32)


def _dot_nt(a, b):
    return lax.dot_general(a, b, (((1,), (1,)), ((), ())), preferred_element_type=F32)


def _layer_norm(v, g, b):
    mu = jnp.mean(v, axis=-1, keepdims=True)
    d = v - mu
    var = jnp.mean(d * d, axis=-1, keepdims=True)
    return d * lax.rsqrt(var + LN_EPS) * g + b


def _in_proj_kernel(x_ref, w_ref, o_ref):
    o_ref[...] = _dot(x_ref[...], w_ref[...]).astype(o_ref.dtype)


def _in_proj(x2d, w):
    t, d = x2d.shape
    n = w.shape[1]
    return pl.pallas_call(
        _in_proj_kernel,
        grid=(t // TM_PROJ,),
        in_specs=[pl.BlockSpec((TM_PROJ, d), lambda i: (i, 0)),
                  pl.BlockSpec((d, n), lambda i: (0, 0), pipeline_mode=pl.Buffered(1))],
        out_specs=pl.BlockSpec((TM_PROJ, n), lambda i: (i, 0)),
        out_shape=jax.ShapeDtypeStruct((t, n), BF16),
        compiler_params=pltpu.CompilerParams(dimension_semantics=("arbitrary",),
                                             vmem_limit_bytes=VMEM_LIMIT),
        name="in_proj",
    )(x2d, w)


def _diff_attn_kernel(slopes_ref, lq1_ref, lk1_ref, lq2_ref, lk2_ref, q_ref, k_ref, v_ref,
                      diagb_ref, g_ref, o_ref, m_ref, l_ref, acc_ref, s_ref, mb_ref):
    h = pl.program_id(1)
    qi = pl.program_id(2)
    slope2 = slopes_ref[h] * LOG2E

    lane = lax.broadcasted_iota(jnp.int32, (TQ, LANES), 1)
    qf = q_ref[0].astype(F32) * (HEAD_DIM ** -0.5 * LOG2E)
    qm = (jnp.where(lane < HEAD_DIM, qf, 0.0).astype(BF16),
          jnp.where(lane >= HEAD_DIM, qf, 0.0).astype(BF16))

    m_ref[...] = jnp.full_like(m_ref, NEG_INF)
    l_ref[...] = jnp.zeros_like(l_ref)
    acc_ref[...] = jnp.zeros_like(acc_ref)

    col = lax.broadcasted_iota(jnp.int32, (1, TQ), 1).astype(F32)
    row = lax.broadcasted_iota(jnp.int32, (TQ, LANES), 0).astype(F32)
    col_bias = slope2 * col

    def scores(slot, kstart, bias):
        kb = k_ref[0, kstart:kstart + TQ, :]
        for mp in range(2):
            s = _dot_nt(qm[mp], kb) + bias
            s_ref[slot, mp] = s
            mb_ref[slot, mp] = jnp.broadcast_to(jnp.max(s, axis=-1, keepdims=True), (TQ, LANES))

    def softmax_pv(slot, kstart, row_shift):
        vb = v_ref[0, kstart:kstart + TQ, :]
        for mp in range(2):
            m_old = m_ref[mp]
            m_new = jnp.maximum(m_old, mb_ref[slot, mp] + row_shift)
            alpha = jnp.exp2(m_old - m_new)
            mrel = m_new - row_shift
            ps = [jnp.exp2(s_ref[slot, mp, :, c * LANES:(c + 1) * LANES] - mrel)
                  for c in range(TQ // LANES)]
            psum = ps[0]
            for pc in ps[1:]:
                psum = psum + pc
            l_ref[mp] = alpha * l_ref[mp] + psum
            p = jnp.concatenate([pc.astype(BF16) for pc in ps], axis=1)
            acc_ref[mp] = alpha * acc_ref[mp] + _dot(p, vb)
            m_ref[mp] = m_new

    def query_block(nb):
        def issue(i):
            if i < nb:
                scores(i % 2, i * TQ, col_bias)
            else:
                scores(i % 2, nb * TQ, diagb_ref[0])

        issue(0)
        for i in range(nb + 1):
            if i < nb:
                issue(i + 1)
            row_shift = -slope2 * (row + float((nb - i) * TQ)) if i < nb else jnp.zeros_like(row)
            softmax_pv(i % 2, i * TQ, row_shift)

    for nb in range(NQ):
        pl.when(qi == nb)(lambda nb=nb: query_block(nb))

    lam = (jnp.exp(jnp.sum(lq1_ref[...] * lk1_ref[...]))
           - jnp.exp(jnp.sum(lq2_ref[...] * lk2_ref[...])) + LAMBDA_INIT)
    l1 = jnp.sum(l_ref[0], axis=-1, keepdims=True)
    l2 = jnp.sum(l_ref[1], axis=-1, keepdims=True)
    out = acc_ref[0] * (1.0 / l1) - lam * (acc_ref[1] * (1.0 / l2))
    ms = jnp.mean(out * out, axis=-1, keepdims=True)
    out = out * lax.rsqrt(ms + RMS_EPS) * g_ref[...] * (1.0 - LAMBDA_INIT)
    o_ref[0] = out.astype(o_ref.dtype)


def _diff_attn(proj, slopes, lq1, lk1, lq2, lk2, diagb, g):
    b, s, _ = proj.shape
    nq = s // TQ
    lam_spec = pl.BlockSpec((1, HEAD_DIM), lambda bi, hi, qi: (0, 0))
    return pl.pallas_call(
        _diff_attn_kernel,
        grid=(b, N_DIFF_HEADS, nq),
        in_specs=[
            pl.BlockSpec(memory_space=pltpu.SMEM),
            lam_spec, lam_spec, lam_spec, lam_spec,
            pl.BlockSpec((1, TQ, LANES), lambda bi, hi, qi: (bi, qi, QD_BLK + hi)),
            pl.BlockSpec((1, s, LANES), lambda bi, hi, qi: (bi, 0, KD_BLK + hi)),
            pl.BlockSpec((1, s, LANES), lambda bi, hi, qi: (bi, 0, VD_BLK + hi)),
            pl.BlockSpec((1, TQ, TQ), lambda bi, hi, qi: (hi, 0, 0)),
            pl.BlockSpec((1, DIFF_V_DIM), lambda bi, hi, qi: (0, 0)),
        ],
        out_specs=pl.BlockSpec((1, TQ, LANES), lambda bi, hi, qi: (bi, qi, hi)),
        out_shape=jax.ShapeDtypeStruct((b, s, N_DIFF_HEADS * DIFF_V_DIM), BF16),
        scratch_shapes=[pltpu.VMEM((2, TQ, LANES), F32), pltpu.VMEM((2, TQ, LANES), F32),
                        pltpu.VMEM((2, TQ, LANES), F32), pltpu.VMEM((2, 2, TQ, TQ), F32),
                        pltpu.VMEM((2, 2, TQ, LANES), F32)],
        compiler_params=pltpu.CompilerParams(
            dimension_semantics=("arbitrary", "arbitrary", "arbitrary"),
            vmem_limit_bytes=VMEM_LIMIT),
        name="diff_attn",
    )(slopes, lq1, lk1, lq2, lk2, proj, proj, proj, diagb, g)


def _sb_attn_kernel(q_ref, k_ref, v_ref, tri_ref, g_ref, o_ref, acc_ref, carry_ref, reach_ref):
    qi = pl.program_id(2)
    lane = lax.broadcasted_iota(jnp.int32, (TQ, LANES), 1)
    lo_half = lane < HEAD_DIM
    qf = q_ref[0].astype(F32) * (HEAD_DIM ** -0.5 * LOG2E)
    qm = (jnp.where(lo_half, qf, 0.0).astype(BF16),
          jnp.where(lo_half, 0.0, qf).astype(BF16))
    tri = tri_ref[...]

    row = lax.broadcasted_iota(jnp.int32, (TQ, SB_W), 0)
    colk = lax.broadcasted_iota(jnp.int32, (TQ, SB_W), 1)
    strict = colk < row
    t0 = pl.multiple_of(qi * TQ, TQ)

    def scores(kstart, nkeys):
        kb = k_ref[0, pl.ds(kstart, nkeys), :]
        return [_dot_nt(qm[hh], kb) for hh in range(2)]

    def softplus_split(zs, nsub, own):
        z, x = {}, {}
        for hh in range(2):
            for u in reversed(range(nsub)):
                r0 = u * SB_W if own else 0
                zc = zs[hh][r0:, u * SB_W:(u + 1) * SB_W]
                sp = jnp.maximum(zc, 0.0) + jnp.log(1.0 + jnp.exp2(-jnp.abs(zc))) * LOG2E
                if own:
                    sp = jnp.where(strict[:TQ - r0], sp, 0.0)
                hi = sp.astype(BF16)
                lo = (sp - hi.astype(F32)).astype(BF16)
                z[hh, u] = zc
                x[hh, u] = jnp.concatenate([hi, lo], axis=1)
        return z, x

    def suffix_sums(x):
        return {key: _dot(val, tri) for key, val in x.items()}

    def weights_pv(z, r, nsub, own, kstart, carry, acc):
        vb = v_ref[0, pl.ds(kstart, nsub * SB_W), :]
        for hh in range(2):
            a = {}
            for u in reversed(range(nsub)):
                r0 = u * SB_W if own else 0
                av = jnp.exp2(z[hh, u] - r[hh, u][:, :SB_W] - carry[hh][r0:])
                if own:
                    av = jnp.where(strict[:TQ - r0], av, 0.0)
                avb, tot = av.astype(BF16), r[hh, u][:, SB_W:]
                if r0:
                    a[u] = jnp.concatenate([jnp.zeros((r0, SB_W), BF16), avb], axis=0)
                    carry[hh] = jnp.concatenate([carry[hh][:r0], carry[hh][r0:] + tot], axis=0)
                else:
                    a[u], carry[hh] = avb, carry[hh] + tot
            acc[hh] = acc[hh] + _dot(jnp.concatenate([a[u] for u in range(nsub)], axis=1), vb)

    def near_keys(with_previous):
        carry = [jnp.zeros((TQ, SB_W), F32), jnp.zeros((TQ, SB_W), F32)]
        acc = [jnp.zeros((TQ, LANES), F32), jnp.zeros((TQ, LANES), F32)]
        prev = pl.multiple_of(t0 - SB_NEAR, SB_NEAR)
        z0, x0 = softplus_split(scores(t0, TQ), SB_U, True)
        if with_previous:
            zs1 = scores(prev, SB_NEAR)
        r0 = suffix_sums(x0)
        if with_previous:
            z1, x1 = softplus_split(zs1, SB_NEAR // SB_W, False)
        weights_pv(z0, r0, SB_U, True, t0, carry, acc)
        if with_previous:
            weights_pv(z1, suffix_sums(x1), SB_NEAR // SB_W, False, prev, carry, acc)
        park(carry, acc)

    def park(carry, acc):
        for hh in range(2):
            carry_ref[hh] = carry[hh]
            acc_ref[hh] = acc[hh]
        reach_ref[0] = jnp.min(jnp.minimum(carry[0], carry[1]))

    pl.when(qi == 0)(lambda: near_keys(False))
    pl.when(qi > 0)(lambda: near_keys(True))

    n_far = qi * (TQ // SB_NEAR) - 1

    def far_chunk(c):
        kstart = pl.multiple_of(t0 - (c + 2) * SB_NEAR, SB_NEAR)
        nsub = SB_NEAR // SB_W
        z, x = softplus_split(scores(kstart, SB_NEAR), nsub, False)
        carry = [carry_ref[0], carry_ref[1]]
        acc = [acc_ref[0], acc_ref[1]]
        weights_pv(z, suffix_sums(x), nsub, False, kstart, carry, acc)
        park(carry, acc)
        return c + 1

    lax.while_loop(lambda c: jnp.logical_and(c < n_far, reach_ref[0] < SB_EXIT), far_chunk, 0)

    out = jnp.where(lo_half, acc_ref[0], acc_ref[1])
    sq = out * out
    ss_lo = jnp.sum(jnp.where(lo_half, sq, 0.0), axis=-1, keepdims=True)
    ss_hi = jnp.sum(jnp.where(lo_half, 0.0, sq), axis=-1, keepdims=True)
    ms = jnp.where(lo_half, ss_lo, ss_hi) * (1.0 / HEAD_DIM)
    o_ref[0] = (out * lax.rsqrt(ms + RMS_EPS) * g_ref[0]).astype(o_ref.dtype)


def _sb_attn(proj, tri, g):
    b, s, _ = proj.shape
    nq = s // TQ
    npair = N_SB_HEADS // 2
    return pl.pallas_call(
        _sb_attn_kernel,
        grid=(b, npair, nq),
        in_specs=[
            pl.BlockSpec((1, TQ, LANES), lambda bi, pi, qi: (bi, qi, QS_BLK + pi)),
            pl.BlockSpec((1, s, LANES), lambda bi, pi, qi: (bi, 0, KS_BLK + pi)),
            pl.BlockSpec((1, s, LANES), lambda bi, pi, qi: (bi, 0, VS_BLK + pi)),
            pl.BlockSpec((2 * SB_W, 2 * SB_W), lambda bi, pi, qi: (0, 0)),
            pl.BlockSpec((1, 1, LANES), lambda bi, pi, qi: (pi, 0, 0)),
        ],
        out_specs=pl.BlockSpec((1, TQ, LANES), lambda bi, pi, qi: (bi, qi, pi)),
        out_shape=jax.ShapeDtypeStruct((b, s, N_SB_HEADS * HEAD_DIM), BF16),
        scratch_shapes=[pltpu.VMEM((2, TQ, LANES), F32), pltpu.VMEM((2, TQ, SB_W), F32),
                        pltpu.SMEM((1,), F32)],
        compiler_params=pltpu.CompilerParams(
            dimension_semantics=("arbitrary", "arbitrary", "arbitrary"),
            vmem_limit_bytes=VMEM_LIMIT),
        name="sb_attn",
    )(proj, proj, proj, tri, g)


def _xkv_kernel(mem_ref, w_ref, o_ref):
    o_ref[0] = _dot(mem_ref[0], w_ref[...]).astype(o_ref.dtype)


def _xkv_proj(mem, w):
    b, n, d = mem.shape
    return pl.pallas_call(
        _xkv_kernel,
        grid=(b,),
        in_specs=[pl.BlockSpec((1, n, d), lambda i: (i, 0, 0)),
                  pl.BlockSpec((d, 2 * d), lambda i: (0, 0), pipeline_mode=pl.Buffered(1))],
        out_specs=pl.BlockSpec((1, n, 2 * d), lambda i: (i, 0, 0)),
        out_shape=jax.ShapeDtypeStruct((b, n, 2 * d), BF16),
        compiler_params=pltpu.CompilerParams(dimension_semantics=("arbitrary",),
                                             vmem_limit_bytes=VMEM_LIMIT),
        name="xkv_proj",
    )(mem, w)


def _tail_kernel(yd_ref, ys_ref, x_ref, kv_ref, wod_ref, wos_ref, g1_ref, b1_ref, wq_ref, wxo_ref,
                 g2_ref, b2_ref, wg_ref, wu_ref, wd_ref, g3_ref, b3_ref, o_ref):
    subs = [slice(t * TM_ROW, (t + 1) * TM_ROW) for t in range(N_SUB)]
    hcols = [slice(h * XHEAD_DIM, (h + 1) * XHEAD_DIM) for h in range(N_XHEADS)]

    mix = [_dot(yd_ref[r], wod_ref[...]) + _dot(ys_ref[r], wos_ref[...]) for r in subs]
    x1 = [_layer_norm(ALPHA * x_ref[r] + m, g1_ref[...], b1_ref[...]) for r, m in zip(subs, mix)]

    q = [(_dot(v, wq_ref[...]) * (XHEAD_DIM ** -0.5)).astype(BF16) for v in x1]
    sc = [[_dot_nt(qt[:, hc], kv_ref[0, :, hc]) for hc in hcols] for qt in q]
    heads = []
    for st in sc:
        ps = [jnp.exp(sh - jnp.max(sh, axis=-1, keepdims=True)) for sh in st]
        inv = [1.0 / jnp.sum(p, axis=-1, keepdims=True) for p in ps]
        pv = [_dot(p.astype(BF16), kv_ref[0, :, D_MODEL + h * XHEAD_DIM:D_MODEL + (h + 1) * XHEAD_DIM])
              for h, p in enumerate(ps)]
        heads.append(jnp.concatenate([o * i for o, i in zip(pv, inv)], axis=1))
    x2 = [_layer_norm(ALPHA * v + _dot(o, wxo_ref[...]), g2_ref[...], b2_ref[...])
          for v, o in zip(x1, heads)]

    x2b = [v.astype(BF16) for v in x2]
    gate = [_dot(v, wg_ref[...]) for v in x2b]
    up = [_dot(v, wu_ref[...]) for v in x2b]
    hidden = [(g * jax.nn.sigmoid(g) * u).astype(BF16) for g, u in zip(gate, up)]
    for r, v, hd in zip(subs, x2, hidden):
        o_ref[r] = _layer_norm(ALPHA * v + _dot(hd, wd_ref[...]), g3_ref[...], b3_ref[...])


def _tail(yd, ys, x2d, kv, w_od, w_os, g1, b1, wq, wxo, g2, b2, wg, wu, wd, g3, b3, seq):
    t, d = x2d.shape
    half = yd.shape[1]
    n = kv.shape[1]
    f = wg.shape[1]
    tm = N_SUB * TM_ROW
    per_batch = seq // tm
    row = lambda i: (i, 0)
    const = lambda i: (0, 0)
    resident = lambda shape: pl.BlockSpec(shape, const, pipeline_mode=pl.Buffered(1))
    return pl.pallas_call(
        _tail_kernel,
        grid=(t // tm,),
        in_specs=[pl.BlockSpec((tm, half), row), pl.BlockSpec((tm, half), row),
                  pl.BlockSpec((tm, d), row),
                  pl.BlockSpec((1, n, 2 * d), lambda i: (i // per_batch, 0, 0)),
                  resident((half, d)), resident((half, d)), resident((1, d)), resident((1, d)),
                  resident((d, d)), resident((d, d)), resident((1, d)), resident((1, d)),
                  resident((d, f)), resident((d, f)), resident((f, d)),
                  resident((1, d)), resident((1, d))],
        out_specs=pl.BlockSpec((tm, d), row),
        out_shape=jax.ShapeDtypeStruct((t, d), F32),
        compiler_params=pltpu.CompilerParams(dimension_semantics=("arbitrary",),
                                             vmem_limit_bytes=VMEM_LIMIT),
        name="tail",
    )(yd, ys, x2d, kv, w_od, w_os, g1, b1, wq, wxo, g2, b2, wg, wu, wd, g3, b3)


def _alibi_diag_table(slopes):
    i = jnp.arange(TQ, dtype=jnp.int32)[:, None]
    j = jnp.arange(TQ, dtype=jnp.int32)[None, :]
    dist = jnp.abs(i - j).astype(F32)
    allowed = (j // CHUNK) <= (i // CHUNK)
    return jnp.where(allowed[None], -(slopes * LOG2E)[:, None, None] * dist[None], NEG_INF)


def _suffix_sum_matrix():
    j = jnp.arange(2 * SB_W, dtype=jnp.int32)[:, None] % SB_W
    c = jnp.arange(2 * SB_W, dtype=jnp.int32)[None, :]
    return jnp.where((c >= SB_W) | (j >= c), 1.0, 0.0).astype(BF16)


def kernel(x, mem, w_in, diff_lambda_q1, diff_lambda_k1, diff_lambda_q2, diff_lambda_k2,
           diff_subln_g, sb_norm_g, w_o, ln1_g, ln1_b, w_xq, w_xkv, w_xo, ln2_g, ln2_b,
           w_gate, w_up, w_down, ln3_g, ln3_b):
    b, s, d = x.shape
    assert (b, s, d) == (8, SEQ, D_MODEL) and w_in.shape == (DEPTH, D_MODEL, D_IN)
    x2d = x.reshape(b * s, d)
    slopes = jnp.exp2(-8.0 * jnp.arange(1, N_DIFF_HEADS + 1, dtype=F32) / N_DIFF_HEADS)
    diagb = _alibi_diag_table(slopes)
    tri = _suffix_sum_matrix()
    vec = lambda a: a[0].reshape(1, -1)

    proj = _in_proj(x2d, w_in[0]).reshape(b, s, D_IN)
    y_diff = _diff_attn(proj, slopes, vec(diff_lambda_q1), vec(diff_lambda_k1),
                        vec(diff_lambda_q2), vec(diff_lambda_k2), diagb, vec(diff_subln_g))
    y_sb = _sb_attn(proj, tri, sb_norm_g[0].reshape(N_SB_HEADS // 2, 1, LANES))

    half = N_DIFF_HEADS * DIFF_V_DIM
    kv = _xkv_proj(mem, w_xkv[0])
    out = _tail(y_diff.reshape(b * s, half), y_sb.reshape(b * s, -1), x2d, kv,
                w_o[0, :half], w_o[0, half:], vec(ln1_g), vec(ln1_b),
                w_xq[0], w_xo[0], vec(ln2_g), vec(ln2_b),
                w_gate[0].astype(BF16), w_up[0].astype(BF16), w_down[0].astype(BF16),
                vec(ln3_g), vec(ln3_b), s)
    return out.reshape(b, s, d)
```

```python
import math

import jax
import jax.numpy as jnp
from jax import lax
from jax.experimental import pallas as pl
from jax.experimental.pallas import tpu as pltpu

D_MODEL = 1024
DEPTH = 1
CHUNK = 64
N_MEM = 256
HEAD_DIM = 64
N_DIFF_HEADS = 4
DIFF_V_DIM = 2 * HEAD_DIM
N_SB_HEADS = 8
N_XHEADS = 4
XHEAD_DIM = D_MODEL // N_XHEADS
D_FF = 2816
ALPHA = (2.0 * DEPTH) ** 0.25
LN_EPS = 1e-5
RMS_EPS = 1e-5
NEG_INF = -1e30
LAMBDA_INIT = 0.8 - 0.6 * math.exp(-0.3 * 0)
LOG2E = math.log2(math.e)

LANES = 128
QD_BLK, KD_BLK, VD_BLK, QS_BLK, KS_BLK, VS_BLK = 0, 4, 8, 12, 16, 20
D_IN = 24 * LANES

TM_PROJ = 512
TM_ROW = 256
N_SUB = 2
TQ = 512
SEQ = 2048
NQ = SEQ // TQ
DIFF_HP = 2
SB_W = 128
SB_U = TQ // SB_W
SB_NEAR = 256
SB_EXIT = 150.0
VMEM_LIMIT = 56 * 1024 * 1024

BF16 = jnp.bfloat16
F32 = jnp.float32


def _dot(a, b):
    return jnp.dot(a, b, preferred_element_type=F32)


def _dot_nt(a, b):
    return lax.dot_general(a, b, (((1,), (1,)), ((), ())), preferred_element_type=F32)


def _layer_norm(v, g, b):
    mu = jnp.mean(v, axis=-1, keepdims=True)
    d = v - mu
    var = jnp.mean(d * d, axis=-1, keepdims=True)
    return d * lax.rsqrt(var + LN_EPS) * g + b


def _hi_lo(v):
    hi = v.astype(BF16)
    lo = (v - hi.astype(F32)).astype(BF16)
    return jnp.concatenate([hi, lo], axis=1)


def _in_proj_kernel(x_ref, w_ref, o_ref):
    o_ref[...] = _dot(x_ref[...], w_ref[...]).astype(o_ref.dtype)


def _in_proj(x2d, w):
    t, d = x2d.shape
    n = w.shape[1]
    return pl.pallas_call(
        _in_proj_kernel,
        grid=(t // TM_PROJ,),
        in_specs=[pl.BlockSpec((TM_PROJ, d), lambda i: (i, 0)),
                  pl.BlockSpec((d, n), lambda i: (0, 0), pipeline_mode=pl.Buffered(1))],
        out_specs=pl.BlockSpec((TM_PROJ, n), lambda i: (i, 0)),
        out_shape=jax.ShapeDtypeStruct((t, n), BF16),
        compiler_params=pltpu.CompilerParams(dimension_semantics=("arbitrary",),
                                             vmem_limit_bytes=VMEM_LIMIT),
        name="in_proj",
    )(x2d, w)


def _diff_attn_kernel(slopes_ref, lq1_ref, lk1_ref, lq2_ref, lk2_ref, q_ref, k_ref, v_ref,
                      diagb_ref, g_ref, o_ref, m_ref, l_ref, acc_ref, s_ref, mb_ref):
    hp = pl.program_id(1)
    qi = pl.program_id(2)
    for nb in range(NQ):
        pl.when(qi == nb)(lambda nb=nb: _diff_query_block(
            nb, hp, slopes_ref, lq1_ref, lk1_ref, lq2_ref, lk2_ref, q_ref, k_ref, v_ref,
            diagb_ref, g_ref, o_ref, m_ref, l_ref, acc_ref, s_ref, mb_ref))


def _diff_query_block(nb, hp, slopes_ref, lq1_ref, lk1_ref, lq2_ref, lk2_ref, q_ref, k_ref, v_ref,
                      diagb_ref, g_ref, o_ref, m_ref, l_ref, acc_ref, s_ref, mb_ref):
    heads = range(DIFF_HP)
    hcols = [slice(hd * LANES, (hd + 1) * LANES) for hd in heads]
    slope2 = [slopes_ref[hp * DIFF_HP + hd] * LOG2E for hd in heads]

    lane = lax.broadcasted_iota(jnp.int32, (TQ, LANES), 1)
    qm = []
    for hd in heads:
        qf = q_ref[0, :, hcols[hd]].astype(F32) * (HEAD_DIM ** -0.5 * LOG2E)
        qm.append((jnp.where(lane < HEAD_DIM, qf, 0.0).astype(BF16),
                   jnp.where(lane >= HEAD_DIM, qf, 0.0).astype(BF16)))

    m_ref[...] = jnp.full_like(m_ref, NEG_INF)
    l_ref[...] = jnp.zeros_like(l_ref)
    acc_ref[...] = jnp.zeros_like(acc_ref)

    col = lax.broadcasted_iota(jnp.int32, (1, TQ), 1).astype(F32)
    row = lax.broadcasted_iota(jnp.int32, (TQ, LANES), 0).astype(F32)

    def scores(hd, slot, kstart, diagonal):
        kb = k_ref[0, kstart:kstart + TQ, hcols[hd]]
        bias = diagb_ref[hd] if diagonal else slope2[hd] * col
        for mp in range(2):
            s = _dot_nt(qm[hd][mp], kb) + bias
            s_ref[hd, slot, mp] = s
            mb_ref[hd, slot, mp] = jnp.broadcast_to(jnp.max(s, axis=-1, keepdims=True), (TQ, LANES))

    def softmax_pv(hd, slot, kstart, row_shift):
        vb = v_ref[0, kstart:kstart + TQ, hcols[hd]]
        for mp in range(2):
            m_old = m_ref[hd, mp]
            m_new = jnp.maximum(m_old, mb_ref[hd, slot, mp] + row_shift)
            alpha = jnp.exp2(m_old - m_new)
            mrel = m_new - row_shift
            ps = [jnp.exp2(s_ref[hd, slot, mp, :, c * LANES:(c + 1) * LANES] - mrel)
                  for c in range(TQ // LANES)]
            psum = ps[0]
            for pc in ps[1:]:
                psum = psum + pc
            l_ref[hd, mp] = alpha * l_ref[hd, mp] + psum
            p = jnp.concatenate([pc.astype(BF16) for pc in ps], axis=1)
            acc_ref[hd, mp] = alpha * acc_ref[hd, mp] + _dot(p, vb)
            m_ref[hd, mp] = m_new

    lam = (jnp.exp(jnp.sum(lq1_ref[...] * lk1_ref[...]))
           - jnp.exp(jnp.sum(lq2_ref[...] * lk2_ref[...])) + LAMBDA_INIT)

    def normalise(hd):
        l1 = jnp.sum(l_ref[hd, 0], axis=-1, keepdims=True)
        l2 = jnp.sum(l_ref[hd, 1], axis=-1, keepdims=True)
        out = acc_ref[hd, 0] * (1.0 / l1) - lam * (acc_ref[hd, 1] * (1.0 / l2))
        ms = jnp.mean(out * out, axis=-1, keepdims=True)
        out = out * lax.rsqrt(ms + RMS_EPS) * g_ref[...] * (1.0 - LAMBDA_INIT)
        o_ref[0, :, hcols[hd]] = out.astype(o_ref.dtype)

    for hd in heads:
        scores(hd, 0, 0, nb == 0)
    for i in range(nb + 1):
        if i < nb:
            for hd in heads:
                scores(hd, (i + 1) % 2, (i + 1) * TQ, i + 1 == nb)
        for hd in heads:
            if i < nb:
                row_shift = -slope2[hd] * (row + float((nb - i) * TQ))
            else:
                row_shift = jnp.zeros_like(row)
            softmax_pv(hd, i % 2, i * TQ, row_shift)
            if i == nb:
                normalise(hd)


def _diff_attn(proj, slopes, lq1, lk1, lq2, lk2, diagb, g):
    b, s, _ = proj.shape
    nq = s // TQ
    wide = DIFF_HP * LANES
    lam_spec = pl.BlockSpec((1, HEAD_DIM), lambda bi, hi, qi: (0, 0))
    return pl.pallas_call(
        _diff_attn_kernel,
        grid=(b, N_DIFF_HEADS // DIFF_HP, nq),
        in_specs=[
            pl.BlockSpec(memory_space=pltpu.SMEM),
            lam_spec, lam_spec, lam_spec, lam_spec,
            pl.BlockSpec((1, TQ, wide), lambda bi, hi, qi: (bi, qi, QD_BLK // DIFF_HP + hi)),
            pl.BlockSpec((1, s, wide), lambda bi, hi, qi: (bi, 0, KD_BLK // DIFF_HP + hi)),
            pl.BlockSpec((1, s, wide), lambda bi, hi, qi: (bi, 0, VD_BLK // DIFF_HP + hi)),
            pl.BlockSpec((DIFF_HP, TQ, TQ), lambda bi, hi, qi: (hi, 0, 0)),
            pl.BlockSpec((1, DIFF_V_DIM), lambda bi, hi, qi: (0, 0)),
        ],
        out_specs=pl.BlockSpec((1, TQ, wide), lambda bi, hi, qi: (bi, qi, hi)),
        out_shape=jax.ShapeDtypeStruct((b, s, N_DIFF_HEADS * DIFF_V_DIM), BF16),
        scratch_shapes=[pltpu.VMEM((DIFF_HP, 2, TQ, LANES), F32),
                        pltpu.VMEM((DIFF_HP, 2, TQ, LANES), F32),
                        pltpu.VMEM((DIFF_HP, 2, TQ, LANES), F32),
                        pltpu.VMEM((DIFF_HP, 2, 2, TQ, TQ), F32),
                        pltpu.VMEM((DIFF_HP, 2, 2, TQ, LANES), F32)],
        compiler_params=pltpu.CompilerParams(
            dimension_semantics=("arbitrary", "arbitrary", "arbitrary"),
            vmem_limit_bytes=VMEM_LIMIT),
        name="diff_attn",
    )(slopes, lq1, lk1, lq2, lk2, proj, proj, proj, diagb, g)


def _sb_attn_kernel(q_ref, k_ref, v_ref, tri_ref, g_ref, o_ref, acc_ref, carry_ref, reach_ref):
    qi = pl.program_id(2)
    lane = lax.broadcasted_iota(jnp.int32, (TQ, LANES), 1)
    lo_half = lane < HEAD_DIM
    qf = q_ref[0].astype(F32) * (HEAD_DIM ** -0.5 * LOG2E)
    qm = (jnp.where(lo_half, qf, 0.0).astype(BF16),
          jnp.where(lo_half, 0.0, qf).astype(BF16))
    tri = tri_ref[...]

    row = lax.broadcasted_iota(jnp.int32, (TQ, SB_W), 0)
    colk = lax.broadcasted_iota(jnp.int32, (TQ, SB_W), 1)
    strict = colk < row
    t0 = pl.multiple_of(qi * TQ, TQ)

    def scores(kstart, nkeys):
        kb = k_ref[0, pl.ds(kstart, nkeys), :]
        return [_dot_nt(qm[hh], kb) for hh in range(2)]

    def softplus_split(zs, nsub, own):
        z, x = {}, {}
        for hh in range(2):
            for u in reversed(range(nsub)):
                r0 = u * SB_W if own else 0
                zc = zs[hh][r0:, u * SB_W:(u + 1) * SB_W]
                sp = jnp.maximum(zc, 0.0) + jnp.log(1.0 + jnp.exp2(-jnp.abs(zc))) * LOG2E
                if own:
                    sp = jnp.where(strict[:TQ - r0], sp, 0.0)
                z[hh, u] = zc
                x[hh, u] = _hi_lo(sp)
        return z, x

    def suffix_sums(x):
        return {key: _dot(val, tri) for key, val in x.items()}

    def weights_pv(z, r, nsub, own, kstart, carry, acc):
        vb = v_ref[0, pl.ds(kstart, nsub * SB_W), :]
        for hh in range(2):
            a = {}
            for u in reversed(range(nsub)):
                r0 = u * SB_W if own else 0
                av = jnp.exp2(z[hh, u] - r[hh, u][:, :SB_W] - carry[hh][r0:])
                if own:
                    av = jnp.where(strict[:TQ - r0], av, 0.0)
                avb, tot = av.astype(BF16), r[hh, u][:, SB_W:]
                if r0:
                    a[u] = jnp.concatenate([jnp.zeros((r0, SB_W), BF16), avb], axis=0)
                    carry[hh] = jnp.concatenate([carry[hh][:r0], carry[hh][r0:] + tot], axis=0)
                else:
                    a[u], carry[hh] = avb, carry[hh] + tot
            acc[hh] = acc[hh] + _dot(jnp.concatenate([a[u] for u in range(nsub)], axis=1), vb)

    def near_keys(with_previous):
        carry = [jnp.zeros((TQ, SB_W), F32), jnp.zeros((TQ, SB_W), F32)]
        acc = [jnp.zeros((TQ, LANES), F32), jnp.zeros((TQ, LANES), F32)]
        prev = pl.multiple_of(t0 - SB_NEAR, SB_NEAR)
        z0, x0 = softplus_split(scores(t0, TQ), SB_U, True)
        if with_previous:
            zs1 = scores(prev, SB_NEAR)
        r0 = suffix_sums(x0)
        if with_previous:
            z1, x1 = softplus_split(zs1, SB_NEAR // SB_W, False)
        weights_pv(z0, r0, SB_U, True, t0, carry, acc)
        if with_previous:
            weights_pv(z1, suffix_sums(x1), SB_NEAR // SB_W, False, prev, carry, acc)
        park(carry, acc)

    def park(carry, acc):
        for hh in range(2):
            carry_ref[hh] = carry[hh]
            acc_ref[hh] = acc[hh]
        reach_ref[0] = jnp.min(jnp.minimum(carry[0], carry[1]))

    pl.when(qi == 0)(lambda: near_keys(False))
    pl.when(qi > 0)(lambda: near_keys(True))

    n_far = qi * (TQ // SB_NEAR) - 1

    def far_chunk(c):
        kstart = pl.multiple_of(t0 - (c + 2) * SB_NEAR, SB_NEAR)
        nsub = SB_NEAR // SB_W
        z, x = softplus_split(scores(kstart, SB_NEAR), nsub, False)
        carry = [carry_ref[0], carry_ref[1]]
        acc = [acc_ref[0], acc_ref[1]]
        weights_pv(z, suffix_sums(x), nsub, False, kstart, carry, acc)
        park(carry, acc)
        return c + 1

    lax.while_loop(lambda c: jnp.logical_and(c < n_far, reach_ref[0] < SB_EXIT), far_chunk, 0)

    out = jnp.where(lo_half, acc_ref[0], acc_ref[1])
    sq = out * out
    ss_lo = jnp.sum(jnp.where(lo_half, sq, 0.0), axis=-1, keepdims=True)
    ss_hi = jnp.sum(jnp.where(lo_half, 0.0, sq), axis=-1, keepdims=True)
    ms = jnp.where(lo_half, ss_lo, ss_hi) * (1.0 / HEAD_DIM)
    o_ref[0] = (out * lax.rsqrt(ms + RMS_EPS) * g_ref[0]).astype(o_ref.dtype)


def _sb_attn(proj, tri, g):
    b, s, _ = proj.shape
    nq = s // TQ
    npair = N_SB_HEADS // 2
    return pl.pallas_call(
        _sb_attn_kernel,
        grid=(b, npair, nq),
        in_specs=[
            pl.BlockSpec((1, TQ, LANES), lambda bi, pi, qi: (bi, qi, QS_BLK + pi)),
            pl.BlockSpec((1, s, LANES), lambda bi, pi, qi: (bi, 0, KS_BLK + pi)),
            pl.BlockSpec((1, s, LANES), lambda bi, pi, qi: (bi, 0, VS_BLK + pi)),
            pl.BlockSpec((2 * SB_W, 2 * SB_W), lambda bi, pi, qi: (0, 0)),
            pl.BlockSpec((1, 1, LANES), lambda bi, pi, qi: (pi, 0, 0)),
        ],
        out_specs=pl.BlockSpec((1, TQ, LANES), lambda bi, pi, qi: (bi, qi, pi)),
        out_shape=jax.ShapeDtypeStruct((b, s, N_SB_HEADS * HEAD_DIM), BF16),
        scratch_shapes=[pltpu.VMEM((2, TQ, LANES), F32), pltpu.VMEM((2, TQ, SB_W), F32),
                        pltpu.SMEM((1,), F32)],
        compiler_params=pltpu.CompilerParams(
            dimension_semantics=("arbitrary", "arbitrary", "arbitrary"),
            vmem_limit_bytes=VMEM_LIMIT),
        name="sb_attn",
    )(proj, proj, proj, tri, g)


def _xkv_kernel(mem_ref, w_ref, o_ref):
    o_ref[0] = _dot(mem_ref[0], w_ref[...]).astype(o_ref.dtype)


def _xkv_proj(mem, w):
    b, n, d = mem.shape
    return pl.pallas_call(
        _xkv_kernel,
        grid=(b,),
        in_specs=[pl.BlockSpec((1, n, d), lambda i: (i, 0, 0)),
                  pl.BlockSpec((d, 2 * d), lambda i: (0, 0), pipeline_mode=pl.Buffered(1))],
        out_specs=pl.BlockSpec((1, n, 2 * d), lambda i: (i, 0, 0)),
        out_shape=jax.ShapeDtypeStruct((b, n, 2 * d), BF16),
        compiler_params=pltpu.CompilerParams(dimension_semantics=("arbitrary",),
                                             vmem_limit_bytes=VMEM_LIMIT),
        name="xkv_proj",
    )(mem, w)


def _tail_kernel(yd_ref, ys_ref, x_ref, kv_ref, wod_ref, wos_ref, g1_ref, b1_ref, wq_ref, wxo_ref,
                 g2_ref, b2_ref, wg_ref, wu_ref, wd_ref, g3_ref, b3_ref, o_ref):
    subs = [slice(t * TM_ROW, (t + 1) * TM_ROW) for t in range(N_SUB)]
    hcols = [slice(h * XHEAD_DIM, (h + 1) * XHEAD_DIM) for h in range(N_XHEADS)]

    mix = [_dot(yd_ref[r], wod_ref[...]) + _dot(ys_ref[r], wos_ref[...]) for r in subs]
    x1 = [_layer_norm(ALPHA * x_ref[r] + m, g1_ref[...], b1_ref[...]) for r, m in zip(subs, mix)]

    q = [(_dot(v, wq_ref[...]) * (XHEAD_DIM ** -0.5)).astype(BF16) for v in x1]
    sc = [[_dot_nt(qt[:, hc], kv_ref[0, :, hc]) for hc in hcols] for qt in q]
    heads = []
    for st in sc:
        ps = [jnp.exp(sh - jnp.max(sh, axis=-1, keepdims=True)) for sh in st]
        inv = [1.0 / jnp.sum(p, axis=-1, keepdims=True) for p in ps]
        pv = [_dot(p.astype(BF16), kv_ref[0, :, D_MODEL + h * XHEAD_DIM:D_MODEL + (h + 1) * XHEAD_DIM])
              for h, p in enumerate(ps)]
        heads.append(jnp.concatenate([o * i for o, i in zip(pv, inv)], axis=1))
    x2 = [_layer_norm(ALPHA * v + _dot(o, wxo_ref[...]), g2_ref[...], b2_ref[...])
          for v, o in zip(x1, heads)]

    x2b = [v.astype(BF16) for v in x2]
    gate = [_dot(v, wg_ref[...]) for v in x2b]
    up = [_dot(v, wu_ref[...]) for v in x2b]
    hidden = [(g * jax.nn.sigmoid(g) * u).astype(BF16) for g, u in zip(gate, up)]
    for r, v, hd in zip(subs, x2, hidden):
        o_ref[r] = _layer_norm(ALPHA * v + _dot(hd, wd_ref[...]), g3_ref[...], b3_ref[...])


def _tail(yd, ys, x2d, kv, w_od, w_os, g1, b1, wq, wxo, g2, b2, wg, wu, wd, g3, b3, seq):
    t, d = x2d.shape
    half = yd.shape[1]
    n = kv.shape[1]
    f = wg.shape[1]
    tm = N_SUB * TM_ROW
    per_batch = seq // tm
    row = lambda i: (i, 0)
    const = lambda i: (0, 0)
    resident = lambda shape: pl.BlockSpec(shape, const, pipeline_mode=pl.Buffered(1))
    return pl.pallas_call(
        _tail_kernel,
        grid=(t // tm,),
        in_specs=[pl.BlockSpec((tm, half), row), pl.BlockSpec((tm, half), row),
                  pl.BlockSpec((tm, d), row),
                  pl.BlockSpec((1, n, 2 * d), lambda i: (i // per_batch, 0, 0)),
                  resident((half, d)), resident((half, d)), resident((1, d)), resident((1, d)),
                  resident((d, d)), resident((d, d)), resident((1, d)), resident((1, d)),
                  resident((d, f)), resident((d, f)), resident((f, d)),
                  resident((1, d)), resident((1, d))],
        out_specs=pl.BlockSpec((tm, d), row),
        out_shape=jax.ShapeDtypeStruct((t, d), F32),
        compiler_params=pltpu.CompilerParams(dimension_semantics=("arbitrary",),
                                             vmem_limit_bytes=VMEM_LIMIT),
        name="tail",
    )(yd, ys, x2d, kv, w_od, w_os, g1, b1, wq, wxo, g2, b2, wg, wu, wd, g3, b3)


def _alibi_diag_table(slopes):
    i = jnp.arange(TQ, dtype=jnp.int32)[:, None]
    j = jnp.arange(TQ, dtype=jnp.int32)[None, :]
    dist = jnp.abs(i - j).astype(F32)
    allowed = (j // CHUNK) <= (i // CHUNK)
    return jnp.where(allowed[None], -(slopes * LOG2E)[:, None, None] * dist[None], NEG_INF)


def _suffix_sum_matrix():
    j = jnp.arange(2 * SB_W, dtype=jnp.int32)[:, None] % SB_W
    c = jnp.arange(2 * SB_W, dtype=jnp.int32)[None, :]
    return jnp.where((c >= SB_W) | (j >= c), 1.0, 0.0).astype(BF16)


def kernel(x, mem, w_in, diff_lambda_q1, diff_lambda_k1, diff_lambda_q2, diff_lambda_k2,
           diff_subln_g, sb_norm_g, w_o, ln1_g, ln1_b, w_xq, w_xkv, w_xo, ln2_g, ln2_b,
           w_gate, w_up, w_down, ln3_g, ln3_b):
    b, s, d = x.shape
    assert (b, s, d) == (8, SEQ, D_MODEL) and w_in.shape == (DEPTH, D_MODEL, D_IN)
    x2d = x.reshape(b * s, d)
    slopes = jnp.exp2(-8.0 * jnp.arange(1, N_DIFF_HEADS + 1, dtype=F32) / N_DIFF_HEADS)
    diagb = _alibi_diag_table(slopes)
    tri = _suffix_sum_matrix()
    vec = lambda a: a[0].reshape(1, -1)

    proj = _in_proj(x2d, w_in[0]).reshape(b, s, D_IN)
    y_diff = _diff_attn(proj, slopes, vec(diff_lambda_q1), vec(diff_lambda_k1),
                        vec(diff_lambda_q2), vec(diff_lambda_k2), diagb, vec(diff_subln_g))
    y_sb = _sb_attn(proj, tri, sb_norm_g[0].reshape(N_SB_HEADS // 2, 1, LANES))

    half = N_DIFF_HEADS * DIFF_V_DIM
    kv = _xkv_proj(mem, w_xkv[0])
    out = _tail(y_diff.reshape(b * s, half), y_sb.reshape(b * s, -1), x2d, kv,
                w_o[0, :half], w_o[0, half:], vec(ln1_g), vec(ln1_b),
                w_xq[0], w_xo[0], vec(ln2_g), vec(ln2_b),
                w_gate[0].astype(BF16), w_up[0].astype(BF16), w_down[0].astype(BF16),
                vec(ln3_g), vec(ln3_b), s)
    return out.reshape(b, s, d)
```

```python
import math

import jax
import jax.numpy as jnp
from jax import lax
from jax.experimental import pallas as pl
from jax.experimental.pallas import tpu as pltpu

D_MODEL = 1024
DEPTH = 1
CHUNK = 64
N_MEM = 256
HEAD_DIM = 64
N_DIFF_HEADS = 4
DIFF_V_DIM = 2 * HEAD_DIM
N_SB_HEADS = 8
N_XHEADS = 4
XHEAD_DIM = D_MODEL // N_XHEADS
D_FF = 2816
ALPHA = (2.0 * DEPTH) ** 0.25
LN_EPS = 1e-5
RMS_EPS = 1e-5
NEG_INF = -1e30
LAMBDA_INIT = 0.8 - 0.6 * math.exp(-0.3 * 0)
LOG2E = math.log2(math.e)

LANES = 128
QD_BLK, KD_BLK, VD_BLK, QS_BLK, KS_BLK, VS_BLK = 0, 4, 8, 12, 16, 20
D_IN = 24 * LANES

TM_PROJ = 512
TM_ROW = 256
N_SUB = 2
TQ = 512
SEQ = 2048
NQ = SEQ // TQ
DIFF_HP = 2
SB_W = 128
SB_U = TQ // SB_W
SB_NEAR = 256
SB_EXIT = 150.0
SB_PP = 2
VMEM_LIMIT = 56 * 1024 * 1024

BF16 = jnp.bfloat16
F32 = jnp.float32


def _dot(a, b):
    return jnp.dot(a, b, preferred_element_type=F32)


def _dot_nt(a, b):
    return lax.dot_general(a, b, (((1,), (1,)), ((), ())), preferred_element_type=F32)


def _layer_norm(v, g, b):
    mu = jnp.mean(v, axis=-1, keepdims=True)
    d = v - mu
    var = jnp.mean(d * d, axis=-1, keepdims=True)
    return d * lax.rsqrt(var + LN_EPS) * g + b


def _hi_lo(v):
    hi = v.astype(BF16)
    lo = (v - hi.astype(F32)).astype(BF16)
    return jnp.concatenate([hi, lo], axis=1)


def _in_proj_kernel(x_ref, w_ref, o_ref):
    o_ref[...] = _dot(x_ref[...], w_ref[...]).astype(o_ref.dtype)


def _in_proj(x2d, w):
    t, d = x2d.shape
    n = w.shape[1]
    return pl.pallas_call(
        _in_proj_kernel,
        grid=(t // TM_PROJ,),
        in_specs=[pl.BlockSpec((TM_PROJ, d), lambda i: (i, 0)),
                  pl.BlockSpec((d, n), lambda i: (0, 0), pipeline_mode=pl.Buffered(1))],
        out_specs=pl.BlockSpec((TM_PROJ, n), lambda i: (i, 0)),
        out_shape=jax.ShapeDtypeStruct((t, n), BF16),
        compiler_params=pltpu.CompilerParams(dimension_semantics=("arbitrary",),
                                             vmem_limit_bytes=VMEM_LIMIT),
        name="in_proj",
    )(x2d, w)


def _diff_attn_kernel(slopes_ref, lq1_ref, lk1_ref, lq2_ref, lk2_ref, q_ref, k_ref, v_ref,
                      diagb_ref, g_ref, o_ref, m_ref, l_ref, acc_ref, s_ref, mb_ref):
    hp = pl.program_id(1)
    qi = pl.program_id(2)
    for nb in range(NQ):
        pl.when(qi == nb)(lambda nb=nb: _diff_query_block(
            nb, hp, slopes_ref, lq1_ref, lk1_ref, lq2_ref, lk2_ref, q_ref, k_ref, v_ref,
            diagb_ref, g_ref, o_ref, m_ref, l_ref, acc_ref, s_ref, mb_ref))


def _diff_query_block(nb, hp, slopes_ref, lq1_ref, lk1_ref, lq2_ref, lk2_ref, q_ref, k_ref, v_ref,
                      diagb_ref, g_ref, o_ref, m_ref, l_ref, acc_ref, s_ref, mb_ref):
    heads = range(DIFF_HP)
    hcols = [slice(hd * LANES, (hd + 1) * LANES) for hd in heads]
    slope2 = [slopes_ref[hp * DIFF_HP + hd] * LOG2E for hd in heads]

    lane = lax.broadcasted_iota(jnp.int32, (TQ, LANES), 1)
    qm = []
    for hd in heads:
        qf = q_ref[0, :, hcols[hd]].astype(F32) * (HEAD_DIM ** -0.5 * LOG2E)
        qm.append((jnp.where(lane < HEAD_DIM, qf, 0.0).astype(BF16),
                   jnp.where(lane >= HEAD_DIM, qf, 0.0).astype(BF16)))

    m_ref[...] = jnp.full_like(m_ref, NEG_INF)
    l_ref[...] = jnp.zeros_like(l_ref)
    acc_ref[...] = jnp.zeros_like(acc_ref)

    col = lax.broadcasted_iota(jnp.int32, (1, TQ), 1).astype(F32)
    row = lax.broadcasted_iota(jnp.int32, (TQ, LANES), 0).astype(F32)

    def scores(hd, slot, kstart, diagonal):
        kb = k_ref[0, kstart:kstart + TQ, hcols[hd]]
        bias = diagb_ref[hd] if diagonal else slope2[hd] * col
        for mp in range(2):
            s = _dot_nt(qm[hd][mp], kb) + bias
            s_ref[hd, slot, mp] = s
            mb_ref[hd, slot, mp] = jnp.broadcast_to(jnp.max(s, axis=-1, keepdims=True), (TQ, LANES))

    def softmax_pv(hd, slot, kstart, row_shift):
        vb = v_ref[0, kstart:kstart + TQ, hcols[hd]]
        for mp in range(2):
            m_old = m_ref[hd, mp]
            m_new = jnp.maximum(m_old, mb_ref[hd, slot, mp] + row_shift)
            alpha = jnp.exp2(m_old - m_new)
            mrel = m_new - row_shift
            ps = [jnp.exp2(s_ref[hd, slot, mp, :, c * LANES:(c + 1) * LANES] - mrel)
                  for c in range(TQ // LANES)]
            psum = ps[0]
            for pc in ps[1:]:
                psum = psum + pc
            l_ref[hd, mp] = alpha * l_ref[hd, mp] + psum
            p = jnp.concatenate([pc.astype(BF16) for pc in ps], axis=1)
            acc_ref[hd, mp] = alpha * acc_ref[hd, mp] + _dot(p, vb)
            m_ref[hd, mp] = m_new

    lam = (jnp.exp(jnp.sum(lq1_ref[...] * lk1_ref[...]))
           - jnp.exp(jnp.sum(lq2_ref[...] * lk2_ref[...])) + LAMBDA_INIT)

    def normalise(hd):
        l1 = jnp.sum(l_ref[hd, 0], axis=-1, keepdims=True)
        l2 = jnp.sum(l_ref[hd, 1], axis=-1, keepdims=True)
        out = acc_ref[hd, 0] * (1.0 / l1) - lam * (acc_ref[hd, 1] * (1.0 / l2))
        ms = jnp.mean(out * out, axis=-1, keepdims=True)
        out = out * lax.rsqrt(ms + RMS_EPS) * g_ref[...] * (1.0 - LAMBDA_INIT)
        o_ref[0, :, hcols[hd]] = out.astype(o_ref.dtype)

    for hd in heads:
        scores(hd, 0, 0, nb == 0)
    for i in range(nb + 1):
        if i < nb:
            for hd in heads:
                scores(hd, (i + 1) % 2, (i + 1) * TQ, i + 1 == nb)
        for hd in heads:
            if i < nb:
                row_shift = -slope2[hd] * (row + float((nb - i) * TQ))
            else:
                row_shift = jnp.zeros_like(row)
            softmax_pv(hd, i % 2, i * TQ, row_shift)
            if i == nb:
                normalise(hd)


def _diff_attn(proj, slopes, lq1, lk1, lq2, lk2, diagb, g):
    b, s, _ = proj.shape
    nq = s // TQ
    wide = DIFF_HP * LANES
    lam_spec = pl.BlockSpec((1, HEAD_DIM), lambda bi, hi, qi: (0, 0))
    return pl.pallas_call(
        _diff_attn_kernel,
        grid=(b, N_DIFF_HEADS // DIFF_HP, nq),
        in_specs=[
            pl.BlockSpec(memory_space=pltpu.SMEM),
            lam_spec, lam_spec, lam_spec, lam_spec,
            pl.BlockSpec((1, TQ, wide), lambda bi, hi, qi: (bi, qi, QD_BLK // DIFF_HP + hi)),
            pl.BlockSpec((1, s, wide), lambda bi, hi, qi: (bi, 0, KD_BLK // DIFF_HP + hi)),
            pl.BlockSpec((1, s, wide), lambda bi, hi, qi: (bi, 0, VD_BLK // DIFF_HP + hi)),
            pl.BlockSpec((DIFF_HP, TQ, TQ), lambda bi, hi, qi: (hi, 0, 0)),
            pl.BlockSpec((1, DIFF_V_DIM), lambda bi, hi, qi: (0, 0)),
        ],
        out_specs=pl.BlockSpec((1, TQ, wide), lambda bi, hi, qi: (bi, qi, hi)),
        out_shape=jax.ShapeDtypeStruct((b, s, N_DIFF_HEADS * DIFF_V_DIM), BF16),
        scratch_shapes=[pltpu.VMEM((DIFF_HP, 2, TQ, LANES), F32),
                        pltpu.VMEM((DIFF_HP, 2, TQ, LANES), F32),
                        pltpu.VMEM((DIFF_HP, 2, TQ, LANES), F32),
                        pltpu.VMEM((DIFF_HP, 2, 2, TQ, TQ), F32),
                        pltpu.VMEM((DIFF_HP, 2, 2, TQ, LANES), F32)],
        compiler_params=pltpu.CompilerParams(
            dimension_semantics=("arbitrary", "arbitrary", "arbitrary"),
            vmem_limit_bytes=VMEM_LIMIT),
        name="diff_attn",
    )(slopes, lq1, lk1, lq2, lk2, proj, proj, proj, diagb, g)


def _sb_attn_kernel(q_ref, k_ref, v_ref, tri_ref, g_ref, o_ref, acc_ref, carry_ref, reach_ref):
    qi = pl.program_id(2)
    pairs = range(SB_PP)
    pcols = [slice(pp * LANES, (pp + 1) * LANES) for pp in pairs]
    lane = lax.broadcasted_iota(jnp.int32, (TQ, LANES), 1)
    lo_half = lane < HEAD_DIM
    row = lax.broadcasted_iota(jnp.int32, (TQ, SB_W), 0)
    colk = lax.broadcasted_iota(jnp.int32, (TQ, SB_W), 1)
    strict = colk < row
    t0 = pl.multiple_of(qi * TQ, TQ)
    tri = tri_ref[...]

    def masked_queries():
        qm = []
        for pp in pairs:
            qf = q_ref[0, :, pcols[pp]].astype(F32) * (HEAD_DIM ** -0.5 * LOG2E)
            qm.append((jnp.where(lo_half, qf, 0.0).astype(BF16),
                       jnp.where(lo_half, 0.0, qf).astype(BF16)))
        return qm

    def scores(qm, pp, kstart, nkeys):
        kb = k_ref[0, pl.ds(kstart, nkeys), pcols[pp]]
        return [_dot_nt(qm[pp][hh], kb) for hh in range(2)]

    def softplus_split(zs, nsub, own):
        z, x = {}, {}
        for hh in range(2):
            for u in reversed(range(nsub)):
                r0 = u * SB_W if own else 0
                zc = zs[hh][r0:, u * SB_W:(u + 1) * SB_W]
                sp = jnp.maximum(zc, 0.0) + jnp.log(1.0 + jnp.exp2(-jnp.abs(zc))) * LOG2E
                if own:
                    sp = jnp.where(strict[:TQ - r0], sp, 0.0)
                z[hh, u] = zc
                x[hh, u] = _hi_lo(sp)
        return z, x

    def suffix_sums(x):
        return {key: _dot(val, tri) for key, val in x.items()}

    def weights_pv(pp, z, r, nsub, own, kstart, carry, acc):
        vb = v_ref[0, pl.ds(kstart, nsub * SB_W), pcols[pp]]
        for hh in range(2):
            a = {}
            for u in reversed(range(nsub)):
                r0 = u * SB_W if own else 0
                av = jnp.exp2(z[hh, u] - r[hh, u][:, :SB_W] - carry[hh][r0:])
                if own:
                    av = jnp.where(strict[:TQ - r0], av, 0.0)
                avb, tot = av.astype(BF16), r[hh, u][:, SB_W:]
                if r0:
                    a[u] = jnp.concatenate([jnp.zeros((r0, SB_W), BF16), avb], axis=0)
                    carry[hh] = jnp.concatenate([carry[hh][:r0], carry[hh][r0:] + tot], axis=0)
                else:
                    a[u], carry[hh] = avb, carry[hh] + tot
            acc[hh] = acc[hh] + _dot(jnp.concatenate([a[u] for u in range(nsub)], axis=1), vb)

    def park(carry, acc):
        reach = None
        for pp in pairs:
            for hh in range(2):
                carry_ref[pp, hh] = carry[pp][hh]
                acc_ref[pp, hh] = acc[pp][hh]
                reach = carry[pp][hh] if reach is None else jnp.minimum(reach, carry[pp][hh])
        return jnp.min(reach)

    def normalise(pp, acc):
        out = jnp.where(lo_half, acc[0], acc[1])
        sq = out * out
        ss_lo = jnp.sum(jnp.where(lo_half, sq, 0.0), axis=-1, keepdims=True)
        ss_hi = jnp.sum(jnp.where(lo_half, 0.0, sq), axis=-1, keepdims=True)
        ms = jnp.where(lo_half, ss_lo, ss_hi) * (1.0 / HEAD_DIM)
        o_ref[0, :, pcols[pp]] = (out * lax.rsqrt(ms + RMS_EPS) * g_ref[pp]).astype(o_ref.dtype)

    def near_keys(with_previous):
        qm = masked_queries()
        nsub1 = SB_NEAR // SB_W
        prev = pl.multiple_of(t0 - SB_NEAR, SB_NEAR)
        carry = [[jnp.zeros((TQ, SB_W), F32), jnp.zeros((TQ, SB_W), F32)] for _ in pairs]
        acc = [[jnp.zeros((TQ, LANES), F32), jnp.zeros((TQ, LANES), F32)] for _ in pairs]
        zx0 = [softplus_split(scores(qm, pp, t0, TQ), SB_U, True) for pp in pairs]
        if with_previous:
            zs1 = [scores(qm, pp, prev, SB_NEAR) for pp in pairs]
        r0 = [suffix_sums(zx0[pp][1]) for pp in pairs]
        if with_previous:
            zx1 = [softplus_split(zs1[pp], nsub1, False) for pp in pairs]
        for pp in pairs:
            weights_pv(pp, zx0[pp][0], r0[pp], SB_U, True, t0, carry[pp], acc[pp])
        if with_previous:
            r1 = [suffix_sums(zx1[pp][1]) for pp in pairs]
            for pp in pairs:
                weights_pv(pp, zx1[pp][0], r1[pp], nsub1, False, prev, carry[pp], acc[pp])
        reach = park(carry, acc)
        reach_ref[0] = reach
        reach_ref[1] = reach
        for pp in pairs:
            normalise(pp, acc[pp])

    pl.when(qi == 0)(lambda: near_keys(False))
    pl.when(qi > 0)(lambda: near_keys(True))

    n_far = qi * (TQ // SB_NEAR) - 1

    def far_chunk(c):
        qm = masked_queries()
        kstart = pl.multiple_of(t0 - (c + 2) * SB_NEAR, SB_NEAR)
        nsub = SB_NEAR // SB_W
        zx = [softplus_split(scores(qm, pp, kstart, SB_NEAR), nsub, False) for pp in pairs]
        r = [suffix_sums(zx[pp][1]) for pp in pairs]
        carry = [[carry_ref[pp, 0], carry_ref[pp, 1]] for pp in pairs]
        acc = [[acc_ref[pp, 0], acc_ref[pp, 1]] for pp in pairs]
        for pp in pairs:
            weights_pv(pp, zx[pp][0], r[pp], nsub, False, kstart, carry[pp], acc[pp])
        reach_ref[0] = park(carry, acc)
        return c + 1

    lax.while_loop(lambda c: jnp.logical_and(c < n_far, reach_ref[0] < SB_EXIT), far_chunk, 0)

    @pl.when(jnp.logical_and(n_far > 0, reach_ref[1] < SB_EXIT))
    def _():
        for pp in pairs:
            normalise(pp, [acc_ref[pp, 0], acc_ref[pp, 1]])


def _sb_attn(proj, tri, g):
    b, s, _ = proj.shape
    nq = s // TQ
    npair = N_SB_HEADS // 2
    wide = SB_PP * LANES
    return pl.pallas_call(
        _sb_attn_kernel,
        grid=(b, npair // SB_PP, nq),
        in_specs=[
            pl.BlockSpec((1, TQ, wide), lambda bi, pi, qi: (bi, qi, QS_BLK // SB_PP + pi)),
            pl.BlockSpec((1, s, wide), lambda bi, pi, qi: (bi, 0, KS_BLK // SB_PP + pi)),
            pl.BlockSpec((1, s, wide), lambda bi, pi, qi: (bi, 0, VS_BLK // SB_PP + pi)),
            pl.BlockSpec((2 * SB_W, 2 * SB_W), lambda bi, pi, qi: (0, 0)),
            pl.BlockSpec((SB_PP, 1, LANES), lambda bi, pi, qi: (pi, 0, 0)),
        ],
        out_specs=pl.BlockSpec((1, TQ, wide), lambda bi, pi, qi: (bi, qi, pi)),
        out_shape=jax.ShapeDtypeStruct((b, s, N_SB_HEADS * HEAD_DIM), BF16),
        scratch_shapes=[pltpu.VMEM((SB_PP, 2, TQ, LANES), F32), pltpu.VMEM((SB_PP, 2, TQ, SB_W), F32),
                        pltpu.SMEM((2,), F32)],
        compiler_params=pltpu.CompilerParams(
            dimension_semantics=("arbitrary", "arbitrary", "arbitrary"),
            vmem_limit_bytes=VMEM_LIMIT),
        name="sb_attn",
    )(proj, proj, proj, tri, g)


def _xkv_kernel(mem_ref, w_ref, o_ref):
    o_ref[0] = _dot(mem_ref[0], w_ref[...]).astype(o_ref.dtype)


def _xkv_proj(mem, w):
    b, n, d = mem.shape
    return pl.pallas_call(
        _xkv_kernel,
        grid=(b,),
        in_specs=[pl.BlockSpec((1, n, d), lambda i: (i, 0, 0)),
                  pl.BlockSpec((d, 2 * d), lambda i: (0, 0), pipeline_mode=pl.Buffered(1))],
        out_specs=pl.BlockSpec((1, n, 2 * d), lambda i: (i, 0, 0)),
        out_shape=jax.ShapeDtypeStruct((b, n, 2 * d), BF16),
        compiler_params=pltpu.CompilerParams(dimension_semantics=("arbitrary",),
                                             vmem_limit_bytes=VMEM_LIMIT),
        name="xkv_proj",
    )(mem, w)


def _tail_kernel(yd_ref, ys_ref, x_ref, kv_ref, wod_ref, wos_ref, g1_ref, b1_ref, wq_ref, wxo_ref,
                 g2_ref, b2_ref, wg_ref, wu_ref, wd_ref, g3_ref, b3_ref, o_ref):
    subs = [slice(t * TM_ROW, (t + 1) * TM_ROW) for t in range(N_SUB)]
    hcols = [slice(h * XHEAD_DIM, (h + 1) * XHEAD_DIM) for h in range(N_XHEADS)]

    mix = [_dot(yd_ref[r], wod_ref[...]) + _dot(ys_ref[r], wos_ref[...]) for r in subs]
    x1 = [_layer_norm(ALPHA * x_ref[r] + m, g1_ref[...], b1_ref[...]) for r, m in zip(subs, mix)]

    q = [(_dot(v, wq_ref[...]) * (XHEAD_DIM ** -0.5)).astype(BF16) for v in x1]
    sc = [[_dot_nt(qt[:, hc], kv_ref[0, :, hc]) for hc in hcols] for qt in q]
    heads = []
    for st in sc:
        ps = [jnp.exp(sh - jnp.max(sh, axis=-1, keepdims=True)) for sh in st]
        inv = [1.0 / jnp.sum(p, axis=-1, keepdims=True) for p in ps]
        pv = [_dot(p.astype(BF16), kv_ref[0, :, D_MODEL + h * XHEAD_DIM:D_MODEL + (h + 1) * XHEAD_DIM])
              for h, p in enumerate(ps)]
        heads.append(jnp.concatenate([o * i for o, i in zip(pv, inv)], axis=1))
    x2 = [_layer_norm(ALPHA * v + _dot(o, wxo_ref[...]), g2_ref[...], b2_ref[...])
          for v, o in zip(x1, heads)]

    x2b = [v.astype(BF16) for v in x2]
    gate = [_dot(v, wg_ref[...]) for v in x2b]
    up = [_dot(v, wu_ref[...]) for v in x2b]
    hidden = [(g * jax.nn.sigmoid(g) * u).astype(BF16) for g, u in zip(gate, up)]
    for r, v, hd in zip(subs, x2, hidden):
        o_ref[r] = _layer_norm(ALPHA * v + _dot(hd, wd_ref[...]), g3_ref[...], b3_ref[...])


def _tail(yd, ys, x2d, kv, w_od, w_os, g1, b1, wq, wxo, g2, b2, wg, wu, wd, g3, b3, seq):
    t, d = x2d.shape
    half = yd.shape[1]
    n = kv.shape[1]
    f = wg.shape[1]
    tm = N_SUB * TM_ROW
    per_batch = seq // tm
    row = lambda i: (i, 0)
    const = lambda i: (0, 0)
    resident = lambda shape: pl.BlockSpec(shape, const, pipeline_mode=pl.Buffered(1))
    return pl.pallas_call(
        _tail_kernel,
        grid=(t // tm,),
        in_specs=[pl.BlockSpec((tm, half), row), pl.BlockSpec((tm, half), row),
                  pl.BlockSpec((tm, d), row),
                  pl.BlockSpec((1, n, 2 * d), lambda i: (i // per_batch, 0, 0)),
                  resident((half, d)), resident((half, d)), resident((1, d)), resident((1, d)),
                  resident((d, d)), resident((d, d)), resident((1, d)), resident((1, d)),
                  resident((d, f)), resident((d, f)), resident((f, d)),
                  resident((1, d)), resident((1, d))],
        out_specs=pl.BlockSpec((tm, d), row),
        out_shape=jax.ShapeDtypeStruct((t, d), F32),
        compiler_params=pltpu.CompilerParams(dimension_semantics=("arbitrary",),
                                             vmem_limit_bytes=VMEM_LIMIT),
        name="tail",
    )(yd, ys, x2d, kv, w_od, w_os, g1, b1, wq, wxo, g2, b2, wg, wu, wd, g3, b3)


def _alibi_diag_table(slopes):
    i = jnp.arange(TQ, dtype=jnp.int32)[:, None]
    j = jnp.arange(TQ, dtype=jnp.int32)[None, :]
    dist = jnp.abs(i - j).astype(F32)
    allowed = (j // CHUNK) <= (i // CHUNK)
    return jnp.where(allowed[None], -(slopes * LOG2E)[:, None, None] * dist[None], NEG_INF)


def _suffix_sum_matrix():
    j = jnp.arange(2 * SB_W, dtype=jnp.int32)[:, None] % SB_W
    c = jnp.arange(2 * SB_W, dtype=jnp.int32)[None, :]
    return jnp.where((c >= SB_W) | (j >= c), 1.0, 0.0).astype(BF16)


def kernel(x, mem, w_in, diff_lambda_q1, diff_lambda_k1, diff_lambda_q2, diff_lambda_k2,
           diff_subln_g, sb_norm_g, w_o, ln1_g, ln1_b, w_xq, w_xkv, w_xo, ln2_g, ln2_b,
           w_gate, w_up, w_down, ln3_g, ln3_b):
    b, s, d = x.shape
    assert (b, s, d) == (8, SEQ, D_MODEL) and w_in.shape == (DEPTH, D_MODEL, D_IN)
    x2d = x.reshape(b * s, d)
    slopes = jnp.exp2(-8.0 * jnp.arange(1, N_DIFF_HEADS + 1, dtype=F32) / N_DIFF_HEADS)
    diagb = _alibi_diag_table(slopes)
    tri = _suffix_sum_matrix()
    vec = lambda a: a[0].reshape(1, -1)

    proj = _in_proj(x2d, w_in[0]).reshape(b, s, D_IN)
    y_diff = _diff_attn(proj, slopes, vec(diff_lambda_q1), vec(diff_lambda_k1),
                        vec(diff_lambda_q2), vec(diff_lambda_k2), diagb, vec(diff_subln_g))
    y_sb = _sb_attn(proj, tri, sb_norm_g[0].reshape(N_SB_HEADS // 2, 1, LANES))

    half = N_DIFF_HEADS * DIFF_V_DIM
    kv = _xkv_proj(mem, w_xkv[0])
    out = _tail(y_diff.reshape(b * s, half), y_sb.reshape(b * s, -1), x2d, kv,
                w_o[0, :half], w_o[0, half:], vec(ln1_g), vec(ln1_b),
                w_xq[0], w_xo[0], vec(ln2_g), vec(ln2_b),
                w_gate[0].astype(BF16), w_up[0].astype(BF16), w_down[0].astype(BF16),
                vec(ln3_g), vec(ln3_b), s)
    return out.reshape(b, s, d)
```

```python
import math

import jax
import jax.numpy as jnp
from jax import lax
from jax.experimental import pallas as pl
from jax.experimental.pallas import tpu as pltpu

D_MODEL = 1024
DEPTH = 1
CHUNK = 64
N_MEM = 256
HEAD_DIM = 64
N_DIFF_HEADS = 4
DIFF_V_DIM = 2 * HEAD_DIM
N_SB_HEADS = 8
N_XHEADS = 4
XHEAD_DIM = D_MODEL // N_XHEADS
D_FF = 2816
ALPHA = (2.0 * DEPTH) ** 0.25
LN_EPS = 1e-5
RMS_EPS = 1e-5
NEG_INF = -1e30
LAMBDA_INIT = 0.8 - 0.6 * math.exp(-0.3 * 0)
LOG2E = math.log2(math.e)

LANES = 128
BF16_SUBLANES = 16
QD_BLK, KD_BLK, VD_BLK, QS_BLK, KS_BLK, VS_BLK = 0, 4, 8, 12, 16, 20
D_IN = 24 * LANES

TM_PROJ = 512
TM_ROW = 256
N_SUB = 2
TQ = 512
SEQ = 2048
NQ = SEQ // TQ
DIFF_HP = 2
SB_W = 128
SB_U = TQ // SB_W
SB_NEAR = 256
SB_EXIT = 150.0
SB_PP = 2
VMEM_LIMIT = 56 * 1024 * 1024

BF16 = jnp.bfloat16
F32 = jnp.float32


def _dot(a, b):
    return jnp.dot(a, b, preferred_element_type=F32)


def _dot_nt(a, b):
    return lax.dot_general(a, b, (((1,), (1,)), ((), ())), preferred_element_type=F32)


def _layer_norm(v, g, b):
    mu = jnp.mean(v, axis=-1, keepdims=True)
    d = v - mu
    var = jnp.mean(d * d, axis=-1, keepdims=True)
    return d * lax.rsqrt(var + LN_EPS) * g + b


def _hi_lo(v):
    hi = v.astype(BF16)
    lo = (v - hi.astype(F32)).astype(BF16)
    return jnp.concatenate([hi, lo], axis=1)


def _in_proj_kernel(x_ref, w_ref, *refs):
    n = len(refs) // 2
    o_ref = refs[n]
    o_ref[...] = _dot(x_ref[...], w_ref[...]).astype(o_ref.dtype)
    for src, dst in zip(refs[:n], refs[n + 1:]):
        dst[...] = src[...].astype(dst.dtype)


def _in_proj(x2d, w, tail_weights):
    t, d = x2d.shape
    n = w.shape[1]
    steps = t // TM_PROJ
    w_specs, w_shapes = [], []
    for tw in tail_weights:
        rows, cols = tw.shape
        if rows % (BF16_SUBLANES * steps) == 0:
            blk, index = rows // steps, (lambda i: (i, 0))
        else:
            assert rows % (BF16_SUBLANES * steps // 2) == 0
            blk, index = 2 * rows // steps, (lambda i: (jnp.minimum(i, steps // 2 - 1), 0))
        w_specs.append(pl.BlockSpec((blk, cols), index))
        w_shapes.append(jax.ShapeDtypeStruct(tw.shape, BF16))
    outs = pl.pallas_call(
        _in_proj_kernel,
        grid=(steps,),
        in_specs=[pl.BlockSpec((TM_PROJ, d), lambda i: (i, 0)),
                  pl.BlockSpec((d, n), lambda i: (0, 0), pipeline_mode=pl.Buffered(1))] + w_specs,
        out_specs=[pl.BlockSpec((TM_PROJ, n), lambda i: (i, 0))] + w_specs,
        out_shape=[jax.ShapeDtypeStruct((t, n), BF16)] + w_shapes,
        compiler_params=pltpu.CompilerParams(dimension_semantics=("arbitrary",),
                                             vmem_limit_bytes=VMEM_LIMIT),
        name="in_proj",
    )(x2d, w, *tail_weights)
    return outs[0], outs[1:]


def _diff_attn_kernel(slopes_ref, lq1_ref, lk1_ref, lq2_ref, lk2_ref, q_ref, k_ref, v_ref,
                      diagb_ref, g_ref, o_ref, m_ref, l_ref, acc_ref, s_ref, mb_ref):
    hp = pl.program_id(1)
    qi = pl.program_id(2)
    for nb in range(NQ):
        pl.when(qi == nb)(lambda nb=nb: _diff_query_block(
            nb, hp, slopes_ref, lq1_ref, lk1_ref, lq2_ref, lk2_ref, q_ref, k_ref, v_ref,
            diagb_ref, g_ref, o_ref, m_ref, l_ref, acc_ref, s_ref, mb_ref))


def _diff_query_block(nb, hp, slopes_ref, lq1_ref, lk1_ref, lq2_ref, lk2_ref, q_ref, k_ref, v_ref,
                      diagb_ref, g_ref, o_ref, m_ref, l_ref, acc_ref, s_ref, mb_ref):
    heads = range(DIFF_HP)
    hcols = [slice(hd * LANES, (hd + 1) * LANES) for hd in heads]
    slope2 = [slopes_ref[hp * DIFF_HP + hd] * LOG2E for hd in heads]

    lane = lax.broadcasted_iota(jnp.int32, (TQ, LANES), 1)
    qm = []
    for hd in heads:
        qf = q_ref[0, :, hcols[hd]].astype(F32) * (HEAD_DIM ** -0.5 * LOG2E)
        qm.append((jnp.where(lane < HEAD_DIM, qf, 0.0).astype(BF16),
                   jnp.where(lane >= HEAD_DIM, qf, 0.0).astype(BF16)))

    m_ref[...] = jnp.full_like(m_ref, NEG_INF)
    l_ref[...] = jnp.zeros_like(l_ref)
    acc_ref[...] = jnp.zeros_like(acc_ref)

    col = lax.broadcasted_iota(jnp.int32, (1, TQ), 1).astype(F32)
    row = lax.broadcasted_iota(jnp.int32, (TQ, LANES), 0).astype(F32)

    def scores(hd, slot, kstart, diagonal):
        kb = k_ref[0, kstart:kstart + TQ, hcols[hd]]
        bias = diagb_ref[hd] if diagonal else slope2[hd] * col
        for mp in range(2):
            s = _dot_nt(qm[hd][mp], kb) + bias
            s_ref[hd, slot, mp] = s
            mb_ref[hd, slot, mp] = jnp.broadcast_to(jnp.max(s, axis=-1, keepdims=True), (TQ, LANES))

    def softmax_pv(hd, slot, kstart, row_shift):
        vb = v_ref[0, kstart:kstart + TQ, hcols[hd]]
        for mp in range(2):
            m_old = m_ref[hd, mp]
            m_new = jnp.maximum(m_old, mb_ref[hd, slot, mp] + row_shift)
            alpha = jnp.exp2(m_old - m_new)
            mrel = m_new - row_shift
            ps = [jnp.exp2(s_ref[hd, slot, mp, :, c * LANES:(c + 1) * LANES] - mrel)
                  for c in range(TQ // LANES)]
            psum = ps[0]
            for pc in ps[1:]:
                psum = psum + pc
            l_ref[hd, mp] = alpha * l_ref[hd, mp] + psum
            p = jnp.concatenate([pc.astype(BF16) for pc in ps], axis=1)
            acc_ref[hd, mp] = alpha * acc_ref[hd, mp] + _dot(p, vb)
            m_ref[hd, mp] = m_new

    lam = (jnp.exp(jnp.sum(lq1_ref[...] * lk1_ref[...]))
           - jnp.exp(jnp.sum(lq2_ref[...] * lk2_ref[...])) + LAMBDA_INIT)

    def normalise(hd):
        l1 = jnp.sum(l_ref[hd, 0], axis=-1, keepdims=True)
        l2 = jnp.sum(l_ref[hd, 1], axis=-1, keepdims=True)
        out = acc_ref[hd, 0] * (1.0 / l1) - lam * (acc_ref[hd, 1] * (1.0 / l2))
        ms = jnp.mean(out * out, axis=-1, keepdims=True)
        out = out * lax.rsqrt(ms + RMS_EPS) * g_ref[...] * (1.0 - LAMBDA_INIT)
        o_ref[0, :, hcols[hd]] = out.astype(o_ref.dtype)

    for hd in heads:
        scores(hd, 0, 0, nb == 0)
    for i in range(nb + 1):
        if i < nb:
            for hd in heads:
                scores(hd, (i + 1) % 2, (i + 1) * TQ, i + 1 == nb)
        for hd in heads:
            if i < nb:
                row_shift = -slope2[hd] * (row + float((nb - i) * TQ))
            else:
                row_shift = jnp.zeros_like(row)
            softmax_pv(hd, i % 2, i * TQ, row_shift)
            if i == nb:
                normalise(hd)


def _diff_attn(proj, slopes, lq1, lk1, lq2, lk2, diagb, g):
    b, s, _ = proj.shape
    nq = s // TQ
    wide = DIFF_HP * LANES
    lam_spec = pl.BlockSpec((1, HEAD_DIM), lambda bi, hi, qi: (0, 0))
    return pl.pallas_call(
        _diff_attn_kernel,
        grid=(b, N_DIFF_HEADS // DIFF_HP, nq),
        in_specs=[
            pl.BlockSpec(memory_space=pltpu.SMEM),
            lam_spec, lam_spec, lam_spec, lam_spec,
            pl.BlockSpec((1, TQ, wide), lambda bi, hi, qi: (bi, qi, QD_BLK // DIFF_HP + hi)),
            pl.BlockSpec((1, s, wide), lambda bi, hi, qi: (bi, 0, KD_BLK // DIFF_HP + hi)),
            pl.BlockSpec((1, s, wide), lambda bi, hi, qi: (bi, 0, VD_BLK // DIFF_HP + hi)),
            pl.BlockSpec((DIFF_HP, TQ, TQ), lambda bi, hi, qi: (hi, 0, 0)),
            pl.BlockSpec((1, DIFF_V_DIM), lambda bi, hi, qi: (0, 0)),
        ],
        out_specs=pl.BlockSpec((1, TQ, wide), lambda bi, hi, qi: (bi, qi, hi)),
        out_shape=jax.ShapeDtypeStruct((b, s, N_DIFF_HEADS * DIFF_V_DIM), BF16),
        scratch_shapes=[pltpu.VMEM((DIFF_HP, 2, TQ, LANES), F32),
                        pltpu.VMEM((DIFF_HP, 2, TQ, LANES), F32),
                        pltpu.VMEM((DIFF_HP, 2, TQ, LANES), F32),
                        pltpu.VMEM((DIFF_HP, 2, 2, TQ, TQ), F32),
                        pltpu.VMEM((DIFF_HP, 2, 2, TQ, LANES), F32)],
        compiler_params=pltpu.CompilerParams(
            dimension_semantics=("arbitrary", "arbitrary", "arbitrary"),
            vmem_limit_bytes=VMEM_LIMIT),
        name="diff_attn",
    )(slopes, lq1, lk1, lq2, lk2, proj, proj, proj, diagb, g)


def _sb_attn_kernel(q_ref, k_ref, v_ref, tri_ref, g_ref, o_ref, acc_ref, carry_ref, reach_ref):
    qi = pl.program_id(2)
    pairs = range(SB_PP)
    pcols = [slice(pp * LANES, (pp + 1) * LANES) for pp in pairs]
    lane = lax.broadcasted_iota(jnp.int32, (TQ, LANES), 1)
    lo_half = lane < HEAD_DIM
    row = lax.broadcasted_iota(jnp.int32, (TQ, SB_W), 0)
    colk = lax.broadcasted_iota(jnp.int32, (TQ, SB_W), 1)
    strict = colk < row
    t0 = pl.multiple_of(qi * TQ, TQ)
    tri = tri_ref[...]

    def masked_queries():
        qm = []
        for pp in pairs:
            qf = q_ref[0, :, pcols[pp]].astype(F32) * (HEAD_DIM ** -0.5 * LOG2E)
            qm.append((jnp.where(lo_half, qf, 0.0).astype(BF16),
                       jnp.where(lo_half, 0.0, qf).astype(BF16)))
        return qm

    def scores(qm, pp, kstart, nkeys):
        kb = k_ref[0, pl.ds(kstart, nkeys), pcols[pp]]
        return [_dot_nt(qm[pp][hh], kb) for hh in range(2)]

    def softplus_split(zs, nsub, own):
        z, x = {}, {}
        for hh in range(2):
            for u in reversed(range(nsub)):
                r0 = u * SB_W if own else 0
                zc = zs[hh][r0:, u * SB_W:(u + 1) * SB_W]
                sp = jnp.maximum(zc, 0.0) + jnp.log(1.0 + jnp.exp2(-jnp.abs(zc))) * LOG2E
                if own:
                    sp = jnp.where(strict[:TQ - r0], sp, 0.0)
                z[hh, u] = zc
                x[hh, u] = _hi_lo(sp)
        return z, x

    def suffix_sums(x):
        return {key: _dot(val, tri) for key, val in x.items()}

    def weights_pv(pp, z, r, nsub, own, kstart, carry, acc):
        vb = v_ref[0, pl.ds(kstart, nsub * SB_W), pcols[pp]]
        for hh in range(2):
            a = {}
            for u in reversed(range(nsub)):
                r0 = u * SB_W if own else 0
                av = jnp.exp2(z[hh, u] - r[hh, u][:, :SB_W] - carry[hh][r0:])
                if own:
                    av = jnp.where(strict[:TQ - r0], av, 0.0)
                avb, tot = av.astype(BF16), r[hh, u][:, SB_W:]
                if r0:
                    a[u] = jnp.concatenate([jnp.zeros((r0, SB_W), BF16), avb], axis=0)
                    carry[hh] = jnp.concatenate([carry[hh][:r0], carry[hh][r0:] + tot], axis=0)
                else:
                    a[u], carry[hh] = avb, carry[hh] + tot
            acc[hh] = acc[hh] + _dot(jnp.concatenate([a[u] for u in range(nsub)], axis=1), vb)

    def park(carry, acc):
        reach = None
        for pp in pairs:
            for hh in range(2):
                carry_ref[pp, hh] = carry[pp][hh]
                acc_ref[pp, hh] = acc[pp][hh]
                reach = carry[pp][hh] if reach is None else jnp.minimum(reach, carry[pp][hh])
        return jnp.min(reach)

    def normalise(pp, acc):
        out = jnp.where(lo_half, acc[0], acc[1])
        sq = out * out
        ss_lo = jnp.sum(jnp.where(lo_half, sq, 0.0), axis=-1, keepdims=True)
        ss_hi = jnp.sum(jnp.where(lo_half, 0.0, sq), axis=-1, keepdims=True)
        ms = jnp.where(lo_half, ss_lo, ss_hi) * (1.0 / HEAD_DIM)
        o_ref[0, :, pcols[pp]] = (out * lax.rsqrt(ms + RMS_EPS) * g_ref[pp]).astype(o_ref.dtype)

    def near_keys(with_previous):
        qm = masked_queries()
        nsub1 = SB_NEAR // SB_W
        prev = pl.multiple_of(t0 - SB_NEAR, SB_NEAR)
        carry = [[jnp.zeros((TQ, SB_W), F32), jnp.zeros((TQ, SB_W), F32)] for _ in pairs]
        acc = [[jnp.zeros((TQ, LANES), F32), jnp.zeros((TQ, LANES), F32)] for _ in pairs]
        zx0 = [softplus_split(scores(qm, pp, t0, TQ), SB_U, True) for pp in pairs]
        if with_previous:
            zs1 = [scores(qm, pp, prev, SB_NEAR) for pp in pairs]
        r0 = [suffix_sums(zx0[pp][1]) for pp in pairs]
        if with_previous:
            zx1 = [softplus_split(zs1[pp], nsub1, False) for pp in pairs]
        for pp in pairs:
            weights_pv(pp, zx0[pp][0], r0[pp], SB_U, True, t0, carry[pp], acc[pp])
        if with_previous:
            r1 = [suffix_sums(zx1[pp][1]) for pp in pairs]
            for pp in pairs:
                weights_pv(pp, zx1[pp][0], r1[pp], nsub1, False, prev, carry[pp], acc[pp])
        reach = park(carry, acc)
        reach_ref[0] = reach
        reach_ref[1] = reach
        for pp in pairs:
            normalise(pp, acc[pp])

    pl.when(qi == 0)(lambda: near_keys(False))
    pl.when(qi > 0)(lambda: near_keys(True))

    n_far = qi * (TQ // SB_NEAR) - 1

    def far_chunk(c):
        qm = masked_queries()
        kstart = pl.multiple_of(t0 - (c + 2) * SB_NEAR, SB_NEAR)
        nsub = SB_NEAR // SB_W
        zx = [softplus_split(scores(qm, pp, kstart, SB_NEAR), nsub, False) for pp in pairs]
        r = [suffix_sums(zx[pp][1]) for pp in pairs]
        carry = [[carry_ref[pp, 0], carry_ref[pp, 1]] for pp in pairs]
        acc = [[acc_ref[pp, 0], acc_ref[pp, 1]] for pp in pairs]
        for pp in pairs:
            weights_pv(pp, zx[pp][0], r[pp], nsub, False, kstart, carry[pp], acc[pp])
        reach_ref[0] = park(carry, acc)
        return c + 1

    lax.while_loop(lambda c: jnp.logical_and(c < n_far, reach_ref[0] < SB_EXIT), far_chunk, 0)

    @pl.when(jnp.logical_and(n_far > 0, reach_ref[1] < SB_EXIT))
    def _():
        for pp in pairs:
            normalise(pp, [acc_ref[pp, 0], acc_ref[pp, 1]])


def _sb_attn(proj, tri, g):
    b, s, _ = proj.shape
    nq = s // TQ
    npair = N_SB_HEADS // 2
    wide = SB_PP * LANES
    return pl.pallas_call(
        _sb_attn_kernel,
        grid=(b, npair // SB_PP, nq),
        in_specs=[
            pl.BlockSpec((1, TQ, wide), lambda bi, pi, qi: (bi, qi, QS_BLK // SB_PP + pi)),
            pl.BlockSpec((1, s, wide), lambda bi, pi, qi: (bi, 0, KS_BLK // SB_PP + pi)),
            pl.BlockSpec((1, s, wide), lambda bi, pi, qi: (bi, 0, VS_BLK // SB_PP + pi)),
            pl.BlockSpec((2 * SB_W, 2 * SB_W), lambda bi, pi, qi: (0, 0)),
            pl.BlockSpec((SB_PP, 1, LANES), lambda bi, pi, qi: (pi, 0, 0)),
        ],
        out_specs=pl.BlockSpec((1, TQ, wide), lambda bi, pi, qi: (bi, qi, pi)),
        out_shape=jax.ShapeDtypeStruct((b, s, N_SB_HEADS * HEAD_DIM), BF16),
        scratch_shapes=[pltpu.VMEM((SB_PP, 2, TQ, LANES), F32), pltpu.VMEM((SB_PP, 2, TQ, SB_W), F32),
                        pltpu.SMEM((2,), F32)],
        compiler_params=pltpu.CompilerParams(
            dimension_semantics=("arbitrary", "arbitrary", "arbitrary"),
            vmem_limit_bytes=VMEM_LIMIT),
        name="sb_attn",
    )(proj, proj, proj, tri, g)


def _xkv_kernel(mem_ref, w_ref, o_ref):
    o_ref[0] = _dot(mem_ref[0], w_ref[...]).astype(o_ref.dtype)


def _xkv_proj(mem, w):
    b, n, d = mem.shape
    return pl.pallas_call(
        _xkv_kernel,
        grid=(b,),
        in_specs=[pl.BlockSpec((1, n, d), lambda i: (i, 0, 0)),
                  pl.BlockSpec((d, 2 * d), lambda i: (0, 0), pipeline_mode=pl.Buffered(1))],
        out_specs=pl.BlockSpec((1, n, 2 * d), lambda i: (i, 0, 0)),
        out_shape=jax.ShapeDtypeStruct((b, n, 2 * d), BF16),
        compiler_params=pltpu.CompilerParams(dimension_semantics=("arbitrary",),
                                             vmem_limit_bytes=VMEM_LIMIT),
        name="xkv_proj",
    )(mem, w)


def _tail_kernel(yd_ref, ys_ref, x_ref, kv_ref, wo_ref, g1_ref, b1_ref, wq_ref, wxo_ref,
                 g2_ref, b2_ref, wg_ref, wu_ref, wd_ref, g3_ref, b3_ref, o_ref):
    subs = [slice(t * TM_ROW, (t + 1) * TM_ROW) for t in range(N_SUB)]
    hcols = [slice(h * XHEAD_DIM, (h + 1) * XHEAD_DIM) for h in range(N_XHEADS)]

    def query(x1):
        return (_dot(x1.astype(BF16), wq_ref[...]) * (XHEAD_DIM ** -0.5)).astype(BF16)

    def scores(q):
        return [_dot_nt(q[:, hc], kv_ref[0, :, hc]) for hc in hcols]

    def attend(sc):
        ps = [jnp.exp(sh - jnp.max(sh, axis=-1, keepdims=True)) for sh in sc]
        inv = [1.0 / jnp.sum(p, axis=-1, keepdims=True) for p in ps]
        pv = [_dot(p.astype(BF16), kv_ref[0, :, D_MODEL + h * XHEAD_DIM:D_MODEL + (h + 1) * XHEAD_DIM])
              for h, p in enumerate(ps)]
        return jnp.concatenate([(o * i).astype(BF16) for o, i in zip(pv, inv)], axis=1)

    def out_proj(x1, heads):
        return _layer_norm(ALPHA * x1 + _dot(heads, wxo_ref[...]), g2_ref[...], b2_ref[...])

    def gate_up(x2):
        xb = x2.astype(BF16)
        gate, up = _dot(xb, wg_ref[...]), _dot(xb, wu_ref[...])
        return (gate * jax.nn.sigmoid(gate) * up).astype(BF16)

    a, b = subs
    half = yd_ref.shape[1]
    mix = [_dot(yd_ref[r], wo_ref[:half, :]) + _dot(ys_ref[r], wo_ref[half:, :]) for r in subs]
    x1 = [_layer_norm(ALPHA * x_ref[r] + m, g1_ref[...], b1_ref[...]) for r, m in zip(subs, mix)]
    q_a = query(x1[0])
    sc_a = scores(q_a)
    q_b = query(x1[1])
    heads_a = attend(sc_a)
    sc_b = scores(q_b)
    x2_a = out_proj(x1[0], heads_a)
    heads_b = attend(sc_b)
    x2_b = out_proj(x1[1], heads_b)
    hidden_a = gate_up(x2_a)
    hidden_b = gate_up(x2_b)
    o_ref[a] = _layer_norm(ALPHA * x2_a + _dot(hidden_a, wd_ref[...]), g3_ref[...], b3_ref[...])
    o_ref[b] = _layer_norm(ALPHA * x2_b + _dot(hidden_b, wd_ref[...]), g3_ref[...], b3_ref[...])


def _tail(yd, ys, x2d, kv, wo, g1, b1, wq, wxo, g2, b2, wg, wu, wd, g3, b3, seq):
    t, d = x2d.shape
    half = yd.shape[1]
    n = kv.shape[1]
    f = wg.shape[1]
    tm = N_SUB * TM_ROW
    per_batch = seq // tm
    row = lambda i: (i, 0)
    const = lambda i: (0, 0)
    resident = lambda shape: pl.BlockSpec(shape, const, pipeline_mode=pl.Buffered(1))
    return pl.pallas_call(
        _tail_kernel,
        grid=(t // tm,),
        in_specs=[pl.BlockSpec((tm, half), row), pl.BlockSpec((tm, half), row),
                  pl.BlockSpec((tm, d), row),
                  pl.BlockSpec((1, n, 2 * d), lambda i: (i // per_batch, 0, 0)),
                  resident((d, d)), resident((1, d)), resident((1, d)),
                  resident((d, d)), resident((d, d)), resident((1, d)), resident((1, d)),
                  resident((d, f)), resident((d, f)), resident((f, d)),
                  resident((1, d)), resident((1, d))],
        out_specs=pl.BlockSpec((tm, d), row),
        out_shape=jax.ShapeDtypeStruct((t, d), F32),
        compiler_params=pltpu.CompilerParams(dimension_semantics=("arbitrary",),
                                             vmem_limit_bytes=VMEM_LIMIT),
        name="tail",
    )(yd, ys, x2d, kv, wo, g1, b1, wq, wxo, g2, b2, wg, wu, wd, g3, b3)


def _alibi_diag_table(slopes):
    i = jnp.arange(TQ, dtype=jnp.int32)[:, None]
    j = jnp.arange(TQ, dtype=jnp.int32)[None, :]
    dist = jnp.abs(i - j).astype(F32)
    allowed = (j // CHUNK) <= (i // CHUNK)
    return jnp.where(allowed[None], -(slopes * LOG2E)[:, None, None] * dist[None], NEG_INF)


def _suffix_sum_matrix():
    j = jnp.arange(2 * SB_W, dtype=jnp.int32)[:, None] % SB_W
    c = jnp.arange(2 * SB_W, dtype=jnp.int32)[None, :]
    return jnp.where((c >= SB_W) | (j >= c), 1.0, 0.0).astype(BF16)


def kernel(x, mem, w_in, diff_lambda_q1, diff_lambda_k1, diff_lambda_q2, diff_lambda_k2,
           diff_subln_g, sb_norm_g, w_o, ln1_g, ln1_b, w_xq, w_xkv, w_xo, ln2_g, ln2_b,
           w_gate, w_up, w_down, ln3_g, ln3_b):
    b, s, d = x.shape
    assert (b, s, d) == (8, SEQ, D_MODEL) and w_in.shape == (DEPTH, D_MODEL, D_IN)
    x2d = x.reshape(b * s, d)
    slopes = jnp.exp2(-8.0 * jnp.arange(1, N_DIFF_HEADS + 1, dtype=F32) / N_DIFF_HEADS)
    diagb = _alibi_diag_table(slopes)
    tri = _suffix_sum_matrix()
    vec = lambda a: a[0].reshape(1, -1)

    proj, (wo16, wq16, wxo16, wg16, wu16, wd16) = _in_proj(
        x2d, w_in[0], [w_o[0], w_xq[0], w_xo[0], w_gate[0], w_up[0], w_down[0]])
    proj = proj.reshape(b, s, D_IN)
    y_diff = _diff_attn(proj, slopes, vec(diff_lambda_q1), vec(diff_lambda_k1),
                        vec(diff_lambda_q2), vec(diff_lambda_k2), diagb, vec(diff_subln_g))
    y_sb = _sb_attn(proj, tri, sb_norm_g[0].reshape(N_SB_HEADS // 2, 1, LANES))

    half = N_DIFF_HEADS * DIFF_V_DIM
    kv = _xkv_proj(mem, w_xkv[0])
    out = _tail(y_diff.reshape(b * s, half), y_sb.reshape(b * s, -1), x2d, kv,
                wo16, vec(ln1_g), vec(ln1_b), wq16, wxo16, vec(ln2_g), vec(ln2_b),
                wg16, wu16, wd16, vec(ln3_g), vec(ln3_b), s)
    return out.reshape(b, s, d)
```

```python
import math

import jax
import jax.numpy as jnp
from jax import lax
from jax.experimental import pallas as pl
from jax.experimental.pallas import tpu as pltpu

D_MODEL = 1024
DEPTH = 1
CHUNK = 64
N_MEM = 256
HEAD_DIM = 64
N_DIFF_HEADS = 4
DIFF_V_DIM = 2 * HEAD_DIM
N_SB_HEADS = 8
N_XHEADS = 4
XHEAD_DIM = D_MODEL // N_XHEADS
D_FF = 2816
ALPHA = (2.0 * DEPTH) ** 0.25
LN_EPS = 1e-5
RMS_EPS = 1e-5
NEG_INF = -1e30
LAMBDA_INIT = 0.8 - 0.6 * math.exp(-0.3 * 0)
LOG2E = math.log2(math.e)

LANES = 128
BF16_SUBLANES = 16
QD_BLK, KD_BLK, VD_BLK, QS_BLK, KS_BLK, VS_BLK = 0, 4, 8, 12, 16, 20
D_IN = 24 * LANES

TM_PROJ = 512
TM_ROW = 256
N_SUB = 2
TQ = 512
SEQ = 2048
NQ = SEQ // TQ
DIFF_HP = 2
SB_W = 128
SB_U = TQ // SB_W
SB_NEAR = 256
SB_EXIT = 150.0
SB_TOP = 256
SB_PP = 2
VMEM_LIMIT = 56 * 1024 * 1024

BF16 = jnp.bfloat16
F32 = jnp.float32


def _dot(a, b):
    return jnp.dot(a, b, preferred_element_type=F32)


def _dot_nt(a, b):
    return lax.dot_general(a, b, (((1,), (1,)), ((), ())), preferred_element_type=F32)


def _layer_norm(v, g, b):
    mu = jnp.mean(v, axis=-1, keepdims=True)
    d = v - mu
    var = jnp.mean(d * d, axis=-1, keepdims=True)
    return d * lax.rsqrt(var + LN_EPS) * g + b


def _hi_lo(v):
    hi = v.astype(BF16)
    lo = (v - hi.astype(F32)).astype(BF16)
    return jnp.concatenate([hi, lo], axis=1)


def _in_proj_kernel(x_ref, w_ref, *refs):
    n = len(refs) // 2
    o_ref = refs[n]
    o_ref[...] = _dot(x_ref[...], w_ref[...]).astype(o_ref.dtype)
    for src, dst in zip(refs[:n], refs[n + 1:]):
        dst[...] = src[...].astype(dst.dtype)


def _in_proj(x2d, w, tail_weights):
    t, d = x2d.shape
    n = w.shape[1]
    steps = t // TM_PROJ
    w_specs, w_shapes = [], []
    for tw in tail_weights:
        rows, cols = tw.shape
        if rows % (BF16_SUBLANES * steps) == 0:
            blk, index = rows // steps, (lambda i: (i, 0))
        else:
            assert rows % (BF16_SUBLANES * steps // 2) == 0
            blk, index = 2 * rows // steps, (lambda i: (jnp.minimum(i, steps // 2 - 1), 0))
        w_specs.append(pl.BlockSpec((blk, cols), index))
        w_shapes.append(jax.ShapeDtypeStruct(tw.shape, BF16))
    outs = pl.pallas_call(
        _in_proj_kernel,
        grid=(steps,),
        in_specs=[pl.BlockSpec((TM_PROJ, d), lambda i: (i, 0)),
                  pl.BlockSpec((d, n), lambda i: (0, 0), pipeline_mode=pl.Buffered(1))] + w_specs,
        out_specs=[pl.BlockSpec((TM_PROJ, n), lambda i: (i, 0))] + w_specs,
        out_shape=[jax.ShapeDtypeStruct((t, n), BF16)] + w_shapes,
        compiler_params=pltpu.CompilerParams(dimension_semantics=("arbitrary",),
                                             vmem_limit_bytes=VMEM_LIMIT),
        name="in_proj",
    )(x2d, w, *tail_weights)
    return outs[0], outs[1:]


def _diff_attn_kernel(slopes_ref, lq1_ref, lk1_ref, lq2_ref, lk2_ref, q_ref, k_ref, v_ref,
                      diagb_ref, g_ref, o_ref, m_ref, l_ref, acc_ref, s_ref, mb_ref):
    hp = pl.program_id(1)
    qi = pl.program_id(2)
    for nb in range(NQ):
        pl.when(qi == nb)(lambda nb=nb: _diff_query_block(
            nb, hp, slopes_ref, lq1_ref, lk1_ref, lq2_ref, lk2_ref, q_ref, k_ref, v_ref,
            diagb_ref, g_ref, o_ref, m_ref, l_ref, acc_ref, s_ref, mb_ref))


def _diff_query_block(nb, hp, slopes_ref, lq1_ref, lk1_ref, lq2_ref, lk2_ref, q_ref, k_ref, v_ref,
                      diagb_ref, g_ref, o_ref, m_ref, l_ref, acc_ref, s_ref, mb_ref):
    heads = range(DIFF_HP)
    hcols = [slice(hd * LANES, (hd + 1) * LANES) for hd in heads]
    slope2 = [slopes_ref[hp * DIFF_HP + hd] * LOG2E for hd in heads]

    lane = lax.broadcasted_iota(jnp.int32, (TQ, LANES), 1)
    qm = []
    for hd in heads:
        qf = q_ref[0, :, hcols[hd]].astype(F32) * (HEAD_DIM ** -0.5 * LOG2E)
        qm.append((jnp.where(lane < HEAD_DIM, qf, 0.0).astype(BF16),
                   jnp.where(lane >= HEAD_DIM, qf, 0.0).astype(BF16)))

    m_ref[...] = jnp.full_like(m_ref, NEG_INF)
    l_ref[...] = jnp.zeros_like(l_ref)
    acc_ref[...] = jnp.zeros_like(acc_ref)

    col = lax.broadcasted_iota(jnp.int32, (1, TQ), 1).astype(F32)
    row = lax.broadcasted_iota(jnp.int32, (TQ, LANES), 0).astype(F32)

    def scores(hd, slot, kstart, diagonal):
        kb = k_ref[0, kstart:kstart + TQ, hcols[hd]]
        bias = diagb_ref[hd] if diagonal else slope2[hd] * col
        for mp in range(2):
            s = _dot_nt(qm[hd][mp], kb) + bias
            s_ref[hd, slot, mp] = s
            mb_ref[hd, slot, mp] = jnp.broadcast_to(jnp.max(s, axis=-1, keepdims=True), (TQ, LANES))

    def softmax_pv(hd, slot, kstart, row_shift):
        vb = v_ref[0, kstart:kstart + TQ, hcols[hd]]
        for mp in range(2):
            m_old = m_ref[hd, mp]
            m_new = jnp.maximum(m_old, mb_ref[hd, slot, mp] + row_shift)
            alpha = jnp.exp2(m_old - m_new)
            mrel = m_new - row_shift
            ps = [jnp.exp2(s_ref[hd, slot, mp, :, c * LANES:(c + 1) * LANES] - mrel)
                  for c in range(TQ // LANES)]
            psum = ps[0]
            for pc in ps[1:]:
                psum = psum + pc
            l_ref[hd, mp] = alpha * l_ref[hd, mp] + psum
            p = jnp.concatenate([pc.astype(BF16) for pc in ps], axis=1)
            acc_ref[hd, mp] = alpha * acc_ref[hd, mp] + _dot(p, vb)
            m_ref[hd, mp] = m_new

    lam = (jnp.exp(jnp.sum(lq1_ref[...] * lk1_ref[...]))
           - jnp.exp(jnp.sum(lq2_ref[...] * lk2_ref[...])) + LAMBDA_INIT)

    def normalise(hd):
        l1 = jnp.sum(l_ref[hd, 0], axis=-1, keepdims=True)
        l2 = jnp.sum(l_ref[hd, 1], axis=-1, keepdims=True)
        out = acc_ref[hd, 0] * (1.0 / l1) - lam * (acc_ref[hd, 1] * (1.0 / l2))
        ms = jnp.mean(out * out, axis=-1, keepdims=True)
        out = out * lax.rsqrt(ms + RMS_EPS) * g_ref[...] * (1.0 - LAMBDA_INIT)
        o_ref[0, :, hcols[hd]] = out.astype(o_ref.dtype)

    for hd in heads:
        scores(hd, 0, 0, nb == 0)
    for i in range(nb + 1):
        if i < nb:
            for hd in heads:
                scores(hd, (i + 1) % 2, (i + 1) * TQ, i + 1 == nb)
        for hd in heads:
            if i < nb:
                row_shift = -slope2[hd] * (row + float((nb - i) * TQ))
            else:
                row_shift = jnp.zeros_like(row)
            softmax_pv(hd, i % 2, i * TQ, row_shift)
            if i == nb:
                normalise(hd)


def _diff_attn(proj, slopes, lq1, lk1, lq2, lk2, diagb, g):
    b, s, _ = proj.shape
    nq = s // TQ
    wide = DIFF_HP * LANES
    lam_spec = pl.BlockSpec((1, HEAD_DIM), lambda bi, hi, qi: (0, 0))
    return pl.pallas_call(
        _diff_attn_kernel,
        grid=(b, N_DIFF_HEADS // DIFF_HP, nq),
        in_specs=[
            pl.BlockSpec(memory_space=pltpu.SMEM),
            lam_spec, lam_spec, lam_spec, lam_spec,
            pl.BlockSpec((1, TQ, wide), lambda bi, hi, qi: (bi, qi, QD_BLK // DIFF_HP + hi)),
            pl.BlockSpec((1, s, wide), lambda bi, hi, qi: (bi, 0, KD_BLK // DIFF_HP + hi)),
            pl.BlockSpec((1, s, wide), lambda bi, hi, qi: (bi, 0, VD_BLK // DIFF_HP + hi)),
            pl.BlockSpec((DIFF_HP, TQ, TQ), lambda bi, hi, qi: (hi, 0, 0)),
            pl.BlockSpec((1, DIFF_V_DIM), lambda bi, hi, qi: (0, 0)),
        ],
        out_specs=pl.BlockSpec((1, TQ, wide), lambda bi, hi, qi: (bi, qi, hi)),
        out_shape=jax.ShapeDtypeStruct((b, s, N_DIFF_HEADS * DIFF_V_DIM), BF16),
        scratch_shapes=[pltpu.VMEM((DIFF_HP, 2, TQ, LANES), F32),
                        pltpu.VMEM((DIFF_HP, 2, TQ, LANES), F32),
                        pltpu.VMEM((DIFF_HP, 2, TQ, LANES), F32),
                        pltpu.VMEM((DIFF_HP, 2, 2, TQ, TQ), F32),
                        pltpu.VMEM((DIFF_HP, 2, 2, TQ, LANES), F32)],
        compiler_params=pltpu.CompilerParams(
            dimension_semantics=("arbitrary", "arbitrary", "arbitrary"),
            vmem_limit_bytes=VMEM_LIMIT),
        name="diff_attn",
    )(slopes, lq1, lk1, lq2, lk2, proj, proj, proj, diagb, g)


def _sb_attn_kernel(q_ref, k_ref, v_ref, tri_ref, g_ref, o_ref, acc_ref, carry_ref, reach_ref):
    qi = pl.program_id(2)
    pairs = range(SB_PP)
    pcols = [slice(pp * LANES, (pp + 1) * LANES) for pp in pairs]
    lane = lax.broadcasted_iota(jnp.int32, (TQ, LANES), 1)
    lo_half = lane < HEAD_DIM
    row = lax.broadcasted_iota(jnp.int32, (TQ, SB_W), 0)
    colk = lax.broadcasted_iota(jnp.int32, (TQ, SB_W), 1)
    strict = colk < row
    t0 = pl.multiple_of(qi * TQ, TQ)
    tri = tri_ref[...]

    def masked_queries():
        qm = []
        for pp in pairs:
            qf = q_ref[0, :, pcols[pp]].astype(F32) * (HEAD_DIM ** -0.5 * LOG2E)
            qm.append((jnp.where(lo_half, qf, 0.0).astype(BF16),
                       jnp.where(lo_half, 0.0, qf).astype(BF16)))
        return qm

    def scores(qm, pp, kstart, nkeys, rows=slice(0, TQ)):
        kb = k_ref[0, pl.ds(kstart, nkeys), pcols[pp]]
        return [_dot_nt(qm[pp][hh][rows], kb) for hh in range(2)]

    def softplus_split(zs, nsub, own):
        z, x = {}, {}
        for hh in range(2):
            for u in reversed(range(nsub)):
                r0 = u * SB_W if own else 0
                zc = zs[hh][r0:, u * SB_W:(u + 1) * SB_W]
                sp = jnp.maximum(zc, 0.0) + jnp.log(1.0 + jnp.exp2(-jnp.abs(zc))) * LOG2E
                if own:
                    sp = jnp.where(strict[:TQ - r0], sp, 0.0)
                z[hh, u] = zc
                x[hh, u] = _hi_lo(sp)
        return z, x

    def suffix_sums(x):
        return {key: _dot(val, tri) for key, val in x.items()}

    def add_rows(full, lo, hi, delta):
        parts = [full[:lo]] if lo else []
        parts.append(full[lo:hi] + delta)
        if hi < full.shape[0]:
            parts.append(full[hi:])
        return parts[0] if len(parts) == 1 else jnp.concatenate(parts, axis=0)

    def weights_pv(pp, z, r, nsub, own, kstart, carry, acc, rows=slice(0, TQ)):
        vb = v_ref[0, pl.ds(kstart, nsub * SB_W), pcols[pp]]
        lo, hi = rows.start, rows.stop
        for hh in range(2):
            a = {}
            for u in reversed(range(nsub)):
                r0 = u * SB_W if own else lo
                av = jnp.exp2(z[hh, u] - r[hh, u][:, :SB_W] - carry[hh][r0:hi])
                if own:
                    av = jnp.where(strict[:TQ - r0], av, 0.0)
                avb = av.astype(BF16)
                if own and r0:
                    avb = jnp.concatenate([jnp.zeros((r0, SB_W), BF16), avb], axis=0)
                a[u] = avb
                carry[hh] = add_rows(carry[hh], r0, hi, r[hh, u][:, SB_W:])
            pv = _dot(jnp.concatenate([a[u] for u in range(nsub)], axis=1), vb)
            acc[hh] = add_rows(acc[hh], lo, hi, pv)

    def park(carry, acc):
        reach = None
        for pp in pairs:
            for hh in range(2):
                carry_ref[pp, hh] = carry[pp][hh]
                acc_ref[pp, hh] = acc[pp][hh]
                reach = carry[pp][hh] if reach is None else jnp.minimum(reach, carry[pp][hh])
        return reach

    def reload():
        return ([[carry_ref[pp, 0], carry_ref[pp, 1]] for pp in pairs],
                [[acc_ref[pp, 0], acc_ref[pp, 1]] for pp in pairs])

    def normalise(pp, acc):
        out = jnp.where(lo_half, acc[0], acc[1])
        sq = out * out
        ss_lo = jnp.sum(jnp.where(lo_half, sq, 0.0), axis=-1, keepdims=True)
        ss_hi = jnp.sum(jnp.where(lo_half, 0.0, sq), axis=-1, keepdims=True)
        ms = jnp.where(lo_half, ss_lo, ss_hi) * (1.0 / HEAD_DIM)
        o_ref[0, :, pcols[pp]] = (out * lax.rsqrt(ms + RMS_EPS) * g_ref[pp]).astype(o_ref.dtype)

    def near_keys(with_previous):
        qm = masked_queries()
        nsub1 = SB_NEAR // SB_W
        top = slice(0, SB_TOP)
        prev = pl.multiple_of(t0 - SB_NEAR, SB_NEAR)
        carry = [[jnp.zeros((TQ, SB_W), F32), jnp.zeros((TQ, SB_W), F32)] for _ in pairs]
        acc = [[jnp.zeros((TQ, LANES), F32), jnp.zeros((TQ, LANES), F32)] for _ in pairs]
        zx0 = [softplus_split(scores(qm, pp, t0, TQ), SB_U, True) for pp in pairs]
        if with_previous:
            zs1 = [scores(qm, pp, prev, SB_NEAR, top) for pp in pairs]
        r0 = [suffix_sums(zx0[pp][1]) for pp in pairs]
        if with_previous:
            zx1 = [softplus_split(zs1[pp], nsub1, False) for pp in pairs]
        for pp in pairs:
            weights_pv(pp, zx0[pp][0], r0[pp], SB_U, True, t0, carry[pp], acc[pp])
        if with_previous:
            r1 = [suffix_sums(zx1[pp][1]) for pp in pairs]
            for pp in pairs:
                weights_pv(pp, zx1[pp][0], r1[pp], nsub1, False, prev, carry[pp], acc[pp], top)
        reach = park(carry, acc)
        reach_ref[0] = jnp.min(reach)
        reach_ref[1] = jnp.min(reach[SB_TOP:])
        reach_ref[2] = 0.0
        for pp in pairs:
            normalise(pp, acc[pp])

    pl.when(qi == 0)(lambda: near_keys(False))
    pl.when(qi > 0)(lambda: near_keys(True))

    def chunk(kstart, rows):
        qm = masked_queries()
        nsub = SB_NEAR // SB_W
        zx = [softplus_split(scores(qm, pp, kstart, SB_NEAR, rows), nsub, False) for pp in pairs]
        r = [suffix_sums(zx[pp][1]) for pp in pairs]
        carry, acc = reload()
        for pp in pairs:
            weights_pv(pp, zx[pp][0], r[pp], nsub, False, kstart, carry[pp], acc[pp], rows)
        reach_ref[0] = jnp.min(park(carry, acc))
        reach_ref[2] = 1.0

    @pl.when(jnp.logical_and(qi > 0, reach_ref[1] < SB_EXIT))
    def _():
        chunk(pl.multiple_of(t0 - SB_NEAR, SB_NEAR), slice(SB_TOP, TQ))

    n_far = qi * (TQ // SB_NEAR) - 1

    def far_chunk(c):
        chunk(pl.multiple_of(t0 - (c + 2) * SB_NEAR, SB_NEAR), slice(0, TQ))
        return c + 1

    lax.while_loop(lambda c: jnp.logical_and(c < n_far, reach_ref[0] < SB_EXIT), far_chunk, 0)

    @pl.when(reach_ref[2] > 0.5)
    def _():
        for pp in pairs:
            normalise(pp, [acc_ref[pp, 0], acc_ref[pp, 1]])


def _sb_attn(proj, tri, g):
    b, s, _ = proj.shape
    nq = s // TQ
    npair = N_SB_HEADS // 2
    wide = SB_PP * LANES
    return pl.pallas_call(
        _sb_attn_kernel,
        grid=(b, npair // SB_PP, nq),
        in_specs=[
            pl.BlockSpec((1, TQ, wide), lambda bi, pi, qi: (bi, qi, QS_BLK // SB_PP + pi)),
            pl.BlockSpec((1, s, wide), lambda bi, pi, qi: (bi, 0, KS_BLK // SB_PP + pi)),
            pl.BlockSpec((1, s, wide), lambda bi, pi, qi: (bi, 0, VS_BLK // SB_PP + pi)),
            pl.BlockSpec((2 * SB_W, 2 * SB_W), lambda bi, pi, qi: (0, 0)),
            pl.BlockSpec((SB_PP, 1, LANES), lambda bi, pi, qi: (pi, 0, 0)),
        ],
        out_specs=pl.BlockSpec((1, TQ, wide), lambda bi, pi, qi: (bi, qi, pi)),
        out_shape=jax.ShapeDtypeStruct((b, s, N_SB_HEADS * HEAD_DIM), BF16),
        scratch_shapes=[pltpu.VMEM((SB_PP, 2, TQ, LANES), F32), pltpu.VMEM((SB_PP, 2, TQ, SB_W), F32),
                        pltpu.SMEM((3,), F32)],
        compiler_params=pltpu.CompilerParams(
            dimension_semantics=("arbitrary", "arbitrary", "arbitrary"),
            vmem_limit_bytes=VMEM_LIMIT),
        name="sb_attn",
    )(proj, proj, proj, tri, g)


def _xkv_kernel(mem_ref, w_ref, o_ref):
    o_ref[0] = _dot(mem_ref[0], w_ref[...]).astype(o_ref.dtype)


def _xkv_proj(mem, w):
    b, n, d = mem.shape
    return pl.pallas_call(
        _xkv_kernel,
        grid=(b,),
        in_specs=[pl.BlockSpec((1, n, d), lambda i: (i, 0, 0)),
                  pl.BlockSpec((d, 2 * d), lambda i: (0, 0), pipeline_mode=pl.Buffered(1))],
        out_specs=pl.BlockSpec((1, n, 2 * d), lambda i: (i, 0, 0)),
        out_shape=jax.ShapeDtypeStruct((b, n, 2 * d), BF16),
        compiler_params=pltpu.CompilerParams(dimension_semantics=("arbitrary",),
                                             vmem_limit_bytes=VMEM_LIMIT),
        name="xkv_proj",
    )(mem, w)


def _tail_kernel(yd_ref, ys_ref, x_ref, kv_ref, wo_ref, g1_ref, b1_ref, wq_ref, wxo_ref,
                 g2_ref, b2_ref, wg_ref, wu_ref, wd_ref, g3_ref, b3_ref, o_ref):
    subs = [slice(t * TM_ROW, (t + 1) * TM_ROW) for t in range(N_SUB)]
    hcols = [slice(h * XHEAD_DIM, (h + 1) * XHEAD_DIM) for h in range(N_XHEADS)]

    def query(x1):
        return (_dot(x1.astype(BF16), wq_ref[...]) * (XHEAD_DIM ** -0.5)).astype(BF16)

    def scores(q):
        return [_dot_nt(q[:, hc], kv_ref[0, :, hc]) for hc in hcols]

    def attend(sc):
        ps = [jnp.exp(sh - jnp.max(sh, axis=-1, keepdims=True)) for sh in sc]
        inv = [1.0 / jnp.sum(p, axis=-1, keepdims=True) for p in ps]
        pv = [_dot(p.astype(BF16), kv_ref[0, :, D_MODEL + h * XHEAD_DIM:D_MODEL + (h + 1) * XHEAD_DIM])
              for h, p in enumerate(ps)]
        return jnp.concatenate([(o * i).astype(BF16) for o, i in zip(pv, inv)], axis=1)

    def out_proj(x1, heads):
        return _layer_norm(ALPHA * x1 + _dot(heads, wxo_ref[...]), g2_ref[...], b2_ref[...])

    def gate_up(x2):
        xb = x2.astype(BF16)
        gate, up = _dot(xb, wg_ref[...]), _dot(xb, wu_ref[...])
        return (gate * jax.nn.sigmoid(gate) * up).astype(BF16)

    a, b = subs
    half = yd_ref.shape[1]
    mix = [_dot(yd_ref[r], wo_ref[:half, :]) + _dot(ys_ref[r], wo_ref[half:, :]) for r in subs]
    x1 = [_layer_norm(ALPHA * x_ref[r] + m, g1_ref[...], b1_ref[...]) for r, m in zip(subs, mix)]
    q_a = query(x1[0])
    sc_a = scores(q_a)
    q_b = query(x1[1])
    heads_a = attend(sc_a)
    sc_b = scores(q_b)
    x2_a = out_proj(x1[0], heads_a)
    heads_b = attend(sc_b)
    x2_b = out_proj(x1[1], heads_b)
    hidden_a = gate_up(x2_a)
    hidden_b = gate_up(x2_b)
    o_ref[a] = _layer_norm(ALPHA * x2_a + _dot(hidden_a, wd_ref[...]), g3_ref[...], b3_ref[...])
    o_ref[b] = _layer_norm(ALPHA * x2_b + _dot(hidden_b, wd_ref[...]), g3_ref[...], b3_ref[...])


def _tail(yd, ys, x2d, kv, wo, g1, b1, wq, wxo, g2, b2, wg, wu, wd, g3, b3, seq):
    t, d = x2d.shape
    half = yd.shape[1]
    n = kv.shape[1]
    f = wg.shape[1]
    tm = N_SUB * TM_ROW
    per_batch = seq // tm
    row = lambda i: (i, 0)
    const = lambda i: (0, 0)
    resident = lambda shape: pl.BlockSpec(shape, const, pipeline_mode=pl.Buffered(1))
    return pl.pallas_call(
        _tail_kernel,
        grid=(t // tm,),
        in_specs=[pl.BlockSpec((tm, half), row), pl.BlockSpec((tm, half), row),
                  pl.BlockSpec((tm, d), row),
                  pl.BlockSpec((1, n, 2 * d), lambda i: (i // per_batch, 0, 0)),
                  resident((d, d)), resident((1, d)), resident((1, d)),
                  resident((d, d)), resident((d, d)), resident((1, d)), resident((1, d)),
                  resident((d, f)), resident((d, f)), resident((f, d)),
                  resident((1, d)), resident((1, d))],
        out_specs=pl.BlockSpec((tm, d), row),
        out_shape=jax.ShapeDtypeStruct((t, d), F32),
        compiler_params=pltpu.CompilerParams(dimension_semantics=("arbitrary",),
                                             vmem_limit_bytes=VMEM_LIMIT),
        name="tail",
    )(yd, ys, x2d, kv, wo, g1, b1, wq, wxo, g2, b2, wg, wu, wd, g3, b3)


def _alibi_diag_table(slopes):
    i = jnp.arange(TQ, dtype=jnp.int32)[:, None]
    j = jnp.arange(TQ, dtype=jnp.int32)[None, :]
    dist = jnp.abs(i - j).astype(F32)
    allowed = (j // CHUNK) <= (i // CHUNK)
    return jnp.where(allowed[None], -(slopes * LOG2E)[:, None, None] * dist[None], NEG_INF)


def _suffix_sum_matrix():
    j = jnp.arange(2 * SB_W, dtype=jnp.int32)[:, None] % SB_W
    c = jnp.arange(2 * SB_W, dtype=jnp.int32)[None, :]
    return jnp.where((c >= SB_W) | (j >= c), 1.0, 0.0).astype(BF16)


def kernel(x, mem, w_in, diff_lambda_q1, diff_lambda_k1, diff_lambda_q2, diff_lambda_k2,
           diff_subln_g, sb_norm_g, w_o, ln1_g, ln1_b, w_xq, w_xkv, w_xo, ln2_g, ln2_b,
           w_gate, w_up, w_down, ln3_g, ln3_b):
    b, s, d = x.shape
    assert (b, s, d) == (8, SEQ, D_MODEL) and w_in.shape == (DEPTH, D_MODEL, D_IN)
    x2d = x.reshape(b * s, d)
    slopes = jnp.exp2(-8.0 * jnp.arange(1, N_DIFF_HEADS + 1, dtype=F32) / N_DIFF_HEADS)
    diagb = _alibi_diag_table(slopes)
    tri = _suffix_sum_matrix()
    vec = lambda a: a[0].reshape(1, -1)

    proj, (wo16, wq16, wxo16, wg16, wu16, wd16) = _in_proj(
        x2d, w_in[0], [w_o[0], w_xq[0], w_xo[0], w_gate[0], w_up[0], w_down[0]])
    proj = proj.reshape(b, s, D_IN)
    y_diff = _diff_attn(proj, slopes, vec(diff_lambda_q1), vec(diff_lambda_k1),
                        vec(diff_lambda_q2), vec(diff_lambda_k2), diagb, vec(diff_subln_g))
    y_sb = _sb_attn(proj, tri, sb_norm_g[0].reshape(N_SB_HEADS // 2, 1, LANES))

    half = N_DIFF_HEADS * DIFF_V_DIM
    kv = _xkv_proj(mem, w_xkv[0])
    out = _tail(y_diff.reshape(b * s, half), y_sb.reshape(b * s, -1), x2d, kv,
                wo16, vec(ln1_g), vec(ln1_b), wq16, wxo16, vec(ln2_g), vec(ln2_b),
                wg16, wu16, wd16, vec(ln3_g), vec(ln3_b), s)
    return out.reshape(b, s, d)
```

```python
import math

import jax
import jax.numpy as jnp
from jax import lax
from jax.experimental import pallas as pl
from jax.experimental.pallas import tpu as pltpu

D_MODEL = 1024
DEPTH = 1
CHUNK = 64
N_MEM = 256
HEAD_DIM = 64
N_DIFF_HEADS = 4
DIFF_V_DIM = 2 * HEAD_DIM
N_SB_HEADS = 8
N_XHEADS = 4
XHEAD_DIM = D_MODEL // N_XHEADS
D_FF = 2816
ALPHA = (2.0 * DEPTH) ** 0.25
LN_EPS = 1e-5
RMS_EPS = 1e-5
NEG_INF = -1e30
LAMBDA_INIT = 0.8 - 0.6 * math.exp(-0.3 * 0)
LOG2E = math.log2(math.e)

LANES = 128
BF16_SUBLANES = 16
QD_BLK, KD_BLK, VD_BLK, QS_BLK, KS_BLK, VS_BLK = 0, 4, 8, 12, 16, 20
D_IN = 24 * LANES

TM_PROJ = 512
TM_ROW = 256
N_SUB = 2
TQ = 512
SEQ = 2048
NQ = SEQ // TQ
DIFF_HP = 2
DIAG_HALF = TQ // 2
SB_W = 128
SB_U = TQ // SB_W
SB_NEAR = 256
SB_EXIT = 150.0
SB_TOP = 256
SB_PP = 2
VMEM_LIMIT = 56 * 1024 * 1024

BF16 = jnp.bfloat16
F32 = jnp.float32


def _dot(a, b):
    return jnp.dot(a, b, preferred_element_type=F32)


def _dot_nt(a, b):
    return lax.dot_general(a, b, (((1,), (1,)), ((), ())), preferred_element_type=F32)


def _layer_norm(v, g, b):
    mu = jnp.mean(v, axis=-1, keepdims=True)
    d = v - mu
    var = jnp.mean(d * d, axis=-1, keepdims=True)
    return d * lax.rsqrt(var + LN_EPS) * g + b


def _hi_lo(v):
    hi = v.astype(BF16)
    lo = (v - hi.astype(F32)).astype(BF16)
    return jnp.concatenate([hi, lo], axis=1)


def _in_proj_kernel(x_ref, w_ref, *refs):
    n = len(refs) // 2
    o_ref = refs[n]
    o_ref[...] = _dot(x_ref[...], w_ref[...]).astype(o_ref.dtype)
    for src, dst in zip(refs[:n], refs[n + 1:]):
        dst[...] = src[...].astype(dst.dtype)


def _in_proj(x2d, w, tail_weights):
    t, d = x2d.shape
    n = w.shape[1]
    steps = t // TM_PROJ
    w_specs, w_shapes = [], []
    for tw in tail_weights:
        rows, cols = tw.shape
        if rows % (BF16_SUBLANES * steps) == 0:
            blk, index = rows // steps, (lambda i: (i, 0))
        else:
            assert rows % (BF16_SUBLANES * steps // 2) == 0
            blk, index = 2 * rows // steps, (lambda i: (jnp.minimum(i, steps // 2 - 1), 0))
        w_specs.append(pl.BlockSpec((blk, cols), index))
        w_shapes.append(jax.ShapeDtypeStruct(tw.shape, BF16))
    outs = pl.pallas_call(
        _in_proj_kernel,
        grid=(steps,),
        in_specs=[pl.BlockSpec((TM_PROJ, d), lambda i: (i, 0)),
                  pl.BlockSpec((d, n), lambda i: (0, 0), pipeline_mode=pl.Buffered(1))] + w_specs,
        out_specs=[pl.BlockSpec((TM_PROJ, n), lambda i: (i, 0))] + w_specs,
        out_shape=[jax.ShapeDtypeStruct((t, n), BF16)] + w_shapes,
        compiler_params=pltpu.CompilerParams(dimension_semantics=("arbitrary",),
                                             vmem_limit_bytes=VMEM_LIMIT),
        name="in_proj",
    )(x2d, w, *tail_weights)
    return outs[0], outs[1:]


def _diff_attn_kernel(slopes_ref, lq1_ref, lk1_ref, lq2_ref, lk2_ref, q_ref, k_ref, v_ref,
                      diagb_ref, g_ref, o_ref, m_ref, l_ref, acc_ref, s_ref, mb_ref):
    hp = pl.program_id(1)
    qi = pl.program_id(2)
    for nb in range(NQ):
        pl.when(qi == nb)(lambda nb=nb: _diff_query_block(
            nb, hp, slopes_ref, lq1_ref, lk1_ref, lq2_ref, lk2_ref, q_ref, k_ref, v_ref,
            diagb_ref, g_ref, o_ref, m_ref, l_ref, acc_ref, s_ref, mb_ref))


def _diff_query_block(nb, hp, slopes_ref, lq1_ref, lk1_ref, lq2_ref, lk2_ref, q_ref, k_ref, v_ref,
                      diagb_ref, g_ref, o_ref, m_ref, l_ref, acc_ref, s_ref, mb_ref):
    heads = range(DIFF_HP)
    hcols = [slice(hd * LANES, (hd + 1) * LANES) for hd in heads]
    slope2 = [slopes_ref[hp * DIFF_HP + hd] * LOG2E for hd in heads]

    lane = lax.broadcasted_iota(jnp.int32, (TQ, LANES), 1)
    qm = []
    for hd in heads:
        qf = q_ref[0, :, hcols[hd]].astype(F32) * (HEAD_DIM ** -0.5 * LOG2E)
        qm.append((jnp.where(lane < HEAD_DIM, qf, 0.0).astype(BF16),
                   jnp.where(lane >= HEAD_DIM, qf, 0.0).astype(BF16)))

    m_ref[...] = jnp.full_like(m_ref, NEG_INF)
    l_ref[...] = jnp.zeros_like(l_ref)
    acc_ref[...] = jnp.zeros_like(acc_ref)

    col = lax.broadcasted_iota(jnp.int32, (1, TQ), 1).astype(F32)
    row = lax.broadcasted_iota(jnp.int32, (TQ, LANES), 0).astype(F32)

    full = slice(0, TQ)
    diag_parts = ((slice(0, DIAG_HALF), DIAG_HALF), (slice(DIAG_HALF, TQ), TQ))

    def scores(hd, slot, kstart, nkeys, rows, diagonal):
        kb = k_ref[0, kstart:kstart + nkeys, hcols[hd]]
        bias = diagb_ref[hd, rows, :nkeys] if diagonal else slope2[hd] * col
        for mp in range(2):
            s = _dot_nt(qm[hd][mp][rows], kb) + bias
            s_ref[hd, slot, mp, rows, :nkeys] = s
            mb_ref[hd, slot, mp, rows] = jnp.broadcast_to(jnp.max(s, axis=-1, keepdims=True),
                                                          (rows.stop - rows.start, LANES))

    def softmax_pv(hd, slot, kstart, nkeys, rows, row_shift):
        vb = v_ref[0, kstart:kstart + nkeys, hcols[hd]]
        for mp in range(2):
            m_old = m_ref[hd, mp, rows]
            mb = mb_ref[hd, slot, mp, rows]
            m_new = jnp.maximum(m_old, mb if row_shift is None else mb + row_shift)
            alpha = jnp.exp2(m_old - m_new)
            mrel = m_new if row_shift is None else m_new - row_shift
            ps = [jnp.exp2(s_ref[hd, slot, mp, rows, c * LANES:(c + 1) * LANES] - mrel)
                  for c in range(nkeys // LANES)]
            psum = ps[0]
            for pc in ps[1:]:
                psum = psum + pc
            l_ref[hd, mp, rows] = alpha * l_ref[hd, mp, rows] + psum
            p = jnp.concatenate([pc.astype(BF16) for pc in ps], axis=1)
            acc_ref[hd, mp, rows] = alpha * acc_ref[hd, mp, rows] + _dot(p, vb)
            m_ref[hd, mp, rows] = m_new

    def issue(hd, i):
        if i < nb:
            scores(hd, i % 2, i * TQ, TQ, full, False)
        else:
            for rows, nkeys in diag_parts:
                scores(hd, i % 2, nb * TQ, nkeys, rows, True)

    def consume(hd, i):
        if i < nb:
            softmax_pv(hd, i % 2, i * TQ, TQ, full, -slope2[hd] * (row + float((nb - i) * TQ)))
        else:
            for rows, nkeys in diag_parts:
                softmax_pv(hd, i % 2, nb * TQ, nkeys, rows, None)

    lam = (jnp.exp(jnp.sum(lq1_ref[...] * lk1_ref[...]))
           - jnp.exp(jnp.sum(lq2_ref[...] * lk2_ref[...])) + LAMBDA_INIT)

    def normalise(hd):
        l1 = jnp.sum(l_ref[hd, 0], axis=-1, keepdims=True)
        l2 = jnp.sum(l_ref[hd, 1], axis=-1, keepdims=True)
        out = acc_ref[hd, 0] * (1.0 / l1) - lam * (acc_ref[hd, 1] * (1.0 / l2))
        ms = jnp.mean(out * out, axis=-1, keepdims=True)
        out = out * lax.rsqrt(ms + RMS_EPS) * g_ref[...] * (1.0 - LAMBDA_INIT)
        o_ref[0, :, hcols[hd]] = out.astype(o_ref.dtype)

    for hd in heads:
        issue(hd, 0)
    for i in range(nb + 1):
        if i < nb:
            for hd in heads:
                issue(hd, i + 1)
        for hd in heads:
            consume(hd, i)
            if i == nb:
                normalise(hd)


def _diff_attn(proj, slopes, lq1, lk1, lq2, lk2, diagb, g):
    b, s, _ = proj.shape
    nq = s // TQ
    wide = DIFF_HP * LANES
    lam_spec = pl.BlockSpec((1, HEAD_DIM), lambda bi, hi, qi: (0, 0))
    return pl.pallas_call(
        _diff_attn_kernel,
        grid=(b, N_DIFF_HEADS // DIFF_HP, nq),
        in_specs=[
            pl.BlockSpec(memory_space=pltpu.SMEM),
            lam_spec, lam_spec, lam_spec, lam_spec,
            pl.BlockSpec((1, TQ, wide), lambda bi, hi, qi: (bi, qi, QD_BLK // DIFF_HP + hi)),
            pl.BlockSpec((1, s, wide), lambda bi, hi, qi: (bi, 0, KD_BLK // DIFF_HP + hi)),
            pl.BlockSpec((1, s, wide), lambda bi, hi, qi: (bi, 0, VD_BLK // DIFF_HP + hi)),
            pl.BlockSpec((DIFF_HP, TQ, TQ), lambda bi, hi, qi: (hi, 0, 0)),
            pl.BlockSpec((1, DIFF_V_DIM), lambda bi, hi, qi: (0, 0)),
        ],
        out_specs=pl.BlockSpec((1, TQ, wide), lambda bi, hi, qi: (bi, qi, hi)),
        out_shape=jax.ShapeDtypeStruct((b, s, N_DIFF_HEADS * DIFF_V_DIM), BF16),
        scratch_shapes=[pltpu.VMEM((DIFF_HP, 2, TQ, LANES), F32),
                        pltpu.VMEM((DIFF_HP, 2, TQ, LANES), F32),
                        pltpu.VMEM((DIFF_HP, 2, TQ, LANES), F32),
                        pltpu.VMEM((DIFF_HP, 2, 2, TQ, TQ), F32),
                        pltpu.VMEM((DIFF_HP, 2, 2, TQ, LANES), F32)],
        compiler_params=pltpu.CompilerParams(
            dimension_semantics=("arbitrary", "arbitrary", "arbitrary"),
            vmem_limit_bytes=VMEM_LIMIT),
        name="diff_attn",
    )(slopes, lq1, lk1, lq2, lk2, proj, proj, proj, diagb, g)


def _sb_attn_kernel(q_ref, k_ref, v_ref, tri_ref, g_ref, o_ref, acc_ref, carry_ref, reach_ref):
    qi = pl.program_id(2)
    pairs = range(SB_PP)
    pcols = [slice(pp * LANES, (pp + 1) * LANES) for pp in pairs]
    lane = lax.broadcasted_iota(jnp.int32, (TQ, LANES), 1)
    lo_half = lane < HEAD_DIM
    row = lax.broadcasted_iota(jnp.int32, (TQ, SB_W), 0)
    colk = lax.broadcasted_iota(jnp.int32, (TQ, SB_W), 1)
    strict = colk < row
    t0 = pl.multiple_of(qi * TQ, TQ)
    tri = tri_ref[...]

    def masked_queries():
        qm = []
        for pp in pairs:
            qf = q_ref[0, :, pcols[pp]].astype(F32) * (HEAD_DIM ** -0.5 * LOG2E)
            qm.append((jnp.where(lo_half, qf, 0.0).astype(BF16),
                       jnp.where(lo_half, 0.0, qf).astype(BF16)))
        return qm

    def scores(qm, pp, kstart, nkeys, rows=slice(0, TQ)):
        kb = k_ref[0, pl.ds(kstart, nkeys), pcols[pp]]
        return [_dot_nt(qm[pp][hh][rows], kb) for hh in range(2)]

    def softplus_split(zs, nsub, own):
        z, x = {}, {}
        for hh in range(2):
            for u in reversed(range(nsub)):
                r0 = u * SB_W if own else 0
                zc = zs[hh][r0:, u * SB_W:(u + 1) * SB_W]
                sp = jnp.maximum(zc, 0.0) + jnp.log(1.0 + jnp.exp2(-jnp.abs(zc))) * LOG2E
                if own:
                    sp = jnp.where(strict[:TQ - r0], sp, 0.0)
                z[hh, u] = zc
                x[hh, u] = _hi_lo(sp)
        return z, x

    def suffix_sums(x):
        return {key: _dot(val, tri) for key, val in x.items()}

    def add_rows(full, lo, hi, delta):
        parts = [full[:lo]] if lo else []
        parts.append(full[lo:hi] + delta)
        if hi < full.shape[0]:
            parts.append(full[hi:])
        return parts[0] if len(parts) == 1 else jnp.concatenate(parts, axis=0)

    def weights_pv(pp, z, r, nsub, own, kstart, carry, acc, rows=slice(0, TQ)):
        vb = v_ref[0, pl.ds(kstart, nsub * SB_W), pcols[pp]]
        lo, hi = rows.start, rows.stop
        for hh in range(2):
            a = {}
            for u in reversed(range(nsub)):
                r0 = u * SB_W if own else lo
                av = jnp.exp2(z[hh, u] - r[hh, u][:, :SB_W] - carry[hh][r0:hi])
                if own:
                    av = jnp.where(strict[:TQ - r0], av, 0.0)
                avb = av.astype(BF16)
                if own and r0:
                    avb = jnp.concatenate([jnp.zeros((r0, SB_W), BF16), avb], axis=0)
                a[u] = avb
                carry[hh] = add_rows(carry[hh], r0, hi, r[hh, u][:, SB_W:])
            pv = _dot(jnp.concatenate([a[u] for u in range(nsub)], axis=1), vb)
            acc[hh] = add_rows(acc[hh], lo, hi, pv)

    def park(carry, acc):
        reach = None
        for pp in pairs:
            for hh in range(2):
                carry_ref[pp, hh] = carry[pp][hh]
                acc_ref[pp, hh] = acc[pp][hh]
                reach = carry[pp][hh] if reach is None else jnp.minimum(reach, carry[pp][hh])
        return reach

    def reload():
        return ([[carry_ref[pp, 0], carry_ref[pp, 1]] for pp in pairs],
                [[acc_ref[pp, 0], acc_ref[pp, 1]] for pp in pairs])

    def normalise(pp, acc):
        out = jnp.where(lo_half, acc[0], acc[1])
        sq = out * out
        ss_lo = jnp.sum(jnp.where(lo_half, sq, 0.0), axis=-1, keepdims=True)
        ss_hi = jnp.sum(jnp.where(lo_half, 0.0, sq), axis=-1, keepdims=True)
        ms = jnp.where(lo_half, ss_lo, ss_hi) * (1.0 / HEAD_DIM)
        o_ref[0, :, pcols[pp]] = (out * lax.rsqrt(ms + RMS_EPS) * g_ref[pp]).astype(o_ref.dtype)

    def near_keys(with_previous):
        qm = masked_queries()
        nsub1 = SB_NEAR // SB_W
        top = slice(0, SB_TOP)
        prev = pl.multiple_of(t0 - SB_NEAR, SB_NEAR)
        carry = [[jnp.zeros((TQ, SB_W), F32), jnp.zeros((TQ, SB_W), F32)] for _ in pairs]
        acc = [[jnp.zeros((TQ, LANES), F32), jnp.zeros((TQ, LANES), F32)] for _ in pairs]
        zx0 = [softplus_split(scores(qm, pp, t0, TQ), SB_U, True) for pp in pairs]
        if with_previous:
            zs1 = [scores(qm, pp, prev, SB_NEAR, top) for pp in pairs]
        r0 = [suffix_sums(zx0[pp][1]) for pp in pairs]
        if with_previous:
            zx1 = [softplus_split(zs1[pp], nsub1, False) for pp in pairs]
        for pp in pairs:
            weights_pv(pp, zx0[pp][0], r0[pp], SB_U, True, t0, carry[pp], acc[pp])
        if with_previous:
            r1 = [suffix_sums(zx1[pp][1]) for pp in pairs]
            for pp in pairs:
                weights_pv(pp, zx1[pp][0], r1[pp], nsub1, False, prev, carry[pp], acc[pp], top)
        reach = park(carry, acc)
        reach_ref[0] = jnp.min(reach)
        reach_ref[1] = jnp.min(reach[SB_TOP:])
        reach_ref[2] = 0.0
        for pp in pairs:
            normalise(pp, acc[pp])

    pl.when(qi == 0)(lambda: near_keys(False))
    pl.when(qi > 0)(lambda: near_keys(True))

    def chunk(kstart, rows):
        qm = masked_queries()
        nsub = SB_NEAR // SB_W
        zx = [softplus_split(scores(qm, pp, kstart, SB_NEAR, rows), nsub, False) for pp in pairs]
        r = [suffix_sums(zx[pp][1]) for pp in pairs]
        carry, acc = reload()
        for pp in pairs:
            weights_pv(pp, zx[pp][0], r[pp], nsub, False, kstart, carry[pp], acc[pp], rows)
        reach_ref[0] = jnp.min(park(carry, acc))
        reach_ref[2] = 1.0

    @pl.when(jnp.logical_and(qi > 0, reach_ref[1] < SB_EXIT))
    def _():
        chunk(pl.multiple_of(t0 - SB_NEAR, SB_NEAR), slice(SB_TOP, TQ))

    n_far = qi * (TQ // SB_NEAR) - 1

    def far_chunk(c):
        chunk(pl.multiple_of(t0 - (c + 2) * SB_NEAR, SB_NEAR), slice(0, TQ))
        return c + 1

    lax.while_loop(lambda c: jnp.logical_and(c < n_far, reach_ref[0] < SB_EXIT), far_chunk, 0)

    @pl.when(reach_ref[2] > 0.5)
    def _():
        for pp in pairs:
            normalise(pp, [acc_ref[pp, 0], acc_ref[pp, 1]])


def _sb_attn(proj, tri, g):
    b, s, _ = proj.shape
    nq = s // TQ
    npair = N_SB_HEADS // 2
    wide = SB_PP * LANES
    return pl.pallas_call(
        _sb_attn_kernel,
        grid=(b, npair // SB_PP, nq),
        in_specs=[
            pl.BlockSpec((1, TQ, wide), lambda bi, pi, qi: (bi, qi, QS_BLK // SB_PP + pi)),
            pl.BlockSpec((1, s, wide), lambda bi, pi, qi: (bi, 0, KS_BLK // SB_PP + pi)),
            pl.BlockSpec((1, s, wide), lambda bi, pi, qi: (bi, 0, VS_BLK // SB_PP + pi)),
            pl.BlockSpec((2 * SB_W, 2 * SB_W), lambda bi, pi, qi: (0, 0)),
            pl.BlockSpec((SB_PP, 1, LANES), lambda bi, pi, qi: (pi, 0, 0)),
        ],
        out_specs=pl.BlockSpec((1, TQ, wide), lambda bi, pi, qi: (bi, qi, pi)),
        out_shape=jax.ShapeDtypeStruct((b, s, N_SB_HEADS * HEAD_DIM), BF16),
        scratch_shapes=[pltpu.VMEM((SB_PP, 2, TQ, LANES), F32), pltpu.VMEM((SB_PP, 2, TQ, SB_W), F32),
                        pltpu.SMEM((3,), F32)],
        compiler_params=pltpu.CompilerParams(
            dimension_semantics=("arbitrary", "arbitrary", "arbitrary"),
            vmem_limit_bytes=VMEM_LIMIT),
        name="sb_attn",
    )(proj, proj, proj, tri, g)


def _xkv_kernel(mem_ref, w_ref, o_ref):
    o_ref[0] = _dot(mem_ref[0], w_ref[...]).astype(o_ref.dtype)


def _xkv_proj(mem, w):
    b, n, d = mem.shape
    return pl.pallas_call(
        _xkv_kernel,
        grid=(b,),
        in_specs=[pl.BlockSpec((1, n, d), lambda i: (i, 0, 0)),
                  pl.BlockSpec((d, 2 * d), lambda i: (0, 0), pipeline_mode=pl.Buffered(1))],
        out_specs=pl.BlockSpec((1, n, 2 * d), lambda i: (i, 0, 0)),
        out_shape=jax.ShapeDtypeStruct((b, n, 2 * d), BF16),
        compiler_params=pltpu.CompilerParams(dimension_semantics=("arbitrary",),
                                             vmem_limit_bytes=VMEM_LIMIT),
        name="xkv_proj",
    )(mem, w)


def _tail_kernel(yd_ref, ys_ref, x_ref, kv_ref, wo_ref, g1_ref, b1_ref, wq_ref, wxo_ref,
                 g2_ref, b2_ref, wg_ref, wu_ref, wd_ref, g3_ref, b3_ref, o_ref):
    subs = [slice(t * TM_ROW, (t + 1) * TM_ROW) for t in range(N_SUB)]
    hcols = [slice(h * XHEAD_DIM, (h + 1) * XHEAD_DIM) for h in range(N_XHEADS)]

    def query(x1):
        return (_dot(x1.astype(BF16), wq_ref[...]) * (XHEAD_DIM ** -0.5)).astype(BF16)

    def scores(q):
        return [_dot_nt(q[:, hc], kv_ref[0, :, hc]) for hc in hcols]

    def attend(sc):
        ps = [jnp.exp(sh - jnp.max(sh, axis=-1, keepdims=True)) for sh in sc]
        inv = [1.0 / jnp.sum(p, axis=-1, keepdims=True) for p in ps]
        pv = [_dot(p.astype(BF16), kv_ref[0, :, D_MODEL + h * XHEAD_DIM:D_MODEL + (h + 1) * XHEAD_DIM])
              for h, p in enumerate(ps)]
        return jnp.concatenate([(o * i).astype(BF16) for o, i in zip(pv, inv)], axis=1)

    def out_proj(x1, heads):
        return _layer_norm(ALPHA * x1 + _dot(heads, wxo_ref[...]), g2_ref[...], b2_ref[...])

    def gate_up(x2):
        xb = x2.astype(BF16)
        gate, up = _dot(xb, wg_ref[...]), _dot(xb, wu_ref[...])
        return (gate * jax.nn.sigmoid(gate) * up).astype(BF16)

    a, b = subs
    half = yd_ref.shape[1]
    mix = [_dot(yd_ref[r], wo_ref[:half, :]) + _dot(ys_ref[r], wo_ref[half:, :]) for r in subs]
    x1 = [_layer_norm(ALPHA * x_ref[r] + m, g1_ref[...], b1_ref[...]) for r, m in zip(subs, mix)]
    q_a = query(x1[0])
    sc_a = scores(q_a)
    q_b = query(x1[1])
    heads_a = attend(sc_a)
    sc_b = scores(q_b)
    x2_a = out_proj(x1[0], heads_a)
    heads_b = attend(sc_b)
    x2_b = out_proj(x1[1], heads_b)
    hidden_a = gate_up(x2_a)
    hidden_b = gate_up(x2_b)
    o_ref[a] = _layer_norm(ALPHA * x2_a + _dot(hidden_a, wd_ref[...]), g3_ref[...], b3_ref[...])
    o_ref[b] = _layer_norm(ALPHA * x2_b + _dot(hidden_b, wd_ref[...]), g3_ref[...], b3_ref[...])


def _tail(yd, ys, x2d, kv, wo, g1, b1, wq, wxo, g2, b2, wg, wu, wd, g3, b3, seq):
    t, d = x2d.shape
    half = yd.shape[1]
    n = kv.shape[1]
    f = wg.shape[1]
    tm = N_SUB * TM_ROW
    per_batch = seq // tm
    row = lambda i: (i, 0)
    const = lambda i: (0, 0)
    resident = lambda shape: pl.BlockSpec(shape, const, pipeline_mode=pl.Buffered(1))
    return pl.pallas_call(
        _tail_kernel,
        grid=(t // tm,),
        in_specs=[pl.BlockSpec((tm, half), row), pl.BlockSpec((tm, half), row),
                  pl.BlockSpec((tm, d), row),
                  pl.BlockSpec((1, n, 2 * d), lambda i: (i // per_batch, 0, 0)),
                  resident((d, d)), resident((1, d)), resident((1, d)),
                  resident((d, d)), resident((d, d)), resident((1, d)), resident((1, d)),
                  resident((d, f)), resident((d, f)), resident((f, d)),
                  resident((1, d)), resident((1, d))],
        out_specs=pl.BlockSpec((tm, d), row),
        out_shape=jax.ShapeDtypeStruct((t, d), F32),
        compiler_params=pltpu.CompilerParams(dimension_semantics=("arbitrary",),
                                             vmem_limit_bytes=VMEM_LIMIT),
        name="tail",
    )(yd, ys, x2d, kv, wo, g1, b1, wq, wxo, g2, b2, wg, wu, wd, g3, b3)


def _alibi_diag_table(slopes):
    i = jnp.arange(TQ, dtype=jnp.int32)[:, None]
    j = jnp.arange(TQ, dtype=jnp.int32)[None, :]
    dist = jnp.abs(i - j).astype(F32)
    allowed = (j // CHUNK) <= (i // CHUNK)
    return jnp.where(allowed[None], -(slopes * LOG2E)[:, None, None] * dist[None], NEG_INF)


def _suffix_sum_matrix():
    j = jnp.arange(2 * SB_W, dtype=jnp.int32)[:, None] % SB_W
    c = jnp.arange(2 * SB_W, dtype=jnp.int32)[None, :]
    return jnp.where((c >= SB_W) | (j >= c), 1.0, 0.0).astype(BF16)


def kernel(x, mem, w_in, diff_lambda_q1, diff_lambda_k1, diff_lambda_q2, diff_lambda_k2,
           diff_subln_g, sb_norm_g, w_o, ln1_g, ln1_b, w_xq, w_xkv, w_xo, ln2_g, ln2_b,
           w_gate, w_up, w_down, ln3_g, ln3_b):
    b, s, d = x.shape
    assert (b, s, d) == (8, SEQ, D_MODEL) and w_in.shape == (DEPTH, D_MODEL, D_IN)
    x2d = x.reshape(b * s, d)
    slopes = jnp.exp2(-8.0 * jnp.arange(1, N_DIFF_HEADS + 1, dtype=F32) / N_DIFF_HEADS)
    diagb = _alibi_diag_table(slopes)
    tri = _suffix_sum_matrix()
    vec = lambda a: a[0].reshape(1, -1)

    proj, (wo16, wq16, wxo16, wg16, wu16, wd16) = _in_proj(
        x2d, w_in[0], [w_o[0], w_xq[0], w_xo[0], w_gate[0], w_up[0], w_down[0]])
    proj = proj.reshape(b, s, D_IN)
    y_diff = _diff_attn(proj, slopes, vec(diff_lambda_q1), vec(diff_lambda_k1),
                        vec(diff_lambda_q2), vec(diff_lambda_k2), diagb, vec(diff_subln_g))
    y_sb = _sb_attn(proj, tri, sb_norm_g[0].reshape(N_SB_HEADS // 2, 1, LANES))

    half = N_DIFF_HEADS * DIFF_V_DIM
    kv = _xkv_proj(mem, w_xkv[0])
    out = _tail(y_diff.reshape(b * s, half), y_sb.reshape(b * s, -1), x2d, kv,
                wo16, vec(ln1_g), vec(ln1_b), wq16, wxo16, vec(ln2_g), vec(ln2_b),
                wg16, wu16, wd16, vec(ln3_g), vec(ln3_b), s)
    return out.reshape(b, s, d)
```

```python
import math

import jax
import jax.numpy as jnp
from jax import lax
from jax.experimental import pallas as pl
from jax.experimental.pallas import tpu as pltpu

D_MODEL = 1024
DEPTH = 1
CHUNK = 64
N_MEM = 256
HEAD_DIM = 64
N_DIFF_HEADS = 4
DIFF_V_DIM = 2 * HEAD_DIM
N_SB_HEADS = 8
N_XHEADS = 4
XHEAD_DIM = D_MODEL // N_XHEADS
D_FF = 2816
ALPHA = (2.0 * DEPTH) ** 0.25
LN_EPS = 1e-5
RMS_EPS = 1e-5
NEG_INF = -1e30
LAMBDA_INIT = 0.8 - 0.6 * math.exp(-0.3 * 0)
LOG2E = math.log2(math.e)

LANES = 128
BF16_SUBLANES = 16
QD_BLK, KD_BLK, VD_BLK, QS_BLK, KS_BLK, VS_BLK = 0, 4, 8, 12, 16, 20
D_IN = 24 * LANES

TM_PROJ = 1024
TM_ROW = 256
N_SUB = 2
TQ = 512
SEQ = 2048
NQ = SEQ // TQ
DIFF_HP = 2
DIAG_HALF = TQ // 2
SB_W = 128
SB_U = TQ // SB_W
SB_NEAR = 256
SB_EXIT = 150.0
SB_TOP = 256
SB_PP = 4
VMEM_LIMIT = 56 * 1024 * 1024

BF16 = jnp.bfloat16
F32 = jnp.float32


def _dot(a, b):
    return jnp.dot(a, b, preferred_element_type=F32)


def _dot_nt(a, b):
    return lax.dot_general(a, b, (((1,), (1,)), ((), ())), preferred_element_type=F32)


def _layer_norm(v, g, b):
    mu = jnp.mean(v, axis=-1, keepdims=True)
    d = v - mu
    var = jnp.mean(d * d, axis=-1, keepdims=True)
    return d * lax.rsqrt(var + LN_EPS) * g + b


def _hi_lo(v):
    hi = v.astype(BF16)
    lo = (v - hi.astype(F32)).astype(BF16)
    return jnp.concatenate([hi, lo], axis=1)


def _in_proj_kernel(x_ref, w_ref, *refs):
    n = len(refs) // 2
    o_ref = refs[n]
    o_ref[...] = _dot(x_ref[...], w_ref[...]).astype(o_ref.dtype)
    for src, dst in zip(refs[:n], refs[n + 1:]):
        dst[...] = src[...].astype(dst.dtype)


def _in_proj(x2d, w, tail_weights):
    t, d = x2d.shape
    n = w.shape[1]
    steps = t // TM_PROJ
    w_specs, w_shapes = [], []
    for tw in tail_weights:
        rows, cols = tw.shape
        if rows % (BF16_SUBLANES * steps) == 0:
            blk, index = rows // steps, (lambda i: (i, 0))
        else:
            assert rows % (BF16_SUBLANES * steps // 2) == 0
            blk, index = 2 * rows // steps, (lambda i: (jnp.minimum(i, steps // 2 - 1), 0))
        w_specs.append(pl.BlockSpec((blk, cols), index))
        w_shapes.append(jax.ShapeDtypeStruct(tw.shape, BF16))
    outs = pl.pallas_call(
        _in_proj_kernel,
        grid=(steps,),
        in_specs=[pl.BlockSpec((TM_PROJ, d), lambda i: (i, 0)),
                  pl.BlockSpec((d, n), lambda i: (0, 0), pipeline_mode=pl.Buffered(1))] + w_specs,
        out_specs=[pl.BlockSpec((TM_PROJ, n), lambda i: (i, 0))] + w_specs,
        out_shape=[jax.ShapeDtypeStruct((t, n), BF16)] + w_shapes,
        compiler_params=pltpu.CompilerParams(dimension_semantics=("arbitrary",),
                                             vmem_limit_bytes=VMEM_LIMIT),
        name="in_proj",
    )(x2d, w, *tail_weights)
    return outs[0], outs[1:]


def _diff_attn_kernel(slopes_ref, lq1_ref, lk1_ref, lq2_ref, lk2_ref, q_ref, k_ref, v_ref,
                      diagb_ref, g_ref, o_ref, m_ref, l_ref, acc_ref, s_ref, mb_ref):
    hp = pl.program_id(1)
    qi = pl.program_id(2)
    for nb in range(NQ):
        pl.when(qi == nb)(lambda nb=nb: _diff_query_block(
            nb, hp, slopes_ref, lq1_ref, lk1_ref, lq2_ref, lk2_ref, q_ref, k_ref, v_ref,
            diagb_ref, g_ref, o_ref, m_ref, l_ref, acc_ref, s_ref, mb_ref))


def _diff_query_block(nb, hp, slopes_ref, lq1_ref, lk1_ref, lq2_ref, lk2_ref, q_ref, k_ref, v_ref,
                      diagb_ref, g_ref, o_ref, m_ref, l_ref, acc_ref, s_ref, mb_ref):
    heads = range(DIFF_HP)
    hcols = [slice(hd * LANES, (hd + 1) * LANES) for hd in heads]
    slope2 = [slopes_ref[hp * DIFF_HP + hd] * LOG2E for hd in heads]

    lane = lax.broadcasted_iota(jnp.int32, (TQ, LANES), 1)
    qm = []
    for hd in heads:
        qf = q_ref[0, :, hcols[hd]].astype(F32) * (HEAD_DIM ** -0.5 * LOG2E)
        qm.append((jnp.where(lane < HEAD_DIM, qf, 0.0).astype(BF16),
                   jnp.where(lane >= HEAD_DIM, qf, 0.0).astype(BF16)))

    m_ref[...] = jnp.full_like(m_ref, NEG_INF)
    l_ref[...] = jnp.zeros_like(l_ref)
    acc_ref[...] = jnp.zeros_like(acc_ref)

    col = lax.broadcasted_iota(jnp.int32, (1, TQ), 1).astype(F32)
    row = lax.broadcasted_iota(jnp.int32, (TQ, LANES), 0).astype(F32)

    full = slice(0, TQ)
    diag_parts = ((slice(0, DIAG_HALF), DIAG_HALF), (slice(DIAG_HALF, TQ), TQ))

    def scores(hd, slot, kstart, nkeys, rows, diagonal):
        kb = k_ref[0, kstart:kstart + nkeys, hcols[hd]]
        bias = diagb_ref[hd, rows, :nkeys] if diagonal else slope2[hd] * col
        for mp in range(2):
            s = _dot_nt(qm[hd][mp][rows], kb) + bias
            s_ref[hd, slot, mp, rows, :nkeys] = s
            mb_ref[hd, slot, mp, rows] = jnp.broadcast_to(jnp.max(s, axis=-1, keepdims=True),
                                                          (rows.stop - rows.start, LANES))

    def softmax_pv(hd, slot, kstart, nkeys, rows, row_shift):
        vb = v_ref[0, kstart:kstart + nkeys, hcols[hd]]
        for mp in range(2):
            m_old = m_ref[hd, mp, rows]
            mb = mb_ref[hd, slot, mp, rows]
            m_new = jnp.maximum(m_old, mb if row_shift is None else mb + row_shift)
            alpha = jnp.exp2(m_old - m_new)
            mrel = m_new if row_shift is None else m_new - row_shift
            ps = [jnp.exp2(s_ref[hd, slot, mp, rows, c * LANES:(c + 1) * LANES] - mrel)
                  for c in range(nkeys // LANES)]
            psum = ps[0]
            for pc in ps[1:]:
                psum = psum + pc
            l_ref[hd, mp, rows] = alpha * l_ref[hd, mp, rows] + psum
            p = jnp.concatenate([pc.astype(BF16) for pc in ps], axis=1)
            acc_ref[hd, mp, rows] = alpha * acc_ref[hd, mp, rows] + _dot(p, vb)
            m_ref[hd, mp, rows] = m_new

    def issue(hd, i):
        if i < nb:
            scores(hd, i % 2, i * TQ, TQ, full, False)
        else:
            for rows, nkeys in diag_parts:
                scores(hd, i % 2, nb * TQ, nkeys, rows, True)

    def consume(hd, i):
        if i < nb:
            softmax_pv(hd, i % 2, i * TQ, TQ, full, -slope2[hd] * (row + float((nb - i) * TQ)))
        else:
            for rows, nkeys in diag_parts:
                softmax_pv(hd, i % 2, nb * TQ, nkeys, rows, None)

    lam = (jnp.exp(jnp.sum(lq1_ref[...] * lk1_ref[...]))
           - jnp.exp(jnp.sum(lq2_ref[...] * lk2_ref[...])) + LAMBDA_INIT)

    def normalise(hd):
        l1 = jnp.sum(l_ref[hd, 0], axis=-1, keepdims=True)
        l2 = jnp.sum(l_ref[hd, 1], axis=-1, keepdims=True)
        out = acc_ref[hd, 0] * (1.0 / l1) - lam * (acc_ref[hd, 1] * (1.0 / l2))
        ms = jnp.mean(out * out, axis=-1, keepdims=True)
        out = out * lax.rsqrt(ms + RMS_EPS) * g_ref[...] * (1.0 - LAMBDA_INIT)
        o_ref[0, :, hcols[hd]] = out.astype(o_ref.dtype)

    for hd in heads:
        issue(hd, 0)
    for i in range(nb + 1):
        if i < nb:
            for hd in heads:
                issue(hd, i + 1)
        for hd in heads:
            consume(hd, i)
            if i == nb:
                normalise(hd)


def _diff_attn(proj, slopes, lq1, lk1, lq2, lk2, diagb, g):
    b, s, _ = proj.shape
    nq = s // TQ
    wide = DIFF_HP * LANES
    lam_spec = pl.BlockSpec((1, HEAD_DIM), lambda bi, hi, qi: (0, 0))
    return pl.pallas_call(
        _diff_attn_kernel,
        grid=(b, N_DIFF_HEADS // DIFF_HP, nq),
        in_specs=[
            pl.BlockSpec(memory_space=pltpu.SMEM),
            lam_spec, lam_spec, lam_spec, lam_spec,
            pl.BlockSpec((1, TQ, wide), lambda bi, hi, qi: (bi, qi, QD_BLK // DIFF_HP + hi)),
            pl.BlockSpec((1, s, wide), lambda bi, hi, qi: (bi, 0, KD_BLK // DIFF_HP + hi)),
            pl.BlockSpec((1, s, wide), lambda bi, hi, qi: (bi, 0, VD_BLK // DIFF_HP + hi)),
            pl.BlockSpec((DIFF_HP, TQ, TQ), lambda bi, hi, qi: (hi, 0, 0)),
            pl.BlockSpec((1, DIFF_V_DIM), lambda bi, hi, qi: (0, 0)),
        ],
        out_specs=pl.BlockSpec((1, TQ, wide), lambda bi, hi, qi: (bi, qi, hi)),
        out_shape=jax.ShapeDtypeStruct((b, s, N_DIFF_HEADS * DIFF_V_DIM), BF16),
        scratch_shapes=[pltpu.VMEM((DIFF_HP, 2, TQ, LANES), F32),
                        pltpu.VMEM((DIFF_HP, 2, TQ, LANES), F32),
                        pltpu.VMEM((DIFF_HP, 2, TQ, LANES), F32),
                        pltpu.VMEM((DIFF_HP, 2, 2, TQ, TQ), F32),
                        pltpu.VMEM((DIFF_HP, 2, 2, TQ, LANES), F32)],
        compiler_params=pltpu.CompilerParams(
            dimension_semantics=("arbitrary", "arbitrary", "arbitrary"),
            vmem_limit_bytes=VMEM_LIMIT),
        name="diff_attn",
    )(slopes, lq1, lk1, lq2, lk2, proj, proj, proj, diagb, g)


def _sb_attn_kernel(q_ref, k_ref, v_ref, tri_ref, g_ref, o_ref, acc_ref, carry_ref, reach_ref):
    qi = pl.program_id(2)
    pairs = range(SB_PP)
    pcols = [slice(pp * LANES, (pp + 1) * LANES) for pp in pairs]
    lane = lax.broadcasted_iota(jnp.int32, (TQ, LANES), 1)
    lo_half = lane < HEAD_DIM
    row = lax.broadcasted_iota(jnp.int32, (TQ, SB_W), 0)
    colk = lax.broadcasted_iota(jnp.int32, (TQ, SB_W), 1)
    strict = colk < row
    t0 = pl.multiple_of(qi * TQ, TQ)
    tri = tri_ref[...]

    def masked_queries():
        qm = []
        for pp in pairs:
            qf = q_ref[0, :, pcols[pp]].astype(F32) * (HEAD_DIM ** -0.5 * LOG2E)
            qm.append((jnp.where(lo_half, qf, 0.0).astype(BF16),
                       jnp.where(lo_half, 0.0, qf).astype(BF16)))
        return qm

    def scores(qm, pp, kstart, nkeys, rows=slice(0, TQ)):
        kb = k_ref[0, pl.ds(kstart, nkeys), pcols[pp]]
        return [_dot_nt(qm[pp][hh][rows], kb) for hh in range(2)]

    def softplus_split(zs, nsub, own):
        z, x = {}, {}
        for hh in range(2):
            for u in reversed(range(nsub)):
                r0 = u * SB_W if own else 0
                zc = zs[hh][r0:, u * SB_W:(u + 1) * SB_W]
                sp = jnp.maximum(zc, 0.0) + jnp.log(1.0 + jnp.exp2(-jnp.abs(zc))) * LOG2E
                if own:
                    sp = jnp.where(strict[:TQ - r0], sp, 0.0)
                z[hh, u] = zc
                x[hh, u] = _hi_lo(sp)
        return z, x

    def suffix_sums(x):
        return {key: _dot(val, tri) for key, val in x.items()}

    def add_rows(full, lo, hi, delta):
        parts = [full[:lo]] if lo else []
        parts.append(full[lo:hi] + delta)
        if hi < full.shape[0]:
            parts.append(full[hi:])
        return parts[0] if len(parts) == 1 else jnp.concatenate(parts, axis=0)

    def weights_pv(pp, z, r, nsub, own, kstart, carry, acc, rows=slice(0, TQ)):
        vb = v_ref[0, pl.ds(kstart, nsub * SB_W), pcols[pp]]
        lo, hi = rows.start, rows.stop
        for hh in range(2):
            a = {}
            for u in reversed(range(nsub)):
                r0 = u * SB_W if own else lo
                av = jnp.exp2(z[hh, u] - r[hh, u][:, :SB_W] - carry[hh][r0:hi])
                if own:
                    av = jnp.where(strict[:TQ - r0], av, 0.0)
                avb = av.astype(BF16)
                if own and r0:
                    avb = jnp.concatenate([jnp.zeros((r0, SB_W), BF16), avb], axis=0)
                a[u] = avb
                carry[hh] = add_rows(carry[hh], r0, hi, r[hh, u][:, SB_W:])
            pv = _dot(jnp.concatenate([a[u] for u in range(nsub)], axis=1), vb)
            acc[hh] = add_rows(acc[hh], lo, hi, pv)

    def park(carry, acc):
        reach = None
        for pp in pairs:
            for hh in range(2):
                carry_ref[pp, hh] = carry[pp][hh]
                acc_ref[pp, hh] = acc[pp][hh]
                reach = carry[pp][hh] if reach is None else jnp.minimum(reach, carry[pp][hh])
        return reach

    def reload():
        return ([[carry_ref[pp, 0], carry_ref[pp, 1]] for pp in pairs],
                [[acc_ref[pp, 0], acc_ref[pp, 1]] for pp in pairs])

    def normalise(pp, acc):
        out = jnp.where(lo_half, acc[0], acc[1])
        sq = out * out
        ss_lo = jnp.sum(jnp.where(lo_half, sq, 0.0), axis=-1, keepdims=True)
        ss_hi = jnp.sum(jnp.where(lo_half, 0.0, sq), axis=-1, keepdims=True)
        ms = jnp.where(lo_half, ss_lo, ss_hi) * (1.0 / HEAD_DIM)
        o_ref[0, :, pcols[pp]] = (out * lax.rsqrt(ms + RMS_EPS) * g_ref[pp]).astype(o_ref.dtype)

    def near_keys(with_previous):
        qm = masked_queries()
        nsub1 = SB_NEAR // SB_W
        top = slice(0, SB_TOP)
        prev = pl.multiple_of(t0 - SB_NEAR, SB_NEAR)
        carry = [[jnp.zeros((TQ, SB_W), F32), jnp.zeros((TQ, SB_W), F32)] for _ in pairs]
        acc = [[jnp.zeros((TQ, LANES), F32), jnp.zeros((TQ, LANES), F32)] for _ in pairs]
        zx0 = [softplus_split(scores(qm, pp, t0, TQ), SB_U, True) for pp in pairs]
        if with_previous:
            zs1 = [scores(qm, pp, prev, SB_NEAR, top) for pp in pairs]
        r0 = [suffix_sums(zx0[pp][1]) for pp in pairs]
        if with_previous:
            zx1 = [softplus_split(zs1[pp], nsub1, False) for pp in pairs]
        for pp in pairs:
            weights_pv(pp, zx0[pp][0], r0[pp], SB_U, True, t0, carry[pp], acc[pp])
        if with_previous:
            r1 = [suffix_sums(zx1[pp][1]) for pp in pairs]
            for pp in pairs:
                weights_pv(pp, zx1[pp][0], r1[pp], nsub1, False, prev, carry[pp], acc[pp], top)
        reach = park(carry, acc)
        reach_ref[0] = jnp.min(reach)
        reach_ref[1] = jnp.min(reach[SB_TOP:])
        reach_ref[2] = 0.0
        for pp in pairs:
            normalise(pp, acc[pp])

    pl.when(qi == 0)(lambda: near_keys(False))
    pl.when(qi > 0)(lambda: near_keys(True))

    def chunk(kstart, rows):
        qm = masked_queries()
        nsub = SB_NEAR // SB_W
        zx = [softplus_split(scores(qm, pp, kstart, SB_NEAR, rows), nsub, False) for pp in pairs]
        r = [suffix_sums(zx[pp][1]) for pp in pairs]
        carry, acc = reload()
        for pp in pairs:
            weights_pv(pp, zx[pp][0], r[pp], nsub, False, kstart, carry[pp], acc[pp], rows)
        reach_ref[0] = jnp.min(park(carry, acc))
        reach_ref[2] = 1.0

    @pl.when(jnp.logical_and(qi > 0, reach_ref[1] < SB_EXIT))
    def _():
        chunk(pl.multiple_of(t0 - SB_NEAR, SB_NEAR), slice(SB_TOP, TQ))

    n_far = qi * (TQ // SB_NEAR) - 1

    def far_chunk(c):
        chunk(pl.multiple_of(t0 - (c + 2) * SB_NEAR, SB_NEAR), slice(0, TQ))
        return c + 1

    lax.while_loop(lambda c: jnp.logical_and(c < n_far, reach_ref[0] < SB_EXIT), far_chunk, 0)

    @pl.when(reach_ref[2] > 0.5)
    def _():
        for pp in pairs:
            normalise(pp, [acc_ref[pp, 0], acc_ref[pp, 1]])


def _sb_attn(proj, tri, g):
    b, s, _ = proj.shape
    nq = s // TQ
    npair = N_SB_HEADS // 2
    wide = SB_PP * LANES
    return pl.pallas_call(
        _sb_attn_kernel,
        grid=(b, npair // SB_PP, nq),
        in_specs=[
            pl.BlockSpec((1, TQ, wide), lambda bi, pi, qi: (bi, qi, QS_BLK // SB_PP + pi)),
            pl.BlockSpec((1, s, wide), lambda bi, pi, qi: (bi, 0, KS_BLK // SB_PP + pi)),
            pl.BlockSpec((1, s, wide), lambda bi, pi, qi: (bi, 0, VS_BLK // SB_PP + pi)),
            pl.BlockSpec((2 * SB_W, 2 * SB_W), lambda bi, pi, qi: (0, 0)),
            pl.BlockSpec((SB_PP, 1, LANES), lambda bi, pi, qi: (pi, 0, 0)),
        ],
        out_specs=pl.BlockSpec((1, TQ, wide), lambda bi, pi, qi: (bi, qi, pi)),
        out_shape=jax.ShapeDtypeStruct((b, s, N_SB_HEADS * HEAD_DIM), BF16),
        scratch_shapes=[pltpu.VMEM((SB_PP, 2, TQ, LANES), F32), pltpu.VMEM((SB_PP, 2, TQ, SB_W), F32),
                        pltpu.SMEM((3,), F32)],
        compiler_params=pltpu.CompilerParams(
            dimension_semantics=("arbitrary", "arbitrary", "arbitrary"),
            vmem_limit_bytes=VMEM_LIMIT),
        name="sb_attn",
    )(proj, proj, proj, tri, g)


def _xkv_kernel(mem_ref, w_ref, o_ref):
    o_ref[0] = _dot(mem_ref[0], w_ref[...]).astype(o_ref.dtype)


def _xkv_proj(mem, w):
    b, n, d = mem.shape
    return pl.pallas_call(
        _xkv_kernel,
        grid=(b,),
        in_specs=[pl.BlockSpec((1, n, d), lambda i: (i, 0, 0)),
                  pl.BlockSpec((d, 2 * d), lambda i: (0, 0), pipeline_mode=pl.Buffered(1))],
        out_specs=pl.BlockSpec((1, n, 2 * d), lambda i: (i, 0, 0)),
        out_shape=jax.ShapeDtypeStruct((b, n, 2 * d), BF16),
        compiler_params=pltpu.CompilerParams(dimension_semantics=("arbitrary",),
                                             vmem_limit_bytes=VMEM_LIMIT),
        name="xkv_proj",
    )(mem, w)


def _tail_kernel(yd_ref, ys_ref, x_ref, kv_ref, wo_ref, g1_ref, b1_ref, wq_ref, wxo_ref,
                 g2_ref, b2_ref, wg_ref, wu_ref, wd_ref, g3_ref, b3_ref, o_ref):
    subs = [slice(t * TM_ROW, (t + 1) * TM_ROW) for t in range(N_SUB)]
    hcols = [slice(h * XHEAD_DIM, (h + 1) * XHEAD_DIM) for h in range(N_XHEADS)]

    def query(x1):
        return (_dot(x1.astype(BF16), wq_ref[...]) * (XHEAD_DIM ** -0.5)).astype(BF16)

    def scores(q):
        return [_dot_nt(q[:, hc], kv_ref[0, :, hc]) for hc in hcols]

    def attend(sc):
        ps = [jnp.exp(sh - jnp.max(sh, axis=-1, keepdims=True)) for sh in sc]
        inv = [1.0 / jnp.sum(p, axis=-1, keepdims=True) for p in ps]
        pv = [_dot(p.astype(BF16), kv_ref[0, :, D_MODEL + h * XHEAD_DIM:D_MODEL + (h + 1) * XHEAD_DIM])
              for h, p in enumerate(ps)]
        return jnp.concatenate([(o * i).astype(BF16) for o, i in zip(pv, inv)], axis=1)

    def out_proj(x1, heads):
        return _layer_norm(ALPHA * x1 + _dot(heads, wxo_ref[...]), g2_ref[...], b2_ref[...])

    def gate_up(x2):
        xb = x2.astype(BF16)
        gate, up = _dot(xb, wg_ref[...]), _dot(xb, wu_ref[...])
        return (gate * jax.nn.sigmoid(gate) * up).astype(BF16)

    a, b = subs
    half = yd_ref.shape[1]
    mix = [_dot(yd_ref[r], wo_ref[:half, :]) + _dot(ys_ref[r], wo_ref[half:, :]) for r in subs]
    x1 = [_layer_norm(ALPHA * x_ref[r] + m, g1_ref[...], b1_ref[...]) for r, m in zip(subs, mix)]
    q_a = query(x1[0])
    sc_a = scores(q_a)
    q_b = query(x1[1])
    heads_a = attend(sc_a)
    sc_b = scores(q_b)
    x2_a = out_proj(x1[0], heads_a)
    heads_b = attend(sc_b)
    x2_b = out_proj(x1[1], heads_b)
    hidden_a = gate_up(x2_a)
    hidden_b = gate_up(x2_b)
    o_ref[a] = _layer_norm(ALPHA * x2_a + _dot(hidden_a, wd_ref[...]), g3_ref[...], b3_ref[...])
    o_ref[b] = _layer_norm(ALPHA * x2_b + _dot(hidden_b, wd_ref[...]), g3_ref[...], b3_ref[...])


def _tail(yd, ys, x2d, kv, wo, g1, b1, wq, wxo, g2, b2, wg, wu, wd, g3, b3, seq):
    t, d = x2d.shape
    half = yd.shape[1]
    n = kv.shape[1]
    f = wg.shape[1]
    tm = N_SUB * TM_ROW
    per_batch = seq // tm
    row = lambda i: (i, 0)
    const = lambda i: (0, 0)
    resident = lambda shape: pl.BlockSpec(shape, const, pipeline_mode=pl.Buffered(1))
    return pl.pallas_call(
        _tail_kernel,
        grid=(t // tm,),
        in_specs=[pl.BlockSpec((tm, half), row), pl.BlockSpec((tm, half), row),
                  pl.BlockSpec((tm, d), row),
                  pl.BlockSpec((1, n, 2 * d), lambda i: (i // per_batch, 0, 0)),
                  resident((d, d)), resident((1, d)), resident((1, d)),
                  resident((d, d)), resident((d, d)), resident((1, d)), resident((1, d)),
                  resident((d, f)), resident((d, f)), resident((f, d)),
                  resident((1, d)), resident((1, d))],
        out_specs=pl.BlockSpec((tm, d), row),
        out_shape=jax.ShapeDtypeStruct((t, d), F32),
        compiler_params=pltpu.CompilerParams(dimension_semantics=("arbitrary",),
                                             vmem_limit_bytes=VMEM_LIMIT),
        name="tail",
    )(yd, ys, x2d, kv, wo, g1, b1, wq, wxo, g2, b2, wg, wu, wd, g3, b3)


def _alibi_diag_table(slopes):
    i = jnp.arange(TQ, dtype=jnp.int32)[:, None]
    j = jnp.arange(TQ, dtype=jnp.int32)[None, :]
    dist = jnp.abs(i - j).astype(F32)
    allowed = (j // CHUNK) <= (i // CHUNK)
    return jnp.where(allowed[None], -(slopes * LOG2E)[:, None, None] * dist[None], NEG_INF)


def _suffix_sum_matrix():
    j = jnp.arange(2 * SB_W, dtype=jnp.int32)[:, None] % SB_W
    c = jnp.arange(2 * SB_W, dtype=jnp.int32)[None, :]
    return jnp.where((c >= SB_W) | (j >= c), 1.0, 0.0).astype(BF16)


def kernel(x, mem, w_in, diff_lambda_q1, diff_lambda_k1, diff_lambda_q2, diff_lambda_k2,
           diff_subln_g, sb_norm_g, w_o, ln1_g, ln1_b, w_xq, w_xkv, w_xo, ln2_g, ln2_b,
           w_gate, w_up, w_down, ln3_g, ln3_b):
    b, s, d = x.shape
    assert (b, s, d) == (8, SEQ, D_MODEL) and w_in.shape == (DEPTH, D_MODEL, D_IN)
    x2d = x.reshape(b * s, d)
    slopes = jnp.exp2(-8.0 * jnp.arange(1, N_DIFF_HEADS + 1, dtype=F32) / N_DIFF_HEADS)
    diagb = _alibi_diag_table(slopes)
    tri = _suffix_sum_matrix()
    vec = lambda a: a[0].reshape(1, -1)

    proj, (wo16, wq16, wxo16, wg16, wu16, wd16) = _in_proj(
        x2d, w_in[0], [w_o[0], w_xq[0], w_xo[0], w_gate[0], w_up[0], w_down[0]])
    proj = proj.reshape(b, s, D_IN)
    y_diff = _diff_attn(proj, slopes, vec(diff_lambda_q1), vec(diff_lambda_k1),
                        vec(diff_lambda_q2), vec(diff_lambda_k2), diagb, vec(diff_subln_g))
    y_sb = _sb_attn(proj, tri, sb_norm_g[0].reshape(N_SB_HEADS // 2, 1, LANES))

    half = N_DIFF_HEADS * DIFF_V_DIM
    kv = _xkv_proj(mem, w_xkv[0])
    out = _tail(y_diff.reshape(b * s, half), y_sb.reshape(b * s, -1), x2d, kv,
                wo16, vec(ln1_g), vec(ln1_b), wq16, wxo16, vec(ln2_g), vec(ln2_b),
                wg16, wu16, wd16, vec(ln3_g), vec(ln3_b), s)
    return out.reshape(b, s, d)
```

```python
import math

import jax
import jax.numpy as jnp
from jax import lax
from jax.experimental import pallas as pl
from jax.experimental.pallas import tpu as pltpu

D_MODEL = 1024
DEPTH = 1
CHUNK = 64
N_MEM = 256
HEAD_DIM = 64
N_DIFF_HEADS = 4
DIFF_V_DIM = 2 * HEAD_DIM
N_SB_HEADS = 8
N_XHEADS = 4
XHEAD_DIM = D_MODEL // N_XHEADS
D_FF = 2816
ALPHA = (2.0 * DEPTH) ** 0.25
LN_EPS = 1e-5
RMS_EPS = 1e-5
NEG_INF = -1e30
LAMBDA_INIT = 0.8 - 0.6 * math.exp(-0.3 * 0)
LOG2E = math.log2(math.e)

LANES = 128
BF16_SUBLANES = 16
QD_BLK, KD_BLK, VD_BLK, QS_BLK, KS_BLK, VS_BLK = 0, 4, 8, 12, 16, 20
D_IN = 24 * LANES

TM_PROJ = 512
TM_ROW = 256
N_SUB = 2
TQ = 512
SEQ = 2048
NQ = SEQ // TQ
DIFF_HP = 2
DIAG_HALF = TQ // 2
SB_W = 128
SB_U = TQ // SB_W
SB_NEAR = 256
SB_EXIT = 150.0
SB_TOP = 256
SB_PP = 2
VMEM_LIMIT = 56 * 1024 * 1024

BF16 = jnp.bfloat16
F32 = jnp.float32


def _dot(a, b):
    return jnp.dot(a, b, preferred_element_type=F32)


def _dot_nt(a, b):
    return lax.dot_general(a, b, (((1,), (1,)), ((), ())), preferred_element_type=F32)


def _layer_norm(v, g, b):
    mu = jnp.mean(v, axis=-1, keepdims=True)
    d = v - mu
    var = jnp.mean(d * d, axis=-1, keepdims=True)
    return d * lax.rsqrt(var + LN_EPS) * g + b


def _hi_lo(v):
    hi = v.astype(BF16)
    lo = (v - hi.astype(F32)).astype(BF16)
    return jnp.concatenate([hi, lo], axis=1)


def _in_proj_kernel(x_ref, w_ref, *refs):
    n = len(refs) // 2
    o_ref = refs[n]
    o_ref[...] = _dot(x_ref[...], w_ref[...]).astype(o_ref.dtype)
    for src, dst in zip(refs[:n], refs[n + 1:]):
        dst[...] = src[...].astype(dst.dtype)


def _in_proj(x2d, w, tail_weights):
    t, d = x2d.shape
    n = w.shape[1]
    steps = t // TM_PROJ
    w_specs, w_shapes = [], []
    for tw in tail_weights:
        rows, cols = tw.shape
        if rows % (BF16_SUBLANES * steps) == 0:
            blk, index = rows // steps, (lambda i: (i, 0))
        else:
            assert rows % (BF16_SUBLANES * steps // 2) == 0
            blk, index = 2 * rows // steps, (lambda i: (jnp.minimum(i, steps // 2 - 1), 0))
        w_specs.append(pl.BlockSpec((blk, cols), index))
        w_shapes.append(jax.ShapeDtypeStruct(tw.shape, BF16))
    outs = pl.pallas_call(
        _in_proj_kernel,
        grid=(steps,),
        in_specs=[pl.BlockSpec((TM_PROJ, d), lambda i: (i, 0)),
                  pl.BlockSpec((d, n), lambda i: (0, 0), pipeline_mode=pl.Buffered(1))] + w_specs,
        out_specs=[pl.BlockSpec((TM_PROJ, n), lambda i: (i, 0))] + w_specs,
        out_shape=[jax.ShapeDtypeStruct((t, n), BF16)] + w_shapes,
        compiler_params=pltpu.CompilerParams(dimension_semantics=("arbitrary",),
                                             vmem_limit_bytes=VMEM_LIMIT),
        name="in_proj",
    )(x2d, w, *tail_weights)
    return outs[0], outs[1:]


def _diff_stages(nb, hp, slopes_ref, lq1_ref, lk1_ref, lq2_ref, lk2_ref, q_ref, k_ref, v_ref,
                 diagb_ref, g_ref, o_ref, m_ref, l_ref, acc_ref, s_ref, mb_ref):
    heads = range(DIFF_HP)
    hcols = [slice(hd * LANES, (hd + 1) * LANES) for hd in heads]
    slope2 = [slopes_ref[hp * DIFF_HP + hd] * LOG2E for hd in heads]

    lane = lax.broadcasted_iota(jnp.int32, (TQ, LANES), 1)
    qm = []
    for hd in heads:
        qf = q_ref[0, :, hcols[hd]].astype(F32) * (HEAD_DIM ** -0.5 * LOG2E)
        qm.append((jnp.where(lane < HEAD_DIM, qf, 0.0).astype(BF16),
                   jnp.where(lane >= HEAD_DIM, qf, 0.0).astype(BF16)))

    m_ref[...] = jnp.full_like(m_ref, NEG_INF)
    l_ref[...] = jnp.zeros_like(l_ref)
    acc_ref[...] = jnp.zeros_like(acc_ref)

    col = lax.broadcasted_iota(jnp.int32, (1, TQ), 1).astype(F32)
    row = lax.broadcasted_iota(jnp.int32, (TQ, LANES), 0).astype(F32)

    full = slice(0, TQ)
    diag_parts = ((slice(0, DIAG_HALF), DIAG_HALF), (slice(DIAG_HALF, TQ), TQ))

    def scores(hd, slot, kstart, nkeys, rows, diagonal):
        kb = k_ref[0, kstart:kstart + nkeys, hcols[hd]]
        bias = diagb_ref[hd, rows, :nkeys] if diagonal else slope2[hd] * col
        for mp in range(2):
            s = _dot_nt(qm[hd][mp][rows], kb) + bias
            s_ref[hd, slot, mp, rows, :nkeys] = s
            mb_ref[hd, slot, mp, rows] = jnp.broadcast_to(jnp.max(s, axis=-1, keepdims=True),
                                                          (rows.stop - rows.start, LANES))

    def softmax_pv(hd, slot, kstart, nkeys, rows, row_shift):
        vb = v_ref[0, kstart:kstart + nkeys, hcols[hd]]
        for mp in range(2):
            m_old = m_ref[hd, mp, rows]
            mb = mb_ref[hd, slot, mp, rows]
            m_new = jnp.maximum(m_old, mb if row_shift is None else mb + row_shift)
            alpha = jnp.exp2(m_old - m_new)
            mrel = m_new if row_shift is None else m_new - row_shift
            ps = [jnp.exp2(s_ref[hd, slot, mp, rows, c * LANES:(c + 1) * LANES] - mrel)
                  for c in range(nkeys // LANES)]
            psum = ps[0]
            for pc in ps[1:]:
                psum = psum + pc
            l_ref[hd, mp, rows] = alpha * l_ref[hd, mp, rows] + psum
            p = jnp.concatenate([pc.astype(BF16) for pc in ps], axis=1)
            acc_ref[hd, mp, rows] = alpha * acc_ref[hd, mp, rows] + _dot(p, vb)
            m_ref[hd, mp, rows] = m_new

    def issue(hd, i):
        if i < nb:
            scores(hd, i % 2, i * TQ, TQ, full, False)
        else:
            for rows, nkeys in diag_parts:
                scores(hd, i % 2, nb * TQ, nkeys, rows, True)

    def consume(hd, i):
        if i < nb:
            softmax_pv(hd, i % 2, i * TQ, TQ, full, -slope2[hd] * (row + float((nb - i) * TQ)))
        else:
            for rows, nkeys in diag_parts:
                softmax_pv(hd, i % 2, nb * TQ, nkeys, rows, None)

    lam = (jnp.exp(jnp.sum(lq1_ref[...] * lk1_ref[...]))
           - jnp.exp(jnp.sum(lq2_ref[...] * lk2_ref[...])) + LAMBDA_INIT)

    def normalise(hd):
        l1 = jnp.sum(l_ref[hd, 0], axis=-1, keepdims=True)
        l2 = jnp.sum(l_ref[hd, 1], axis=-1, keepdims=True)
        out = acc_ref[hd, 0] * (1.0 / l1) - lam * (acc_ref[hd, 1] * (1.0 / l2))
        ms = jnp.mean(out * out, axis=-1, keepdims=True)
        out = out * lax.rsqrt(ms + RMS_EPS) * g_ref[...] * (1.0 - LAMBDA_INIT)
        o_ref[0, :, hcols[hd]] = out.astype(o_ref.dtype)

    for hd in heads:
        issue(hd, 0)
    yield
    for i in range(nb + 1):
        if i < nb:
            for hd in heads:
                issue(hd, i + 1)
            yield
        for hd in heads:
            consume(hd, i)
            if i == nb:
                normalise(hd)
            yield


def _sb_program(qi, q_ref, k_ref, v_ref, tri_ref, g_ref, o_ref, acc_ref, carry_ref, reach_ref):
    pairs = range(SB_PP)
    pcols = [slice(pp * LANES, (pp + 1) * LANES) for pp in pairs]
    lane = lax.broadcasted_iota(jnp.int32, (TQ, LANES), 1)
    lo_half = lane < HEAD_DIM
    row = lax.broadcasted_iota(jnp.int32, (TQ, SB_W), 0)
    colk = lax.broadcasted_iota(jnp.int32, (TQ, SB_W), 1)
    strict = colk < row
    t0 = pl.multiple_of(qi * TQ, TQ)
    tri = tri_ref[...]

    def masked_queries():
        qm = []
        for pp in pairs:
            qf = q_ref[0, :, pcols[pp]].astype(F32) * (HEAD_DIM ** -0.5 * LOG2E)
            qm.append((jnp.where(lo_half, qf, 0.0).astype(BF16),
                       jnp.where(lo_half, 0.0, qf).astype(BF16)))
        return qm

    def scores(qm, pp, kstart, nkeys, rows=slice(0, TQ)):
        kb = k_ref[0, pl.ds(kstart, nkeys), pcols[pp]]
        return [_dot_nt(qm[pp][hh][rows], kb) for hh in range(2)]

    def softplus_split(zs, nsub, own):
        z, x = {}, {}
        for hh in range(2):
            for u in reversed(range(nsub)):
                r0 = u * SB_W if own else 0
                zc = zs[hh][r0:, u * SB_W:(u + 1) * SB_W]
                sp = jnp.maximum(zc, 0.0) + jnp.log(1.0 + jnp.exp2(-jnp.abs(zc))) * LOG2E
                if own:
                    sp = jnp.where(strict[:TQ - r0], sp, 0.0)
                z[hh, u] = zc
                x[hh, u] = _hi_lo(sp)
        return z, x

    def suffix_sums(x):
        return {key: _dot(val, tri) for key, val in x.items()}

    def add_rows(full, lo, hi, delta):
        parts = [full[:lo]] if lo else []
        parts.append(full[lo:hi] + delta)
        if hi < full.shape[0]:
            parts.append(full[hi:])
        return parts[0] if len(parts) == 1 else jnp.concatenate(parts, axis=0)

    def weights_pv(pp, z, r, nsub, own, kstart, carry, acc, rows=slice(0, TQ)):
        vb = v_ref[0, pl.ds(kstart, nsub * SB_W), pcols[pp]]
        lo, hi = rows.start, rows.stop
        for hh in range(2):
            a = {}
            for u in reversed(range(nsub)):
                r0 = u * SB_W if own else lo
                av = jnp.exp2(z[hh, u] - r[hh, u][:, :SB_W] - carry[hh][r0:hi])
                if own:
                    av = jnp.where(strict[:TQ - r0], av, 0.0)
                avb = av.astype(BF16)
                if own and r0:
                    avb = jnp.concatenate([jnp.zeros((r0, SB_W), BF16), avb], axis=0)
                a[u] = avb
                carry[hh] = add_rows(carry[hh], r0, hi, r[hh, u][:, SB_W:])
            pv = _dot(jnp.concatenate([a[u] for u in range(nsub)], axis=1), vb)
            acc[hh] = add_rows(acc[hh], lo, hi, pv)

    def park(carry, acc):
        reach = None
        for pp in pairs:
            for hh in range(2):
                carry_ref[pp, hh] = carry[pp][hh]
                acc_ref[pp, hh] = acc[pp][hh]
                reach = carry[pp][hh] if reach is None else jnp.minimum(reach, carry[pp][hh])
        return reach

    def reload():
        return ([[carry_ref[pp, 0], carry_ref[pp, 1]] for pp in pairs],
                [[acc_ref[pp, 0], acc_ref[pp, 1]] for pp in pairs])

    def normalise(pp, acc):
        out = jnp.where(lo_half, acc[0], acc[1])
        sq = out * out
        ss_lo = jnp.sum(jnp.where(lo_half, sq, 0.0), axis=-1, keepdims=True)
        ss_hi = jnp.sum(jnp.where(lo_half, 0.0, sq), axis=-1, keepdims=True)
        ms = jnp.where(lo_half, ss_lo, ss_hi) * (1.0 / HEAD_DIM)
        o_ref[0, :, pcols[pp]] = (out * lax.rsqrt(ms + RMS_EPS) * g_ref[pp]).astype(o_ref.dtype)

    def near_keys(with_previous):
        qm = masked_queries()
        nsub1 = SB_NEAR // SB_W
        top = slice(0, SB_TOP)
        prev = pl.multiple_of(t0 - SB_NEAR, SB_NEAR)
        carry = [[jnp.zeros((TQ, SB_W), F32), jnp.zeros((TQ, SB_W), F32)] for _ in pairs]
        acc = [[jnp.zeros((TQ, LANES), F32), jnp.zeros((TQ, LANES), F32)] for _ in pairs]
        zs0 = [scores(qm, pp, t0, TQ) for pp in pairs]
        yield
        zx0 = [softplus_split(zs0[pp], SB_U, True) for pp in pairs]
        if with_previous:
            zs1 = [scores(qm, pp, prev, SB_NEAR, top) for pp in pairs]
        yield
        r0 = [suffix_sums(zx0[pp][1]) for pp in pairs]
        if with_previous:
            zx1 = [softplus_split(zs1[pp], nsub1, False) for pp in pairs]
        yield
        for pp in pairs:
            weights_pv(pp, zx0[pp][0], r0[pp], SB_U, True, t0, carry[pp], acc[pp])
            yield
        if with_previous:
            r1 = [suffix_sums(zx1[pp][1]) for pp in pairs]
            yield
            for pp in pairs:
                weights_pv(pp, zx1[pp][0], r1[pp], nsub1, False, prev, carry[pp], acc[pp], top)
                yield
        reach = park(carry, acc)
        reach_ref[0] = jnp.min(reach)
        reach_ref[1] = jnp.min(reach[SB_TOP:])
        reach_ref[2] = 0.0
        for pp in pairs:
            normalise(pp, acc[pp])

    def chunk(kstart, rows):
        qm = masked_queries()
        nsub = SB_NEAR // SB_W
        zx = [softplus_split(scores(qm, pp, kstart, SB_NEAR, rows), nsub, False) for pp in pairs]
        r = [suffix_sums(zx[pp][1]) for pp in pairs]
        carry, acc = reload()
        for pp in pairs:
            weights_pv(pp, zx[pp][0], r[pp], nsub, False, kstart, carry[pp], acc[pp], rows)
        reach_ref[0] = jnp.min(park(carry, acc))
        reach_ref[2] = 1.0

    def finish():
        @pl.when(jnp.logical_and(qi > 0, reach_ref[1] < SB_EXIT))
        def _():
            chunk(pl.multiple_of(t0 - SB_NEAR, SB_NEAR), slice(SB_TOP, TQ))

        n_far = qi * (TQ // SB_NEAR) - 1

        def far_chunk(c):
            chunk(pl.multiple_of(t0 - (c + 2) * SB_NEAR, SB_NEAR), slice(0, TQ))
            return c + 1

        lax.while_loop(lambda c: jnp.logical_and(c < n_far, reach_ref[0] < SB_EXIT), far_chunk, 0)

        @pl.when(reach_ref[2] > 0.5)
        def _():
            for pp in pairs:
                normalise(pp, [acc_ref[pp, 0], acc_ref[pp, 1]])

    return near_keys, finish


def _weave(*stage_generators):
    live = list(stage_generators)
    while live:
        for gen in list(live):
            try:
                next(gen)
            except StopIteration:
                live.remove(gen)


def _attn_kernel(slopes_ref, lq1_ref, lk1_ref, lq2_ref, lk2_ref, qd_ref, kd_ref, vd_ref, diagb_ref,
                 gd_ref, qs_ref, ks_ref, vs_ref, tri_ref, gs_ref, od_ref, os_ref,
                 m_ref, l_ref, dacc_ref, s_ref, mb_ref, sacc_ref, carry_ref, reach_ref):
    grp = pl.program_id(1)
    qi = pl.program_id(2)
    sb_near, sb_finish = _sb_program(qi, qs_ref, ks_ref, vs_ref, tri_ref, gs_ref, os_ref,
                                     sacc_ref, carry_ref, reach_ref)
    for nb in range(NQ):
        def query_block(nb=nb):
            _weave(_diff_stages(nb, grp, slopes_ref, lq1_ref, lk1_ref, lq2_ref, lk2_ref,
                                qd_ref, kd_ref, vd_ref, diagb_ref, gd_ref, od_ref,
                                m_ref, l_ref, dacc_ref, s_ref, mb_ref),
                   sb_near(nb > 0))
        pl.when(qi == nb)(query_block)
    sb_finish()


def _attention(proj, slopes, lq1, lk1, lq2, lk2, diagb, g_diff, tri, g_sb):
    b, s, _ = proj.shape
    nq = s // TQ
    groups = N_DIFF_HEADS // DIFF_HP
    assert groups == (N_SB_HEADS // 2) // SB_PP
    dwide, swide = DIFF_HP * LANES, SB_PP * LANES
    lam_spec = pl.BlockSpec((1, HEAD_DIM), lambda bi, gi, qi: (0, 0))
    return pl.pallas_call(
        _attn_kernel,
        grid=(b, groups, nq),
        in_specs=[
            pl.BlockSpec(memory_space=pltpu.SMEM),
            lam_spec, lam_spec, lam_spec, lam_spec,
            pl.BlockSpec((1, TQ, dwide), lambda bi, gi, qi: (bi, qi, QD_BLK // DIFF_HP + gi)),
            pl.BlockSpec((1, s, dwide), lambda bi, gi, qi: (bi, 0, KD_BLK // DIFF_HP + gi)),
            pl.BlockSpec((1, s, dwide), lambda bi, gi, qi: (bi, 0, VD_BLK // DIFF_HP + gi)),
            pl.BlockSpec((DIFF_HP, TQ, TQ), lambda bi, gi, qi: (gi, 0, 0)),
            pl.BlockSpec((1, DIFF_V_DIM), lambda bi, gi, qi: (0, 0)),
            pl.BlockSpec((1, TQ, swide), lambda bi, gi, qi: (bi, qi, QS_BLK // SB_PP + gi)),
            pl.BlockSpec((1, s, swide), lambda bi, gi, qi: (bi, 0, KS_BLK // SB_PP + gi)),
            pl.BlockSpec((1, s, swide), lambda bi, gi, qi: (bi, 0, VS_BLK // SB_PP + gi)),
            pl.BlockSpec((2 * SB_W, 2 * SB_W), lambda bi, gi, qi: (0, 0)),
            pl.BlockSpec((SB_PP, 1, LANES), lambda bi, gi, qi: (gi, 0, 0)),
        ],
        out_specs=[pl.BlockSpec((1, TQ, dwide), lambda bi, gi, qi: (bi, qi, gi)),
                   pl.BlockSpec((1, TQ, swide), lambda bi, gi, qi: (bi, qi, gi))],
        out_shape=[jax.ShapeDtypeStruct((b, s, N_DIFF_HEADS * DIFF_V_DIM), BF16),
                   jax.ShapeDtypeStruct((b, s, N_SB_HEADS * HEAD_DIM), BF16)],
        scratch_shapes=[pltpu.VMEM((DIFF_HP, 2, TQ, LANES), F32),
                        pltpu.VMEM((DIFF_HP, 2, TQ, LANES), F32),
                        pltpu.VMEM((DIFF_HP, 2, TQ, LANES), F32),
                        pltpu.VMEM((DIFF_HP, 2, 2, TQ, TQ), F32),
                        pltpu.VMEM((DIFF_HP, 2, 2, TQ, LANES), F32),
                        pltpu.VMEM((SB_PP, 2, TQ, LANES), F32),
                        pltpu.VMEM((SB_PP, 2, TQ, SB_W), F32),
                        pltpu.SMEM((3,), F32)],
        compiler_params=pltpu.CompilerParams(
            dimension_semantics=("arbitrary", "arbitrary", "arbitrary"),
            vmem_limit_bytes=VMEM_LIMIT),
        name="attention",
    )(slopes, lq1, lk1, lq2, lk2, proj, proj, proj, diagb, g_diff, proj, proj, proj, tri, g_sb)


def _xkv_kernel(mem_ref, w_ref, o_ref):
    o_ref[0] = _dot(mem_ref[0], w_ref[...]).astype(o_ref.dtype)


def _xkv_proj(mem, w):
    b, n, d = mem.shape
    return pl.pallas_call(
        _xkv_kernel,
        grid=(b,),
        in_specs=[pl.BlockSpec((1, n, d), lambda i: (i, 0, 0)),
                  pl.BlockSpec((d, 2 * d), lambda i: (0, 0), pipeline_mode=pl.Buffered(1))],
        out_specs=pl.BlockSpec((1, n, 2 * d), lambda i: (i, 0, 0)),
        out_shape=jax.ShapeDtypeStruct((b, n, 2 * d), BF16),
        compiler_params=pltpu.CompilerParams(dimension_semantics=("arbitrary",),
                                             vmem_limit_bytes=VMEM_LIMIT),
        name="xkv_proj",
    )(mem, w)


def _tail_kernel(yd_ref, ys_ref, x_ref, kv_ref, wo_ref, g1_ref, b1_ref, wq_ref, wxo_ref,
                 g2_ref, b2_ref, wg_ref, wu_ref, wd_ref, g3_ref, b3_ref, o_ref):
    subs = [slice(t * TM_ROW, (t + 1) * TM_ROW) for t in range(N_SUB)]
    hcols = [slice(h * XHEAD_DIM, (h + 1) * XHEAD_DIM) for h in range(N_XHEADS)]

    def query(x1):
        return (_dot(x1.astype(BF16), wq_ref[...]) * (XHEAD_DIM ** -0.5)).astype(BF16)

    def scores(q):
        return [_dot_nt(q[:, hc], kv_ref[0, :, hc]) for hc in hcols]

    def attend(sc):
        ps = [jnp.exp(sh - jnp.max(sh, axis=-1, keepdims=True)) for sh in sc]
        inv = [1.0 / jnp.sum(p, axis=-1, keepdims=True) for p in ps]
        pv = [_dot(p.astype(BF16), kv_ref[0, :, D_MODEL + h * XHEAD_DIM:D_MODEL + (h + 1) * XHEAD_DIM])
              for h, p in enumerate(ps)]
        return jnp.concatenate([(o * i).astype(BF16) for o, i in zip(pv, inv)], axis=1)

    def out_proj(x1, heads):
        return _layer_norm(ALPHA * x1 + _dot(heads, wxo_ref[...]), g2_ref[...], b2_ref[...])

    def gate_up(x2):
        xb = x2.astype(BF16)
        gate, up = _dot(xb, wg_ref[...]), _dot(xb, wu_ref[...])
        return (gate * jax.nn.sigmoid(gate) * up).astype(BF16)

    a, b = subs
    half = yd_ref.shape[1]
    mix = [_dot(yd_ref[r], wo_ref[:half, :]) + _dot(ys_ref[r], wo_ref[half:, :]) for r in subs]
    x1 = [_layer_norm(ALPHA * x_ref[r] + m, g1_ref[...], b1_ref[...]) for r, m in zip(subs, mix)]
    q_a = query(x1[0])
    sc_a = scores(q_a)
    q_b = query(x1[1])
    heads_a = attend(sc_a)
    sc_b = scores(q_b)
    x2_a = out_proj(x1[0], heads_a)
    heads_b = attend(sc_b)
    x2_b = out_proj(x1[1], heads_b)
    hidden_a = gate_up(x2_a)
    hidden_b = gate_up(x2_b)
    o_ref[a] = _layer_norm(ALPHA * x2_a + _dot(hidden_a, wd_ref[...]), g3_ref[...], b3_ref[...])
    o_ref[b] = _layer_norm(ALPHA * x2_b + _dot(hidden_b, wd_ref[...]), g3_ref[...], b3_ref[...])


def _tail(yd, ys, x2d, kv, wo, g1, b1, wq, wxo, g2, b2, wg, wu, wd, g3, b3, seq):
    t, d = x2d.shape
    half = yd.shape[1]
    n = kv.shape[1]
    f = wg.shape[1]
    tm = N_SUB * TM_ROW
    per_batch = seq // tm
    row = lambda i: (i, 0)
    const = lambda i: (0, 0)
    resident = lambda shape: pl.BlockSpec(shape, const, pipeline_mode=pl.Buffered(1))
    return pl.pallas_call(
        _tail_kernel,
        grid=(t // tm,),
        in_specs=[pl.BlockSpec((tm, half), row), pl.BlockSpec((tm, half), row),
                  pl.BlockSpec((tm, d), row),
                  pl.BlockSpec((1, n, 2 * d), lambda i: (i // per_batch, 0, 0)),
                  resident((d, d)), resident((1, d)), resident((1, d)),
                  resident((d, d)), resident((d, d)), resident((1, d)), resident((1, d)),
                  resident((d, f)), resident((d, f)), resident((f, d)),
                  resident((1, d)), resident((1, d))],
        out_specs=pl.BlockSpec((tm, d), row),
        out_shape=jax.ShapeDtypeStruct((t, d), F32),
        compiler_params=pltpu.CompilerParams(dimension_semantics=("arbitrary",),
                                             vmem_limit_bytes=VMEM_LIMIT),
        name="tail",
    )(yd, ys, x2d, kv, wo, g1, b1, wq, wxo, g2, b2, wg, wu, wd, g3, b3)


def _alibi_diag_table(slopes):
    i = jnp.arange(TQ, dtype=jnp.int32)[:, None]
    j = jnp.arange(TQ, dtype=jnp.int32)[None, :]
    dist = jnp.abs(i - j).astype(F32)
    allowed = (j // CHUNK) <= (i // CHUNK)
    return jnp.where(allowed[None], -(slopes * LOG2E)[:, None, None] * dist[None], NEG_INF)


def _suffix_sum_matrix():
    j = jnp.arange(2 * SB_W, dtype=jnp.int32)[:, None] % SB_W
    c = jnp.arange(2 * SB_W, dtype=jnp.int32)[None, :]
    return jnp.where((c >= SB_W) | (j >= c), 1.0, 0.0).astype(BF16)


def kernel(x, mem, w_in, diff_lambda_q1, diff_lambda_k1, diff_lambda_q2, diff_lambda_k2,
           diff_subln_g, sb_norm_g, w_o, ln1_g, ln1_b, w_xq, w_xkv, w_xo, ln2_g, ln2_b,
           w_gate, w_up, w_down, ln3_g, ln3_b):
    b, s, d = x.shape
    assert (b, s, d) == (8, SEQ, D_MODEL) and w_in.shape == (DEPTH, D_MODEL, D_IN)
    x2d = x.reshape(b * s, d)
    slopes = jnp.exp2(-8.0 * jnp.arange(1, N_DIFF_HEADS + 1, dtype=F32) / N_DIFF_HEADS)
    diagb = _alibi_diag_table(slopes)
    tri = _suffix_sum_matrix()
    vec = lambda a: a[0].reshape(1, -1)

    proj, (wo16, wq16, wxo16, wg16, wu16, wd16) = _in_proj(
        x2d, w_in[0], [w_o[0], w_xq[0], w_xo[0], w_gate[0], w_up[0], w_down[0]])
    proj = proj.reshape(b, s, D_IN)
    y_diff, y_sb = _attention(proj, slopes, vec(diff_lambda_q1), vec(diff_lambda_k1),
                              vec(diff_lambda_q2), vec(diff_lambda_k2), diagb, vec(diff_subln_g),
                              tri, sb_norm_g[0].reshape(N_SB_HEADS // 2, 1, LANES))

    half = N_DIFF_HEADS * DIFF_V_DIM
    kv = _xkv_proj(mem, w_xkv[0])
    out = _tail(y_diff.reshape(b * s, half), y_sb.reshape(b * s, -1), x2d, kv,
                wo16, vec(ln1_g), vec(ln1_b), wq16, wxo16, vec(ln2_g), vec(ln2_b),
                wg16, wu16, wd16, vec(ln3_g), vec(ln3_b), s)
    return out.reshape(b, s, d)
```

```python
import math

import jax
import jax.numpy as jnp
from jax import lax
from jax.experimental import pallas as pl
from jax.experimental.pallas import tpu as pltpu

D_MODEL = 1024
DEPTH = 1
CHUNK = 64
N_MEM = 256
HEAD_DIM = 64
N_DIFF_HEADS = 4
DIFF_V_DIM = 2 * HEAD_DIM
N_SB_HEADS = 8
N_XHEADS = 4
XHEAD_DIM = D_MODEL // N_XHEADS
D_FF = 2816
ALPHA = (2.0 * DEPTH) ** 0.25
LN_EPS = 1e-5
RMS_EPS = 1e-5
NEG_INF = -1e30
LAMBDA_INIT = 0.8 - 0.6 * math.exp(-0.3 * 0)
LOG2E = math.log2(math.e)

LANES = 128
BF16_SUBLANES = 16
QD_BLK, KD_BLK, VD_BLK, QS_BLK, KS_BLK, VS_BLK = 0, 4, 8, 12, 16, 20
D_IN = 24 * LANES

TM_PROJ = 512
TM_ROW = 256
N_SUB = 2
TQ = 512
SEQ = 2048
NQ = SEQ // TQ
DIFF_HP = 2
DIAG_HALF = TQ // 2
FINITE_LIMIT = 3.0e38
N_FEAT = 3
SB_W = 128
SB_U = TQ // SB_W
SB_NEAR = 256
SB_EXIT = 150.0
SB_TOP = 256
SB_PP = 2
VMEM_LIMIT = 56 * 1024 * 1024

BF16 = jnp.bfloat16
F32 = jnp.float32


def _dot(a, b):
    return jnp.dot(a, b, preferred_element_type=F32)


def _dot_nt(a, b):
    return lax.dot_general(a, b, (((1,), (1,)), ((), ())), preferred_element_type=F32)


def _layer_norm(v, g, b):
    mu = jnp.mean(v, axis=-1, keepdims=True)
    d = v - mu
    var = jnp.mean(d * d, axis=-1, keepdims=True)
    return d * lax.rsqrt(var + LN_EPS) * g + b


def _hi_lo(v):
    hi = v.astype(BF16)
    lo = (v - hi.astype(F32)).astype(BF16)
    return jnp.concatenate([hi, lo], axis=1)


def _in_proj_kernel(x_ref, w_ref, *refs):
    n = len(refs) // 2
    o_ref = refs[n]
    o_ref[...] = _dot(x_ref[...], w_ref[...]).astype(o_ref.dtype)
    for src, dst in zip(refs[:n], refs[n + 1:]):
        dst[...] = src[...].astype(dst.dtype)


def _in_proj(x2d, w, tail_weights):
    t, d = x2d.shape
    n = w.shape[1]
    steps = t // TM_PROJ
    w_specs, w_shapes = [], []
    for tw in tail_weights:
        rows, cols = tw.shape
        if rows % (BF16_SUBLANES * steps) == 0:
            blk, index = rows // steps, (lambda i: (i, 0))
        else:
            assert rows % (BF16_SUBLANES * steps // 2) == 0
            blk, index = 2 * rows // steps, (lambda i: (jnp.minimum(i, steps // 2 - 1), 0))
        w_specs.append(pl.BlockSpec((blk, cols), index))
        w_shapes.append(jax.ShapeDtypeStruct(tw.shape, BF16))
    outs = pl.pallas_call(
        _in_proj_kernel,
        grid=(steps,),
        in_specs=[pl.BlockSpec((TM_PROJ, d), lambda i: (i, 0)),
                  pl.BlockSpec((d, n), lambda i: (0, 0), pipeline_mode=pl.Buffered(1))] + w_specs,
        out_specs=[pl.BlockSpec((TM_PROJ, n), lambda i: (i, 0))] + w_specs,
        out_shape=[jax.ShapeDtypeStruct((t, n), BF16)] + w_shapes,
        compiler_params=pltpu.CompilerParams(dimension_semantics=("arbitrary",),
                                             vmem_limit_bytes=VMEM_LIMIT),
        name="in_proj",
    )(x2d, w, *tail_weights)
    return outs[0], outs[1:]


def _diff_attn_kernel(slopes_ref, lq1_ref, lk1_ref, lq2_ref, lk2_ref, q_ref, k_ref, v_ref,
                      feat1_ref, feat2_ref, diagb_ref, g_ref, o_ref,
                      m_ref, l_ref, acc_ref, s_ref, mb_ref, kaug_ref, vaug_ref, facc_ref):
    hp = pl.program_id(1)
    qi = pl.program_id(2)

    @pl.when(qi == 0)
    def _():
        lane = lax.broadcasted_iota(jnp.int32, (TQ, LANES), 1)
        first = lane < HEAD_DIM
        for hd in range(DIFF_HP):
            cols = slice(hd * LANES, (hd + 1) * LANES)
            f1 = feat1_ref[hd].astype(F32)
            f2 = feat2_ref[hd].astype(F32)
            for blk in range(NQ):
                rows = slice(blk * TQ, (blk + 1) * TQ)
                kblk = k_ref[0, rows, cols].astype(F32)
                kaug_ref[hd, 0, rows, :] = jnp.where(first, kblk, f1).astype(BF16)
                kaug_ref[hd, 1, rows, :] = jnp.where(first, f2, kblk).astype(BF16)
                vaug_ref[hd, rows, :LANES] = v_ref[0, rows, cols]
                vaug_ref[hd, rows, LANES:] = jnp.ones((TQ, LANES), BF16)

    for nb in range(NQ):
        pl.when(qi == nb)(lambda nb=nb: _diff_query_block_fast(
            nb, hp, slopes_ref, lq1_ref, lk1_ref, lq2_ref, lk2_ref, q_ref, k_ref, v_ref,
            diagb_ref, g_ref, o_ref, m_ref, l_ref, acc_ref, s_ref, mb_ref,
            kaug_ref, vaug_ref, facc_ref))


def _diff_query_block_fast(nb, hp, slopes_ref, lq1_ref, lk1_ref, lq2_ref, lk2_ref, q_ref, k_ref,
                           v_ref, diagb_ref, g_ref, o_ref, m_ref, l_ref, acc_ref, s_ref, mb_ref,
                           kaug_ref, vaug_ref, facc_ref):
    heads = range(DIFF_HP)
    hcols = [slice(hd * LANES, (hd + 1) * LANES) for hd in heads]
    slope2 = [slopes_ref[hp * DIFF_HP + hd] * LOG2E for hd in heads]
    t0 = nb * TQ
    lane = lax.broadcasted_iota(jnp.int32, (TQ, LANES), 1)
    first = lane < HEAD_DIM
    ones1 = jnp.where(lane < HEAD_DIM + N_FEAT, 1.0, 0.0)
    ones2 = jnp.where(lane < N_FEAT, 1.0, 0.0)
    row = lax.broadcasted_iota(jnp.int32, (TQ, LANES), 0).astype(F32)
    halves = (slice(0, DIAG_HALF), slice(DIAG_HALF, TQ))

    qm, ref0 = [], []
    for hd in heads:
        qf = q_ref[0, :, hcols[hd]].astype(F32) * (HEAD_DIM ** -0.5 * LOG2E)
        qm.append((jnp.where(first, qf, ones1).astype(BF16), jnp.where(first, ones2, qf).astype(BF16)))
        own = qf * k_ref[0, t0:t0 + TQ, hcols[hd]].astype(F32)
        ref0.append((jnp.sum(jnp.where(first, own, 0.0), axis=-1, keepdims=True),
                     jnp.sum(jnp.where(first, 0.0, own), axis=-1, keepdims=True)))

    units = []
    for blk in range(nb, -1, -1):
        for hd in heads:
            for mp in range(2):
                for rows in halves:
                    nkeys = rows.stop if blk == nb else TQ
                    units.append((hd, mp, rows, blk * TQ, nkeys, blk))

    def qk(u):
        hd, mp, rows, kstart, nkeys, _ = u
        return _dot_nt(qm[hd][mp][rows], kaug_ref[hd, mp, kstart:kstart + nkeys, :])

    seen = set()

    def consume(u, s):
        hd, mp, rows, kstart, nkeys, blk = u
        if blk == nb:
            s = s + diagb_ref[hd, rows, :nkeys]
            shift = -ref0[hd][mp][rows]
        else:
            shift = -slope2[hd] * (row[rows] + float((nb - blk) * TQ)) - ref0[hd][mp][rows]
        p = jnp.concatenate([jnp.exp2(s[:, c * LANES:(c + 1) * LANES] + shift).astype(BF16)
                             for c in range(nkeys // LANES)], axis=1)
        pv = _dot(p, vaug_ref[hd, kstart:kstart + nkeys, :])
        key = (hd, mp, rows.start)
        if key in seen:
            facc_ref[hd, mp, rows] += pv
        else:
            seen.add(key)
            facc_ref[hd, mp, rows] = pv

    s_next = qk(units[0])
    for n, u in enumerate(units):
        s_cur = s_next
        if n + 1 < len(units):
            s_next = qk(units[n + 1])
        consume(u, s_cur)

    lam = (jnp.exp(jnp.sum(lq1_ref[...] * lk1_ref[...]))
           - jnp.exp(jnp.sum(lq2_ref[...] * lk2_ref[...])) + LAMBDA_INIT)
    bad = jnp.zeros((TQ, LANES), F32)
    for hd in heads:
        l1 = facc_ref[hd, 0, :, LANES:]
        l2 = facc_ref[hd, 1, :, LANES:]
        out = facc_ref[hd, 0, :, :LANES] * (1.0 / l1) - lam * (facc_ref[hd, 1, :, :LANES] * (1.0 / l2))
        ms = jnp.mean(out * out, axis=-1, keepdims=True)
        out = out * lax.rsqrt(ms + RMS_EPS) * g_ref[...] * (1.0 - LAMBDA_INIT)
        o_ref[0, :, hcols[hd]] = out.astype(o_ref.dtype)
        for chk in (out, l1 + l2):
            bad = jnp.maximum(bad, jnp.where(jnp.abs(chk) < FINITE_LIMIT, 0.0, 1.0))

    @pl.when(jnp.max(bad) > 0.0)
    def _():
        _diff_query_block(nb, hp, slopes_ref, lq1_ref, lk1_ref, lq2_ref, lk2_ref, q_ref, k_ref,
                          v_ref, diagb_ref, g_ref, o_ref, m_ref, l_ref, acc_ref, s_ref, mb_ref)


def _diff_query_block(nb, hp, slopes_ref, lq1_ref, lk1_ref, lq2_ref, lk2_ref, q_ref, k_ref, v_ref,
                      diagb_ref, g_ref, o_ref, m_ref, l_ref, acc_ref, s_ref, mb_ref):
    heads = range(DIFF_HP)
    hcols = [slice(hd * LANES, (hd + 1) * LANES) for hd in heads]
    slope2 = [slopes_ref[hp * DIFF_HP + hd] * LOG2E for hd in heads]

    lane = lax.broadcasted_iota(jnp.int32, (TQ, LANES), 1)
    qm = []
    for hd in heads:
        qf = q_ref[0, :, hcols[hd]].astype(F32) * (HEAD_DIM ** -0.5 * LOG2E)
        qm.append((jnp.where(lane < HEAD_DIM, qf, 0.0).astype(BF16),
                   jnp.where(lane >= HEAD_DIM, qf, 0.0).astype(BF16)))

    m_ref[...] = jnp.full_like(m_ref, NEG_INF)
    l_ref[...] = jnp.zeros_like(l_ref)
    acc_ref[...] = jnp.zeros_like(acc_ref)

    col = lax.broadcasted_iota(jnp.int32, (1, TQ), 1).astype(F32)
    row = lax.broadcasted_iota(jnp.int32, (TQ, LANES), 0).astype(F32)

    full = slice(0, TQ)
    diag_parts = ((slice(0, DIAG_HALF), DIAG_HALF), (slice(DIAG_HALF, TQ), TQ))

    def scores(hd, slot, kstart, nkeys, rows, diagonal):
        kb = k_ref[0, kstart:kstart + nkeys, hcols[hd]]
        bias = slope2[hd] * col[:, :nkeys]
        if diagonal:
            bias = bias + diagb_ref[hd, rows, :nkeys]
        for mp in range(2):
            s = _dot_nt(qm[hd][mp][rows], kb) + bias
            s_ref[hd, slot, mp, rows, :nkeys] = s
            mb_ref[hd, slot, mp, rows] = jnp.broadcast_to(jnp.max(s, axis=-1, keepdims=True),
                                                          (rows.stop - rows.start, LANES))

    def softmax_pv(hd, slot, kstart, nkeys, rows, row_shift):
        vb = v_ref[0, kstart:kstart + nkeys, hcols[hd]]
        for mp in range(2):
            m_old = m_ref[hd, mp, rows]
            mb = mb_ref[hd, slot, mp, rows]
            m_new = jnp.maximum(m_old, mb if row_shift is None else mb + row_shift)
            alpha = jnp.exp2(m_old - m_new)
            mrel = m_new if row_shift is None else m_new - row_shift
            ps = [jnp.exp2(s_ref[hd, slot, mp, rows, c * LANES:(c + 1) * LANES] - mrel)
                  for c in range(nkeys // LANES)]
            psum = ps[0]
            for pc in ps[1:]:
                psum = psum + pc
            l_ref[hd, mp, rows] = alpha * l_ref[hd, mp, rows] + psum
            p = jnp.concatenate([pc.astype(BF16) for pc in ps], axis=1)
            acc_ref[hd, mp, rows] = alpha * acc_ref[hd, mp, rows] + _dot(p, vb)
            m_ref[hd, mp, rows] = m_new

    def issue(hd, i):
        if i < nb:
            scores(hd, i % 2, i * TQ, TQ, full, False)
        else:
            for rows, nkeys in diag_parts:
                scores(hd, i % 2, nb * TQ, nkeys, rows, True)

    def consume(hd, i):
        if i < nb:
            softmax_pv(hd, i % 2, i * TQ, TQ, full, -slope2[hd] * (row + float((nb - i) * TQ)))
        else:
            for rows, nkeys in diag_parts:
                softmax_pv(hd, i % 2, nb * TQ, nkeys, rows, None)

    lam = (jnp.exp(jnp.sum(lq1_ref[...] * lk1_ref[...]))
           - jnp.exp(jnp.sum(lq2_ref[...] * lk2_ref[...])) + LAMBDA_INIT)

    def normalise(hd):
        l1 = jnp.sum(l_ref[hd, 0], axis=-1, keepdims=True)
        l2 = jnp.sum(l_ref[hd, 1], axis=-1, keepdims=True)
        out = acc_ref[hd, 0] * (1.0 / l1) - lam * (acc_ref[hd, 1] * (1.0 / l2))
        ms = jnp.mean(out * out, axis=-1, keepdims=True)
        out = out * lax.rsqrt(ms + RMS_EPS) * g_ref[...] * (1.0 - LAMBDA_INIT)
        o_ref[0, :, hcols[hd]] = out.astype(o_ref.dtype)

    for hd in heads:
        issue(hd, 0)
    for i in range(nb + 1):
        if i < nb:
            for hd in heads:
                issue(hd, i + 1)
        for hd in heads:
            consume(hd, i)
            if i == nb:
                normalise(hd)


def _diff_attn(proj, slopes, lq1, lk1, lq2, lk2, feat1, feat2, diagb, g):
    b, s, _ = proj.shape
    nq = s // TQ
    wide = DIFF_HP * LANES
    lam_spec = pl.BlockSpec((1, HEAD_DIM), lambda bi, hi, qi: (0, 0))
    return pl.pallas_call(
        _diff_attn_kernel,
        grid=(b, N_DIFF_HEADS // DIFF_HP, nq),
        in_specs=[
            pl.BlockSpec(memory_space=pltpu.SMEM),
            lam_spec, lam_spec, lam_spec, lam_spec,
            pl.BlockSpec((1, TQ, wide), lambda bi, hi, qi: (bi, qi, QD_BLK // DIFF_HP + hi)),
            pl.BlockSpec((1, s, wide), lambda bi, hi, qi: (bi, 0, KD_BLK // DIFF_HP + hi)),
            pl.BlockSpec((1, s, wide), lambda bi, hi, qi: (bi, 0, VD_BLK // DIFF_HP + hi)),
            pl.BlockSpec((DIFF_HP, TQ, LANES), lambda bi, hi, qi: (hi, 0, 0)),
            pl.BlockSpec((DIFF_HP, TQ, LANES), lambda bi, hi, qi: (hi, 0, 0)),
            pl.BlockSpec((DIFF_HP, TQ, TQ), lambda bi, hi, qi: (hi, 0, 0)),
            pl.BlockSpec((1, DIFF_V_DIM), lambda bi, hi, qi: (0, 0)),
        ],
        out_specs=pl.BlockSpec((1, TQ, wide), lambda bi, hi, qi: (bi, qi, hi)),
        out_shape=jax.ShapeDtypeStruct((b, s, N_DIFF_HEADS * DIFF_V_DIM), BF16),
        scratch_shapes=[pltpu.VMEM((DIFF_HP, 2, TQ, LANES), F32),
                        pltpu.VMEM((DIFF_HP, 2, TQ, LANES), F32),
                        pltpu.VMEM((DIFF_HP, 2, TQ, LANES), F32),
                        pltpu.VMEM((DIFF_HP, 2, 2, TQ, TQ), F32),
                        pltpu.VMEM((DIFF_HP, 2, 2, TQ, LANES), F32),
                        pltpu.VMEM((DIFF_HP, 2, s, LANES), BF16),
                        pltpu.VMEM((DIFF_HP, s, 2 * LANES), BF16),
                        pltpu.VMEM((DIFF_HP, 2, TQ, 2 * LANES), F32)],
        compiler_params=pltpu.CompilerParams(
            dimension_semantics=("arbitrary", "arbitrary", "arbitrary"),
            vmem_limit_bytes=VMEM_LIMIT),
        name="diff_attn",
    )(slopes, lq1, lk1, lq2, lk2, proj, proj, proj, feat1, feat2, diagb, g)


def _sb_attn_kernel(q_ref, k_ref, v_ref, tri_ref, g_ref, o_ref, acc_ref, carry_ref, reach_ref):
    qi = pl.program_id(2)
    pairs = range(SB_PP)
    pcols = [slice(pp * LANES, (pp + 1) * LANES) for pp in pairs]
    lane = lax.broadcasted_iota(jnp.int32, (TQ, LANES), 1)
    lo_half = lane < HEAD_DIM
    row = lax.broadcasted_iota(jnp.int32, (TQ, SB_W), 0)
    colk = lax.broadcasted_iota(jnp.int32, (TQ, SB_W), 1)
    strict = colk < row
    t0 = pl.multiple_of(qi * TQ, TQ)
    tri = tri_ref[...]

    def masked_queries():
        qm = []
        for pp in pairs:
            qf = q_ref[0, :, pcols[pp]].astype(F32) * (HEAD_DIM ** -0.5 * LOG2E)
            qm.append((jnp.where(lo_half, qf, 0.0).astype(BF16),
                       jnp.where(lo_half, 0.0, qf).astype(BF16)))
        return qm

    def scores(qm, pp, kstart, nkeys, rows=slice(0, TQ)):
        kb = k_ref[0, pl.ds(kstart, nkeys), pcols[pp]]
        return [_dot_nt(qm[pp][hh][rows], kb) for hh in range(2)]

    def softplus_split(zs, nsub, own):
        z, x = {}, {}
        for hh in range(2):
            for u in reversed(range(nsub)):
                r0 = u * SB_W if own else 0
                zc = zs[hh][r0:, u * SB_W:(u + 1) * SB_W]
                sp = jnp.maximum(zc, 0.0) + jnp.log(1.0 + jnp.exp2(-jnp.abs(zc))) * LOG2E
                if own:
                    sp = jnp.where(strict[:TQ - r0], sp, 0.0)
                z[hh, u] = zc
                x[hh, u] = _hi_lo(sp)
        return z, x

    def suffix_sums(x):
        return {key: _dot(val, tri) for key, val in x.items()}

    def add_rows(full, lo, hi, delta):
        parts = [full[:lo]] if lo else []
        parts.append(full[lo:hi] + delta)
        if hi < full.shape[0]:
            parts.append(full[hi:])
        return parts[0] if len(parts) == 1 else jnp.concatenate(parts, axis=0)

    def weights_pv(pp, z, r, nsub, own, kstart, carry, acc, rows=slice(0, TQ)):
        vb = v_ref[0, pl.ds(kstart, nsub * SB_W), pcols[pp]]
        lo, hi = rows.start, rows.stop
        for hh in range(2):
            a = {}
            for u in reversed(range(nsub)):
                r0 = u * SB_W if own else lo
                av = jnp.exp2(z[hh, u] - r[hh, u][:, :SB_W] - carry[hh][r0:hi])
                if own:
                    av = jnp.where(strict[:TQ - r0], av, 0.0)
                avb = av.astype(BF16)
                if own and r0:
                    avb = jnp.concatenate([jnp.zeros((r0, SB_W), BF16), avb], axis=0)
                a[u] = avb
                carry[hh] = add_rows(carry[hh], r0, hi, r[hh, u][:, SB_W:])
            pv = _dot(jnp.concatenate([a[u] for u in range(nsub)], axis=1), vb)
            acc[hh] = add_rows(acc[hh], lo, hi, pv)

    def park(carry, acc):
        reach = None
        for pp in pairs:
            for hh in range(2):
                carry_ref[pp, hh] = carry[pp][hh]
                acc_ref[pp, hh] = acc[pp][hh]
                reach = carry[pp][hh] if reach is None else jnp.minimum(reach, carry[pp][hh])
        return reach

    def reload():
        return ([[carry_ref[pp, 0], carry_ref[pp, 1]] for pp in pairs],
                [[acc_ref[pp, 0], acc_ref[pp, 1]] for pp in pairs])

    def normalise(pp, acc):
        out = jnp.where(lo_half, acc[0], acc[1])
        sq = out * out
        ss_lo = jnp.sum(jnp.where(lo_half, sq, 0.0), axis=-1, keepdims=True)
        ss_hi = jnp.sum(jnp.where(lo_half, 0.0, sq), axis=-1, keepdims=True)
        ms = jnp.where(lo_half, ss_lo, ss_hi) * (1.0 / HEAD_DIM)
        o_ref[0, :, pcols[pp]] = (out * lax.rsqrt(ms + RMS_EPS) * g_ref[pp]).astype(o_ref.dtype)

    def near_keys(with_previous):
        qm = masked_queries()
        nsub1 = SB_NEAR // SB_W
        top = slice(0, SB_TOP)
        prev = pl.multiple_of(t0 - SB_NEAR, SB_NEAR)
        carry = [[jnp.zeros((TQ, SB_W), F32), jnp.zeros((TQ, SB_W), F32)] for _ in pairs]
        acc = [[jnp.zeros((TQ, LANES), F32), jnp.zeros((TQ, LANES), F32)] for _ in pairs]
        zx0 = [softplus_split(scores(qm, pp, t0, TQ), SB_U, True) for pp in pairs]
        if with_previous:
            zs1 = [scores(qm, pp, prev, SB_NEAR, top) for pp in pairs]
        r0 = [suffix_sums(zx0[pp][1]) for pp in pairs]
        if with_previous:
            zx1 = [softplus_split(zs1[pp], nsub1, False) for pp in pairs]
        for pp in pairs:
            weights_pv(pp, zx0[pp][0], r0[pp], SB_U, True, t0, carry[pp], acc[pp])
        if with_previous:
            r1 = [suffix_sums(zx1[pp][1]) for pp in pairs]
            for pp in pairs:
                weights_pv(pp, zx1[pp][0], r1[pp], nsub1, False, prev, carry[pp], acc[pp], top)
        reach = park(carry, acc)
        reach_ref[0] = jnp.min(reach)
        reach_ref[1] = jnp.min(reach[SB_TOP:])
        reach_ref[2] = 0.0
        for pp in pairs:
            normalise(pp, acc[pp])

    pl.when(qi == 0)(lambda: near_keys(False))
    pl.when(qi > 0)(lambda: near_keys(True))

    def chunk(kstart, rows):
        qm = masked_queries()
        nsub = SB_NEAR // SB_W
        zx = [softplus_split(scores(qm, pp, kstart, SB_NEAR, rows), nsub, False) for pp in pairs]
        r = [suffix_sums(zx[pp][1]) for pp in pairs]
        carry, acc = reload()
        for pp in pairs:
            weights_pv(pp, zx[pp][0], r[pp], nsub, False, kstart, carry[pp], acc[pp], rows)
        reach_ref[0] = jnp.min(park(carry, acc))
        reach_ref[2] = 1.0

    @pl.when(jnp.logical_and(qi > 0, reach_ref[1] < SB_EXIT))
    def _():
        chunk(pl.multiple_of(t0 - SB_NEAR, SB_NEAR), slice(SB_TOP, TQ))

    n_far = qi * (TQ // SB_NEAR) - 1

    def far_chunk(c):
        chunk(pl.multiple_of(t0 - (c + 2) * SB_NEAR, SB_NEAR), slice(0, TQ))
        return c + 1

    lax.while_loop(lambda c: jnp.logical_and(c < n_far, reach_ref[0] < SB_EXIT), far_chunk, 0)

    @pl.when(reach_ref[2] > 0.5)
    def _():
        for pp in pairs:
            normalise(pp, [acc_ref[pp, 0], acc_ref[pp, 1]])


def _sb_attn(proj, tri, g):
    b, s, _ = proj.shape
    nq = s // TQ
    npair = N_SB_HEADS // 2
    wide = SB_PP * LANES
    return pl.pallas_call(
        _sb_attn_kernel,
        grid=(b, npair // SB_PP, nq),
        in_specs=[
            pl.BlockSpec((1, TQ, wide), lambda bi, pi, qi: (bi, qi, QS_BLK // SB_PP + pi)),
            pl.BlockSpec((1, s, wide), lambda bi, pi, qi: (bi, 0, KS_BLK // SB_PP + pi)),
            pl.BlockSpec((1, s, wide), lambda bi, pi, qi: (bi, 0, VS_BLK // SB_PP + pi)),
            pl.BlockSpec((2 * SB_W, 2 * SB_W), lambda bi, pi, qi: (0, 0)),
            pl.BlockSpec((SB_PP, 1, LANES), lambda bi, pi, qi: (pi, 0, 0)),
        ],
        out_specs=pl.BlockSpec((1, TQ, wide), lambda bi, pi, qi: (bi, qi, pi)),
        out_shape=jax.ShapeDtypeStruct((b, s, N_SB_HEADS * HEAD_DIM), BF16),
        scratch_shapes=[pltpu.VMEM((SB_PP, 2, TQ, LANES), F32), pltpu.VMEM((SB_PP, 2, TQ, SB_W), F32),
                        pltpu.SMEM((3,), F32)],
        compiler_params=pltpu.CompilerParams(
            dimension_semantics=("arbitrary", "arbitrary", "arbitrary"),
            vmem_limit_bytes=VMEM_LIMIT),
        name="sb_attn",
    )(proj, proj, proj, tri, g)


def _xkv_kernel(mem_ref, w_ref, o_ref):
    o_ref[0] = _dot(mem_ref[0], w_ref[...]).astype(o_ref.dtype)


def _xkv_proj(mem, w):
    b, n, d = mem.shape
    return pl.pallas_call(
        _xkv_kernel,
        grid=(b,),
        in_specs=[pl.BlockSpec((1, n, d), lambda i: (i, 0, 0)),
                  pl.BlockSpec((d, 2 * d), lambda i: (0, 0), pipeline_mode=pl.Buffered(1))],
        out_specs=pl.BlockSpec((1, n, 2 * d), lambda i: (i, 0, 0)),
        out_shape=jax.ShapeDtypeStruct((b, n, 2 * d), BF16),
        compiler_params=pltpu.CompilerParams(dimension_semantics=("arbitrary",),
                                             vmem_limit_bytes=VMEM_LIMIT),
        name="xkv_proj",
    )(mem, w)


def _tail_kernel(yd_ref, ys_ref, x_ref, kv_ref, wo_ref, g1_ref, b1_ref, wq_ref, wxo_ref,
                 g2_ref, b2_ref, wg_ref, wu_ref, wd_ref, g3_ref, b3_ref, o_ref):
    subs = [slice(t * TM_ROW, (t + 1) * TM_ROW) for t in range(N_SUB)]
    hcols = [slice(h * XHEAD_DIM, (h + 1) * XHEAD_DIM) for h in range(N_XHEADS)]

    def query(x1):
        return (_dot(x1.astype(BF16), wq_ref[...]) * (XHEAD_DIM ** -0.5)).astype(BF16)

    def scores(q):
        return [_dot_nt(q[:, hc], kv_ref[0, :, hc]) for hc in hcols]

    def attend(sc):
        ps = [jnp.exp(sh - jnp.max(sh, axis=-1, keepdims=True)) for sh in sc]
        inv = [1.0 / jnp.sum(p, axis=-1, keepdims=True) for p in ps]
        pv = [_dot(p.astype(BF16), kv_ref[0, :, D_MODEL + h * XHEAD_DIM:D_MODEL + (h + 1) * XHEAD_DIM])
              for h, p in enumerate(ps)]
        return jnp.concatenate([(o * i).astype(BF16) for o, i in zip(pv, inv)], axis=1)

    def out_proj(x1, heads):
        return _layer_norm(ALPHA * x1 + _dot(heads, wxo_ref[...]), g2_ref[...], b2_ref[...])

    def gate_up(x2):
        xb = x2.astype(BF16)
        gate, up = _dot(xb, wg_ref[...]), _dot(xb, wu_ref[...])
        return (gate * jax.nn.sigmoid(gate) * up).astype(BF16)

    a, b = subs
    half = yd_ref.shape[1]
    mix = [_dot(yd_ref[r], wo_ref[:half, :]) + _dot(ys_ref[r], wo_ref[half:, :]) for r in subs]
    x1 = [_layer_norm(ALPHA * x_ref[r] + m, g1_ref[...], b1_ref[...]) for r, m in zip(subs, mix)]
    q_a = query(x1[0])
    sc_a = scores(q_a)
    q_b = query(x1[1])
    heads_a = attend(sc_a)
    sc_b = scores(q_b)
    x2_a = out_proj(x1[0], heads_a)
    heads_b = attend(sc_b)
    x2_b = out_proj(x1[1], heads_b)
    hidden_a = gate_up(x2_a)
    hidden_b = gate_up(x2_b)
    o_ref[a] = _layer_norm(ALPHA * x2_a + _dot(hidden_a, wd_ref[...]), g3_ref[...], b3_ref[...])
    o_ref[b] = _layer_norm(ALPHA * x2_b + _dot(hidden_b, wd_ref[...]), g3_ref[...], b3_ref[...])


def _tail(yd, ys, x2d, kv, wo, g1, b1, wq, wxo, g2, b2, wg, wu, wd, g3, b3, seq):
    t, d = x2d.shape
    half = yd.shape[1]
    n = kv.shape[1]
    f = wg.shape[1]
    tm = N_SUB * TM_ROW
    per_batch = seq // tm
    row = lambda i: (i, 0)
    const = lambda i: (0, 0)
    resident = lambda shape: pl.BlockSpec(shape, const, pipeline_mode=pl.Buffered(1))
    return pl.pallas_call(
        _tail_kernel,
        grid=(t // tm,),
        in_specs=[pl.BlockSpec((tm, half), row), pl.BlockSpec((tm, half), row),
                  pl.BlockSpec((tm, d), row),
                  pl.BlockSpec((1, n, 2 * d), lambda i: (i // per_batch, 0, 0)),
                  resident((d, d)), resident((1, d)), resident((1, d)),
                  resident((d, d)), resident((d, d)), resident((1, d)), resident((1, d)),
                  resident((d, f)), resident((d, f)), resident((f, d)),
                  resident((1, d)), resident((1, d))],
        out_specs=pl.BlockSpec((tm, d), row),
        out_shape=jax.ShapeDtypeStruct((t, d), F32),
        compiler_params=pltpu.CompilerParams(dimension_semantics=("arbitrary",),
                                             vmem_limit_bytes=VMEM_LIMIT),
        name="tail",
    )(yd, ys, x2d, kv, wo, g1, b1, wq, wxo, g2, b2, wg, wu, wd, g3, b3)


def _alibi_tables(slopes):
    slope2 = slopes * LOG2E
    pos = jnp.arange(TQ, dtype=F32)
    rest = slope2[:, None] * pos[None, :]
    key_term = rest
    terms = []
    for _ in range(N_FEAT):
        top = lax.bitcast_convert_type(rest, jnp.uint32) & jnp.uint32(0xFFFF0000)
        t = lax.bitcast_convert_type(top, F32)
        terms.append(t.astype(BF16))
        rest = rest - t
    feat = jnp.stack(terms, axis=-1)
    zeros = lambda n: jnp.zeros((N_DIFF_HEADS, TQ, n), BF16)
    feat1 = jnp.concatenate([zeros(HEAD_DIM), feat, zeros(LANES - HEAD_DIM - N_FEAT)], axis=-1)
    feat2 = jnp.concatenate([feat, zeros(LANES - N_FEAT)], axis=-1)
    i = jnp.arange(TQ, dtype=jnp.int32)[:, None]
    j = jnp.arange(TQ, dtype=jnp.int32)[None, :]
    dist = jnp.abs(i - j).astype(F32)
    allowed = (j // CHUNK) <= (i // CHUNK)
    diag = jnp.where(allowed[None], -slope2[:, None, None] * dist[None], NEG_INF)
    return feat1, feat2, diag - key_term[:, None, :]


def _suffix_sum_matrix():
    j = jnp.arange(2 * SB_W, dtype=jnp.int32)[:, None] % SB_W
    c = jnp.arange(2 * SB_W, dtype=jnp.int32)[None, :]
    return jnp.where((c >= SB_W) | (j >= c), 1.0, 0.0).astype(BF16)


def kernel(x, mem, w_in, diff_lambda_q1, diff_lambda_k1, diff_lambda_q2, diff_lambda_k2,
           diff_subln_g, sb_norm_g, w_o, ln1_g, ln1_b, w_xq, w_xkv, w_xo, ln2_g, ln2_b,
           w_gate, w_up, w_down, ln3_g, ln3_b):
    b, s, d = x.shape
    assert (b, s, d) == (8, SEQ, D_MODEL) and w_in.shape == (DEPTH, D_MODEL, D_IN)
    x2d = x.reshape(b * s, d)
    slopes = jnp.exp2(-8.0 * jnp.arange(1, N_DIFF_HEADS + 1, dtype=F32) / N_DIFF_HEADS)
    feat1, feat2, diagb = _alibi_tables(slopes)
    tri = _suffix_sum_matrix()
    vec = lambda a: a[0].reshape(1, -1)

    proj, (wo16, wq16, wxo16, wg16, wu16, wd16) = _in_proj(
        x2d, w_in[0], [w_o[0], w_xq[0], w_xo[0], w_gate[0], w_up[0], w_down[0]])
    proj = proj.reshape(b, s, D_IN)
    y_diff = _diff_attn(proj, slopes, vec(diff_lambda_q1), vec(diff_lambda_k1),
                        vec(diff_lambda_q2), vec(diff_lambda_k2), feat1, feat2, diagb,
                        vec(diff_subln_g))
    y_sb = _sb_attn(proj, tri, sb_norm_g[0].reshape(N_SB_HEADS // 2, 1, LANES))

    half = N_DIFF_HEADS * DIFF_V_DIM
    kv = _xkv_proj(mem, w_xkv[0])
    out = _tail(y_diff.reshape(b * s, half), y_sb.reshape(b * s, -1), x2d, kv,
                wo16, vec(ln1_g), vec(ln1_b), wq16, wxo16, vec(ln2_g), vec(ln2_b),
                wg16, wu16, wd16, vec(ln3_g), vec(ln3_b), s)
    return out.reshape(b, s, d)
```

```python
import math

import jax
import jax.numpy as jnp
from jax import lax
from jax.experimental import pallas as pl
from jax.experimental.pallas import tpu as pltpu

D_MODEL = 1024
DEPTH = 1
CHUNK = 64
N_MEM = 256
HEAD_DIM = 64
N_DIFF_HEADS = 4
DIFF_V_DIM = 2 * HEAD_DIM
N_SB_HEADS = 8
N_XHEADS = 4
XHEAD_DIM = D_MODEL // N_XHEADS
D_FF = 2816
ALPHA = (2.0 * DEPTH) ** 0.25
LN_EPS = 1e-5
RMS_EPS = 1e-5
NEG_INF = -1e30
LAMBDA_INIT = 0.8 - 0.6 * math.exp(-0.3 * 0)
LOG2E = math.log2(math.e)

LANES = 128
BF16_SUBLANES = 16
QD_BLK, KD_BLK, VD_BLK, QS_BLK, KS_BLK, VS_BLK = 0, 4, 8, 12, 16, 20
D_IN = 24 * LANES

TM_PROJ = 512
TM_ROW = 256
N_SUB = 2
TQ = 512
SEQ = 2048
NQ = SEQ // TQ
DIFF_HP = 2
DIAG_HALF = TQ // 2
FINITE_LIMIT = 3.0e38
N_FEAT = 3
SB_W = 128
SB_U = TQ // SB_W
SB_NEAR = 256
SB_EXIT = 150.0
SB_TOP = 256
SB_PP = 2
VMEM_LIMIT = 56 * 1024 * 1024

BF16 = jnp.bfloat16
F32 = jnp.float32


def _dot(a, b):
    return jnp.dot(a, b, preferred_element_type=F32)


def _dot_nt(a, b):
    return lax.dot_general(a, b, (((1,), (1,)), ((), ())), preferred_element_type=F32)


def _layer_norm(v, g, b):
    mu = jnp.mean(v, axis=-1, keepdims=True)
    d = v - mu
    var = jnp.mean(d * d, axis=-1, keepdims=True)
    return d * lax.rsqrt(var + LN_EPS) * g + b


def _hi_lo(v):
    hi = v.astype(BF16)
    lo = (v - hi.astype(F32)).astype(BF16)
    return jnp.concatenate([hi, lo], axis=1)


def _in_proj_kernel(x_ref, w_ref, *refs):
    n = len(refs) // 2
    o_ref = refs[n]
    o_ref[...] = _dot(x_ref[...], w_ref[...]).astype(o_ref.dtype)
    for src, dst in zip(refs[:n], refs[n + 1:]):
        dst[...] = src[...].astype(dst.dtype)


def _in_proj(x2d, w, tail_weights):
    t, d = x2d.shape
    n = w.shape[1]
    steps = t // TM_PROJ
    w_specs, w_shapes = [], []
    for tw in tail_weights:
        rows, cols = tw.shape
        if rows % (BF16_SUBLANES * steps) == 0:
            blk, index = rows // steps, (lambda i: (i, 0))
        else:
            assert rows % (BF16_SUBLANES * steps // 2) == 0
            blk, index = 2 * rows // steps, (lambda i: (jnp.minimum(i, steps // 2 - 1), 0))
        w_specs.append(pl.BlockSpec((blk, cols), index))
        w_shapes.append(jax.ShapeDtypeStruct(tw.shape, BF16))
    outs = pl.pallas_call(
        _in_proj_kernel,
        grid=(steps,),
        in_specs=[pl.BlockSpec((TM_PROJ, d), lambda i: (i, 0)),
                  pl.BlockSpec((d, n), lambda i: (0, 0), pipeline_mode=pl.Buffered(1))] + w_specs,
        out_specs=[pl.BlockSpec((TM_PROJ, n), lambda i: (i, 0))] + w_specs,
        out_shape=[jax.ShapeDtypeStruct((t, n), BF16)] + w_shapes,
        compiler_params=pltpu.CompilerParams(dimension_semantics=("arbitrary",),
                                             vmem_limit_bytes=VMEM_LIMIT),
        name="in_proj",
    )(x2d, w, *tail_weights)
    return outs[0], outs[1:]


def _diff_attn_kernel(slopes_ref, lq1_ref, lk1_ref, lq2_ref, lk2_ref, q_ref, k_ref, v_ref,
                      feat1_ref, feat2_ref, diagb_ref, g_ref, o_ref,
                      m_ref, l_ref, acc_ref, s_ref, mb_ref, kaug_ref, vaug_ref, facc_ref):
    hp = pl.program_id(1)
    qi = pl.program_id(2)

    @pl.when(qi == 0)
    def _():
        lane = lax.broadcasted_iota(jnp.int32, (TQ, LANES), 1)
        first = lane < HEAD_DIM
        for hd in range(DIFF_HP):
            cols = slice(hd * LANES, (hd + 1) * LANES)
            f1 = feat1_ref[hd].astype(F32)
            f2 = feat2_ref[hd].astype(F32)
            for blk in range(NQ):
                rows = slice(blk * TQ, (blk + 1) * TQ)
                kblk = k_ref[0, rows, cols].astype(F32)
                kaug_ref[hd, 0, rows, :] = jnp.where(first, kblk, f1).astype(BF16)
                kaug_ref[hd, 1, rows, :] = jnp.where(first, f2, kblk).astype(BF16)
                vaug_ref[hd, rows, :LANES] = v_ref[0, rows, cols]
                vaug_ref[hd, rows, LANES:] = jnp.ones((TQ, LANES), BF16)

    for nb in range(NQ):
        pl.when(qi == nb)(lambda nb=nb: _diff_query_block_fast(
            nb, hp, slopes_ref, lq1_ref, lk1_ref, lq2_ref, lk2_ref, q_ref, k_ref, v_ref,
            diagb_ref, g_ref, o_ref, m_ref, l_ref, acc_ref, s_ref, mb_ref,
            kaug_ref, vaug_ref, facc_ref))


def _diff_query_block_fast(nb, hp, slopes_ref, lq1_ref, lk1_ref, lq2_ref, lk2_ref, q_ref, k_ref,
                           v_ref, diagb_ref, g_ref, o_ref, m_ref, l_ref, acc_ref, s_ref, mb_ref,
                           kaug_ref, vaug_ref, facc_ref):
    heads = range(DIFF_HP)
    hcols = [slice(hd * LANES, (hd + 1) * LANES) for hd in heads]
    slope2 = [slopes_ref[hp * DIFF_HP + hd] * LOG2E for hd in heads]
    t0 = nb * TQ
    lane = lax.broadcasted_iota(jnp.int32, (TQ, LANES), 1)
    first = lane < HEAD_DIM
    ones1 = jnp.where(lane < HEAD_DIM + N_FEAT, 1.0, 0.0)
    ones2 = jnp.where(lane < N_FEAT, 1.0, 0.0)
    row = lax.broadcasted_iota(jnp.int32, (TQ, LANES), 0).astype(F32)
    halves = (slice(0, DIAG_HALF), slice(DIAG_HALF, TQ))

    qm, ref0 = [], []
    for hd in heads:
        qf = q_ref[0, :, hcols[hd]].astype(F32) * (HEAD_DIM ** -0.5 * LOG2E)
        qm.append((jnp.where(first, qf, ones1).astype(BF16), jnp.where(first, ones2, qf).astype(BF16)))
        own = qf * k_ref[0, t0:t0 + TQ, hcols[hd]].astype(F32)
        ref0.append((jnp.sum(jnp.where(first, own, 0.0), axis=-1, keepdims=True),
                     jnp.sum(jnp.where(first, 0.0, own), axis=-1, keepdims=True)))

    units = []
    for blk in range(nb, -1, -1):
        for hd in heads:
            for mp in range(2):
                if blk == nb:
                    for rows in halves:
                        units.append((hd, mp, rows, blk * TQ, rows.stop, blk))
                else:
                    units.append((hd, mp, slice(0, TQ), blk * TQ, TQ, blk))

    def qk(u):
        hd, mp, rows, kstart, nkeys, _ = u
        return _dot_nt(qm[hd][mp][rows], kaug_ref[hd, mp, kstart:kstart + nkeys, :])

    seen = set()

    def consume(u, s):
        hd, mp, rows, kstart, nkeys, blk = u
        if blk == nb:
            s = s + diagb_ref[hd, rows, :nkeys]
            shift = -ref0[hd][mp][rows]
        else:
            shift = -slope2[hd] * (row[rows] + float((nb - blk) * TQ)) - ref0[hd][mp][rows]
        p = jnp.concatenate([jnp.exp2(s[:, c * LANES:(c + 1) * LANES] + shift).astype(BF16)
                             for c in range(nkeys // LANES)], axis=1)
        pv = _dot(p, vaug_ref[hd, kstart:kstart + nkeys, :])
        key = (hd, mp, rows.start)
        if key in seen:
            facc_ref[hd, mp, rows] += pv
        else:
            seen.add(key)
            facc_ref[hd, mp, rows] = pv

    s_next = qk(units[0])
    for n, u in enumerate(units):
        s_cur = s_next
        if n + 1 < len(units):
            s_next = qk(units[n + 1])
        consume(u, s_cur)

    lam = (jnp.exp(jnp.sum(lq1_ref[...] * lk1_ref[...]))
           - jnp.exp(jnp.sum(lq2_ref[...] * lk2_ref[...])) + LAMBDA_INIT)
    bad = jnp.zeros((TQ, LANES), F32)
    for hd in heads:
        l1 = facc_ref[hd, 0, :, LANES:]
        l2 = facc_ref[hd, 1, :, LANES:]
        out = facc_ref[hd, 0, :, :LANES] * (1.0 / l1) - lam * (facc_ref[hd, 1, :, :LANES] * (1.0 / l2))
        ms = jnp.mean(out * out, axis=-1, keepdims=True)
        out = out * lax.rsqrt(ms + RMS_EPS) * g_ref[...] * (1.0 - LAMBDA_INIT)
        o_ref[0, :, hcols[hd]] = out.astype(o_ref.dtype)
        for chk in (out, l1 + l2):
            bad = jnp.maximum(bad, jnp.where(jnp.abs(chk) < FINITE_LIMIT, 0.0, 1.0))

    @pl.when(jnp.max(bad) > 0.0)
    def _():
        _diff_query_block(nb, hp, slopes_ref, lq1_ref, lk1_ref, lq2_ref, lk2_ref, q_ref, k_ref,
                          v_ref, diagb_ref, g_ref, o_ref, m_ref, l_ref, acc_ref, s_ref, mb_ref)


def _diff_query_block(nb, hp, slopes_ref, lq1_ref, lk1_ref, lq2_ref, lk2_ref, q_ref, k_ref, v_ref,
                      diagb_ref, g_ref, o_ref, m_ref, l_ref, acc_ref, s_ref, mb_ref):
    heads = range(DIFF_HP)
    hcols = [slice(hd * LANES, (hd + 1) * LANES) for hd in heads]
    slope2 = [slopes_ref[hp * DIFF_HP + hd] * LOG2E for hd in heads]

    lane = lax.broadcasted_iota(jnp.int32, (TQ, LANES), 1)
    qm = []
    for hd in heads:
        qf = q_ref[0, :, hcols[hd]].astype(F32) * (HEAD_DIM ** -0.5 * LOG2E)
        qm.append((jnp.where(lane < HEAD_DIM, qf, 0.0).astype(BF16),
                   jnp.where(lane >= HEAD_DIM, qf, 0.0).astype(BF16)))

    m_ref[...] = jnp.full_like(m_ref, NEG_INF)
    l_ref[...] = jnp.zeros_like(l_ref)
    acc_ref[...] = jnp.zeros_like(acc_ref)

    col = lax.broadcasted_iota(jnp.int32, (1, TQ), 1).astype(F32)
    row = lax.broadcasted_iota(jnp.int32, (TQ, LANES), 0).astype(F32)

    full = slice(0, TQ)
    diag_parts = ((slice(0, DIAG_HALF), DIAG_HALF), (slice(DIAG_HALF, TQ), TQ))

    def scores(hd, slot, kstart, nkeys, rows, diagonal):
        kb = k_ref[0, kstart:kstart + nkeys, hcols[hd]]
        bias = slope2[hd] * col[:, :nkeys]
        if diagonal:
            bias = bias + diagb_ref[hd, rows, :nkeys]
        for mp in range(2):
            s = _dot_nt(qm[hd][mp][rows], kb) + bias
            s_ref[hd, slot, mp, rows, :nkeys] = s
            mb_ref[hd, slot, mp, rows] = jnp.broadcast_to(jnp.max(s, axis=-1, keepdims=True),
                                                          (rows.stop - rows.start, LANES))

    def softmax_pv(hd, slot, kstart, nkeys, rows, row_shift):
        vb = v_ref[0, kstart:kstart + nkeys, hcols[hd]]
        for mp in range(2):
            m_old = m_ref[hd, mp, rows]
            mb = mb_ref[hd, slot, mp, rows]
            m_new = jnp.maximum(m_old, mb if row_shift is None else mb + row_shift)
            alpha = jnp.exp2(m_old - m_new)
            mrel = m_new if row_shift is None else m_new - row_shift
            ps = [jnp.exp2(s_ref[hd, slot, mp, rows, c * LANES:(c + 1) * LANES] - mrel)
                  for c in range(nkeys // LANES)]
            psum = ps[0]
            for pc in ps[1:]:
                psum = psum + pc
            l_ref[hd, mp, rows] = alpha * l_ref[hd, mp, rows] + psum
            p = jnp.concatenate([pc.astype(BF16) for pc in ps], axis=1)
            acc_ref[hd, mp, rows] = alpha * acc_ref[hd, mp, rows] + _dot(p, vb)
            m_ref[hd, mp, rows] = m_new

    def issue(hd, i):
        if i < nb:
            scores(hd, i % 2, i * TQ, TQ, full, False)
        else:
            for rows, nkeys in diag_parts:
                scores(hd, i % 2, nb * TQ, nkeys, rows, True)

    def consume(hd, i):
        if i < nb:
            softmax_pv(hd, i % 2, i * TQ, TQ, full, -slope2[hd] * (row + float((nb - i) * TQ)))
        else:
            for rows, nkeys in diag_parts:
                softmax_pv(hd, i % 2, nb * TQ, nkeys, rows, None)

    lam = (jnp.exp(jnp.sum(lq1_ref[...] * lk1_ref[...]))
           - jnp.exp(jnp.sum(lq2_ref[...] * lk2_ref[...])) + LAMBDA_INIT)

    def normalise(hd):
        l1 = jnp.sum(l_ref[hd, 0], axis=-1, keepdims=True)
        l2 = jnp.sum(l_ref[hd, 1], axis=-1, keepdims=True)
        out = acc_ref[hd, 0] * (1.0 / l1) - lam * (acc_ref[hd, 1] * (1.0 / l2))
        ms = jnp.mean(out * out, axis=-1, keepdims=True)
        out = out * lax.rsqrt(ms + RMS_EPS) * g_ref[...] * (1.0 - LAMBDA_INIT)
        o_ref[0, :, hcols[hd]] = out.astype(o_ref.dtype)

    for hd in heads:
        issue(hd, 0)
    for i in range(nb + 1):
        if i < nb:
            for hd in heads:
                issue(hd, i + 1)
        for hd in heads:
            consume(hd, i)
            if i == nb:
                normalise(hd)


def _diff_attn(proj, slopes, lq1, lk1, lq2, lk2, feat1, feat2, diagb, g):
    b, s, _ = proj.shape
    nq = s // TQ
    wide = DIFF_HP * LANES
    lam_spec = pl.BlockSpec((1, HEAD_DIM), lambda bi, hi, qi: (0, 0))
    return pl.pallas_call(
        _diff_attn_kernel,
        grid=(b, N_DIFF_HEADS // DIFF_HP, nq),
        in_specs=[
            pl.BlockSpec(memory_space=pltpu.SMEM),
            lam_spec, lam_spec, lam_spec, lam_spec,
            pl.BlockSpec((1, TQ, wide), lambda bi, hi, qi: (bi, qi, QD_BLK // DIFF_HP + hi)),
            pl.BlockSpec((1, s, wide), lambda bi, hi, qi: (bi, 0, KD_BLK // DIFF_HP + hi)),
            pl.BlockSpec((1, s, wide), lambda bi, hi, qi: (bi, 0, VD_BLK // DIFF_HP + hi)),
            pl.BlockSpec((DIFF_HP, TQ, LANES), lambda bi, hi, qi: (hi, 0, 0)),
            pl.BlockSpec((DIFF_HP, TQ, LANES), lambda bi, hi, qi: (hi, 0, 0)),
            pl.BlockSpec((DIFF_HP, TQ, TQ), lambda bi, hi, qi: (hi, 0, 0)),
            pl.BlockSpec((1, DIFF_V_DIM), lambda bi, hi, qi: (0, 0)),
        ],
        out_specs=pl.BlockSpec((1, TQ, wide), lambda bi, hi, qi: (bi, qi, hi)),
        out_shape=jax.ShapeDtypeStruct((b, s, N_DIFF_HEADS * DIFF_V_DIM), BF16),
        scratch_shapes=[pltpu.VMEM((DIFF_HP, 2, TQ, LANES), F32),
                        pltpu.VMEM((DIFF_HP, 2, TQ, LANES), F32),
                        pltpu.VMEM((DIFF_HP, 2, TQ, LANES), F32),
                        pltpu.VMEM((DIFF_HP, 2, 2, TQ, TQ), F32),
                        pltpu.VMEM((DIFF_HP, 2, 2, TQ, LANES), F32),
                        pltpu.VMEM((DIFF_HP, 2, s, LANES), BF16),
                        pltpu.VMEM((DIFF_HP, s, 2 * LANES), BF16),
                        pltpu.VMEM((DIFF_HP, 2, TQ, 2 * LANES), F32)],
        compiler_params=pltpu.CompilerParams(
            dimension_semantics=("arbitrary", "arbitrary", "arbitrary"),
            vmem_limit_bytes=VMEM_LIMIT),
        name="diff_attn",
    )(slopes, lq1, lk1, lq2, lk2, proj, proj, proj, feat1, feat2, diagb, g)


def _sb_attn_kernel(q_ref, k_ref, v_ref, tri_ref, g_ref, o_ref, acc_ref, carry_ref, reach_ref):
    qi = pl.program_id(2)
    pairs = range(SB_PP)
    pcols = [slice(pp * LANES, (pp + 1) * LANES) for pp in pairs]
    lane = lax.broadcasted_iota(jnp.int32, (TQ, LANES), 1)
    lo_half = lane < HEAD_DIM
    row = lax.broadcasted_iota(jnp.int32, (TQ, SB_W), 0)
    colk = lax.broadcasted_iota(jnp.int32, (TQ, SB_W), 1)
    strict = colk < row
    t0 = pl.multiple_of(qi * TQ, TQ)
    tri = tri_ref[...]

    def masked_queries():
        qm = []
        for pp in pairs:
            qf = q_ref[0, :, pcols[pp]].astype(F32) * (HEAD_DIM ** -0.5 * LOG2E)
            qm.append((jnp.where(lo_half, qf, 0.0).astype(BF16),
                       jnp.where(lo_half, 0.0, qf).astype(BF16)))
        return qm

    def scores(qm, pp, kstart, nkeys, rows=slice(0, TQ)):
        kb = k_ref[0, pl.ds(kstart, nkeys), pcols[pp]]
        return [_dot_nt(qm[pp][hh][rows], kb) for hh in range(2)]

    def softplus_split(zs, nsub, own):
        z, x = {}, {}
        for hh in range(2):
            for u in reversed(range(nsub)):
                r0 = u * SB_W if own else 0
                zc = zs[hh][r0:, u * SB_W:(u + 1) * SB_W]
                sp = jnp.maximum(zc, 0.0) + jnp.log(1.0 + jnp.exp2(-jnp.abs(zc))) * LOG2E
                if own:
                    sp = jnp.where(strict[:TQ - r0], sp, 0.0)
                z[hh, u] = zc
                x[hh, u] = _hi_lo(sp)
        return z, x

    def suffix_sums(x):
        return {key: _dot(val, tri) for key, val in x.items()}

    def add_rows(full, lo, hi, delta):
        parts = [full[:lo]] if lo else []
        parts.append(full[lo:hi] + delta)
        if hi < full.shape[0]:
            parts.append(full[hi:])
        return parts[0] if len(parts) == 1 else jnp.concatenate(parts, axis=0)

    def weights_pv(pp, z, r, nsub, own, kstart, carry, acc, rows=slice(0, TQ)):
        vb = v_ref[0, pl.ds(kstart, nsub * SB_W), pcols[pp]]
        lo, hi = rows.start, rows.stop
        for hh in range(2):
            a = {}
            for u in reversed(range(nsub)):
                r0 = u * SB_W if own else lo
                av = jnp.exp2(z[hh, u] - r[hh, u][:, :SB_W] - carry[hh][r0:hi])
                if own:
                    av = jnp.where(strict[:TQ - r0], av, 0.0)
                avb = av.astype(BF16)
                if own and r0:
                    avb = jnp.concatenate([jnp.zeros((r0, SB_W), BF16), avb], axis=0)
                a[u] = avb
                carry[hh] = add_rows(carry[hh], r0, hi, r[hh, u][:, SB_W:])
            pv = _dot(jnp.concatenate([a[u] for u in range(nsub)], axis=1), vb)
            acc[hh] = add_rows(acc[hh], lo, hi, pv)

    def park(carry, acc):
        reach = None
        for pp in pairs:
            for hh in range(2):
                carry_ref[pp, hh] = carry[pp][hh]
                acc_ref[pp, hh] = acc[pp][hh]
                reach = carry[pp][hh] if reach is None else jnp.minimum(reach, carry[pp][hh])
        return reach

    def reload():
        return ([[carry_ref[pp, 0], carry_ref[pp, 1]] for pp in pairs],
                [[acc_ref[pp, 0], acc_ref[pp, 1]] for pp in pairs])

    def normalise(pp, acc):
        out = jnp.where(lo_half, acc[0], acc[1])
        sq = out * out
        ss_lo = jnp.sum(jnp.where(lo_half, sq, 0.0), axis=-1, keepdims=True)
        ss_hi = jnp.sum(jnp.where(lo_half, 0.0, sq), axis=-1, keepdims=True)
        ms = jnp.where(lo_half, ss_lo, ss_hi) * (1.0 / HEAD_DIM)
        o_ref[0, :, pcols[pp]] = (out * lax.rsqrt(ms + RMS_EPS) * g_ref[pp]).astype(o_ref.dtype)

    def near_keys(with_previous):
        qm = masked_queries()
        nsub1 = SB_NEAR // SB_W
        top = slice(0, SB_TOP)
        prev = pl.multiple_of(t0 - SB_NEAR, SB_NEAR)
        carry = [[jnp.zeros((TQ, SB_W), F32), jnp.zeros((TQ, SB_W), F32)] for _ in pairs]
        acc = [[jnp.zeros((TQ, LANES), F32), jnp.zeros((TQ, LANES), F32)] for _ in pairs]
        zx0 = [softplus_split(scores(qm, pp, t0, TQ), SB_U, True) for pp in pairs]
        if with_previous:
            zs1 = [scores(qm, pp, prev, SB_NEAR, top) for pp in pairs]
        r0 = [suffix_sums(zx0[pp][1]) for pp in pairs]
        if with_previous:
            zx1 = [softplus_split(zs1[pp], nsub1, False) for pp in pairs]
        for pp in pairs:
            weights_pv(pp, zx0[pp][0], r0[pp], SB_U, True, t0, carry[pp], acc[pp])
        if with_previous:
            r1 = [suffix_sums(zx1[pp][1]) for pp in pairs]
            for pp in pairs:
                weights_pv(pp, zx1[pp][0], r1[pp], nsub1, False, prev, carry[pp], acc[pp], top)
        reach = park(carry, acc)
        reach_ref[0] = jnp.min(reach)
        reach_ref[1] = jnp.min(reach[SB_TOP:])
        reach_ref[2] = 0.0
        for pp in pairs:
            normalise(pp, acc[pp])

    pl.when(qi == 0)(lambda: near_keys(False))
    pl.when(qi > 0)(lambda: near_keys(True))

    def chunk(kstart, rows):
        qm = masked_queries()
        nsub = SB_NEAR // SB_W
        zx = [softplus_split(scores(qm, pp, kstart, SB_NEAR, rows), nsub, False) for pp in pairs]
        r = [suffix_sums(zx[pp][1]) for pp in pairs]
        carry, acc = reload()
        for pp in pairs:
            weights_pv(pp, zx[pp][0], r[pp], nsub, False, kstart, carry[pp], acc[pp], rows)
        reach_ref[0] = jnp.min(park(carry, acc))
        reach_ref[2] = 1.0

    @pl.when(jnp.logical_and(qi > 0, reach_ref[1] < SB_EXIT))
    def _():
        chunk(pl.multiple_of(t0 - SB_NEAR, SB_NEAR), slice(SB_TOP, TQ))

    n_far = qi * (TQ // SB_NEAR) - 1

    def far_chunk(c):
        chunk(pl.multiple_of(t0 - (c + 2) * SB_NEAR, SB_NEAR), slice(0, TQ))
        return c + 1

    lax.while_loop(lambda c: jnp.logical_and(c < n_far, reach_ref[0] < SB_EXIT), far_chunk, 0)

    @pl.when(reach_ref[2] > 0.5)
    def _():
        for pp in pairs:
            normalise(pp, [acc_ref[pp, 0], acc_ref[pp, 1]])


def _sb_attn(proj, tri, g):
    b, s, _ = proj.shape
    nq = s // TQ
    npair = N_SB_HEADS // 2
    wide = SB_PP * LANES
    return pl.pallas_call(
        _sb_attn_kernel,
        grid=(b, npair // SB_PP, nq),
        in_specs=[
            pl.BlockSpec((1, TQ, wide), lambda bi, pi, qi: (bi, qi, QS_BLK // SB_PP + pi)),
            pl.BlockSpec((1, s, wide), lambda bi, pi, qi: (bi, 0, KS_BLK // SB_PP + pi)),
            pl.BlockSpec((1, s, wide), lambda bi, pi, qi: (bi, 0, VS_BLK // SB_PP + pi)),
            pl.BlockSpec((2 * SB_W, 2 * SB_W), lambda bi, pi, qi: (0, 0)),
            pl.BlockSpec((SB_PP, 1, LANES), lambda bi, pi, qi: (pi, 0, 0)),
        ],
        out_specs=pl.BlockSpec((1, TQ, wide), lambda bi, pi, qi: (bi, qi, pi)),
        out_shape=jax.ShapeDtypeStruct((b, s, N_SB_HEADS * HEAD_DIM), BF16),
        scratch_shapes=[pltpu.VMEM((SB_PP, 2, TQ, LANES), F32), pltpu.VMEM((SB_PP, 2, TQ, SB_W), F32),
                        pltpu.SMEM((3,), F32)],
        compiler_params=pltpu.CompilerParams(
            dimension_semantics=("arbitrary", "arbitrary", "arbitrary"),
            vmem_limit_bytes=VMEM_LIMIT),
        name="sb_attn",
    )(proj, proj, proj, tri, g)


def _xkv_kernel(mem_ref, w_ref, o_ref):
    o_ref[0] = _dot(mem_ref[0], w_ref[...]).astype(o_ref.dtype)


def _xkv_proj(mem, w):
    b, n, d = mem.shape
    return pl.pallas_call(
        _xkv_kernel,
        grid=(b,),
        in_specs=[pl.BlockSpec((1, n, d), lambda i: (i, 0, 0)),
                  pl.BlockSpec((d, 2 * d), lambda i: (0, 0), pipeline_mode=pl.Buffered(1))],
        out_specs=pl.BlockSpec((1, n, 2 * d), lambda i: (i, 0, 0)),
        out_shape=jax.ShapeDtypeStruct((b, n, 2 * d), BF16),
        compiler_params=pltpu.CompilerParams(dimension_semantics=("arbitrary",),
                                             vmem_limit_bytes=VMEM_LIMIT),
        name="xkv_proj",
    )(mem, w)


def _tail_kernel(yd_ref, ys_ref, x_ref, kv_ref, wo_ref, g1_ref, b1_ref, wq_ref, wxo_ref,
                 g2_ref, b2_ref, wg_ref, wu_ref, wd_ref, g3_ref, b3_ref, o_ref):
    subs = [slice(t * TM_ROW, (t + 1) * TM_ROW) for t in range(N_SUB)]
    hcols = [slice(h * XHEAD_DIM, (h + 1) * XHEAD_DIM) for h in range(N_XHEADS)]

    def query(x1):
        return (_dot(x1.astype(BF16), wq_ref[...]) * (XHEAD_DIM ** -0.5)).astype(BF16)

    def scores(q):
        return [_dot_nt(q[:, hc], kv_ref[0, :, hc]) for hc in hcols]

    def attend(sc):
        ps = [jnp.exp(sh - jnp.max(sh, axis=-1, keepdims=True)) for sh in sc]
        inv = [1.0 / jnp.sum(p, axis=-1, keepdims=True) for p in ps]
        pv = [_dot(p.astype(BF16), kv_ref[0, :, D_MODEL + h * XHEAD_DIM:D_MODEL + (h + 1) * XHEAD_DIM])
              for h, p in enumerate(ps)]
        return jnp.concatenate([(o * i).astype(BF16) for o, i in zip(pv, inv)], axis=1)

    def out_proj(x1, heads):
        return _layer_norm(ALPHA * x1 + _dot(heads, wxo_ref[...]), g2_ref[...], b2_ref[...])

    def gate_up(x2):
        xb = x2.astype(BF16)
        gate, up = _dot(xb, wg_ref[...]), _dot(xb, wu_ref[...])
        return (gate * jax.nn.sigmoid(gate) * up).astype(BF16)

    a, b = subs
    half = yd_ref.shape[1]
    mix = [_dot(yd_ref[r], wo_ref[:half, :]) + _dot(ys_ref[r], wo_ref[half:, :]) for r in subs]
    x1 = [_layer_norm(ALPHA * x_ref[r] + m, g1_ref[...], b1_ref[...]) for r, m in zip(subs, mix)]
    q_a = query(x1[0])
    sc_a = scores(q_a)
    q_b = query(x1[1])
    heads_a = attend(sc_a)
    sc_b = scores(q_b)
    x2_a = out_proj(x1[0], heads_a)
    heads_b = attend(sc_b)
    x2_b = out_proj(x1[1], heads_b)
    hidden_a = gate_up(x2_a)
    hidden_b = gate_up(x2_b)
    o_ref[a] = _layer_norm(ALPHA * x2_a + _dot(hidden_a, wd_ref[...]), g3_ref[...], b3_ref[...])
    o_ref[b] = _layer_norm(ALPHA * x2_b + _dot(hidden_b, wd_ref[...]), g3_ref[...], b3_ref[...])


def _tail(yd, ys, x2d, kv, wo, g1, b1, wq, wxo, g2, b2, wg, wu, wd, g3, b3, seq):
    t, d = x2d.shape
    half = yd.shape[1]
    n = kv.shape[1]
    f = wg.shape[1]
    tm = N_SUB * TM_ROW
    per_batch = seq // tm
    row = lambda i: (i, 0)
    const = lambda i: (0, 0)
    resident = lambda shape: pl.BlockSpec(shape, const, pipeline_mode=pl.Buffered(1))
    return pl.pallas_call(
        _tail_kernel,
        grid=(t // tm,),
        in_specs=[pl.BlockSpec((tm, half), row), pl.BlockSpec((tm, half), row),
                  pl.BlockSpec((tm, d), row),
                  pl.BlockSpec((1, n, 2 * d), lambda i: (i // per_batch, 0, 0)),
                  resident((d, d)), resident((1, d)), resident((1, d)),
                  resident((d, d)), resident((d, d)), resident((1, d)), resident((1, d)),
                  resident((d, f)), resident((d, f)), resident((f, d)),
                  resident((1, d)), resident((1, d))],
        out_specs=pl.BlockSpec((tm, d), row),
        out_shape=jax.ShapeDtypeStruct((t, d), F32),
        compiler_params=pltpu.CompilerParams(dimension_semantics=("arbitrary",),
                                             vmem_limit_bytes=VMEM_LIMIT),
        name="tail",
    )(yd, ys, x2d, kv, wo, g1, b1, wq, wxo, g2, b2, wg, wu, wd, g3, b3)


def _alibi_tables(slopes):
    slope2 = slopes * LOG2E
    pos = jnp.arange(TQ, dtype=F32)
    rest = slope2[:, None] * pos[None, :]
    key_term = rest
    terms = []
    for _ in range(N_FEAT):
        top = lax.bitcast_convert_type(rest, jnp.uint32) & jnp.uint32(0xFFFF0000)
        t = lax.bitcast_convert_type(top, F32)
        terms.append(t.astype(BF16))
        rest = rest - t
    feat = jnp.stack(terms, axis=-1)
    zeros = lambda n: jnp.zeros((N_DIFF_HEADS, TQ, n), BF16)
    feat1 = jnp.concatenate([zeros(HEAD_DIM), feat, zeros(LANES - HEAD_DIM - N_FEAT)], axis=-1)
    feat2 = jnp.concatenate([feat, zeros(LANES - N_FEAT)], axis=-1)
    i = jnp.arange(TQ, dtype=jnp.int32)[:, None]
    j = jnp.arange(TQ, dtype=jnp.int32)[None, :]
    dist = jnp.abs(i - j).astype(F32)
    allowed = (j // CHUNK) <= (i // CHUNK)
    diag = jnp.where(allowed[None], -slope2[:, None, None] * dist[None], NEG_INF)
    return feat1, feat2, diag - key_term[:, None, :]


def _suffix_sum_matrix():
    j = jnp.arange(2 * SB_W, dtype=jnp.int32)[:, None] % SB_W
    c = jnp.arange(2 * SB_W, dtype=jnp.int32)[None, :]
    return jnp.where((c >= SB_W) | (j >= c), 1.0, 0.0).astype(BF16)


def kernel(x, mem, w_in, diff_lambda_q1, diff_lambda_k1, diff_lambda_q2, diff_lambda_k2,
           diff_subln_g, sb_norm_g, w_o, ln1_g, ln1_b, w_xq, w_xkv, w_xo, ln2_g, ln2_b,
           w_gate, w_up, w_down, ln3_g, ln3_b):
    b, s, d = x.shape
    assert (b, s, d) == (8, SEQ, D_MODEL) and w_in.shape == (DEPTH, D_MODEL, D_IN)
    x2d = x.reshape(b * s, d)
    slopes = jnp.exp2(-8.0 * jnp.arange(1, N_DIFF_HEADS + 1, dtype=F32) / N_DIFF_HEADS)
    feat1, feat2, diagb = _alibi_tables(slopes)
    tri = _suffix_sum_matrix()
    vec = lambda a: a[0].reshape(1, -1)

    proj, (wo16, wq16, wxo16, wg16, wu16, wd16) = _in_proj(
        x2d, w_in[0], [w_o[0], w_xq[0], w_xo[0], w_gate[0], w_up[0], w_down[0]])
    proj = proj.reshape(b, s, D_IN)
    y_diff = _diff_attn(proj, slopes, vec(diff_lambda_q1), vec(diff_lambda_k1),
                        vec(diff_lambda_q2), vec(diff_lambda_k2), feat1, feat2, diagb,
                        vec(diff_subln_g))
    y_sb = _sb_attn(proj, tri, sb_norm_g[0].reshape(N_SB_HEADS // 2, 1, LANES))

    half = N_DIFF_HEADS * DIFF_V_DIM
    kv = _xkv_proj(mem, w_xkv[0])
    out = _tail(y_diff.reshape(b * s, half), y_sb.reshape(b * s, -1), x2d, kv,
                wo16, vec(ln1_g), vec(ln1_b), wq16, wxo16, vec(ln2_g), vec(ln2_b),
                wg16, wu16, wd16, vec(ln3_g), vec(ln3_b), s)
    return out.reshape(b, s, d)
```

```python
import math

import jax
import jax.numpy as jnp
from jax import lax
from jax.experimental import pallas as pl
from jax.experimental.pallas import tpu as pltpu

D_MODEL = 1024
DEPTH = 1
CHUNK = 64
N_MEM = 256
HEAD_DIM = 64
N_DIFF_HEADS = 4
DIFF_V_DIM = 2 * HEAD_DIM
N_SB_HEADS = 8
N_XHEADS = 4
XHEAD_DIM = D_MODEL // N_XHEADS
D_FF = 2816
ALPHA = (2.0 * DEPTH) ** 0.25
LN_EPS = 1e-5
RMS_EPS = 1e-5
NEG_INF = -1e30
LAMBDA_INIT = 0.8 - 0.6 * math.exp(-0.3 * 0)
LOG2E = math.log2(math.e)

LANES = 128
BF16_SUBLANES = 16
QD_BLK, KD_BLK, VD_BLK, QS_BLK, KS_BLK, VS_BLK = 0, 4, 8, 12, 16, 20
D_IN = 24 * LANES

TM_PROJ = 512
TM_ROW = 256
N_SUB = 2
TQ = 512
SEQ = 2048
NQ = SEQ // TQ
DIFF_HP = 2
DIAG_HALF = TQ // 2
FINITE_LIMIT = 3.0e38
N_FEAT = 3
SB_W = 128
SB_U = TQ // SB_W
SB_NEAR = 256
SB_EXIT = 150.0
SB_TOP = 256
SB_PP = 2
VMEM_LIMIT = 56 * 1024 * 1024

BF16 = jnp.bfloat16
F32 = jnp.float32


def _dot(a, b):
    return jnp.dot(a, b, preferred_element_type=F32)


def _dot_nt(a, b):
    return lax.dot_general(a, b, (((1,), (1,)), ((), ())), preferred_element_type=F32)


def _layer_norm(v, g, b):
    mu = jnp.mean(v, axis=-1, keepdims=True)
    d = v - mu
    var = jnp.mean(d * d, axis=-1, keepdims=True)
    return d * lax.rsqrt(var + LN_EPS) * g + b


def _hi_lo(v):
    hi = v.astype(BF16)
    lo = (v - hi.astype(F32)).astype(BF16)
    return jnp.concatenate([hi, lo], axis=1)


def _in_proj_kernel(x_ref, w_ref, *refs):
    n = len(refs) // 2
    o_ref = refs[n]
    o_ref[...] = _dot(x_ref[...], w_ref[...]).astype(o_ref.dtype)
    for src, dst in zip(refs[:n], refs[n + 1:]):
        dst[...] = src[...].astype(dst.dtype)


def _in_proj(x2d, w, tail_weights):
    t, d = x2d.shape
    n = w.shape[1]
    steps = t // TM_PROJ
    w_specs, w_shapes = [], []
    for tw in tail_weights:
        rows, cols = tw.shape
        if rows % (BF16_SUBLANES * steps) == 0:
            blk, index = rows // steps, (lambda i: (i, 0))
        else:
            assert rows % (BF16_SUBLANES * steps // 2) == 0
            blk, index = 2 * rows // steps, (lambda i: (jnp.minimum(i, steps // 2 - 1), 0))
        w_specs.append(pl.BlockSpec((blk, cols), index))
        w_shapes.append(jax.ShapeDtypeStruct(tw.shape, BF16))
    outs = pl.pallas_call(
        _in_proj_kernel,
        grid=(steps,),
        in_specs=[pl.BlockSpec((TM_PROJ, d), lambda i: (i, 0)),
                  pl.BlockSpec((d, n), lambda i: (0, 0), pipeline_mode=pl.Buffered(1))] + w_specs,
        out_specs=[pl.BlockSpec((TM_PROJ, n), lambda i: (i, 0))] + w_specs,
        out_shape=[jax.ShapeDtypeStruct((t, n), BF16)] + w_shapes,
        compiler_params=pltpu.CompilerParams(dimension_semantics=("arbitrary",),
                                             vmem_limit_bytes=VMEM_LIMIT),
        name="in_proj",
    )(x2d, w, *tail_weights)
    return outs[0], outs[1:]


def _diff_attn_kernel(slopes_ref, lq1_ref, lk1_ref, lq2_ref, lk2_ref, q_ref, k_ref, v_ref,
                      feat1_ref, feat2_ref, diagb_ref, g_ref, o_ref, flag_ref,
                      kaug_ref, vaug_ref, facc_ref):
    hp = pl.program_id(1)
    qi = pl.program_id(2)

    @pl.when(qi == 0)
    def _():
        lane = lax.broadcasted_iota(jnp.int32, (TQ, LANES), 1)
        first = lane < HEAD_DIM
        for hd in range(DIFF_HP):
            cols = slice(hd * LANES, (hd + 1) * LANES)
            f1 = feat1_ref[hd].astype(F32)
            f2 = feat2_ref[hd].astype(F32)
            for blk in range(NQ):
                rows = slice(blk * TQ, (blk + 1) * TQ)
                kblk = k_ref[0, rows, cols].astype(F32)
                kaug_ref[hd, 0, rows, :] = jnp.where(first, kblk, f1).astype(BF16)
                kaug_ref[hd, 1, rows, :] = jnp.where(first, f2, kblk).astype(BF16)
                vaug_ref[hd, rows, :LANES] = v_ref[0, rows, cols]
                vaug_ref[hd, rows, LANES:] = jnp.ones((TQ, LANES), BF16)

    for nb in range(NQ):
        pl.when(qi == nb)(lambda nb=nb: _diff_query_block_fast(
            nb, hp, slopes_ref, lq1_ref, lk1_ref, lq2_ref, lk2_ref, q_ref, k_ref,
            diagb_ref, g_ref, o_ref, flag_ref, kaug_ref, vaug_ref, facc_ref))


def _diff_query_block_fast(nb, hp, slopes_ref, lq1_ref, lk1_ref, lq2_ref, lk2_ref, q_ref, k_ref,
                           diagb_ref, g_ref, o_ref, flag_ref, kaug_ref, vaug_ref, facc_ref):
    heads = range(DIFF_HP)
    hcols = [slice(hd * LANES, (hd + 1) * LANES) for hd in heads]
    slope2 = [slopes_ref[hp * DIFF_HP + hd] * LOG2E for hd in heads]
    t0 = nb * TQ
    lane = lax.broadcasted_iota(jnp.int32, (TQ, LANES), 1)
    first = lane < HEAD_DIM
    ones1 = jnp.where(lane < HEAD_DIM + N_FEAT, 1.0, 0.0)
    ones2 = jnp.where(lane < N_FEAT, 1.0, 0.0)
    row = lax.broadcasted_iota(jnp.int32, (TQ, LANES), 0).astype(F32)
    halves = (slice(0, DIAG_HALF), slice(DIAG_HALF, TQ))

    qm, ref0 = [], []
    for hd in heads:
        qf = q_ref[0, :, hcols[hd]].astype(F32) * (HEAD_DIM ** -0.5 * LOG2E)
        qm.append((jnp.where(first, qf, ones1).astype(BF16), jnp.where(first, ones2, qf).astype(BF16)))
        own = qf * k_ref[0, t0:t0 + TQ, hcols[hd]].astype(F32)
        ref0.append((jnp.sum(jnp.where(first, own, 0.0), axis=-1, keepdims=True),
                     jnp.sum(jnp.where(first, 0.0, own), axis=-1, keepdims=True)))

    units = []
    for blk in range(nb, -1, -1):
        for hd in heads:
            for mp in range(2):
                for rows in halves:
                    nkeys = rows.stop if blk == nb else TQ
                    units.append((hd, mp, rows, blk * TQ, nkeys, blk))

    def qk(u):
        hd, mp, rows, kstart, nkeys, _ = u
        return _dot_nt(qm[hd][mp][rows], kaug_ref[hd, mp, kstart:kstart + nkeys, :])

    seen = set()

    def consume(u, s):
        hd, mp, rows, kstart, nkeys, blk = u
        if blk == nb:
            s = s + diagb_ref[hd, rows, :nkeys]
            shift = -ref0[hd][mp][rows]
        else:
            shift = -slope2[hd] * (row[rows] + float((nb - blk) * TQ)) - ref0[hd][mp][rows]
        p = jnp.concatenate([jnp.exp2(s[:, c * LANES:(c + 1) * LANES] + shift).astype(BF16)
                             for c in range(nkeys // LANES)], axis=1)
        pv = _dot(p, vaug_ref[hd, kstart:kstart + nkeys, :])
        key = (hd, mp, rows.start)
        if key in seen:
            facc_ref[hd, mp, rows] += pv
        else:
            seen.add(key)
            facc_ref[hd, mp, rows] = pv

    s_next = qk(units[0])
    for n, u in enumerate(units):
        s_cur = s_next
        if n + 1 < len(units):
            s_next = qk(units[n + 1])
        consume(u, s_cur)

    lam = (jnp.exp(jnp.sum(lq1_ref[...] * lk1_ref[...]))
           - jnp.exp(jnp.sum(lq2_ref[...] * lk2_ref[...])) + LAMBDA_INIT)
    bad = jnp.zeros((TQ, LANES), F32)
    for hd in heads:
        l1 = facc_ref[hd, 0, :, LANES:]
        l2 = facc_ref[hd, 1, :, LANES:]
        out = facc_ref[hd, 0, :, :LANES] * (1.0 / l1) - lam * (facc_ref[hd, 1, :, :LANES] * (1.0 / l2))
        ms = jnp.mean(out * out, axis=-1, keepdims=True)
        out = out * lax.rsqrt(ms + RMS_EPS) * g_ref[...] * (1.0 - LAMBDA_INIT)
        o_ref[0, :, hcols[hd]] = out.astype(o_ref.dtype)
        for chk in (out, l1 + l2):
            bad = jnp.maximum(bad, jnp.where(jnp.abs(chk) < FINITE_LIMIT, 0.0, 1.0))

    flag_ref[...] = jnp.broadcast_to(jnp.max(bad, axis=0, keepdims=True)[None, None, None],
                                     flag_ref.shape)


def _diff_robust_kernel(slopes_ref, lq1_ref, lk1_ref, lq2_ref, lk2_ref, q_ref, k_ref, v_ref,
                        diagb_ref, g_ref, o_ref, m_ref, l_ref, acc_ref, s_ref, mb_ref):
    hp = pl.program_id(1)
    qi = pl.program_id(2)
    for nb in range(NQ):
        pl.when(qi == nb)(lambda nb=nb: _diff_query_block(
            nb, hp, slopes_ref, lq1_ref, lk1_ref, lq2_ref, lk2_ref, q_ref, k_ref, v_ref,
            diagb_ref, g_ref, o_ref, m_ref, l_ref, acc_ref, s_ref, mb_ref))


def _diff_query_block(nb, hp, slopes_ref, lq1_ref, lk1_ref, lq2_ref, lk2_ref, q_ref, k_ref, v_ref,
                      diagb_ref, g_ref, o_ref, m_ref, l_ref, acc_ref, s_ref, mb_ref):
    heads = range(DIFF_HP)
    hcols = [slice(hd * LANES, (hd + 1) * LANES) for hd in heads]
    slope2 = [slopes_ref[hp * DIFF_HP + hd] * LOG2E for hd in heads]

    lane = lax.broadcasted_iota(jnp.int32, (TQ, LANES), 1)
    qm = []
    for hd in heads:
        qf = q_ref[0, :, hcols[hd]].astype(F32) * (HEAD_DIM ** -0.5 * LOG2E)
        qm.append((jnp.where(lane < HEAD_DIM, qf, 0.0).astype(BF16),
                   jnp.where(lane >= HEAD_DIM, qf, 0.0).astype(BF16)))

    m_ref[...] = jnp.full_like(m_ref, NEG_INF)
    l_ref[...] = jnp.zeros_like(l_ref)
    acc_ref[...] = jnp.zeros_like(acc_ref)

    col = lax.broadcasted_iota(jnp.int32, (1, TQ), 1).astype(F32)
    row = lax.broadcasted_iota(jnp.int32, (TQ, LANES), 0).astype(F32)

    full = slice(0, TQ)
    diag_parts = ((slice(0, DIAG_HALF), DIAG_HALF), (slice(DIAG_HALF, TQ), TQ))

    def scores(hd, slot, kstart, nkeys, rows, diagonal):
        kb = k_ref[0, kstart:kstart + nkeys, hcols[hd]]
        bias = slope2[hd] * col[:, :nkeys]
        if diagonal:
            bias = bias + diagb_ref[hd, rows, :nkeys]
        for mp in range(2):
            s = _dot_nt(qm[hd][mp][rows], kb) + bias
            s_ref[hd, slot, mp, rows, :nkeys] = s
            mb_ref[hd, slot, mp, rows] = jnp.broadcast_to(jnp.max(s, axis=-1, keepdims=True),
                                                          (rows.stop - rows.start, LANES))

    def softmax_pv(hd, slot, kstart, nkeys, rows, row_shift):
        vb = v_ref[0, kstart:kstart + nkeys, hcols[hd]]
        for mp in range(2):
            m_old = m_ref[hd, mp, rows]
            mb = mb_ref[hd, slot, mp, rows]
            m_new = jnp.maximum(m_old, mb if row_shift is None else mb + row_shift)
            alpha = jnp.exp2(m_old - m_new)
            mrel = m_new if row_shift is None else m_new - row_shift
            ps = [jnp.exp2(s_ref[hd, slot, mp, rows, c * LANES:(c + 1) * LANES] - mrel)
                  for c in range(nkeys // LANES)]
            psum = ps[0]
            for pc in ps[1:]:
                psum = psum + pc
            l_ref[hd, mp, rows] = alpha * l_ref[hd, mp, rows] + psum
            p = jnp.concatenate([pc.astype(BF16) for pc in ps], axis=1)
            acc_ref[hd, mp, rows] = alpha * acc_ref[hd, mp, rows] + _dot(p, vb)
            m_ref[hd, mp, rows] = m_new

    def issue(hd, i):
        if i < nb:
            scores(hd, i % 2, i * TQ, TQ, full, False)
        else:
            for rows, nkeys in diag_parts:
                scores(hd, i % 2, nb * TQ, nkeys, rows, True)

    def consume(hd, i):
        if i < nb:
            softmax_pv(hd, i % 2, i * TQ, TQ, full, -slope2[hd] * (row + float((nb - i) * TQ)))
        else:
            for rows, nkeys in diag_parts:
                softmax_pv(hd, i % 2, nb * TQ, nkeys, rows, None)

    lam = (jnp.exp(jnp.sum(lq1_ref[...] * lk1_ref[...]))
           - jnp.exp(jnp.sum(lq2_ref[...] * lk2_ref[...])) + LAMBDA_INIT)

    def normalise(hd):
        l1 = jnp.sum(l_ref[hd, 0], axis=-1, keepdims=True)
        l2 = jnp.sum(l_ref[hd, 1], axis=-1, keepdims=True)
        out = acc_ref[hd, 0] * (1.0 / l1) - lam * (acc_ref[hd, 1] * (1.0 / l2))
        ms = jnp.mean(out * out, axis=-1, keepdims=True)
        out = out * lax.rsqrt(ms + RMS_EPS) * g_ref[...] * (1.0 - LAMBDA_INIT)
        o_ref[0, :, hcols[hd]] = out.astype(o_ref.dtype)

    for hd in heads:
        issue(hd, 0)
    for i in range(nb + 1):
        if i < nb:
            for hd in heads:
                issue(hd, i + 1)
        for hd in heads:
            consume(hd, i)
            if i == nb:
                normalise(hd)


def _diff_attn(proj, slopes, lq1, lk1, lq2, lk2, feat1, feat2, diagb, g):
    b, s, _ = proj.shape
    nq = s // TQ
    groups = N_DIFF_HEADS // DIFF_HP
    wide = DIFF_HP * LANES
    lam_spec = pl.BlockSpec((1, HEAD_DIM), lambda bi, hi, qi: (0, 0))
    common_in = [
        pl.BlockSpec(memory_space=pltpu.SMEM),
        lam_spec, lam_spec, lam_spec, lam_spec,
        pl.BlockSpec((1, TQ, wide), lambda bi, hi, qi: (bi, qi, QD_BLK // DIFF_HP + hi)),
        pl.BlockSpec((1, s, wide), lambda bi, hi, qi: (bi, 0, KD_BLK // DIFF_HP + hi)),
        pl.BlockSpec((1, s, wide), lambda bi, hi, qi: (bi, 0, VD_BLK // DIFF_HP + hi)),
    ]
    table_in = [pl.BlockSpec((DIFF_HP, TQ, TQ), lambda bi, hi, qi: (hi, 0, 0)),
                pl.BlockSpec((1, DIFF_V_DIM), lambda bi, hi, qi: (0, 0))]
    feat_in = [pl.BlockSpec((DIFF_HP, TQ, LANES), lambda bi, hi, qi: (hi, 0, 0))] * 2
    out_spec = pl.BlockSpec((1, TQ, wide), lambda bi, hi, qi: (bi, qi, hi))
    out_shape = jax.ShapeDtypeStruct((b, s, N_DIFF_HEADS * DIFF_V_DIM), BF16)
    params = pltpu.CompilerParams(dimension_semantics=("arbitrary", "arbitrary", "arbitrary"),
                                  vmem_limit_bytes=VMEM_LIMIT)
    y_fast, flags = pl.pallas_call(
        _diff_attn_kernel,
        grid=(b, groups, nq),
        in_specs=common_in + feat_in + table_in,
        out_specs=[out_spec, pl.BlockSpec((1, 1, 1, 8, LANES), lambda bi, hi, qi: (bi, hi, qi, 0, 0))],
        out_shape=[out_shape, jax.ShapeDtypeStruct((b, groups, nq, 8, LANES), F32)],
        scratch_shapes=[pltpu.VMEM((DIFF_HP, 2, s, LANES), BF16),
                        pltpu.VMEM((DIFF_HP, s, 2 * LANES), BF16),
                        pltpu.VMEM((DIFF_HP, 2, TQ, 2 * LANES), F32)],
        compiler_params=params,
        name="diff_attn",
    )(slopes, lq1, lk1, lq2, lk2, proj, proj, proj, feat1, feat2, diagb, g)

    def running_max_version():
        return pl.pallas_call(
            _diff_robust_kernel,
            grid=(b, groups, nq),
            in_specs=common_in + table_in,
            out_specs=out_spec,
            out_shape=out_shape,
            scratch_shapes=[pltpu.VMEM((DIFF_HP, 2, TQ, LANES), F32),
                            pltpu.VMEM((DIFF_HP, 2, TQ, LANES), F32),
                            pltpu.VMEM((DIFF_HP, 2, TQ, LANES), F32),
                            pltpu.VMEM((DIFF_HP, 2, 2, TQ, TQ), F32),
                            pltpu.VMEM((DIFF_HP, 2, 2, TQ, LANES), F32)],
            compiler_params=params,
            name="diff_attn_running_max",
        )(slopes, lq1, lk1, lq2, lk2, proj, proj, proj, diagb, g)

    return lax.cond(jnp.max(flags) > 0.0, running_max_version, lambda: y_fast)


def _sb_attn_kernel(q_ref, k_ref, v_ref, tri_ref, g_ref, o_ref, acc_ref, carry_ref, reach_ref):
    qi = pl.program_id(2)
    pairs = range(SB_PP)
    pcols = [slice(pp * LANES, (pp + 1) * LANES) for pp in pairs]
    lane = lax.broadcasted_iota(jnp.int32, (TQ, LANES), 1)
    lo_half = lane < HEAD_DIM
    row = lax.broadcasted_iota(jnp.int32, (TQ, SB_W), 0)
    colk = lax.broadcasted_iota(jnp.int32, (TQ, SB_W), 1)
    strict = colk < row
    t0 = pl.multiple_of(qi * TQ, TQ)
    tri = tri_ref[...]

    def masked_queries():
        qm = []
        for pp in pairs:
            qf = q_ref[0, :, pcols[pp]].astype(F32) * (HEAD_DIM ** -0.5 * LOG2E)
            qm.append((jnp.where(lo_half, qf, 0.0).astype(BF16),
                       jnp.where(lo_half, 0.0, qf).astype(BF16)))
        return qm

    def scores(qm, pp, kstart, nkeys, rows=slice(0, TQ)):
        kb = k_ref[0, pl.ds(kstart, nkeys), pcols[pp]]
        return [_dot_nt(qm[pp][hh][rows], kb) for hh in range(2)]

    def softplus_split(zs, nsub, own):
        z, x = {}, {}
        for hh in range(2):
            for u in reversed(range(nsub)):
                r0 = u * SB_W if own else 0
                zc = zs[hh][r0:, u * SB_W:(u + 1) * SB_W]
                sp = jnp.maximum(zc, 0.0) + jnp.log(1.0 + jnp.exp2(-jnp.abs(zc))) * LOG2E
                if own:
                    sp = jnp.where(strict[:TQ - r0], sp, 0.0)
                z[hh, u] = zc
                x[hh, u] = _hi_lo(sp)
        return z, x

    def suffix_sums(x):
        return {key: _dot(val, tri) for key, val in x.items()}

    def add_rows(full, lo, hi, delta):
        parts = [full[:lo]] if lo else []
        parts.append(full[lo:hi] + delta)
        if hi < full.shape[0]:
            parts.append(full[hi:])
        return parts[0] if len(parts) == 1 else jnp.concatenate(parts, axis=0)

    def weights_pv(pp, z, r, nsub, own, kstart, carry, acc, rows=slice(0, TQ)):
        vb = v_ref[0, pl.ds(kstart, nsub * SB_W), pcols[pp]]
        lo, hi = rows.start, rows.stop
        for hh in range(2):
            a = {}
            for u in reversed(range(nsub)):
                r0 = u * SB_W if own else lo
                av = jnp.exp2(z[hh, u] - r[hh, u][:, :SB_W] - carry[hh][r0:hi])
                if own:
                    av = jnp.where(strict[:TQ - r0], av, 0.0)
                avb = av.astype(BF16)
                if own and r0:
                    avb = jnp.concatenate([jnp.zeros((r0, SB_W), BF16), avb], axis=0)
                a[u] = avb
                carry[hh] = add_rows(carry[hh], r0, hi, r[hh, u][:, SB_W:])
            pv = _dot(jnp.concatenate([a[u] for u in range(nsub)], axis=1), vb)
            acc[hh] = add_rows(acc[hh], lo, hi, pv)

    def park(carry, acc):
        reach = None
        for pp in pairs:
            for hh in range(2):
                carry_ref[pp, hh] = carry[pp][hh]
                acc_ref[pp, hh] = acc[pp][hh]
                reach = carry[pp][hh] if reach is None else jnp.minimum(reach, carry[pp][hh])
        return reach

    def reload():
        return ([[carry_ref[pp, 0], carry_ref[pp, 1]] for pp in pairs],
                [[acc_ref[pp, 0], acc_ref[pp, 1]] for pp in pairs])

    def normalise(pp, acc):
        out = jnp.where(lo_half, acc[0], acc[1])
        sq = out * out
        ss_lo = jnp.sum(jnp.where(lo_half, sq, 0.0), axis=-1, keepdims=True)
        ss_hi = jnp.sum(jnp.where(lo_half, 0.0, sq), axis=-1, keepdims=True)
        ms = jnp.where(lo_half, ss_lo, ss_hi) * (1.0 / HEAD_DIM)
        o_ref[0, :, pcols[pp]] = (out * lax.rsqrt(ms + RMS_EPS) * g_ref[pp]).astype(o_ref.dtype)

    def near_keys(with_previous):
        qm = masked_queries()
        nsub1 = SB_NEAR // SB_W
        top = slice(0, SB_TOP)
        prev = pl.multiple_of(t0 - SB_NEAR, SB_NEAR)
        carry = [[jnp.zeros((TQ, SB_W), F32), jnp.zeros((TQ, SB_W), F32)] for _ in pairs]
        acc = [[jnp.zeros((TQ, LANES), F32), jnp.zeros((TQ, LANES), F32)] for _ in pairs]
        zx0 = [softplus_split(scores(qm, pp, t0, TQ), SB_U, True) for pp in pairs]
        if with_previous:
            zs1 = [scores(qm, pp, prev, SB_NEAR, top) for pp in pairs]
        r0 = [suffix_sums(zx0[pp][1]) for pp in pairs]
        if with_previous:
            zx1 = [softplus_split(zs1[pp], nsub1, False) for pp in pairs]
        for pp in pairs:
            weights_pv(pp, zx0[pp][0], r0[pp], SB_U, True, t0, carry[pp], acc[pp])
        if with_previous:
            r1 = [suffix_sums(zx1[pp][1]) for pp in pairs]
            for pp in pairs:
                weights_pv(pp, zx1[pp][0], r1[pp], nsub1, False, prev, carry[pp], acc[pp], top)
        reach = park(carry, acc)
        reach_ref[0] = jnp.min(reach)
        reach_ref[1] = jnp.min(reach[SB_TOP:])
        reach_ref[2] = 0.0
        for pp in pairs:
            normalise(pp, acc[pp])

    pl.when(qi == 0)(lambda: near_keys(False))
    pl.when(qi > 0)(lambda: near_keys(True))

    def chunk(kstart, rows):
        qm = masked_queries()
        nsub = SB_NEAR // SB_W
        zx = [softplus_split(scores(qm, pp, kstart, SB_NEAR, rows), nsub, False) for pp in pairs]
        r = [suffix_sums(zx[pp][1]) for pp in pairs]
        carry, acc = reload()
        for pp in pairs:
            weights_pv(pp, zx[pp][0], r[pp], nsub, False, kstart, carry[pp], acc[pp], rows)
        reach_ref[0] = jnp.min(park(carry, acc))
        reach_ref[2] = 1.0

    @pl.when(jnp.logical_and(qi > 0, reach_ref[1] < SB_EXIT))
    def _():
        chunk(pl.multiple_of(t0 - SB_NEAR, SB_NEAR), slice(SB_TOP, TQ))

    n_far = qi * (TQ // SB_NEAR) - 1

    def far_chunk(c):
        chunk(pl.multiple_of(t0 - (c + 2) * SB_NEAR, SB_NEAR), slice(0, TQ))
        return c + 1

    lax.while_loop(lambda c: jnp.logical_and(c < n_far, reach_ref[0] < SB_EXIT), far_chunk, 0)

    @pl.when(reach_ref[2] > 0.5)
    def _():
        for pp in pairs:
            normalise(pp, [acc_ref[pp, 0], acc_ref[pp, 1]])


def _sb_attn(proj, tri, g):
    b, s, _ = proj.shape
    nq = s // TQ
    npair = N_SB_HEADS // 2
    wide = SB_PP * LANES
    return pl.pallas_call(
        _sb_attn_kernel,
        grid=(b, npair // SB_PP, nq),
        in_specs=[
            pl.BlockSpec((1, TQ, wide), lambda bi, pi, qi: (bi, qi, QS_BLK // SB_PP + pi)),
            pl.BlockSpec((1, s, wide), lambda bi, pi, qi: (bi, 0, KS_BLK // SB_PP + pi)),
            pl.BlockSpec((1, s, wide), lambda bi, pi, qi: (bi, 0, VS_BLK // SB_PP + pi)),
            pl.BlockSpec((2 * SB_W, 2 * SB_W), lambda bi, pi, qi: (0, 0)),
            pl.BlockSpec((SB_PP, 1, LANES), lambda bi, pi, qi: (pi, 0, 0)),
        ],
        out_specs=pl.BlockSpec((1, TQ, wide), lambda bi, pi, qi: (bi, qi, pi)),
        out_shape=jax.ShapeDtypeStruct((b, s, N_SB_HEADS * HEAD_DIM), BF16),
        scratch_shapes=[pltpu.VMEM((SB_PP, 2, TQ, LANES), F32), pltpu.VMEM((SB_PP, 2, TQ, SB_W), F32),
                        pltpu.SMEM((3,), F32)],
        compiler_params=pltpu.CompilerParams(
            dimension_semantics=("arbitrary", "arbitrary", "arbitrary"),
            vmem_limit_bytes=VMEM_LIMIT),
        name="sb_attn",
    )(proj, proj, proj, tri, g)


def _xkv_kernel(mem_ref, w_ref, o_ref):
    o_ref[0] = _dot(mem_ref[0], w_ref[...]).astype(o_ref.dtype)


def _xkv_proj(mem, w):
    b, n, d = mem.shape
    return pl.pallas_call(
        _xkv_kernel,
        grid=(b,),
        in_specs=[pl.BlockSpec((1, n, d), lambda i: (i, 0, 0)),
                  pl.BlockSpec((d, 2 * d), lambda i: (0, 0), pipeline_mode=pl.Buffered(1))],
        out_specs=pl.BlockSpec((1, n, 2 * d), lambda i: (i, 0, 0)),
        out_shape=jax.ShapeDtypeStruct((b, n, 2 * d), BF16),
        compiler_params=pltpu.CompilerParams(dimension_semantics=("arbitrary",),
                                             vmem_limit_bytes=VMEM_LIMIT),
        name="xkv_proj",
    )(mem, w)


def _tail_kernel(yd_ref, ys_ref, x_ref, kv_ref, wo_ref, g1_ref, b1_ref, wq_ref, wxo_ref,
                 g2_ref, b2_ref, wg_ref, wu_ref, wd_ref, g3_ref, b3_ref, o_ref):
    subs = [slice(t * TM_ROW, (t + 1) * TM_ROW) for t in range(N_SUB)]
    hcols = [slice(h * XHEAD_DIM, (h + 1) * XHEAD_DIM) for h in range(N_XHEADS)]

    def query(x1):
        return (_dot(x1.astype(BF16), wq_ref[...]) * (XHEAD_DIM ** -0.5)).astype(BF16)

    def scores(q):
        return [_dot_nt(q[:, hc], kv_ref[0, :, hc]) for hc in hcols]

    def attend(sc):
        ps = [jnp.exp(sh - jnp.max(sh, axis=-1, keepdims=True)) for sh in sc]
        inv = [1.0 / jnp.sum(p, axis=-1, keepdims=True) for p in ps]
        pv = [_dot(p.astype(BF16), kv_ref[0, :, D_MODEL + h * XHEAD_DIM:D_MODEL + (h + 1) * XHEAD_DIM])
              for h, p in enumerate(ps)]
        return jnp.concatenate([(o * i).astype(BF16) for o, i in zip(pv, inv)], axis=1)

    def out_proj(x1, heads):
        return _layer_norm(ALPHA * x1 + _dot(heads, wxo_ref[...]), g2_ref[...], b2_ref[...])

    def gate_up(x2):
        xb = x2.astype(BF16)
        gate, up = _dot(xb, wg_ref[...]), _dot(xb, wu_ref[...])
        return (gate * jax.nn.sigmoid(gate) * up).astype(BF16)

    a, b = subs
    half = yd_ref.shape[1]
    mix = [_dot(yd_ref[r], wo_ref[:half, :]) + _dot(ys_ref[r], wo_ref[half:, :]) for r in subs]
    x1 = [_layer_norm(ALPHA * x_ref[r] + m, g1_ref[...], b1_ref[...]) for r, m in zip(subs, mix)]
    q_a = query(x1[0])
    sc_a = scores(q_a)
    q_b = query(x1[1])
    heads_a = attend(sc_a)
    sc_b = scores(q_b)
    x2_a = out_proj(x1[0], heads_a)
    heads_b = attend(sc_b)
    x2_b = out_proj(x1[1], heads_b)
    hidden_a = gate_up(x2_a)
    hidden_b = gate_up(x2_b)
    o_ref[a] = _layer_norm(ALPHA * x2_a + _dot(hidden_a, wd_ref[...]), g3_ref[...], b3_ref[...])
    o_ref[b] = _layer_norm(ALPHA * x2_b + _dot(hidden_b, wd_ref[...]), g3_ref[...], b3_ref[...])


def _tail(yd, ys, x2d, kv, wo, g1, b1, wq, wxo, g2, b2, wg, wu, wd, g3, b3, seq):
    t, d = x2d.shape
    half = yd.shape[1]
    n = kv.shape[1]
    f = wg.shape[1]
    tm = N_SUB * TM_ROW
    per_batch = seq // tm
    row = lambda i: (i, 0)
    const = lambda i: (0, 0)
    resident = lambda shape: pl.BlockSpec(shape, const, pipeline_mode=pl.Buffered(1))
    return pl.pallas_call(
        _tail_kernel,
        grid=(t // tm,),
        in_specs=[pl.BlockSpec((tm, half), row), pl.BlockSpec((tm, half), row),
                  pl.BlockSpec((tm, d), row),
                  pl.BlockSpec((1, n, 2 * d), lambda i: (i // per_batch, 0, 0)),
                  resident((d, d)), resident((1, d)), resident((1, d)),
                  resident((d, d)), resident((d, d)), resident((1, d)), resident((1, d)),
                  resident((d, f)), resident((d, f)), resident((f, d)),
                  resident((1, d)), resident((1, d))],
        out_specs=pl.BlockSpec((tm, d), row),
        out_shape=jax.ShapeDtypeStruct((t, d), F32),
        compiler_params=pltpu.CompilerParams(dimension_semantics=("arbitrary",),
                                             vmem_limit_bytes=VMEM_LIMIT),
        name="tail",
    )(yd, ys, x2d, kv, wo, g1, b1, wq, wxo, g2, b2, wg, wu, wd, g3, b3)


def _alibi_tables(slopes):
    slope2 = slopes * LOG2E
    pos = jnp.arange(TQ, dtype=F32)
    rest = slope2[:, None] * pos[None, :]
    key_term = rest
    terms = []
    for _ in range(N_FEAT):
        top = lax.bitcast_convert_type(rest, jnp.uint32) & jnp.uint32(0xFFFF0000)
        t = lax.bitcast_convert_type(top, F32)
        terms.append(t.astype(BF16))
        rest = rest - t
    feat = jnp.stack(terms, axis=-1)
    zeros = lambda n: jnp.zeros((N_DIFF_HEADS, TQ, n), BF16)
    feat1 = jnp.concatenate([zeros(HEAD_DIM), feat, zeros(LANES - HEAD_DIM - N_FEAT)], axis=-1)
    feat2 = jnp.concatenate([feat, zeros(LANES - N_FEAT)], axis=-1)
    i = jnp.arange(TQ, dtype=jnp.int32)[:, None]
    j = jnp.arange(TQ, dtype=jnp.int32)[None, :]
    dist = jnp.abs(i - j).astype(F32)
    allowed = (j // CHUNK) <= (i // CHUNK)
    diag = jnp.where(allowed[None], -slope2[:, None, None] * dist[None], NEG_INF)
    return feat1, feat2, diag - key_term[:, None, :]


def _suffix_sum_matrix():
    j = jnp.arange(2 * SB_W, dtype=jnp.int32)[:, None] % SB_W
    c = jnp.arange(2 * SB_W, dtype=jnp.int32)[None, :]
    return jnp.where((c >= SB_W) | (j >= c), 1.0, 0.0).astype(BF16)


def kernel(x, mem, w_in, diff_lambda_q1, diff_lambda_k1, diff_lambda_q2, diff_lambda_k2,
           diff_subln_g, sb_norm_g, w_o, ln1_g, ln1_b, w_xq, w_xkv, w_xo, ln2_g, ln2_b,
           w_gate, w_up, w_down, ln3_g, ln3_b):
    b, s, d = x.shape
    assert (b, s, d) == (8, SEQ, D_MODEL) and w_in.shape == (DEPTH, D_MODEL, D_IN)
    x2d = x.reshape(b * s, d)
    slopes = jnp.exp2(-8.0 * jnp.arange(1, N_DIFF_HEADS + 1, dtype=F32) / N_DIFF_HEADS)
    feat1, feat2, diagb = _alibi_tables(slopes)
    tri = _suffix_sum_matrix()
    vec = lambda a: a[0].reshape(1, -1)

    proj, (wo16, wq16, wxo16, wg16, wu16, wd16) = _in_proj(
        x2d, w_in[0], [w_o[0], w_xq[0], w_xo[0], w_gate[0], w_up[0], w_down[0]])
    proj = proj.reshape(b, s, D_IN)
    y_diff = _diff_attn(proj, slopes, vec(diff_lambda_q1), vec(diff_lambda_k1),
                        vec(diff_lambda_q2), vec(diff_lambda_k2), feat1, feat2, diagb,
                        vec(diff_subln_g))
    y_sb = _sb_attn(proj, tri, sb_norm_g[0].reshape(N_SB_HEADS // 2, 1, LANES))

    half = N_DIFF_HEADS * DIFF_V_DIM
    kv = _xkv_proj(mem, w_xkv[0])
    out = _tail(y_diff.reshape(b * s, half), y_sb.reshape(b * s, -1), x2d, kv,
                wo16, vec(ln1_g), vec(ln1_b), wq16, wxo16, vec(ln2_g), vec(ln2_b),
                wg16, wu16, wd16, vec(ln3_g), vec(ln3_b), s)
    return out.reshape(b, s, d)
```

```python
import math

import jax
import jax.numpy as jnp
import numpy as np
from jax import lax
from jax.experimental import pallas as pl
from jax.experimental.pallas import tpu as pltpu

D_MODEL = 1024
DEPTH = 1
CHUNK = 64
N_MEM = 256
HEAD_DIM = 64
N_DIFF_HEADS = 4
DIFF_V_DIM = 2 * HEAD_DIM
N_SB_HEADS = 8
N_XHEADS = 4
XHEAD_DIM = D_MODEL // N_XHEADS
D_FF = 2816
ALPHA = (2.0 * DEPTH) ** 0.25
LN_EPS = 1e-5
RMS_EPS = 1e-5
NEG_INF = -1e30
LAMBDA_INIT = 0.8 - 0.6 * math.exp(-0.3 * 0)
LOG2E = math.log2(math.e)

LANES = 128
BF16_SUBLANES = 16
QD_BLK, KD_BLK, VD_BLK, QS_BLK, KS_BLK, VS_BLK = 0, 4, 8, 12, 16, 20
D_IN = 24 * LANES

TM_PROJ = 512
TM_ROW = 256
N_SUB = 2
TQ = 512
SEQ = 2048
NQ = SEQ // TQ
DIFF_HP = 2
DIAG_HALF = TQ // 2
FINITE_LIMIT = 3.0e38
N_FEAT = 3
SB_W = 128
SB_U = TQ // SB_W
SB_NEAR = 256
SB_EXIT = 150.0
SB_TOP = 256
SB_PP = 2
VMEM_LIMIT = 56 * 1024 * 1024

BF16 = jnp.bfloat16
F32 = jnp.float32


def _dot(a, b):
    return jnp.dot(a, b, preferred_element_type=F32)


def _dot_nt(a, b):
    return lax.dot_general(a, b, (((1,), (1,)), ((), ())), preferred_element_type=F32)


def _layer_norm(v, g, b):
    mu = jnp.mean(v, axis=-1, keepdims=True)
    d = v - mu
    var = jnp.mean(d * d, axis=-1, keepdims=True)
    return d * lax.rsqrt(var + LN_EPS) * g + b


def _hi_lo(v):
    hi = v.astype(BF16)
    lo = (v - hi.astype(F32)).astype(BF16)
    return jnp.concatenate([hi, lo], axis=1)


def _in_proj_kernel(x_ref, w_ref, *refs):
    n = len(refs) // 2
    o_ref = refs[n]
    o_ref[...] = _dot(x_ref[...], w_ref[...]).astype(o_ref.dtype)
    for src, dst in zip(refs[:n], refs[n + 1:]):
        dst[...] = src[...].astype(dst.dtype)


def _in_proj(x2d, w, tail_weights):
    t, d = x2d.shape
    n = w.shape[1]
    steps = t // TM_PROJ
    w_specs, w_shapes = [], []
    for tw in tail_weights:
        rows, cols = tw.shape
        if rows % (BF16_SUBLANES * steps) == 0:
            blk, index = rows // steps, (lambda i: (i, 0))
        else:
            assert rows % (BF16_SUBLANES * steps // 2) == 0
            blk, index = 2 * rows // steps, (lambda i: (jnp.minimum(i, steps // 2 - 1), 0))
        w_specs.append(pl.BlockSpec((blk, cols), index))
        w_shapes.append(jax.ShapeDtypeStruct(tw.shape, BF16))
    outs = pl.pallas_call(
        _in_proj_kernel,
        grid=(steps,),
        in_specs=[pl.BlockSpec((TM_PROJ, d), lambda i: (i, 0)),
                  pl.BlockSpec((d, n), lambda i: (0, 0), pipeline_mode=pl.Buffered(1))] + w_specs,
        out_specs=[pl.BlockSpec((TM_PROJ, n), lambda i: (i, 0))] + w_specs,
        out_shape=[jax.ShapeDtypeStruct((t, n), BF16)] + w_shapes,
        compiler_params=pltpu.CompilerParams(dimension_semantics=("arbitrary",),
                                             vmem_limit_bytes=VMEM_LIMIT),
        name="in_proj",
    )(x2d, w, *tail_weights)
    return outs[0], outs[1:]


def _diff_attn_kernel(slopes_ref, lq1_ref, lk1_ref, lq2_ref, lk2_ref, q_ref, k_ref, v_ref,
                      feat1_ref, feat2_ref, diagb_ref, g_ref, o_ref, flag_ref,
                      kaug_ref, vaug_ref, facc_ref):
    hp = pl.program_id(1)
    qi = pl.program_id(2)

    @pl.when(qi == 0)
    def _():
        lane = lax.broadcasted_iota(jnp.int32, (TQ, LANES), 1)
        first = lane < HEAD_DIM
        for hd in range(DIFF_HP):
            cols = slice(hd * LANES, (hd + 1) * LANES)
            f1 = feat1_ref[hd].astype(F32)
            f2 = feat2_ref[hd].astype(F32)
            for blk in range(NQ):
                rows = slice(blk * TQ, (blk + 1) * TQ)
                kblk = k_ref[0, rows, cols].astype(F32)
                kaug_ref[hd, 0, rows, :] = jnp.where(first, kblk, f1).astype(BF16)
                kaug_ref[hd, 1, rows, :] = jnp.where(first, f2, kblk).astype(BF16)
                vaug_ref[hd, rows, :LANES] = v_ref[0, rows, cols]
                vaug_ref[hd, rows, LANES:] = jnp.ones((TQ, LANES), BF16)

    for nb in range(NQ):
        pl.when(qi == nb)(lambda nb=nb: _diff_query_block_fast(
            nb, hp, slopes_ref, lq1_ref, lk1_ref, lq2_ref, lk2_ref, q_ref, k_ref,
            diagb_ref, g_ref, o_ref, flag_ref, kaug_ref, vaug_ref, facc_ref))


def _diff_query_block_fast(nb, hp, slopes_ref, lq1_ref, lk1_ref, lq2_ref, lk2_ref, q_ref, k_ref,
                           diagb_ref, g_ref, o_ref, flag_ref, kaug_ref, vaug_ref, facc_ref):
    heads = range(DIFF_HP)
    hcols = [slice(hd * LANES, (hd + 1) * LANES) for hd in heads]
    slope2 = [slopes_ref[hp * DIFF_HP + hd] * LOG2E for hd in heads]
    t0 = nb * TQ
    lane = lax.broadcasted_iota(jnp.int32, (TQ, LANES), 1)
    first = lane < HEAD_DIM
    ones1 = jnp.where(lane < HEAD_DIM + N_FEAT, 1.0, 0.0)
    ones2 = jnp.where(lane < N_FEAT, 1.0, 0.0)
    row = lax.broadcasted_iota(jnp.int32, (TQ, LANES), 0).astype(F32)
    halves = (slice(0, DIAG_HALF), slice(DIAG_HALF, TQ))

    qm, ref0 = [], []
    for hd in heads:
        qf = q_ref[0, :, hcols[hd]].astype(F32) * (HEAD_DIM ** -0.5 * LOG2E)
        qm.append((jnp.where(first, qf, ones1).astype(BF16), jnp.where(first, ones2, qf).astype(BF16)))
        own = qf * k_ref[0, t0:t0 + TQ, hcols[hd]].astype(F32)
        ref0.append((jnp.sum(jnp.where(first, own, 0.0), axis=-1, keepdims=True),
                     jnp.sum(jnp.where(first, 0.0, own), axis=-1, keepdims=True)))

    units = []
    for blk in range(nb, -1, -1):
        for hd in heads:
            for mp in range(2):
                for rows in halves:
                    nkeys = rows.stop if blk == nb else TQ
                    units.append((hd, mp, rows, blk * TQ, nkeys, blk))

    def qk(u):
        hd, mp, rows, kstart, nkeys, _ = u
        return _dot_nt(qm[hd][mp][rows], kaug_ref[hd, mp, kstart:kstart + nkeys, :])

    seen = set()

    def consume(u, s):
        hd, mp, rows, kstart, nkeys, blk = u
        if blk == nb:
            s = s + diagb_ref[hd, rows, :nkeys]
            shift = -ref0[hd][mp][rows]
        else:
            shift = -slope2[hd] * (row[rows] + float((nb - blk) * TQ)) - ref0[hd][mp][rows]
        p = jnp.concatenate([jnp.exp2(s[:, c * LANES:(c + 1) * LANES] + shift).astype(BF16)
                             for c in range(nkeys // LANES)], axis=1)
        pv = _dot(p, vaug_ref[hd, kstart:kstart + nkeys, :])
        key = (hd, mp, rows.start)
        if key in seen:
            facc_ref[hd, mp, rows] += pv
        else:
            seen.add(key)
            facc_ref[hd, mp, rows] = pv

    s_next = qk(units[0])
    for n, u in enumerate(units):
        s_cur = s_next
        if n + 1 < len(units):
            s_next = qk(units[n + 1])
        consume(u, s_cur)

    lam = (jnp.exp(jnp.sum(lq1_ref[...] * lk1_ref[...]))
           - jnp.exp(jnp.sum(lq2_ref[...] * lk2_ref[...])) + LAMBDA_INIT)
    bad = jnp.zeros((TQ, LANES), F32)
    for hd in heads:
        l1 = facc_ref[hd, 0, :, LANES:]
        l2 = facc_ref[hd, 1, :, LANES:]
        out = facc_ref[hd, 0, :, :LANES] * (1.0 / l1) - lam * (facc_ref[hd, 1, :, :LANES] * (1.0 / l2))
        ms = jnp.mean(out * out, axis=-1, keepdims=True)
        out = out * lax.rsqrt(ms + RMS_EPS) * g_ref[...] * (1.0 - LAMBDA_INIT)
        o_ref[0, :, hcols[hd]] = out.astype(o_ref.dtype)
        for chk in (out, l1 + l2):
            bad = jnp.maximum(bad, jnp.where(jnp.abs(chk) < FINITE_LIMIT, 0.0, 1.0))

    flag_ref[...] = jnp.broadcast_to(jnp.max(bad, axis=0, keepdims=True)[None, None, None],
                                     flag_ref.shape)


def _diff_robust_kernel(slopes_ref, lq1_ref, lk1_ref, lq2_ref, lk2_ref, q_ref, k_ref, v_ref,
                        diagb_ref, g_ref, stale_ref, o_ref, m_ref, l_ref, acc_ref, s_ref, mb_ref):
    del stale_ref
    hp = pl.program_id(1)
    qi = pl.program_id(2)
    for nb in range(NQ):
        pl.when(qi == nb)(lambda nb=nb: _diff_query_block(
            nb, hp, slopes_ref, lq1_ref, lk1_ref, lq2_ref, lk2_ref, q_ref, k_ref, v_ref,
            diagb_ref, g_ref, o_ref, m_ref, l_ref, acc_ref, s_ref, mb_ref))


def _diff_query_block(nb, hp, slopes_ref, lq1_ref, lk1_ref, lq2_ref, lk2_ref, q_ref, k_ref, v_ref,
                      diagb_ref, g_ref, o_ref, m_ref, l_ref, acc_ref, s_ref, mb_ref):
    heads = range(DIFF_HP)
    hcols = [slice(hd * LANES, (hd + 1) * LANES) for hd in heads]
    slope2 = [slopes_ref[hp * DIFF_HP + hd] * LOG2E for hd in heads]

    lane = lax.broadcasted_iota(jnp.int32, (TQ, LANES), 1)
    qm = []
    for hd in heads:
        qf = q_ref[0, :, hcols[hd]].astype(F32) * (HEAD_DIM ** -0.5 * LOG2E)
        qm.append((jnp.where(lane < HEAD_DIM, qf, 0.0).astype(BF16),
                   jnp.where(lane >= HEAD_DIM, qf, 0.0).astype(BF16)))

    m_ref[...] = jnp.full_like(m_ref, NEG_INF)
    l_ref[...] = jnp.zeros_like(l_ref)
    acc_ref[...] = jnp.zeros_like(acc_ref)

    col = lax.broadcasted_iota(jnp.int32, (1, TQ), 1).astype(F32)
    row = lax.broadcasted_iota(jnp.int32, (TQ, LANES), 0).astype(F32)

    full = slice(0, TQ)
    diag_parts = ((slice(0, DIAG_HALF), DIAG_HALF), (slice(DIAG_HALF, TQ), TQ))

    def scores(hd, slot, kstart, nkeys, rows, diagonal):
        kb = k_ref[0, kstart:kstart + nkeys, hcols[hd]]
        bias = slope2[hd] * col[:, :nkeys]
        if diagonal:
            bias = bias + diagb_ref[hd, rows, :nkeys]
        for mp in range(2):
            s = _dot_nt(qm[hd][mp][rows], kb) + bias
            s_ref[hd, slot, mp, rows, :nkeys] = s
            mb_ref[hd, slot, mp, rows] = jnp.broadcast_to(jnp.max(s, axis=-1, keepdims=True),
                                                          (rows.stop - rows.start, LANES))

    def softmax_pv(hd, slot, kstart, nkeys, rows, row_shift):
        vb = v_ref[0, kstart:kstart + nkeys, hcols[hd]]
        for mp in range(2):
            m_old = m_ref[hd, mp, rows]
            mb = mb_ref[hd, slot, mp, rows]
            m_new = jnp.maximum(m_old, mb if row_shift is None else mb + row_shift)
            alpha = jnp.exp2(m_old - m_new)
            mrel = m_new if row_shift is None else m_new - row_shift
            ps = [jnp.exp2(s_ref[hd, slot, mp, rows, c * LANES:(c + 1) * LANES] - mrel)
                  for c in range(nkeys // LANES)]
            psum = ps[0]
            for pc in ps[1:]:
                psum = psum + pc
            l_ref[hd, mp, rows] = alpha * l_ref[hd, mp, rows] + psum
            p = jnp.concatenate([pc.astype(BF16) for pc in ps], axis=1)
            acc_ref[hd, mp, rows] = alpha * acc_ref[hd, mp, rows] + _dot(p, vb)
            m_ref[hd, mp, rows] = m_new

    def issue(hd, i):
        if i < nb:
            scores(hd, i % 2, i * TQ, TQ, full, False)
        else:
            for rows, nkeys in diag_parts:
                scores(hd, i % 2, nb * TQ, nkeys, rows, True)

    def consume(hd, i):
        if i < nb:
            softmax_pv(hd, i % 2, i * TQ, TQ, full, -slope2[hd] * (row + float((nb - i) * TQ)))
        else:
            for rows, nkeys in diag_parts:
                softmax_pv(hd, i % 2, nb * TQ, nkeys, rows, None)

    lam = (jnp.exp(jnp.sum(lq1_ref[...] * lk1_ref[...]))
           - jnp.exp(jnp.sum(lq2_ref[...] * lk2_ref[...])) + LAMBDA_INIT)

    def normalise(hd):
        l1 = jnp.sum(l_ref[hd, 0], axis=-1, keepdims=True)
        l2 = jnp.sum(l_ref[hd, 1], axis=-1, keepdims=True)
        out = acc_ref[hd, 0] * (1.0 / l1) - lam * (acc_ref[hd, 1] * (1.0 / l2))
        ms = jnp.mean(out * out, axis=-1, keepdims=True)
        out = out * lax.rsqrt(ms + RMS_EPS) * g_ref[...] * (1.0 - LAMBDA_INIT)
        o_ref[0, :, hcols[hd]] = out.astype(o_ref.dtype)

    for hd in heads:
        issue(hd, 0)
    for i in range(nb + 1):
        if i < nb:
            for hd in heads:
                issue(hd, i + 1)
        for hd in heads:
            consume(hd, i)
            if i == nb:
                normalise(hd)


def _diff_attn(proj, slopes, lq1, lk1, lq2, lk2, feat1, feat2, diagb, g):
    b, s, _ = proj.shape
    nq = s // TQ
    groups = N_DIFF_HEADS // DIFF_HP
    wide = DIFF_HP * LANES
    lam_spec = pl.BlockSpec((1, HEAD_DIM), lambda bi, hi, qi: (0, 0))
    common_in = [
        pl.BlockSpec(memory_space=pltpu.SMEM),
        lam_spec, lam_spec, lam_spec, lam_spec,
        pl.BlockSpec((1, TQ, wide), lambda bi, hi, qi: (bi, qi, QD_BLK // DIFF_HP + hi)),
        pl.BlockSpec((1, s, wide), lambda bi, hi, qi: (bi, 0, KD_BLK // DIFF_HP + hi)),
        pl.BlockSpec((1, s, wide), lambda bi, hi, qi: (bi, 0, VD_BLK // DIFF_HP + hi)),
    ]
    table_in = [pl.BlockSpec((DIFF_HP, TQ, TQ), lambda bi, hi, qi: (hi, 0, 0)),
                pl.BlockSpec((1, DIFF_V_DIM), lambda bi, hi, qi: (0, 0))]
    feat_in = [pl.BlockSpec((DIFF_HP, TQ, LANES), lambda bi, hi, qi: (hi, 0, 0))] * 2
    out_spec = pl.BlockSpec((1, TQ, wide), lambda bi, hi, qi: (bi, qi, hi))
    out_shape = jax.ShapeDtypeStruct((b, s, N_DIFF_HEADS * DIFF_V_DIM), BF16)
    params = pltpu.CompilerParams(dimension_semantics=("arbitrary", "arbitrary", "arbitrary"),
                                  vmem_limit_bytes=VMEM_LIMIT)
    y_fast, flags = pl.pallas_call(
        _diff_attn_kernel,
        grid=(b, groups, nq),
        in_specs=common_in + feat_in + table_in,
        out_specs=[out_spec, pl.BlockSpec((1, 1, 1, 8, LANES), lambda bi, hi, qi: (bi, hi, qi, 0, 0))],
        out_shape=[out_shape, jax.ShapeDtypeStruct((b, groups, nq, 8, LANES), F32)],
        scratch_shapes=[pltpu.VMEM((DIFF_HP, 2, s, LANES), BF16),
                        pltpu.VMEM((DIFF_HP, s, 2 * LANES), BF16),
                        pltpu.VMEM((DIFF_HP, 2, TQ, 2 * LANES), F32)],
        compiler_params=params,
        name="diff_attn",
    )(slopes, lq1, lk1, lq2, lk2, proj, proj, proj, feat1, feat2, diagb, g)

    def running_max_version():
        return pl.pallas_call(
            _diff_robust_kernel,
            grid=(b, groups, nq),
            in_specs=common_in + table_in + [pl.BlockSpec(memory_space=pl.ANY)],
            out_specs=out_spec,
            out_shape=out_shape,
            scratch_shapes=[pltpu.VMEM((DIFF_HP, 2, TQ, LANES), F32),
                            pltpu.VMEM((DIFF_HP, 2, TQ, LANES), F32),
                            pltpu.VMEM((DIFF_HP, 2, TQ, LANES), F32),
                            pltpu.VMEM((DIFF_HP, 2, 2, TQ, TQ), F32),
                            pltpu.VMEM((DIFF_HP, 2, 2, TQ, LANES), F32)],
            input_output_aliases={len(common_in) + len(table_in): 0},
            compiler_params=params,
            name="diff_attn_running_max",
        )(slopes, lq1, lk1, lq2, lk2, proj, proj, proj, diagb, g, y_fast)

    return lax.cond(jnp.max(flags) > 0.0, running_max_version, lambda: y_fast)


def _sb_attn_kernel(q_ref, k_ref, v_ref, tri_ref, g_ref, o_ref, acc_ref, carry_ref, reach_ref):
    qi = pl.program_id(2)
    pairs = range(SB_PP)
    pcols = [slice(pp * LANES, (pp + 1) * LANES) for pp in pairs]
    lane = lax.broadcasted_iota(jnp.int32, (TQ, LANES), 1)
    lo_half = lane < HEAD_DIM
    row = lax.broadcasted_iota(jnp.int32, (TQ, SB_W), 0)
    colk = lax.broadcasted_iota(jnp.int32, (TQ, SB_W), 1)
    strict = colk < row
    t0 = pl.multiple_of(qi * TQ, TQ)
    tri = tri_ref[...]

    def masked_queries():
        qm = []
        for pp in pairs:
            qf = q_ref[0, :, pcols[pp]].astype(F32) * (HEAD_DIM ** -0.5 * LOG2E)
            qm.append((jnp.where(lo_half, qf, 0.0).astype(BF16),
                       jnp.where(lo_half, 0.0, qf).astype(BF16)))
        return qm

    def scores(qm, pp, kstart, nkeys, rows=slice(0, TQ)):
        kb = k_ref[0, pl.ds(kstart, nkeys), pcols[pp]]
        return [_dot_nt(qm[pp][hh][rows], kb) for hh in range(2)]

    def softplus_split(zs, nsub, own):
        z, x = {}, {}
        for hh in range(2):
            for u in reversed(range(nsub)):
                r0 = u * SB_W if own else 0
                zc = zs[hh][r0:, u * SB_W:(u + 1) * SB_W]
                sp = jnp.maximum(zc, 0.0) + jnp.log(1.0 + jnp.exp2(-jnp.abs(zc))) * LOG2E
                if own:
                    sp = jnp.where(strict[:TQ - r0], sp, 0.0)
                z[hh, u] = zc
                x[hh, u] = _hi_lo(sp)
        return z, x

    def suffix_sums(x):
        return {key: _dot(val, tri) for key, val in x.items()}

    def add_rows(full, lo, hi, delta):
        parts = [full[:lo]] if lo else []
        parts.append(full[lo:hi] + delta)
        if hi < full.shape[0]:
            parts.append(full[hi:])
        return parts[0] if len(parts) == 1 else jnp.concatenate(parts, axis=0)

    def weights_pv(pp, z, r, nsub, own, kstart, carry, acc, rows=slice(0, TQ)):
        vb = v_ref[0, pl.ds(kstart, nsub * SB_W), pcols[pp]]
        lo, hi = rows.start, rows.stop
        for hh in range(2):
            a = {}
            for u in reversed(range(nsub)):
                r0 = u * SB_W if own else lo
                av = jnp.exp2(z[hh, u] - r[hh, u][:, :SB_W] - carry[hh][r0:hi])
                if own:
                    av = jnp.where(strict[:TQ - r0], av, 0.0)
                avb = av.astype(BF16)
                if own and r0:
                    avb = jnp.concatenate([jnp.zeros((r0, SB_W), BF16), avb], axis=0)
                a[u] = avb
                carry[hh] = add_rows(carry[hh], r0, hi, r[hh, u][:, SB_W:])
            pv = _dot(jnp.concatenate([a[u] for u in range(nsub)], axis=1), vb)
            acc[hh] = add_rows(acc[hh], lo, hi, pv)

    def park(carry, acc):
        reach = None
        for pp in pairs:
            for hh in range(2):
                carry_ref[pp, hh] = carry[pp][hh]
                acc_ref[pp, hh] = acc[pp][hh]
                reach = carry[pp][hh] if reach is None else jnp.minimum(reach, carry[pp][hh])
        return reach

    def reload():
        return ([[carry_ref[pp, 0], carry_ref[pp, 1]] for pp in pairs],
                [[acc_ref[pp, 0], acc_ref[pp, 1]] for pp in pairs])

    def normalise(pp, acc):
        out = jnp.where(lo_half, acc[0], acc[1])
        sq = out * out
        ss_lo = jnp.sum(jnp.where(lo_half, sq, 0.0), axis=-1, keepdims=True)
        ss_hi = jnp.sum(jnp.where(lo_half, 0.0, sq), axis=-1, keepdims=True)
        ms = jnp.where(lo_half, ss_lo, ss_hi) * (1.0 / HEAD_DIM)
        o_ref[0, :, pcols[pp]] = (out * lax.rsqrt(ms + RMS_EPS) * g_ref[pp]).astype(o_ref.dtype)

    def near_keys(with_previous):
        qm = masked_queries()
        nsub1 = SB_NEAR // SB_W
        top = slice(0, SB_TOP)
        prev = pl.multiple_of(t0 - SB_NEAR, SB_NEAR)
        carry = [[jnp.zeros((TQ, SB_W), F32), jnp.zeros((TQ, SB_W), F32)] for _ in pairs]
        acc = [[jnp.zeros((TQ, LANES), F32), jnp.zeros((TQ, LANES), F32)] for _ in pairs]
        zx0 = [softplus_split(scores(qm, pp, t0, TQ), SB_U, True) for pp in pairs]
        if with_previous:
            zs1 = [scores(qm, pp, prev, SB_NEAR, top) for pp in pairs]
        r0 = [suffix_sums(zx0[pp][1]) for pp in pairs]
        if with_previous:
            zx1 = [softplus_split(zs1[pp], nsub1, False) for pp in pairs]
        for pp in pairs:
            weights_pv(pp, zx0[pp][0], r0[pp], SB_U, True, t0, carry[pp], acc[pp])
        if with_previous:
            r1 = [suffix_sums(zx1[pp][1]) for pp in pairs]
            for pp in pairs:
                weights_pv(pp, zx1[pp][0], r1[pp], nsub1, False, prev, carry[pp], acc[pp], top)
        reach = park(carry, acc)
        reach_ref[0] = jnp.min(reach)
        reach_ref[1] = jnp.min(reach[SB_TOP:])
        reach_ref[2] = 0.0
        for pp in pairs:
            normalise(pp, acc[pp])

    pl.when(qi == 0)(lambda: near_keys(False))
    pl.when(qi > 0)(lambda: near_keys(True))

    def chunk(kstart, rows):
        qm = masked_queries()
        nsub = SB_NEAR // SB_W
        zx = [softplus_split(scores(qm, pp, kstart, SB_NEAR, rows), nsub, False) for pp in pairs]
        r = [suffix_sums(zx[pp][1]) for pp in pairs]
        carry, acc = reload()
        for pp in pairs:
            weights_pv(pp, zx[pp][0], r[pp], nsub, False, kstart, carry[pp], acc[pp], rows)
        reach_ref[0] = jnp.min(park(carry, acc))
        reach_ref[2] = 1.0

    @pl.when(jnp.logical_and(qi > 0, reach_ref[1] < SB_EXIT))
    def _():
        chunk(pl.multiple_of(t0 - SB_NEAR, SB_NEAR), slice(SB_TOP, TQ))

    n_far = qi * (TQ // SB_NEAR) - 1

    def far_chunk(c):
        chunk(pl.multiple_of(t0 - (c + 2) * SB_NEAR, SB_NEAR), slice(0, TQ))
        return c + 1

    lax.while_loop(lambda c: jnp.logical_and(c < n_far, reach_ref[0] < SB_EXIT), far_chunk, 0)

    @pl.when(reach_ref[2] > 0.5)
    def _():
        for pp in pairs:
            normalise(pp, [acc_ref[pp, 0], acc_ref[pp, 1]])


def _sb_attn(proj, tri, g):
    b, s, _ = proj.shape
    nq = s // TQ
    npair = N_SB_HEADS // 2
    wide = SB_PP * LANES
    return pl.pallas_call(
        _sb_attn_kernel,
        grid=(b, npair // SB_PP, nq),
        in_specs=[
            pl.BlockSpec((1, TQ, wide), lambda bi, pi, qi: (bi, qi, QS_BLK // SB_PP + pi)),
            pl.BlockSpec((1, s, wide), lambda bi, pi, qi: (bi, 0, KS_BLK // SB_PP + pi)),
            pl.BlockSpec((1, s, wide), lambda bi, pi, qi: (bi, 0, VS_BLK // SB_PP + pi)),
            pl.BlockSpec((2 * SB_W, 2 * SB_W), lambda bi, pi, qi: (0, 0)),
            pl.BlockSpec((SB_PP, 1, LANES), lambda bi, pi, qi: (pi, 0, 0)),
        ],
        out_specs=pl.BlockSpec((1, TQ, wide), lambda bi, pi, qi: (bi, qi, pi)),
        out_shape=jax.ShapeDtypeStruct((b, s, N_SB_HEADS * HEAD_DIM), BF16),
        scratch_shapes=[pltpu.VMEM((SB_PP, 2, TQ, LANES), F32), pltpu.VMEM((SB_PP, 2, TQ, SB_W), F32),
                        pltpu.SMEM((3,), F32)],
        compiler_params=pltpu.CompilerParams(
            dimension_semantics=("arbitrary", "arbitrary", "arbitrary"),
            vmem_limit_bytes=VMEM_LIMIT),
        name="sb_attn",
    )(proj, proj, proj, tri, g)


def _xkv_kernel(mem_ref, w_ref, o_ref):
    o_ref[0] = _dot(mem_ref[0], w_ref[...]).astype(o_ref.dtype)


def _xkv_proj(mem, w):
    b, n, d = mem.shape
    return pl.pallas_call(
        _xkv_kernel,
        grid=(b,),
        in_specs=[pl.BlockSpec((1, n, d), lambda i: (i, 0, 0)),
                  pl.BlockSpec((d, 2 * d), lambda i: (0, 0), pipeline_mode=pl.Buffered(1))],
        out_specs=pl.BlockSpec((1, n, 2 * d), lambda i: (i, 0, 0)),
        out_shape=jax.ShapeDtypeStruct((b, n, 2 * d), BF16),
        compiler_params=pltpu.CompilerParams(dimension_semantics=("arbitrary",),
                                             vmem_limit_bytes=VMEM_LIMIT),
        name="xkv_proj",
    )(mem, w)


def _tail_kernel(yd_ref, ys_ref, x_ref, kv_ref, wo_ref, g1_ref, b1_ref, wq_ref, wxo_ref,
                 g2_ref, b2_ref, wg_ref, wu_ref, wd_ref, g3_ref, b3_ref, o_ref):
    subs = [slice(t * TM_ROW, (t + 1) * TM_ROW) for t in range(N_SUB)]
    hcols = [slice(h * XHEAD_DIM, (h + 1) * XHEAD_DIM) for h in range(N_XHEADS)]

    def query(x1):
        return (_dot(x1.astype(BF16), wq_ref[...]) * (XHEAD_DIM ** -0.5)).astype(BF16)

    def scores(q):
        return [_dot_nt(q[:, hc], kv_ref[0, :, hc]) for hc in hcols]

    def attend(sc):
        ps = [jnp.exp(sh - jnp.max(sh, axis=-1, keepdims=True)) for sh in sc]
        inv = [1.0 / jnp.sum(p, axis=-1, keepdims=True) for p in ps]
        pv = [_dot(p.astype(BF16), kv_ref[0, :, D_MODEL + h * XHEAD_DIM:D_MODEL + (h + 1) * XHEAD_DIM])
              for h, p in enumerate(ps)]
        return jnp.concatenate([(o * i).astype(BF16) for o, i in zip(pv, inv)], axis=1)

    def out_proj(x1, heads):
        return _layer_norm(ALPHA * x1 + _dot(heads, wxo_ref[...]), g2_ref[...], b2_ref[...])

    def gate_up(x2):
        xb = x2.astype(BF16)
        gate, up = _dot(xb, wg_ref[...]), _dot(xb, wu_ref[...])
        return (gate * jax.nn.sigmoid(gate) * up).astype(BF16)

    a, b = subs
    half = yd_ref.shape[1]
    mix = [_dot(yd_ref[r], wo_ref[:half, :]) + _dot(ys_ref[r], wo_ref[half:, :]) for r in subs]
    x1 = [_layer_norm(ALPHA * x_ref[r] + m, g1_ref[...], b1_ref[...]) for r, m in zip(subs, mix)]
    q_a = query(x1[0])
    sc_a = scores(q_a)
    q_b = query(x1[1])
    heads_a = attend(sc_a)
    sc_b = scores(q_b)
    x2_a = out_proj(x1[0], heads_a)
    heads_b = attend(sc_b)
    x2_b = out_proj(x1[1], heads_b)
    hidden_a = gate_up(x2_a)
    hidden_b = gate_up(x2_b)
    o_ref[a] = _layer_norm(ALPHA * x2_a + _dot(hidden_a, wd_ref[...]), g3_ref[...], b3_ref[...])
    o_ref[b] = _layer_norm(ALPHA * x2_b + _dot(hidden_b, wd_ref[...]), g3_ref[...], b3_ref[...])


def _tail(yd, ys, x2d, kv, wo, g1, b1, wq, wxo, g2, b2, wg, wu, wd, g3, b3, seq):
    t, d = x2d.shape
    half = yd.shape[1]
    n = kv.shape[1]
    f = wg.shape[1]
    tm = N_SUB * TM_ROW
    per_batch = seq // tm
    row = lambda i: (i, 0)
    const = lambda i: (0, 0)
    resident = lambda shape: pl.BlockSpec(shape, const, pipeline_mode=pl.Buffered(1))
    return pl.pallas_call(
        _tail_kernel,
        grid=(t // tm,),
        in_specs=[pl.BlockSpec((tm, half), row), pl.BlockSpec((tm, half), row),
                  pl.BlockSpec((tm, d), row),
                  pl.BlockSpec((1, n, 2 * d), lambda i: (i // per_batch, 0, 0)),
                  resident((d, d)), resident((1, d)), resident((1, d)),
                  resident((d, d)), resident((d, d)), resident((1, d)), resident((1, d)),
                  resident((d, f)), resident((d, f)), resident((f, d)),
                  resident((1, d)), resident((1, d))],
        out_specs=pl.BlockSpec((tm, d), row),
        out_shape=jax.ShapeDtypeStruct((t, d), F32),
        compiler_params=pltpu.CompilerParams(dimension_semantics=("arbitrary",),
                                             vmem_limit_bytes=VMEM_LIMIT),
        name="tail",
    )(yd, ys, x2d, kv, wo, g1, b1, wq, wxo, g2, b2, wg, wu, wd, g3, b3)


def _alibi_tables():
    f32 = np.float32
    slopes = np.exp2(-8.0 * np.arange(1, N_DIFF_HEADS + 1, dtype=f32) / N_DIFF_HEADS).astype(f32)
    slope2 = (slopes * f32(LOG2E)).astype(f32)
    key_term = (slope2[:, None] * np.arange(TQ, dtype=f32)[None, :]).astype(f32)
    rest, terms = key_term, []
    for _ in range(N_FEAT):
        top = (rest.view(np.uint32) & np.uint32(0xFFFF0000)).view(f32)
        terms.append(top)
        rest = (rest - top).astype(f32)
    feat = np.stack(terms, axis=-1)
    feat1 = np.zeros((N_DIFF_HEADS, TQ, LANES), f32)
    feat2 = np.zeros((N_DIFF_HEADS, TQ, LANES), f32)
    feat1[:, :, HEAD_DIM:HEAD_DIM + N_FEAT] = feat
    feat2[:, :, :N_FEAT] = feat
    i = np.arange(TQ)[:, None]
    j = np.arange(TQ)[None, :]
    dist = np.abs(i - j).astype(f32)
    allowed = (j // CHUNK) <= (i // CHUNK)
    diag = np.where(allowed[None], -slope2[:, None, None] * dist[None], f32(NEG_INF)).astype(f32)
    return (jnp.asarray(slopes), jnp.asarray(feat1, BF16), jnp.asarray(feat2, BF16),
            jnp.asarray(diag - key_term[:, None, :]))


def _suffix_sum_matrix():
    j = np.arange(2 * SB_W)[:, None] % SB_W
    c = np.arange(2 * SB_W)[None, :]
    return jnp.asarray(np.where((c >= SB_W) | (j >= c), 1.0, 0.0), BF16)


def kernel(x, mem, w_in, diff_lambda_q1, diff_lambda_k1, diff_lambda_q2, diff_lambda_k2,
           diff_subln_g, sb_norm_g, w_o, ln1_g, ln1_b, w_xq, w_xkv, w_xo, ln2_g, ln2_b,
           w_gate, w_up, w_down, ln3_g, ln3_b):
    b, s, d = x.shape
    assert (b, s, d) == (8, SEQ, D_MODEL) and w_in.shape == (DEPTH, D_MODEL, D_IN)
    x2d = x.reshape(b * s, d)
    slopes, feat1, feat2, diagb = _alibi_tables()
    tri = _suffix_sum_matrix()
    vec = lambda a: a[0].reshape(1, -1)

    proj, (wo16, wq16, wxo16, wg16, wu16, wd16) = _in_proj(
        x2d, w_in[0], [w_o[0], w_xq[0], w_xo[0], w_gate[0], w_up[0], w_down[0]])
    proj = proj.reshape(b, s, D_IN)
    y_diff = _diff_attn(proj, slopes, vec(diff_lambda_q1), vec(diff_lambda_k1),
                        vec(diff_lambda_q2), vec(diff_lambda_k2), feat1, feat2, diagb,
                        vec(diff_subln_g))
    y_sb = _sb_attn(proj, tri, sb_norm_g[0].reshape(N_SB_HEADS // 2, 1, LANES))

    half = N_DIFF_HEADS * DIFF_V_DIM
    kv = _xkv_proj(mem, w_xkv[0])
    out = _tail(y_diff.reshape(b * s, half), y_sb.reshape(b * s, -1), x2d, kv,
                wo16, vec(ln1_g), vec(ln1_b), wq16, wxo16, vec(ln2_g), vec(ln2_b),
                wg16, wu16, wd16, vec(ln3_g), vec(ln3_b), s)
    return out.reshape(b, s, d)
```

```python
import math

import jax
import jax.numpy as jnp
import numpy as np
from jax import lax
from jax.experimental import pallas as pl
from jax.experimental.pallas import tpu as pltpu

D_MODEL = 1024
DEPTH = 1
CHUNK = 64
N_MEM = 256
HEAD_DIM = 64
N_DIFF_HEADS = 4
DIFF_V_DIM = 2 * HEAD_DIM
N_SB_HEADS = 8
N_XHEADS = 4
XHEAD_DIM = D_MODEL // N_XHEADS
D_FF = 2816
ALPHA = (2.0 * DEPTH) ** 0.25
LN_EPS = 1e-5
RMS_EPS = 1e-5
NEG_INF = -1e30
LAMBDA_INIT = 0.8 - 0.6 * math.exp(-0.3 * 0)
LOG2E = math.log2(math.e)

LANES = 128
BF16_SUBLANES = 16
QD_BLK, KD_BLK, VD_BLK, QS_BLK, KS_BLK, VS_BLK = 0, 4, 8, 12, 16, 20
D_IN = 24 * LANES

TM_PROJ = 512
TM_ROW = 256
N_SUB = 2
TQ = 512
SEQ = 2048
NQ = SEQ // TQ
DIFF_HP = 2
DIAG_HALF = TQ // 2
FINITE_LIMIT = 3.0e38
N_FEAT = 3
SB_W = 128
SB_U = TQ // SB_W
SB_NEAR = 256
SB_EXIT = 150.0
SB_TOP = 256
SB_PP = 2
VMEM_LIMIT = 56 * 1024 * 1024

BF16 = jnp.bfloat16
F32 = jnp.float32


def _dot(a, b):
    return jnp.dot(a, b, preferred_element_type=F32)


def _dot_nt(a, b):
    return lax.dot_general(a, b, (((1,), (1,)), ((), ())), preferred_element_type=F32)


def _layer_norm(v, g, b):
    mu = jnp.mean(v, axis=-1, keepdims=True)
    d = v - mu
    var = jnp.mean(d * d, axis=-1, keepdims=True)
    return d * lax.rsqrt(var + LN_EPS) * g + b


def _hi_lo(v):
    hi = v.astype(BF16)
    lo = (v - hi.astype(F32)).astype(BF16)
    return jnp.concatenate([hi, lo], axis=1)


def _in_proj_kernel(x_ref, w_ref, *refs):
    n = len(refs) // 2
    o_ref = refs[n]
    o_ref[...] = _dot(x_ref[...], w_ref[...]).astype(o_ref.dtype)
    for src, dst in zip(refs[:n], refs[n + 1:]):
        dst[...] = src[...].astype(dst.dtype)


def _in_proj(x2d, w, tail_weights):
    t, d = x2d.shape
    n = w.shape[1]
    steps = t // TM_PROJ
    w_specs, w_shapes = [], []
    for tw in tail_weights:
        rows, cols = tw.shape
        if rows % (BF16_SUBLANES * steps) == 0:
            blk, index = rows // steps, (lambda i: (i, 0))
        else:
            assert rows % (BF16_SUBLANES * steps // 2) == 0
            blk, index = 2 * rows // steps, (lambda i: (jnp.minimum(i, steps // 2 - 1), 0))
        w_specs.append(pl.BlockSpec((blk, cols), index))
        w_shapes.append(jax.ShapeDtypeStruct(tw.shape, BF16))
    outs = pl.pallas_call(
        _in_proj_kernel,
        grid=(steps,),
        in_specs=[pl.BlockSpec((TM_PROJ, d), lambda i: (i, 0)),
                  pl.BlockSpec((d, n), lambda i: (0, 0), pipeline_mode=pl.Buffered(1))] + w_specs,
        out_specs=[pl.BlockSpec((TM_PROJ, n), lambda i: (i, 0))] + w_specs,
        out_shape=[jax.ShapeDtypeStruct((t, n), BF16)] + w_shapes,
        compiler_params=pltpu.CompilerParams(dimension_semantics=("arbitrary",),
                                             vmem_limit_bytes=VMEM_LIMIT),
        name="in_proj",
    )(x2d, w, *tail_weights)
    return outs[0], outs[1:]


def _diff_attn_kernel(slopes_ref, lq1_ref, lk1_ref, lq2_ref, lk2_ref, q_ref, k_ref, v_ref,
                      feat1_ref, feat2_ref, diagb_ref, g_ref, o_ref, flag_ref,
                      kaug_ref, vaug_ref, facc_ref):
    hp = pl.program_id(1)
    qi = pl.program_id(2)

    @pl.when(qi == 0)
    def _():
        lane = lax.broadcasted_iota(jnp.int32, (TQ, LANES), 1)
        first = lane < HEAD_DIM
        for hd in range(DIFF_HP):
            cols = slice(hd * LANES, (hd + 1) * LANES)
            f1 = feat1_ref[hd].astype(F32)
            f2 = feat2_ref[hd].astype(F32)
            for blk in range(NQ):
                rows = slice(blk * TQ, (blk + 1) * TQ)
                kblk = k_ref[0, rows, cols].astype(F32)
                kaug_ref[hd, 0, rows, :] = jnp.where(first, kblk, f1).astype(BF16)
                kaug_ref[hd, 1, rows, :] = jnp.where(first, f2, kblk).astype(BF16)
                vaug_ref[hd, rows, :LANES] = v_ref[0, rows, cols]
                vaug_ref[hd, rows, LANES:] = jnp.ones((TQ, LANES), BF16)

    for nb in range(NQ):
        pl.when(qi == nb)(lambda nb=nb: _diff_query_block_fast(
            nb, hp, slopes_ref, lq1_ref, lk1_ref, lq2_ref, lk2_ref, q_ref, k_ref,
            diagb_ref, g_ref, o_ref, flag_ref, kaug_ref, vaug_ref, facc_ref))


def _diff_query_block_fast(nb, hp, slopes_ref, lq1_ref, lk1_ref, lq2_ref, lk2_ref, q_ref, k_ref,
                           diagb_ref, g_ref, o_ref, flag_ref, kaug_ref, vaug_ref, facc_ref):
    heads = range(DIFF_HP)
    hcols = [slice(hd * LANES, (hd + 1) * LANES) for hd in heads]
    slope2 = [slopes_ref[hp * DIFF_HP + hd] * LOG2E for hd in heads]
    t0 = nb * TQ
    lane = lax.broadcasted_iota(jnp.int32, (TQ, LANES), 1)
    first = lane < HEAD_DIM
    ones1 = jnp.where(lane < HEAD_DIM + N_FEAT, 1.0, 0.0)
    ones2 = jnp.where(lane < N_FEAT, 1.0, 0.0)
    row = lax.broadcasted_iota(jnp.int32, (TQ, LANES), 0).astype(F32)
    halves = (slice(0, DIAG_HALF), slice(DIAG_HALF, TQ))

    qm, ref0 = [], []
    for hd in heads:
        qf = q_ref[0, :, hcols[hd]].astype(F32) * (HEAD_DIM ** -0.5 * LOG2E)
        qm.append((jnp.where(first, qf, ones1).astype(BF16), jnp.where(first, ones2, qf).astype(BF16)))
        own = qf * k_ref[0, t0:t0 + TQ, hcols[hd]].astype(F32)
        ref0.append((jnp.sum(jnp.where(first, own, 0.0), axis=-1, keepdims=True),
                     jnp.sum(jnp.where(first, 0.0, own), axis=-1, keepdims=True)))

    units = []
    for blk in range(nb, -1, -1):
        for hd in heads:
            for mp in range(2):
                for rows in halves:
                    nkeys = rows.stop if blk == nb else TQ
                    units.append((hd, mp, rows, blk * TQ, nkeys, blk))

    def qk(u):
        hd, mp, rows, kstart, nkeys, _ = u
        return _dot_nt(qm[hd][mp][rows], kaug_ref[hd, mp, kstart:kstart + nkeys, :])

    seen = set()

    def consume(u, s):
        hd, mp, rows, kstart, nkeys, blk = u
        if blk == nb:
            s = s + diagb_ref[hd, rows, :nkeys]
            shift = -ref0[hd][mp][rows]
        else:
            shift = -slope2[hd] * (row[rows] + float((nb - blk) * TQ)) - ref0[hd][mp][rows]
        p = jnp.concatenate([jnp.exp2(s[:, c * LANES:(c + 1) * LANES] + shift).astype(BF16)
                             for c in range(nkeys // LANES)], axis=1)
        pv = _dot(p, vaug_ref[hd, kstart:kstart + nkeys, :])
        key = (hd, mp, rows.start)
        if key in seen:
            facc_ref[hd, mp, rows] += pv
        else:
            seen.add(key)
            facc_ref[hd, mp, rows] = pv

    s_next = qk(units[0])
    for n, u in enumerate(units):
        s_cur = s_next
        if n + 1 < len(units):
            s_next = qk(units[n + 1])
        consume(u, s_cur)

    lam = (jnp.exp(jnp.sum(lq1_ref[...] * lk1_ref[...]))
           - jnp.exp(jnp.sum(lq2_ref[...] * lk2_ref[...])) + LAMBDA_INIT)
    bad = jnp.zeros((TQ, LANES), F32)
    for hd in heads:
        l1 = facc_ref[hd, 0, :, LANES:]
        l2 = facc_ref[hd, 1, :, LANES:]
        out = facc_ref[hd, 0, :, :LANES] * (1.0 / l1) - lam * (facc_ref[hd, 1, :, :LANES] * (1.0 / l2))
        ms = jnp.mean(out * out, axis=-1, keepdims=True)
        out = out * lax.rsqrt(ms + RMS_EPS) * g_ref[...] * (1.0 - LAMBDA_INIT)
        o_ref[0, :, hcols[hd]] = out.astype(o_ref.dtype)
        for chk in (out, l1 + l2):
            bad = jnp.maximum(bad, jnp.where(jnp.abs(chk) < FINITE_LIMIT, 0.0, 1.0))

    flag_ref[...] = jnp.broadcast_to(jnp.max(bad, axis=0, keepdims=True)[None, None, None],
                                     flag_ref.shape)


def _diff_robust_kernel(slopes_ref, lq1_ref, lk1_ref, lq2_ref, lk2_ref, q_ref, k_ref, v_ref,
                        diagb_ref, g_ref, o_ref, m_ref, l_ref, acc_ref, s_ref, mb_ref):
    hp = pl.program_id(1)
    qi = pl.program_id(2)
    for nb in range(NQ):
        pl.when(qi == nb)(lambda nb=nb: _diff_query_block(
            nb, hp, slopes_ref, lq1_ref, lk1_ref, lq2_ref, lk2_ref, q_ref, k_ref, v_ref,
            diagb_ref, g_ref, o_ref, m_ref, l_ref, acc_ref, s_ref, mb_ref))


def _diff_query_block(nb, hp, slopes_ref, lq1_ref, lk1_ref, lq2_ref, lk2_ref, q_ref, k_ref, v_ref,
                      diagb_ref, g_ref, o_ref, m_ref, l_ref, acc_ref, s_ref, mb_ref):
    heads = range(DIFF_HP)
    hcols = [slice(hd * LANES, (hd + 1) * LANES) for hd in heads]
    slope2 = [slopes_ref[hp * DIFF_HP + hd] * LOG2E for hd in heads]

    lane = lax.broadcasted_iota(jnp.int32, (TQ, LANES), 1)
    qm = []
    for hd in heads:
        qf = q_ref[0, :, hcols[hd]].astype(F32) * (HEAD_DIM ** -0.5 * LOG2E)
        qm.append((jnp.where(lane < HEAD_DIM, qf, 0.0).astype(BF16),
                   jnp.where(lane >= HEAD_DIM, qf, 0.0).astype(BF16)))

    m_ref[...] = jnp.full_like(m_ref, NEG_INF)
    l_ref[...] = jnp.zeros_like(l_ref)
    acc_ref[...] = jnp.zeros_like(acc_ref)

    col = lax.broadcasted_iota(jnp.int32, (1, TQ), 1).astype(F32)
    row = lax.broadcasted_iota(jnp.int32, (TQ, LANES), 0).astype(F32)

    full = slice(0, TQ)
    diag_parts = ((slice(0, DIAG_HALF), DIAG_HALF), (slice(DIAG_HALF, TQ), TQ))

    def scores(hd, slot, kstart, nkeys, rows, diagonal):
        kb = k_ref[0, kstart:kstart + nkeys, hcols[hd]]
        bias = slope2[hd] * col[:, :nkeys]
        if diagonal:
            bias = bias + diagb_ref[hd, rows, :nkeys]
        for mp in range(2):
            s = _dot_nt(qm[hd][mp][rows], kb) + bias
            s_ref[hd, slot, mp, rows, :nkeys] = s
            mb_ref[hd, slot, mp, rows] = jnp.broadcast_to(jnp.max(s, axis=-1, keepdims=True),
                                                          (rows.stop - rows.start, LANES))

    def softmax_pv(hd, slot, kstart, nkeys, rows, row_shift):
        vb = v_ref[0, kstart:kstart + nkeys, hcols[hd]]
        for mp in range(2):
            m_old = m_ref[hd, mp, rows]
            mb = mb_ref[hd, slot, mp, rows]
            m_new = jnp.maximum(m_old, mb if row_shift is None else mb + row_shift)
            alpha = jnp.exp2(m_old - m_new)
            mrel = m_new if row_shift is None else m_new - row_shift
            ps = [jnp.exp2(s_ref[hd, slot, mp, rows, c * LANES:(c + 1) * LANES] - mrel)
                  for c in range(nkeys // LANES)]
            psum = ps[0]
            for pc in ps[1:]:
                psum = psum + pc
            l_ref[hd, mp, rows] = alpha * l_ref[hd, mp, rows] + psum
            p = jnp.concatenate([pc.astype(BF16) for pc in ps], axis=1)
            acc_ref[hd, mp, rows] = alpha * acc_ref[hd, mp, rows] + _dot(p, vb)
            m_ref[hd, mp, rows] = m_new

    def issue(hd, i):
        if i < nb:
            scores(hd, i % 2, i * TQ, TQ, full, False)
        else:
            for rows, nkeys in diag_parts:
                scores(hd, i % 2, nb * TQ, nkeys, rows, True)

    def consume(hd, i):
        if i < nb:
            softmax_pv(hd, i % 2, i * TQ, TQ, full, -slope2[hd] * (row + float((nb - i) * TQ)))
        else:
            for rows, nkeys in diag_parts:
                softmax_pv(hd, i % 2, nb * TQ, nkeys, rows, None)

    lam = (jnp.exp(jnp.sum(lq1_ref[...] * lk1_ref[...]))
           - jnp.exp(jnp.sum(lq2_ref[...] * lk2_ref[...])) + LAMBDA_INIT)

    def normalise(hd):
        l1 = jnp.sum(l_ref[hd, 0], axis=-1, keepdims=True)
        l2 = jnp.sum(l_ref[hd, 1], axis=-1, keepdims=True)
        out = acc_ref[hd, 0] * (1.0 / l1) - lam * (acc_ref[hd, 1] * (1.0 / l2))
        ms = jnp.mean(out * out, axis=-1, keepdims=True)
        out = out * lax.rsqrt(ms + RMS_EPS) * g_ref[...] * (1.0 - LAMBDA_INIT)
        o_ref[0, :, hcols[hd]] = out.astype(o_ref.dtype)

    for hd in heads:
        issue(hd, 0)
    for i in range(nb + 1):
        if i < nb:
            for hd in heads:
                issue(hd, i + 1)
        for hd in heads:
            consume(hd, i)
            if i == nb:
                normalise(hd)


def _diff_attn(proj, slopes, lq1, lk1, lq2, lk2, feat1, feat2, diagb, g):
    b, s, _ = proj.shape
    nq = s // TQ
    groups = N_DIFF_HEADS // DIFF_HP
    wide = DIFF_HP * LANES
    lam_spec = pl.BlockSpec((1, HEAD_DIM), lambda bi, hi, qi: (0, 0))
    common_in = [
        pl.BlockSpec(memory_space=pltpu.SMEM),
        lam_spec, lam_spec, lam_spec, lam_spec,
        pl.BlockSpec((1, TQ, wide), lambda bi, hi, qi: (bi, qi, QD_BLK // DIFF_HP + hi)),
        pl.BlockSpec((1, s, wide), lambda bi, hi, qi: (bi, 0, KD_BLK // DIFF_HP + hi)),
        pl.BlockSpec((1, s, wide), lambda bi, hi, qi: (bi, 0, VD_BLK // DIFF_HP + hi)),
    ]
    table_in = [pl.BlockSpec((DIFF_HP, TQ, TQ), lambda bi, hi, qi: (hi, 0, 0)),
                pl.BlockSpec((1, DIFF_V_DIM), lambda bi, hi, qi: (0, 0))]
    feat_in = [pl.BlockSpec((DIFF_HP, TQ, LANES), lambda bi, hi, qi: (hi, 0, 0))] * 2
    out_spec = pl.BlockSpec((1, TQ, wide), lambda bi, hi, qi: (bi, qi, hi))
    out_shape = jax.ShapeDtypeStruct((b, s, N_DIFF_HEADS * DIFF_V_DIM), BF16)
    params = pltpu.CompilerParams(dimension_semantics=("arbitrary", "arbitrary", "arbitrary"),
                                  vmem_limit_bytes=VMEM_LIMIT)
    y_fast, flags = pl.pallas_call(
        _diff_attn_kernel,
        grid=(b, groups, nq),
        in_specs=common_in + feat_in + table_in,
        out_specs=[out_spec, pl.BlockSpec((1, 1, 1, 8, LANES), lambda bi, hi, qi: (bi, hi, qi, 0, 0))],
        out_shape=[out_shape, jax.ShapeDtypeStruct((b, groups, nq, 8, LANES), F32)],
        scratch_shapes=[pltpu.VMEM((DIFF_HP, 2, s, LANES), BF16),
                        pltpu.VMEM((DIFF_HP, s, 2 * LANES), BF16),
                        pltpu.VMEM((DIFF_HP, 2, TQ, 2 * LANES), F32)],
        compiler_params=params,
        name="diff_attn",
    )(slopes, lq1, lk1, lq2, lk2, proj, proj, proj, feat1, feat2, diagb, g)

    def running_max_version():
        return pl.pallas_call(
            _diff_robust_kernel,
            grid=(b, groups, nq),
            in_specs=common_in + table_in,
            out_specs=out_spec,
            out_shape=out_shape,
            scratch_shapes=[pltpu.VMEM((DIFF_HP, 2, TQ, LANES), F32),
                            pltpu.VMEM((DIFF_HP, 2, TQ, LANES), F32),
                            pltpu.VMEM((DIFF_HP, 2, TQ, LANES), F32),
                            pltpu.VMEM((DIFF_HP, 2, 2, TQ, TQ), F32),
                            pltpu.VMEM((DIFF_HP, 2, 2, TQ, LANES), F32)],
            compiler_params=params,
            name="diff_attn_running_max",
        )(slopes, lq1, lk1, lq2, lk2, proj, proj, proj, diagb, g)

    return lax.cond(jnp.max(flags) > 0.0, running_max_version, lambda: y_fast)


def _sb_attn_kernel(q_ref, k_ref, v_ref, tri_ref, g_ref, o_ref, acc_ref, carry_ref, reach_ref):
    qi = pl.program_id(2)
    pairs = range(SB_PP)
    pcols = [slice(pp * LANES, (pp + 1) * LANES) for pp in pairs]
    lane = lax.broadcasted_iota(jnp.int32, (TQ, LANES), 1)
    lo_half = lane < HEAD_DIM
    row = lax.broadcasted_iota(jnp.int32, (TQ, SB_W), 0)
    colk = lax.broadcasted_iota(jnp.int32, (TQ, SB_W), 1)
    strict = colk < row
    t0 = pl.multiple_of(qi * TQ, TQ)
    tri = tri_ref[...]

    def masked_queries():
        qm = []
        for pp in pairs:
            qf = q_ref[0, :, pcols[pp]].astype(F32) * (HEAD_DIM ** -0.5 * LOG2E)
            qm.append((jnp.where(lo_half, qf, 0.0).astype(BF16),
                       jnp.where(lo_half, 0.0, qf).astype(BF16)))
        return qm

    def scores(qm, pp, kstart, nkeys, rows=slice(0, TQ)):
        kb = k_ref[0, pl.ds(kstart, nkeys), pcols[pp]]
        return [_dot_nt(qm[pp][hh][rows], kb) for hh in range(2)]

    def softplus_split(zs, nsub, own):
        z, x = {}, {}
        for hh in range(2):
            for u in reversed(range(nsub)):
                r0 = u * SB_W if own else 0
                zc = zs[hh][r0:, u * SB_W:(u + 1) * SB_W]
                sp = jnp.maximum(zc, 0.0) + jnp.log(1.0 + jnp.exp2(-jnp.abs(zc))) * LOG2E
                if own:
                    sp = jnp.where(strict[:TQ - r0], sp, 0.0)
                z[hh, u] = zc
                x[hh, u] = _hi_lo(sp)
        return z, x

    def suffix_sums(x):
        return {key: _dot(val, tri) for key, val in x.items()}

    def add_rows(full, lo, hi, delta):
        parts = [full[:lo]] if lo else []
        parts.append(full[lo:hi] + delta)
        if hi < full.shape[0]:
            parts.append(full[hi:])
        return parts[0] if len(parts) == 1 else jnp.concatenate(parts, axis=0)

    def weights_pv(pp, z, r, nsub, own, kstart, carry, acc, rows=slice(0, TQ)):
        vb = v_ref[0, pl.ds(kstart, nsub * SB_W), pcols[pp]]
        lo, hi = rows.start, rows.stop
        for hh in range(2):
            a = {}
            for u in reversed(range(nsub)):
                r0 = u * SB_W if own else lo
                av = jnp.exp2(z[hh, u] - r[hh, u][:, :SB_W] - carry[hh][r0:hi])
                if own:
                    av = jnp.where(strict[:TQ - r0], av, 0.0)
                avb = av.astype(BF16)
                if own and r0:
                    avb = jnp.concatenate([jnp.zeros((r0, SB_W), BF16), avb], axis=0)
                a[u] = avb
                carry[hh] = add_rows(carry[hh], r0, hi, r[hh, u][:, SB_W:])
            pv = _dot(jnp.concatenate([a[u] for u in range(nsub)], axis=1), vb)
            acc[hh] = add_rows(acc[hh], lo, hi, pv)

    def park(carry, acc):
        reach = None
        for pp in pairs:
            for hh in range(2):
                carry_ref[pp, hh] = carry[pp][hh]
                acc_ref[pp, hh] = acc[pp][hh]
                reach = carry[pp][hh] if reach is None else jnp.minimum(reach, carry[pp][hh])
        return reach

    def reload():
        return ([[carry_ref[pp, 0], carry_ref[pp, 1]] for pp in pairs],
                [[acc_ref[pp, 0], acc_ref[pp, 1]] for pp in pairs])

    def normalise(pp, acc):
        out = jnp.where(lo_half, acc[0], acc[1])
        sq = out * out
        ss_lo = jnp.sum(jnp.where(lo_half, sq, 0.0), axis=-1, keepdims=True)
        ss_hi = jnp.sum(jnp.where(lo_half, 0.0, sq), axis=-1, keepdims=True)
        ms = jnp.where(lo_half, ss_lo, ss_hi) * (1.0 / HEAD_DIM)
        o_ref[0, :, pcols[pp]] = (out * lax.rsqrt(ms + RMS_EPS) * g_ref[pp]).astype(o_ref.dtype)

    def near_keys(with_previous):
        qm = masked_queries()
        nsub1 = SB_NEAR // SB_W
        top = slice(0, SB_TOP)
        prev = pl.multiple_of(t0 - SB_NEAR, SB_NEAR)
        carry = [[jnp.zeros((TQ, SB_W), F32), jnp.zeros((TQ, SB_W), F32)] for _ in pairs]
        acc = [[jnp.zeros((TQ, LANES), F32), jnp.zeros((TQ, LANES), F32)] for _ in pairs]
        zx0 = [softplus_split(scores(qm, pp, t0, TQ), SB_U, True) for pp in pairs]
        if with_previous:
            zs1 = [scores(qm, pp, prev, SB_NEAR, top) for pp in pairs]
        r0 = [suffix_sums(zx0[pp][1]) for pp in pairs]
        if with_previous:
            zx1 = [softplus_split(zs1[pp], nsub1, False) for pp in pairs]
        for pp in pairs:
            weights_pv(pp, zx0[pp][0], r0[pp], SB_U, True, t0, carry[pp], acc[pp])
        if with_previous:
            r1 = [suffix_sums(zx1[pp][1]) for pp in pairs]
            for pp in pairs:
                weights_pv(pp, zx1[pp][0], r1[pp], nsub1, False, prev, carry[pp], acc[pp], top)
        reach = park(carry, acc)
        reach_ref[0] = jnp.min(reach)
        reach_ref[1] = jnp.min(reach[SB_TOP:])
        reach_ref[2] = 0.0
        for pp in pairs:
            normalise(pp, acc[pp])

    pl.when(qi == 0)(lambda: near_keys(False))
    pl.when(qi > 0)(lambda: near_keys(True))

    def chunk(kstart, rows):
        qm = masked_queries()
        nsub = SB_NEAR // SB_W
        zx = [softplus_split(scores(qm, pp, kstart, SB_NEAR, rows), nsub, False) for pp in pairs]
        r = [suffix_sums(zx[pp][1]) for pp in pairs]
        carry, acc = reload()
        for pp in pairs:
            weights_pv(pp, zx[pp][0], r[pp], nsub, False, kstart, carry[pp], acc[pp], rows)
        reach_ref[0] = jnp.min(park(carry, acc))
        reach_ref[2] = 1.0

    @pl.when(jnp.logical_and(qi > 0, reach_ref[1] < SB_EXIT))
    def _():
        chunk(pl.multiple_of(t0 - SB_NEAR, SB_NEAR), slice(SB_TOP, TQ))

    n_far = qi * (TQ // SB_NEAR) - 1

    def far_chunk(c):
        chunk(pl.multiple_of(t0 - (c + 2) * SB_NEAR, SB_NEAR), slice(0, TQ))
        return c + 1

    lax.while_loop(lambda c: jnp.logical_and(c < n_far, reach_ref[0] < SB_EXIT), far_chunk, 0)

    @pl.when(reach_ref[2] > 0.5)
    def _():
        for pp in pairs:
            normalise(pp, [acc_ref[pp, 0], acc_ref[pp, 1]])


def _sb_attn(proj, tri, g):
    b, s, _ = proj.shape
    nq = s // TQ
    npair = N_SB_HEADS // 2
    wide = SB_PP * LANES
    return pl.pallas_call(
        _sb_attn_kernel,
        grid=(b, npair // SB_PP, nq),
        in_specs=[
            pl.BlockSpec((1, TQ, wide), lambda bi, pi, qi: (bi, qi, QS_BLK // SB_PP + pi)),
            pl.BlockSpec((1, s, wide), lambda bi, pi, qi: (bi, 0, KS_BLK // SB_PP + pi)),
            pl.BlockSpec((1, s, wide), lambda bi, pi, qi: (bi, 0, VS_BLK // SB_PP + pi)),
            pl.BlockSpec((2 * SB_W, 2 * SB_W), lambda bi, pi, qi: (0, 0)),
            pl.BlockSpec((SB_PP, 1, LANES), lambda bi, pi, qi: (pi, 0, 0)),
        ],
        out_specs=pl.BlockSpec((1, TQ, wide), lambda bi, pi, qi: (bi, qi, pi)),
        out_shape=jax.ShapeDtypeStruct((b, s, N_SB_HEADS * HEAD_DIM), BF16),
        scratch_shapes=[pltpu.VMEM((SB_PP, 2, TQ, LANES), F32), pltpu.VMEM((SB_PP, 2, TQ, SB_W), F32),
                        pltpu.SMEM((3,), F32)],
        compiler_params=pltpu.CompilerParams(
            dimension_semantics=("arbitrary", "arbitrary", "arbitrary"),
            vmem_limit_bytes=VMEM_LIMIT),
        name="sb_attn",
    )(proj, proj, proj, tri, g)


def _xkv_kernel(mem_ref, w_ref, o_ref):
    o_ref[0] = _dot(mem_ref[0], w_ref[...]).astype(o_ref.dtype)


def _xkv_proj(mem, w):
    b, n, d = mem.shape
    return pl.pallas_call(
        _xkv_kernel,
        grid=(b,),
        in_specs=[pl.BlockSpec((1, n, d), lambda i: (i, 0, 0)),
                  pl.BlockSpec((d, 2 * d), lambda i: (0, 0), pipeline_mode=pl.Buffered(1))],
        out_specs=pl.BlockSpec((1, n, 2 * d), lambda i: (i, 0, 0)),
        out_shape=jax.ShapeDtypeStruct((b, n, 2 * d), BF16),
        compiler_params=pltpu.CompilerParams(dimension_semantics=("arbitrary",),
                                             vmem_limit_bytes=VMEM_LIMIT),
        name="xkv_proj",
    )(mem, w)


def _tail_kernel(yd_ref, ys_ref, x_ref, kv_ref, wo_ref, g1_ref, b1_ref, wq_ref, wxo_ref,
                 g2_ref, b2_ref, wg_ref, wu_ref, wd_ref, g3_ref, b3_ref, o_ref, x1_ref, pre_ref):
    subs = [slice(t * TM_ROW, (t + 1) * TM_ROW) for t in range(N_SUB)]
    hcols = [slice(h * XHEAD_DIM, (h + 1) * XHEAD_DIM) for h in range(N_XHEADS)]

    def query(x1):
        return (_dot(x1.astype(BF16), wq_ref[...]) * (XHEAD_DIM ** -0.5)).astype(BF16)

    def scores(q):
        return [_dot_nt(q[:, hc], kv_ref[0, :, hc]) for hc in hcols]

    def attend(sc):
        ps = [jnp.exp(sh - jnp.max(sh, axis=-1, keepdims=True)) for sh in sc]
        inv = [1.0 / jnp.sum(p, axis=-1, keepdims=True) for p in ps]
        pv = [_dot(p.astype(BF16), kv_ref[0, :, D_MODEL + h * XHEAD_DIM:D_MODEL + (h + 1) * XHEAD_DIM])
              for h, p in enumerate(ps)]
        return jnp.concatenate([(o * i).astype(BF16) for o, i in zip(pv, inv)], axis=1)

    def out_proj(x1, heads):
        return _layer_norm(ALPHA * x1 + _dot(heads, wxo_ref[...]), g2_ref[...], b2_ref[...])

    def gate_up(x2):
        xb = x2.astype(BF16)
        gate, up = _dot(xb, wg_ref[...]), _dot(xb, wu_ref[...])
        return (gate * jax.nn.sigmoid(gate) * up).astype(BF16)

    a, b = subs
    half = yd_ref.shape[1]

    def mixed(r):
        return ALPHA * x_ref[r] + (_dot(yd_ref[r], wo_ref[:half, :]) + _dot(ys_ref[r], wo_ref[half:, :]))

    @pl.when(pl.program_id(0) == 0)
    def _():
        x1_ref[...] = _layer_norm(mixed(a), g1_ref[...], b1_ref[...])
        pre_ref[...] = mixed(b)

    @pl.when(pl.program_id(0) > 0)
    def _():
        x1_a = x1_ref[...]
        q_a = query(x1_a)
        x1_b = _layer_norm(pre_ref[...], g1_ref[...], b1_ref[...])
        sc_a = scores(q_a)
        q_b = query(x1_b)
        heads_a = attend(sc_a)
        sc_b = scores(q_b)
        x2_a = out_proj(x1_a, heads_a)
        heads_b = attend(sc_b)
        x2_b = out_proj(x1_b, heads_b)
        hidden_a = gate_up(x2_a)
        hidden_b = gate_up(x2_b)
        down_a = _dot(hidden_a, wd_ref[...])
        down_b = _dot(hidden_b, wd_ref[...])
        o_ref[a] = _layer_norm(ALPHA * x2_a + down_a, g3_ref[...], b3_ref[...])
        next_a = mixed(a)
        next_b = mixed(b)
        o_ref[b] = _layer_norm(ALPHA * x2_b + down_b, g3_ref[...], b3_ref[...])
        x1_ref[...] = _layer_norm(next_a, g1_ref[...], b1_ref[...])
        pre_ref[...] = next_b


def _tail(yd, ys, x2d, kv, wo, g1, b1, wq, wxo, g2, b2, wg, wu, wd, g3, b3, seq):
    t, d = x2d.shape
    half = yd.shape[1]
    n = kv.shape[1]
    f = wg.shape[1]
    tm = N_SUB * TM_ROW
    per_batch = seq // tm
    blocks = t // tm
    ahead = lambda i: (jnp.minimum(i, blocks - 1), 0)
    done = lambda i: (jnp.maximum(i - 1, 0), 0)
    const = lambda i: (0, 0)
    resident = lambda shape: pl.BlockSpec(shape, const, pipeline_mode=pl.Buffered(1))
    return pl.pallas_call(
        _tail_kernel,
        grid=(blocks + 1,),
        in_specs=[pl.BlockSpec((tm, half), ahead), pl.BlockSpec((tm, half), ahead),
                  pl.BlockSpec((tm, d), ahead),
                  pl.BlockSpec((1, n, 2 * d), lambda i: (jnp.maximum(i - 1, 0) // per_batch, 0, 0)),
                  resident((d, d)), resident((1, d)), resident((1, d)),
                  resident((d, d)), resident((d, d)), resident((1, d)), resident((1, d)),
                  resident((d, f)), resident((d, f)), resident((f, d)),
                  resident((1, d)), resident((1, d))],
        out_specs=pl.BlockSpec((tm, d), done),
        out_shape=jax.ShapeDtypeStruct((t, d), F32),
        scratch_shapes=[pltpu.VMEM((TM_ROW, d), F32), pltpu.VMEM((TM_ROW, d), F32)],
        compiler_params=pltpu.CompilerParams(dimension_semantics=("arbitrary",),
                                             vmem_limit_bytes=VMEM_LIMIT),
        name="tail",
    )(yd, ys, x2d, kv, wo, g1, b1, wq, wxo, g2, b2, wg, wu, wd, g3, b3)


def _alibi_tables():
    f32 = np.float32
    slopes = np.exp2(-8.0 * np.arange(1, N_DIFF_HEADS + 1, dtype=f32) / N_DIFF_HEADS).astype(f32)
    slope2 = (slopes * f32(LOG2E)).astype(f32)
    key_term = (slope2[:, None] * np.arange(TQ, dtype=f32)[None, :]).astype(f32)
    rest, terms = key_term, []
    for _ in range(N_FEAT):
        top = (rest.view(np.uint32) & np.uint32(0xFFFF0000)).view(f32)
        terms.append(top)
        rest = (rest - top).astype(f32)
    feat = np.stack(terms, axis=-1)
    feat1 = np.zeros((N_DIFF_HEADS, TQ, LANES), f32)
    feat2 = np.zeros((N_DIFF_HEADS, TQ, LANES), f32)
    feat1[:, :, HEAD_DIM:HEAD_DIM + N_FEAT] = feat
    feat2[:, :, :N_FEAT] = feat
    i = np.arange(TQ)[:, None]
    j = np.arange(TQ)[None, :]
    dist = np.abs(i - j).astype(f32)
    allowed = (j // CHUNK) <= (i // CHUNK)
    diag = np.where(allowed[None], -slope2[:, None, None] * dist[None], f32(NEG_INF)).astype(f32)
    return (jnp.asarray(slopes), jnp.asarray(feat1, BF16), jnp.asarray(feat2, BF16),
            jnp.asarray(diag - key_term[:, None, :]))


def _suffix_sum_matrix():
    j = np.arange(2 * SB_W)[:, None] % SB_W
    c = np.arange(2 * SB_W)[None, :]
    return jnp.asarray(np.where((c >= SB_W) | (j >= c), 1.0, 0.0), BF16)


def kernel(x, mem, w_in, diff_lambda_q1, diff_lambda_k1, diff_lambda_q2, diff_lambda_k2,
           diff_subln_g, sb_norm_g, w_o, ln1_g, ln1_b, w_xq, w_xkv, w_xo, ln2_g, ln2_b,
           w_gate, w_up, w_down, ln3_g, ln3_b):
    b, s, d = x.shape
    assert (b, s, d) == (8, SEQ, D_MODEL) and w_in.shape == (DEPTH, D_MODEL, D_IN)
    x2d = x.reshape(b * s, d)
    slopes, feat1, feat2, diagb = _alibi_tables()
    tri = _suffix_sum_matrix()
    vec = lambda a: a[0].reshape(1, -1)

    proj, (wo16, wq16, wxo16, wg16, wu16, wd16) = _in_proj(
        x2d, w_in[0], [w_o[0], w_xq[0], w_xo[0], w_gate[0], w_up[0], w_down[0]])
    proj = proj.reshape(b, s, D_IN)
    y_diff = _diff_attn(proj, slopes, vec(diff_lambda_q1), vec(diff_lambda_k1),
                        vec(diff_lambda_q2), vec(diff_lambda_k2), feat1, feat2, diagb,
                        vec(diff_subln_g))
    y_sb = _sb_attn(proj, tri, sb_norm_g[0].reshape(N_SB_HEADS // 2, 1, LANES))

    half = N_DIFF_HEADS * DIFF_V_DIM
    kv = _xkv_proj(mem, w_xkv[0])
    out = _tail(y_diff.reshape(b * s, half), y_sb.reshape(b * s, -1), x2d, kv,
                wo16, vec(ln1_g), vec(ln1_b), wq16, wxo16, vec(ln2_g), vec(ln2_b),
                wg16, wu16, wd16, vec(ln3_g), vec(ln3_b), s)
    return out.reshape(b, s, d)
```

```python
import math

import jax
import jax.numpy as jnp
import numpy as np
from jax import lax
from jax.experimental import pallas as pl
from jax.experimental.pallas import tpu as pltpu

D_MODEL = 1024
DEPTH = 1
CHUNK = 64
N_MEM = 256
HEAD_DIM = 64
N_DIFF_HEADS = 4
DIFF_V_DIM = 2 * HEAD_DIM
N_SB_HEADS = 8
N_XHEADS = 4
XHEAD_DIM = D_MODEL // N_XHEADS
D_FF = 2816
ALPHA = (2.0 * DEPTH) ** 0.25
LN_EPS = 1e-5
RMS_EPS = 1e-5
NEG_INF = -1e30
LAMBDA_INIT = 0.8 - 0.6 * math.exp(-0.3 * 0)
LOG2E = math.log2(math.e)

LANES = 128
BF16_SUBLANES = 16
QD_BLK, KD_BLK, VD_BLK, QS_BLK, KS_BLK, VS_BLK = 0, 4, 8, 12, 16, 20
D_IN = 24 * LANES

TM_PROJ = 512
TM_ROW = 256
N_SUB = 2
TQ = 512
SEQ = 2048
NQ = SEQ // TQ
DIFF_HP = 2
DIAG_HALF = TQ // 2
FINITE_LIMIT = 3.0e38
N_FEAT = 3
SB_W = 128
SB_U = TQ // SB_W
SB_NEAR = 256
SB_EXIT = 150.0
SB_LINEAR = 64.0
SB_TOP = 256
SB_PP = 2
VMEM_LIMIT = 56 * 1024 * 1024

BF16 = jnp.bfloat16
F32 = jnp.float32


def _dot(a, b):
    return jnp.dot(a, b, preferred_element_type=F32)


def _dot_nt(a, b):
    return lax.dot_general(a, b, (((1,), (1,)), ((), ())), preferred_element_type=F32)


def _layer_norm(v, g, b):
    mu = jnp.mean(v, axis=-1, keepdims=True)
    d = v - mu
    var = jnp.mean(d * d, axis=-1, keepdims=True)
    return d * lax.rsqrt(var + LN_EPS) * g + b


def _hi_lo(v):
    hi = v.astype(BF16)
    lo = (v - hi.astype(F32)).astype(BF16)
    return jnp.concatenate([hi, lo], axis=1)


def _in_proj_kernel(x_ref, w_ref, *refs):
    n = len(refs) // 2
    o_ref = refs[n]
    o_ref[...] = _dot(x_ref[...], w_ref[...]).astype(o_ref.dtype)
    for src, dst in zip(refs[:n], refs[n + 1:]):
        dst[...] = src[...].astype(dst.dtype)


def _in_proj(x2d, w, tail_weights):
    t, d = x2d.shape
    n = w.shape[1]
    steps = t // TM_PROJ
    w_specs, w_shapes = [], []
    for tw in tail_weights:
        rows, cols = tw.shape
        if rows % (BF16_SUBLANES * steps) == 0:
            blk, index = rows // steps, (lambda i: (i, 0))
        else:
            assert rows % (BF16_SUBLANES * steps // 2) == 0
            blk, index = 2 * rows // steps, (lambda i: (jnp.minimum(i, steps // 2 - 1), 0))
        w_specs.append(pl.BlockSpec((blk, cols), index))
        w_shapes.append(jax.ShapeDtypeStruct(tw.shape, BF16))
    outs = pl.pallas_call(
        _in_proj_kernel,
        grid=(steps,),
        in_specs=[pl.BlockSpec((TM_PROJ, d), lambda i: (i, 0)),
                  pl.BlockSpec((d, n), lambda i: (0, 0), pipeline_mode=pl.Buffered(1))] + w_specs,
        out_specs=[pl.BlockSpec((TM_PROJ, n), lambda i: (i, 0))] + w_specs,
        out_shape=[jax.ShapeDtypeStruct((t, n), BF16)] + w_shapes,
        compiler_params=pltpu.CompilerParams(dimension_semantics=("arbitrary",),
                                             vmem_limit_bytes=VMEM_LIMIT),
        name="in_proj",
    )(x2d, w, *tail_weights)
    return outs[0], outs[1:]


def _diff_attn_kernel(slopes_ref, lq1_ref, lk1_ref, lq2_ref, lk2_ref, q_ref, k_ref, v_ref,
                      feat1_ref, feat2_ref, diagb_ref, g_ref, o_ref, flag_ref,
                      kaug_ref, vaug_ref, facc_ref):
    hp = pl.program_id(1)
    qi = pl.program_id(2)

    @pl.when(qi == 0)
    def _():
        lane = lax.broadcasted_iota(jnp.int32, (TQ, LANES), 1)
        first = lane < HEAD_DIM
        for hd in range(DIFF_HP):
            cols = slice(hd * LANES, (hd + 1) * LANES)
            f1 = feat1_ref[hd].astype(F32)
            f2 = feat2_ref[hd].astype(F32)
            for blk in range(NQ):
                rows = slice(blk * TQ, (blk + 1) * TQ)
                kblk = k_ref[0, rows, cols].astype(F32)
                kaug_ref[hd, 0, rows, :] = jnp.where(first, kblk, f1).astype(BF16)
                kaug_ref[hd, 1, rows, :] = jnp.where(first, f2, kblk).astype(BF16)
                vaug_ref[hd, rows, :LANES] = v_ref[0, rows, cols]
                vaug_ref[hd, rows, LANES:] = jnp.ones((TQ, LANES), BF16)

    for nb in range(NQ):
        pl.when(qi == nb)(lambda nb=nb: _diff_query_block_fast(
            nb, hp, slopes_ref, lq1_ref, lk1_ref, lq2_ref, lk2_ref, q_ref, k_ref,
            diagb_ref, g_ref, o_ref, flag_ref, kaug_ref, vaug_ref, facc_ref))


def _diff_query_block_fast(nb, hp, slopes_ref, lq1_ref, lk1_ref, lq2_ref, lk2_ref, q_ref, k_ref,
                           diagb_ref, g_ref, o_ref, flag_ref, kaug_ref, vaug_ref, facc_ref):
    heads = range(DIFF_HP)
    hcols = [slice(hd * LANES, (hd + 1) * LANES) for hd in heads]
    slope2 = [slopes_ref[hp * DIFF_HP + hd] * LOG2E for hd in heads]
    t0 = nb * TQ
    lane = lax.broadcasted_iota(jnp.int32, (TQ, LANES), 1)
    first = lane < HEAD_DIM
    ones1 = jnp.where(lane < HEAD_DIM + N_FEAT, 1.0, 0.0)
    ones2 = jnp.where(lane < N_FEAT, 1.0, 0.0)
    row = lax.broadcasted_iota(jnp.int32, (TQ, LANES), 0).astype(F32)
    halves = (slice(0, DIAG_HALF), slice(DIAG_HALF, TQ))

    qm, ref0 = [], []
    for hd in heads:
        qf = q_ref[0, :, hcols[hd]].astype(F32) * (HEAD_DIM ** -0.5 * LOG2E)
        qm.append((jnp.where(first, qf, ones1).astype(BF16), jnp.where(first, ones2, qf).astype(BF16)))
        own = qf * k_ref[0, t0:t0 + TQ, hcols[hd]].astype(F32)
        ref0.append((jnp.sum(jnp.where(first, own, 0.0), axis=-1, keepdims=True),
                     jnp.sum(jnp.where(first, 0.0, own), axis=-1, keepdims=True)))

    units = []
    for blk in range(nb, -1, -1):
        for hd in heads:
            for mp in range(2):
                for rows in halves:
                    nkeys = rows.stop if blk == nb else TQ
                    units.append((hd, mp, rows, blk * TQ, nkeys, blk))

    def qk(u):
        hd, mp, rows, kstart, nkeys, _ = u
        return _dot_nt(qm[hd][mp][rows], kaug_ref[hd, mp, kstart:kstart + nkeys, :])

    seen = set()

    def consume(u, s):
        hd, mp, rows, kstart, nkeys, blk = u
        if blk == nb:
            s = s + diagb_ref[hd, rows, :nkeys]
            shift = -ref0[hd][mp][rows]
        else:
            shift = -slope2[hd] * (row[rows] + float((nb - blk) * TQ)) - ref0[hd][mp][rows]
        p = jnp.concatenate([jnp.exp2(s[:, c * LANES:(c + 1) * LANES] + shift).astype(BF16)
                             for c in range(nkeys // LANES)], axis=1)
        pv = _dot(p, vaug_ref[hd, kstart:kstart + nkeys, :])
        key = (hd, mp, rows.start)
        if key in seen:
            facc_ref[hd, mp, rows] += pv
        else:
            seen.add(key)
            facc_ref[hd, mp, rows] = pv

    s_next = qk(units[0])
    for n, u in enumerate(units):
        s_cur = s_next
        if n + 1 < len(units):
            s_next = qk(units[n + 1])
        consume(u, s_cur)

    lam = (jnp.exp(jnp.sum(lq1_ref[...] * lk1_ref[...]))
           - jnp.exp(jnp.sum(lq2_ref[...] * lk2_ref[...])) + LAMBDA_INIT)
    bad = jnp.zeros((TQ, LANES), F32)
    for hd in heads:
        l1 = facc_ref[hd, 0, :, LANES:]
        l2 = facc_ref[hd, 1, :, LANES:]
        out = facc_ref[hd, 0, :, :LANES] * (1.0 / l1) - lam * (facc_ref[hd, 1, :, :LANES] * (1.0 / l2))
        ms = jnp.mean(out * out, axis=-1, keepdims=True)
        out = out * lax.rsqrt(ms + RMS_EPS) * g_ref[...] * (1.0 - LAMBDA_INIT)
        o_ref[0, :, hcols[hd]] = out.astype(o_ref.dtype)
        for chk in (out, l1 + l2):
            bad = jnp.maximum(bad, jnp.where(jnp.abs(chk) < FINITE_LIMIT, 0.0, 1.0))

    flag_ref[...] = jnp.broadcast_to(jnp.max(bad, axis=0, keepdims=True)[None, None, None],
                                     flag_ref.shape)


def _diff_robust_kernel(slopes_ref, lq1_ref, lk1_ref, lq2_ref, lk2_ref, q_ref, k_ref, v_ref,
                        diagb_ref, g_ref, o_ref, m_ref, l_ref, acc_ref, s_ref, mb_ref):
    hp = pl.program_id(1)
    qi = pl.program_id(2)
    for nb in range(NQ):
        pl.when(qi == nb)(lambda nb=nb: _diff_query_block(
            nb, hp, slopes_ref, lq1_ref, lk1_ref, lq2_ref, lk2_ref, q_ref, k_ref, v_ref,
            diagb_ref, g_ref, o_ref, m_ref, l_ref, acc_ref, s_ref, mb_ref))


def _diff_query_block(nb, hp, slopes_ref, lq1_ref, lk1_ref, lq2_ref, lk2_ref, q_ref, k_ref, v_ref,
                      diagb_ref, g_ref, o_ref, m_ref, l_ref, acc_ref, s_ref, mb_ref):
    heads = range(DIFF_HP)
    hcols = [slice(hd * LANES, (hd + 1) * LANES) for hd in heads]
    slope2 = [slopes_ref[hp * DIFF_HP + hd] * LOG2E for hd in heads]

    lane = lax.broadcasted_iota(jnp.int32, (TQ, LANES), 1)
    qm = []
    for hd in heads:
        qf = q_ref[0, :, hcols[hd]].astype(F32) * (HEAD_DIM ** -0.5 * LOG2E)
        qm.append((jnp.where(lane < HEAD_DIM, qf, 0.0).astype(BF16),
                   jnp.where(lane >= HEAD_DIM, qf, 0.0).astype(BF16)))

    m_ref[...] = jnp.full_like(m_ref, NEG_INF)
    l_ref[...] = jnp.zeros_like(l_ref)
    acc_ref[...] = jnp.zeros_like(acc_ref)

    col = lax.broadcasted_iota(jnp.int32, (1, TQ), 1).astype(F32)
    row = lax.broadcasted_iota(jnp.int32, (TQ, LANES), 0).astype(F32)

    full = slice(0, TQ)
    diag_parts = ((slice(0, DIAG_HALF), DIAG_HALF), (slice(DIAG_HALF, TQ), TQ))

    def scores(hd, slot, kstart, nkeys, rows, diagonal):
        kb = k_ref[0, kstart:kstart + nkeys, hcols[hd]]
        bias = slope2[hd] * col[:, :nkeys]
        if diagonal:
            bias = bias + diagb_ref[hd, rows, :nkeys]
        for mp in range(2):
            s = _dot_nt(qm[hd][mp][rows], kb) + bias
            s_ref[hd, slot, mp, rows, :nkeys] = s
            mb_ref[hd, slot, mp, rows] = jnp.broadcast_to(jnp.max(s, axis=-1, keepdims=True),
                                                          (rows.stop - rows.start, LANES))

    def softmax_pv(hd, slot, kstart, nkeys, rows, row_shift):
        vb = v_ref[0, kstart:kstart + nkeys, hcols[hd]]
        for mp in range(2):
            m_old = m_ref[hd, mp, rows]
            mb = mb_ref[hd, slot, mp, rows]
            m_new = jnp.maximum(m_old, mb if row_shift is None else mb + row_shift)
            alpha = jnp.exp2(m_old - m_new)
            mrel = m_new if row_shift is None else m_new - row_shift
            ps = [jnp.exp2(s_ref[hd, slot, mp, rows, c * LANES:(c + 1) * LANES] - mrel)
                  for c in range(nkeys // LANES)]
            psum = ps[0]
            for pc in ps[1:]:
                psum = psum + pc
            l_ref[hd, mp, rows] = alpha * l_ref[hd, mp, rows] + psum
            p = jnp.concatenate([pc.astype(BF16) for pc in ps], axis=1)
            acc_ref[hd, mp, rows] = alpha * acc_ref[hd, mp, rows] + _dot(p, vb)
            m_ref[hd, mp, rows] = m_new

    def issue(hd, i):
        if i < nb:
            scores(hd, i % 2, i * TQ, TQ, full, False)
        else:
            for rows, nkeys in diag_parts:
                scores(hd, i % 2, nb * TQ, nkeys, rows, True)

    def consume(hd, i):
        if i < nb:
            softmax_pv(hd, i % 2, i * TQ, TQ, full, -slope2[hd] * (row + float((nb - i) * TQ)))
        else:
            for rows, nkeys in diag_parts:
                softmax_pv(hd, i % 2, nb * TQ, nkeys, rows, None)

    lam = (jnp.exp(jnp.sum(lq1_ref[...] * lk1_ref[...]))
           - jnp.exp(jnp.sum(lq2_ref[...] * lk2_ref[...])) + LAMBDA_INIT)

    def normalise(hd):
        l1 = jnp.sum(l_ref[hd, 0], axis=-1, keepdims=True)
        l2 = jnp.sum(l_ref[hd, 1], axis=-1, keepdims=True)
        out = acc_ref[hd, 0] * (1.0 / l1) - lam * (acc_ref[hd, 1] * (1.0 / l2))
        ms = jnp.mean(out * out, axis=-1, keepdims=True)
        out = out * lax.rsqrt(ms + RMS_EPS) * g_ref[...] * (1.0 - LAMBDA_INIT)
        o_ref[0, :, hcols[hd]] = out.astype(o_ref.dtype)

    for hd in heads:
        issue(hd, 0)
    for i in range(nb + 1):
        if i < nb:
            for hd in heads:
                issue(hd, i + 1)
        for hd in heads:
            consume(hd, i)
            if i == nb:
                normalise(hd)


def _diff_attn(proj, slopes, lq1, lk1, lq2, lk2, feat1, feat2, diagb, g):
    b, s, _ = proj.shape
    nq = s // TQ
    groups = N_DIFF_HEADS // DIFF_HP
    wide = DIFF_HP * LANES
    lam_spec = pl.BlockSpec((1, HEAD_DIM), lambda bi, hi, qi: (0, 0))
    common_in = [
        pl.BlockSpec(memory_space=pltpu.SMEM),
        lam_spec, lam_spec, lam_spec, lam_spec,
        pl.BlockSpec((1, TQ, wide), lambda bi, hi, qi: (bi, qi, QD_BLK // DIFF_HP + hi)),
        pl.BlockSpec((1, s, wide), lambda bi, hi, qi: (bi, 0, KD_BLK // DIFF_HP + hi)),
        pl.BlockSpec((1, s, wide), lambda bi, hi, qi: (bi, 0, VD_BLK // DIFF_HP + hi)),
    ]
    table_in = [pl.BlockSpec((DIFF_HP, TQ, TQ), lambda bi, hi, qi: (hi, 0, 0)),
                pl.BlockSpec((1, DIFF_V_DIM), lambda bi, hi, qi: (0, 0))]
    feat_in = [pl.BlockSpec((DIFF_HP, TQ, LANES), lambda bi, hi, qi: (hi, 0, 0))] * 2
    out_spec = pl.BlockSpec((1, TQ, wide), lambda bi, hi, qi: (bi, qi, hi))
    out_shape = jax.ShapeDtypeStruct((b, s, N_DIFF_HEADS * DIFF_V_DIM), BF16)
    params = pltpu.CompilerParams(dimension_semantics=("arbitrary", "arbitrary", "arbitrary"),
                                  vmem_limit_bytes=VMEM_LIMIT)
    y_fast, flags = pl.pallas_call(
        _diff_attn_kernel,
        grid=(b, groups, nq),
        in_specs=common_in + feat_in + table_in,
        out_specs=[out_spec, pl.BlockSpec((1, 1, 1, 8, LANES), lambda bi, hi, qi: (bi, hi, qi, 0, 0))],
        out_shape=[out_shape, jax.ShapeDtypeStruct((b, groups, nq, 8, LANES), F32)],
        scratch_shapes=[pltpu.VMEM((DIFF_HP, 2, s, LANES), BF16),
                        pltpu.VMEM((DIFF_HP, s, 2 * LANES), BF16),
                        pltpu.VMEM((DIFF_HP, 2, TQ, 2 * LANES), F32)],
        compiler_params=params,
        name="diff_attn",
    )(slopes, lq1, lk1, lq2, lk2, proj, proj, proj, feat1, feat2, diagb, g)

    def running_max_version():
        return pl.pallas_call(
            _diff_robust_kernel,
            grid=(b, groups, nq),
            in_specs=common_in + table_in,
            out_specs=out_spec,
            out_shape=out_shape,
            scratch_shapes=[pltpu.VMEM((DIFF_HP, 2, TQ, LANES), F32),
                            pltpu.VMEM((DIFF_HP, 2, TQ, LANES), F32),
                            pltpu.VMEM((DIFF_HP, 2, TQ, LANES), F32),
                            pltpu.VMEM((DIFF_HP, 2, 2, TQ, TQ), F32),
                            pltpu.VMEM((DIFF_HP, 2, 2, TQ, LANES), F32)],
            compiler_params=params,
            name="diff_attn_running_max",
        )(slopes, lq1, lk1, lq2, lk2, proj, proj, proj, diagb, g)

    return lax.cond(jnp.max(flags) > 0.0, running_max_version, lambda: y_fast)


def _sb_attn_kernel(q_ref, k_ref, v_ref, tri_ref, g_ref, o_ref, acc_ref, carry_ref, reach_ref):
    qi = pl.program_id(2)
    pairs = range(SB_PP)
    pcols = [slice(pp * LANES, (pp + 1) * LANES) for pp in pairs]
    lane = lax.broadcasted_iota(jnp.int32, (TQ, LANES), 1)
    lo_half = lane < HEAD_DIM
    row = lax.broadcasted_iota(jnp.int32, (TQ, SB_W), 0)
    colk = lax.broadcasted_iota(jnp.int32, (TQ, SB_W), 1)
    strict = colk < row
    t0 = pl.multiple_of(qi * TQ, TQ)
    tri = tri_ref[...]

    def masked_queries():
        qm = []
        for pp in pairs:
            qf = q_ref[0, :, pcols[pp]].astype(F32) * (HEAD_DIM ** -0.5 * LOG2E)
            qm.append((jnp.where(lo_half, qf, 0.0).astype(BF16),
                       jnp.where(lo_half, 0.0, qf).astype(BF16)))
        return qm

    def scores(qm, pp, kstart, nkeys, rows=slice(0, TQ)):
        kb = k_ref[0, pl.ds(kstart, nkeys), pcols[pp]]
        return [_dot_nt(qm[pp][hh][rows], kb) for hh in range(2)]

    def softplus_split(zs, nsub, own):
        z, x = {}, {}
        for hh in range(2):
            for u in reversed(range(nsub)):
                r0 = u * SB_W if own else 0
                zc = zs[hh][r0:, u * SB_W:(u + 1) * SB_W]
                sp = jnp.where(zc > SB_LINEAR, zc, jnp.log2(1.0 + jnp.exp2(zc)))
                if own:
                    sp = jnp.where(strict[:TQ - r0], sp, 0.0)
                z[hh, u] = zc
                x[hh, u] = _hi_lo(sp)
        return z, x

    def suffix_sums(x):
        return {key: _dot(val, tri) for key, val in x.items()}

    def add_rows(full, lo, hi, delta):
        parts = [full[:lo]] if lo else []
        parts.append(full[lo:hi] + delta)
        if hi < full.shape[0]:
            parts.append(full[hi:])
        return parts[0] if len(parts) == 1 else jnp.concatenate(parts, axis=0)

    def weights_pv(pp, z, r, nsub, own, kstart, carry, acc, rows=slice(0, TQ)):
        vb = v_ref[0, pl.ds(kstart, nsub * SB_W), pcols[pp]]
        lo, hi = rows.start, rows.stop
        for hh in range(2):
            a = {}
            for u in reversed(range(nsub)):
                r0 = u * SB_W if own else lo
                av = jnp.exp2(z[hh, u] - r[hh, u][:, :SB_W] - carry[hh][r0:hi])
                if own:
                    av = jnp.where(strict[:TQ - r0], av, 0.0)
                avb = av.astype(BF16)
                if own and r0:
                    avb = jnp.concatenate([jnp.zeros((r0, SB_W), BF16), avb], axis=0)
                a[u] = avb
                carry[hh] = add_rows(carry[hh], r0, hi, r[hh, u][:, SB_W:])
            pv = _dot(jnp.concatenate([a[u] for u in range(nsub)], axis=1), vb)
            acc[hh] = add_rows(acc[hh], lo, hi, pv)

    def park(carry, acc):
        reach = None
        for pp in pairs:
            for hh in range(2):
                carry_ref[pp, hh] = carry[pp][hh]
                acc_ref[pp, hh] = acc[pp][hh]
                reach = carry[pp][hh] if reach is None else jnp.minimum(reach, carry[pp][hh])
        return reach

    def reload():
        return ([[carry_ref[pp, 0], carry_ref[pp, 1]] for pp in pairs],
                [[acc_ref[pp, 0], acc_ref[pp, 1]] for pp in pairs])

    def normalise(pp, acc):
        out = jnp.where(lo_half, acc[0], acc[1])
        sq = out * out
        ss_lo = jnp.sum(jnp.where(lo_half, sq, 0.0), axis=-1, keepdims=True)
        ss_hi = jnp.sum(jnp.where(lo_half, 0.0, sq), axis=-1, keepdims=True)
        ms = jnp.where(lo_half, ss_lo, ss_hi) * (1.0 / HEAD_DIM)
        o_ref[0, :, pcols[pp]] = (out * lax.rsqrt(ms + RMS_EPS) * g_ref[pp]).astype(o_ref.dtype)

    def near_keys(with_previous):
        qm = masked_queries()
        nsub1 = SB_NEAR // SB_W
        top = slice(0, SB_TOP)
        prev = pl.multiple_of(t0 - SB_NEAR, SB_NEAR)
        carry = [[jnp.zeros((TQ, SB_W), F32), jnp.zeros((TQ, SB_W), F32)] for _ in pairs]
        acc = [[jnp.zeros((TQ, LANES), F32), jnp.zeros((TQ, LANES), F32)] for _ in pairs]
        zx0 = [softplus_split(scores(qm, pp, t0, TQ), SB_U, True) for pp in pairs]
        if with_previous:
            zs1 = [scores(qm, pp, prev, SB_NEAR, top) for pp in pairs]
        r0 = [suffix_sums(zx0[pp][1]) for pp in pairs]
        if with_previous:
            zx1 = [softplus_split(zs1[pp], nsub1, False) for pp in pairs]
        for pp in pairs:
            weights_pv(pp, zx0[pp][0], r0[pp], SB_U, True, t0, carry[pp], acc[pp])
        if with_previous:
            r1 = [suffix_sums(zx1[pp][1]) for pp in pairs]
            for pp in pairs:
                weights_pv(pp, zx1[pp][0], r1[pp], nsub1, False, prev, carry[pp], acc[pp], top)
        reach = park(carry, acc)
        reach_ref[0] = jnp.min(reach)
        reach_ref[1] = jnp.min(reach[SB_TOP:])
        reach_ref[2] = 0.0
        for pp in pairs:
            normalise(pp, acc[pp])

    pl.when(qi == 0)(lambda: near_keys(False))
    pl.when(qi > 0)(lambda: near_keys(True))

    def chunk(kstart, rows):
        qm = masked_queries()
        nsub = SB_NEAR // SB_W
        zx = [softplus_split(scores(qm, pp, kstart, SB_NEAR, rows), nsub, False) for pp in pairs]
        r = [suffix_sums(zx[pp][1]) for pp in pairs]
        carry, acc = reload()
        for pp in pairs:
            weights_pv(pp, zx[pp][0], r[pp], nsub, False, kstart, carry[pp], acc[pp], rows)
        reach_ref[0] = jnp.min(park(carry, acc))
        reach_ref[2] = 1.0

    @pl.when(jnp.logical_and(qi > 0, reach_ref[1] < SB_EXIT))
    def _():
        chunk(pl.multiple_of(t0 - SB_NEAR, SB_NEAR), slice(SB_TOP, TQ))

    n_far = qi * (TQ // SB_NEAR) - 1

    def far_chunk(c):
        chunk(pl.multiple_of(t0 - (c + 2) * SB_NEAR, SB_NEAR), slice(0, TQ))
        return c + 1

    lax.while_loop(lambda c: jnp.logical_and(c < n_far, reach_ref[0] < SB_EXIT), far_chunk, 0)

    @pl.when(reach_ref[2] > 0.5)
    def _():
        for pp in pairs:
            normalise(pp, [acc_ref[pp, 0], acc_ref[pp, 1]])


def _sb_attn(proj, tri, g):
    b, s, _ = proj.shape
    nq = s // TQ
    npair = N_SB_HEADS // 2
    wide = SB_PP * LANES
    return pl.pallas_call(
        _sb_attn_kernel,
        grid=(b, npair // SB_PP, nq),
        in_specs=[
            pl.BlockSpec((1, TQ, wide), lambda bi, pi, qi: (bi, qi, QS_BLK // SB_PP + pi)),
            pl.BlockSpec((1, s, wide), lambda bi, pi, qi: (bi, 0, KS_BLK // SB_PP + pi)),
            pl.BlockSpec((1, s, wide), lambda bi, pi, qi: (bi, 0, VS_BLK // SB_PP + pi)),
            pl.BlockSpec((2 * SB_W, 2 * SB_W), lambda bi, pi, qi: (0, 0)),
            pl.BlockSpec((SB_PP, 1, LANES), lambda bi, pi, qi: (pi, 0, 0)),
        ],
        out_specs=pl.BlockSpec((1, TQ, wide), lambda bi, pi, qi: (bi, qi, pi)),
        out_shape=jax.ShapeDtypeStruct((b, s, N_SB_HEADS * HEAD_DIM), BF16),
        scratch_shapes=[pltpu.VMEM((SB_PP, 2, TQ, LANES), F32), pltpu.VMEM((SB_PP, 2, TQ, SB_W), F32),
                        pltpu.SMEM((3,), F32)],
        compiler_params=pltpu.CompilerParams(
            dimension_semantics=("arbitrary", "arbitrary", "arbitrary"),
            vmem_limit_bytes=VMEM_LIMIT),
        name="sb_attn",
    )(proj, proj, proj, tri, g)


def _xkv_kernel(mem_ref, w_ref, o_ref):
    o_ref[0] = _dot(mem_ref[0], w_ref[...]).astype(o_ref.dtype)


def _xkv_proj(mem, w):
    b, n, d = mem.shape
    return pl.pallas_call(
        _xkv_kernel,
        grid=(b,),
        in_specs=[pl.BlockSpec((1, n, d), lambda i: (i, 0, 0)),
                  pl.BlockSpec((d, 2 * d), lambda i: (0, 0), pipeline_mode=pl.Buffered(1))],
        out_specs=pl.BlockSpec((1, n, 2 * d), lambda i: (i, 0, 0)),
        out_shape=jax.ShapeDtypeStruct((b, n, 2 * d), BF16),
        compiler_params=pltpu.CompilerParams(dimension_semantics=("arbitrary",),
                                             vmem_limit_bytes=VMEM_LIMIT),
        name="xkv_proj",
    )(mem, w)


def _tail_kernel(yd_ref, ys_ref, x_ref, kv_ref, wo_ref, g1_ref, b1_ref, wq_ref, wxo_ref,
                 g2_ref, b2_ref, wg_ref, wu_ref, wd_ref, g3_ref, b3_ref, o_ref):
    subs = [slice(t * TM_ROW, (t + 1) * TM_ROW) for t in range(N_SUB)]
    hcols = [slice(h * XHEAD_DIM, (h + 1) * XHEAD_DIM) for h in range(N_XHEADS)]

    def query(x1):
        return (_dot(x1.astype(BF16), wq_ref[...]) * (XHEAD_DIM ** -0.5)).astype(BF16)

    def scores(q):
        return [_dot_nt(q[:, hc], kv_ref[0, :, hc]) for hc in hcols]

    def attend(sc):
        ps = [jnp.exp(sh - jnp.max(sh, axis=-1, keepdims=True)) for sh in sc]
        inv = [1.0 / jnp.sum(p, axis=-1, keepdims=True) for p in ps]
        pv = [_dot(p.astype(BF16), kv_ref[0, :, D_MODEL + h * XHEAD_DIM:D_MODEL + (h + 1) * XHEAD_DIM])
              for h, p in enumerate(ps)]
        return jnp.concatenate([(o * i).astype(BF16) for o, i in zip(pv, inv)], axis=1)

    def out_proj(x1, heads):
        return _layer_norm(ALPHA * x1 + _dot(heads, wxo_ref[...]), g2_ref[...], b2_ref[...])

    def gate_up(x2):
        xb = x2.astype(BF16)
        gate, up = _dot(xb, wg_ref[...]), _dot(xb, wu_ref[...])
        return (gate * jax.nn.sigmoid(gate) * up).astype(BF16)

    a, b = subs
    half = yd_ref.shape[1]
    mix = [_dot(yd_ref[r], wo_ref[:half, :]) + _dot(ys_ref[r], wo_ref[half:, :]) for r in subs]
    x1 = [_layer_norm(ALPHA * x_ref[r] + m, g1_ref[...], b1_ref[...]) for r, m in zip(subs, mix)]
    q_a = query(x1[0])
    sc_a = scores(q_a)
    q_b = query(x1[1])
    heads_a = attend(sc_a)
    sc_b = scores(q_b)
    x2_a = out_proj(x1[0], heads_a)
    heads_b = attend(sc_b)
    x2_b = out_proj(x1[1], heads_b)
    hidden_a = gate_up(x2_a)
    hidden_b = gate_up(x2_b)
    o_ref[a] = _layer_norm(ALPHA * x2_a + _dot(hidden_a, wd_ref[...]), g3_ref[...], b3_ref[...])
    o_ref[b] = _layer_norm(ALPHA * x2_b + _dot(hidden_b, wd_ref[...]), g3_ref[...], b3_ref[...])


def _tail(yd, ys, x2d, kv, wo, g1, b1, wq, wxo, g2, b2, wg, wu, wd, g3, b3, seq):
    t, d = x2d.shape
    half = yd.shape[1]
    n = kv.shape[1]
    f = wg.shape[1]
    tm = N_SUB * TM_ROW
    per_batch = seq // tm
    row = lambda i: (i, 0)
    const = lambda i: (0, 0)
    resident = lambda shape: pl.BlockSpec(shape, const, pipeline_mode=pl.Buffered(1))
    return pl.pallas_call(
        _tail_kernel,
        grid=(t // tm,),
        in_specs=[pl.BlockSpec((tm, half), row), pl.BlockSpec((tm, half), row),
                  pl.BlockSpec((tm, d), row),
                  pl.BlockSpec((1, n, 2 * d), lambda i: (i // per_batch, 0, 0)),
                  resident((d, d)), resident((1, d)), resident((1, d)),
                  resident((d, d)), resident((d, d)), resident((1, d)), resident((1, d)),
                  resident((d, f)), resident((d, f)), resident((f, d)),
                  resident((1, d)), resident((1, d))],
        out_specs=pl.BlockSpec((tm, d), row),
        out_shape=jax.ShapeDtypeStruct((t, d), F32),
        compiler_params=pltpu.CompilerParams(dimension_semantics=("arbitrary",),
                                             vmem_limit_bytes=VMEM_LIMIT),
        name="tail",
    )(yd, ys, x2d, kv, wo, g1, b1, wq, wxo, g2, b2, wg, wu, wd, g3, b3)


def _alibi_tables():
    f32 = np.float32
    slopes = np.exp2(-8.0 * np.arange(1, N_DIFF_HEADS + 1, dtype=f32) / N_DIFF_HEADS).astype(f32)
    slope2 = (slopes * f32(LOG2E)).astype(f32)
    key_term = (slope2[:, None] * np.arange(TQ, dtype=f32)[None, :]).astype(f32)
    rest, terms = key_term, []
    for _ in range(N_FEAT):
        top = (rest.view(np.uint32) & np.uint32(0xFFFF0000)).view(f32)
        terms.append(top)
        rest = (rest - top).astype(f32)
    feat = np.stack(terms, axis=-1)
    feat1 = np.zeros((N_DIFF_HEADS, TQ, LANES), f32)
    feat2 = np.zeros((N_DIFF_HEADS, TQ, LANES), f32)
    feat1[:, :, HEAD_DIM:HEAD_DIM + N_FEAT] = feat
    feat2[:, :, :N_FEAT] = feat
    i = np.arange(TQ)[:, None]
    j = np.arange(TQ)[None, :]
    dist = np.abs(i - j).astype(f32)
    allowed = (j // CHUNK) <= (i // CHUNK)
    diag = np.where(allowed[None], -slope2[:, None, None] * dist[None], f32(NEG_INF)).astype(f32)
    return (jnp.asarray(slopes), jnp.asarray(feat1, BF16), jnp.asarray(feat2, BF16),
            jnp.asarray(diag - key_term[:, None, :]))


def _suffix_sum_matrix():
    j = np.arange(2 * SB_W)[:, None] % SB_W
    c = np.arange(2 * SB_W)[None, :]
    return jnp.asarray(np.where((c >= SB_W) | (j >= c), 1.0, 0.0), BF16)


def kernel(x, mem, w_in, diff_lambda_q1, diff_lambda_k1, diff_lambda_q2, diff_lambda_k2,
           diff_subln_g, sb_norm_g, w_o, ln1_g, ln1_b, w_xq, w_xkv, w_xo, ln2_g, ln2_b,
           w_gate, w_up, w_down, ln3_g, ln3_b):
    b, s, d = x.shape
    assert (b, s, d) == (8, SEQ, D_MODEL) and w_in.shape == (DEPTH, D_MODEL, D_IN)
    x2d = x.reshape(b * s, d)
    slopes, feat1, feat2, diagb = _alibi_tables()
    tri = _suffix_sum_matrix()
    vec = lambda a: a[0].reshape(1, -1)

    proj, (wo16, wq16, wxo16, wg16, wu16, wd16) = _in_proj(
        x2d, w_in[0], [w_o[0], w_xq[0], w_xo[0], w_gate[0], w_up[0], w_down[0]])
    proj = proj.reshape(b, s, D_IN)
    y_diff = _diff_attn(proj, slopes, vec(diff_lambda_q1), vec(diff_lambda_k1),
                        vec(diff_lambda_q2), vec(diff_lambda_k2), feat1, feat2, diagb,
                        vec(diff_subln_g))
    y_sb = _sb_attn(proj, tri, sb_norm_g[0].reshape(N_SB_HEADS // 2, 1, LANES))

    half = N_DIFF_HEADS * DIFF_V_DIM
    kv = _xkv_proj(mem, w_xkv[0])
    out = _tail(y_diff.reshape(b * s, half), y_sb.reshape(b * s, -1), x2d, kv,
                wo16, vec(ln1_g), vec(ln1_b), wq16, wxo16, vec(ln2_g), vec(ln2_b),
                wg16, wu16, wd16, vec(ln3_g), vec(ln3_b), s)
    return out.reshape(b, s, d)
```

```python
import math

import jax
import jax.numpy as jnp
import numpy as np
from jax import lax
from jax.experimental import pallas as pl
from jax.experimental.pallas import tpu as pltpu

D_MODEL = 1024
DEPTH = 1
CHUNK = 64
N_MEM = 256
HEAD_DIM = 64
N_DIFF_HEADS = 4
DIFF_V_DIM = 2 * HEAD_DIM
N_SB_HEADS = 8
N_XHEADS = 4
XHEAD_DIM = D_MODEL // N_XHEADS
D_FF = 2816
ALPHA = (2.0 * DEPTH) ** 0.25
LN_EPS = 1e-5
RMS_EPS = 1e-5
NEG_INF = -1e30
LAMBDA_INIT = 0.8 - 0.6 * math.exp(-0.3 * 0)
LOG2E = math.log2(math.e)

LANES = 128
BF16_SUBLANES = 16
QD_BLK, KD_BLK, VD_BLK, QS_BLK, KS_BLK, VS_BLK = 0, 4, 8, 12, 16, 20
D_IN = 24 * LANES

TM_PROJ = 1024
TM_ROW = 256
N_SUB = 2
TQ = 512
SEQ = 2048
NQ = SEQ // TQ
DIFF_HP = 2
DIAG_HALF = TQ // 2
FINITE_LIMIT = 3.0e38
N_FEAT = 3
SB_W = 128
SB_U = TQ // SB_W
SB_NEAR = 256
SB_EXIT = 150.0
SB_LINEAR = 64.0
SB_TOP = 256
SB_PP = 2
VMEM_LIMIT = 56 * 1024 * 1024

BF16 = jnp.bfloat16
F32 = jnp.float32


def _dot(a, b):
    return jnp.dot(a, b, preferred_element_type=F32)


def _dot_nt(a, b):
    return lax.dot_general(a, b, (((1,), (1,)), ((), ())), preferred_element_type=F32)


def _layer_norm(v, g, b):
    mu = jnp.mean(v, axis=-1, keepdims=True)
    d = v - mu
    var = jnp.mean(d * d, axis=-1, keepdims=True)
    return d * lax.rsqrt(var + LN_EPS) * g + b


def _hi_lo(v):
    hi = v.astype(BF16)
    lo = (v - hi.astype(F32)).astype(BF16)
    return jnp.concatenate([hi, lo], axis=1)


def _in_proj_kernel(x_ref, w_ref, *refs):
    n = len(refs) // 2
    o_ref = refs[n]
    o_ref[...] = _dot(x_ref[...], w_ref[...]).astype(o_ref.dtype)
    for src, dst in zip(refs[:n], refs[n + 1:]):
        dst[...] = src[...].astype(dst.dtype)


def _in_proj(x2d, w, tail_weights):
    t, d = x2d.shape
    n = w.shape[1]
    steps = t // TM_PROJ
    w_specs, w_shapes = [], []
    for tw in tail_weights:
        rows, cols = tw.shape
        if rows % (BF16_SUBLANES * steps) == 0:
            blk, index = rows // steps, (lambda i: (i, 0))
        else:
            assert rows % (BF16_SUBLANES * steps // 2) == 0
            blk, index = 2 * rows // steps, (lambda i: (jnp.minimum(i, steps // 2 - 1), 0))
        w_specs.append(pl.BlockSpec((blk, cols), index))
        w_shapes.append(jax.ShapeDtypeStruct(tw.shape, BF16))
    outs = pl.pallas_call(
        _in_proj_kernel,
        grid=(steps,),
        in_specs=[pl.BlockSpec((TM_PROJ, d), lambda i: (i, 0)),
                  pl.BlockSpec((d, n), lambda i: (0, 0), pipeline_mode=pl.Buffered(1))] + w_specs,
        out_specs=[pl.BlockSpec((TM_PROJ, n), lambda i: (i, 0))] + w_specs,
        out_shape=[jax.ShapeDtypeStruct((t, n), BF16)] + w_shapes,
        compiler_params=pltpu.CompilerParams(dimension_semantics=("arbitrary",),
                                             vmem_limit_bytes=VMEM_LIMIT),
        name="in_proj",
    )(x2d, w, *tail_weights)
    return outs[0], outs[1:]


def _diff_attn_kernel(slopes_ref, lq1_ref, lk1_ref, lq2_ref, lk2_ref, q_ref, k_ref, v_ref,
                      feat1_ref, feat2_ref, diagb_ref, g_ref, o_ref, flag_ref,
                      kaug_ref, vaug_ref, facc_ref):
    hp = pl.program_id(1)
    qi = pl.program_id(2)

    @pl.when(qi == 0)
    def _():
        lane = lax.broadcasted_iota(jnp.int32, (TQ, LANES), 1)
        first = lane < HEAD_DIM
        for hd in range(DIFF_HP):
            cols = slice(hd * LANES, (hd + 1) * LANES)
            f1 = feat1_ref[hd].astype(F32)
            f2 = feat2_ref[hd].astype(F32)
            for blk in range(NQ):
                rows = slice(blk * TQ, (blk + 1) * TQ)
                kblk = k_ref[0, rows, cols].astype(F32)
                kaug_ref[hd, 0, rows, :] = jnp.where(first, kblk, f1).astype(BF16)
                kaug_ref[hd, 1, rows, :] = jnp.where(first, f2, kblk).astype(BF16)
                vaug_ref[hd, rows, :LANES] = v_ref[0, rows, cols]
                vaug_ref[hd, rows, LANES:] = jnp.ones((TQ, LANES), BF16)

    for nb in range(NQ):
        pl.when(qi == nb)(lambda nb=nb: _diff_query_block_fast(
            nb, hp, slopes_ref, lq1_ref, lk1_ref, lq2_ref, lk2_ref, q_ref, k_ref,
            diagb_ref, g_ref, o_ref, flag_ref, kaug_ref, vaug_ref, facc_ref))


def _diff_query_block_fast(nb, hp, slopes_ref, lq1_ref, lk1_ref, lq2_ref, lk2_ref, q_ref, k_ref,
                           diagb_ref, g_ref, o_ref, flag_ref, kaug_ref, vaug_ref, facc_ref):
    heads = range(DIFF_HP)
    hcols = [slice(hd * LANES, (hd + 1) * LANES) for hd in heads]
    slope2 = [slopes_ref[hp * DIFF_HP + hd] * LOG2E for hd in heads]
    t0 = nb * TQ
    lane = lax.broadcasted_iota(jnp.int32, (TQ, LANES), 1)
    first = lane < HEAD_DIM
    ones1 = jnp.where(lane < HEAD_DIM + N_FEAT, 1.0, 0.0)
    ones2 = jnp.where(lane < N_FEAT, 1.0, 0.0)
    row = lax.broadcasted_iota(jnp.int32, (TQ, LANES), 0).astype(F32)
    halves = (slice(0, DIAG_HALF), slice(DIAG_HALF, TQ))

    qm, ref0 = [], []
    for hd in heads:
        qf = q_ref[0, :, hcols[hd]].astype(F32) * (HEAD_DIM ** -0.5 * LOG2E)
        qm.append((jnp.where(first, qf, ones1).astype(BF16), jnp.where(first, ones2, qf).astype(BF16)))
        own = qf * k_ref[0, t0:t0 + TQ, hcols[hd]].astype(F32)
        ref0.append((jnp.sum(jnp.where(first, own, 0.0), axis=-1, keepdims=True),
                     jnp.sum(jnp.where(first, 0.0, own), axis=-1, keepdims=True)))

    units = []
    for blk in range(nb, -1, -1):
        for hd in heads:
            for mp in range(2):
                for rows in halves:
                    nkeys = rows.stop if blk == nb else TQ
                    units.append((hd, mp, rows, blk * TQ, nkeys, blk))

    def qk(u):
        hd, mp, rows, kstart, nkeys, _ = u
        return _dot_nt(qm[hd][mp][rows], kaug_ref[hd, mp, kstart:kstart + nkeys, :])

    seen = set()

    def consume(u, s):
        hd, mp, rows, kstart, nkeys, blk = u
        if blk == nb:
            s = s + diagb_ref[hd, rows, :nkeys]
            shift = -ref0[hd][mp][rows]
        else:
            shift = -slope2[hd] * (row[rows] + float((nb - blk) * TQ)) - ref0[hd][mp][rows]
        p = jnp.concatenate([jnp.exp2(s[:, c * LANES:(c + 1) * LANES] + shift).astype(BF16)
                             for c in range(nkeys // LANES)], axis=1)
        pv = _dot(p, vaug_ref[hd, kstart:kstart + nkeys, :])
        key = (hd, mp, rows.start)
        if key in seen:
            facc_ref[hd, mp, rows] += pv
        else:
            seen.add(key)
            facc_ref[hd, mp, rows] = pv

    s_next = qk(units[0])
    for n, u in enumerate(units):
        s_cur = s_next
        if n + 1 < len(units):
            s_next = qk(units[n + 1])
        consume(u, s_cur)

    lam = (jnp.exp(jnp.sum(lq1_ref[...] * lk1_ref[...]))
           - jnp.exp(jnp.sum(lq2_ref[...] * lk2_ref[...])) + LAMBDA_INIT)
    bad = jnp.zeros((TQ, LANES), F32)
    for hd in heads:
        l1 = facc_ref[hd, 0, :, LANES:]
        l2 = facc_ref[hd, 1, :, LANES:]
        out = facc_ref[hd, 0, :, :LANES] * (1.0 / l1) - lam * (facc_ref[hd, 1, :, :LANES] * (1.0 / l2))
        ms = jnp.mean(out * out, axis=-1, keepdims=True)
        out = out * lax.rsqrt(ms + RMS_EPS) * g_ref[...] * (1.0 - LAMBDA_INIT)
        o_ref[0, :, hcols[hd]] = out.astype(o_ref.dtype)
        for chk in (out, l1 + l2):
            bad = jnp.maximum(bad, jnp.where(jnp.abs(chk) < FINITE_LIMIT, 0.0, 1.0))

    flag_ref[...] = jnp.broadcast_to(jnp.max(bad, axis=0, keepdims=True)[None, None, None],
                                     flag_ref.shape)


def _diff_robust_kernel(slopes_ref, lq1_ref, lk1_ref, lq2_ref, lk2_ref, q_ref, k_ref, v_ref,
                        diagb_ref, g_ref, o_ref, m_ref, l_ref, acc_ref, s_ref, mb_ref):
    hp = pl.program_id(1)
    qi = pl.program_id(2)
    for nb in range(NQ):
        pl.when(qi == nb)(lambda nb=nb: _diff_query_block(
            nb, hp, slopes_ref, lq1_ref, lk1_ref, lq2_ref, lk2_ref, q_ref, k_ref, v_ref,
            diagb_ref, g_ref, o_ref, m_ref, l_ref, acc_ref, s_ref, mb_ref))


def _diff_query_block(nb, hp, slopes_ref, lq1_ref, lk1_ref, lq2_ref, lk2_ref, q_ref, k_ref, v_ref,
                      diagb_ref, g_ref, o_ref, m_ref, l_ref, acc_ref, s_ref, mb_ref):
    heads = range(DIFF_HP)
    hcols = [slice(hd * LANES, (hd + 1) * LANES) for hd in heads]
    slope2 = [slopes_ref[hp * DIFF_HP + hd] * LOG2E for hd in heads]

    lane = lax.broadcasted_iota(jnp.int32, (TQ, LANES), 1)
    qm = []
    for hd in heads:
        qf = q_ref[0, :, hcols[hd]].astype(F32) * (HEAD_DIM ** -0.5 * LOG2E)
        qm.append((jnp.where(lane < HEAD_DIM, qf, 0.0).astype(BF16),
                   jnp.where(lane >= HEAD_DIM, qf, 0.0).astype(BF16)))

    m_ref[...] = jnp.full_like(m_ref, NEG_INF)
    l_ref[...] = jnp.zeros_like(l_ref)
    acc_ref[...] = jnp.zeros_like(acc_ref)

    col = lax.broadcasted_iota(jnp.int32, (1, TQ), 1).astype(F32)
    row = lax.broadcasted_iota(jnp.int32, (TQ, LANES), 0).astype(F32)

    full = slice(0, TQ)
    diag_parts = ((slice(0, DIAG_HALF), DIAG_HALF), (slice(DIAG_HALF, TQ), TQ))

    def scores(hd, slot, kstart, nkeys, rows, diagonal):
        kb = k_ref[0, kstart:kstart + nkeys, hcols[hd]]
        bias = slope2[hd] * col[:, :nkeys]
        if diagonal:
            bias = bias + diagb_ref[hd, rows, :nkeys]
        for mp in range(2):
            s = _dot_nt(qm[hd][mp][rows], kb) + bias
            s_ref[hd, slot, mp, rows, :nkeys] = s
            mb_ref[hd, slot, mp, rows] = jnp.broadcast_to(jnp.max(s, axis=-1, keepdims=True),
                                                          (rows.stop - rows.start, LANES))

    def softmax_pv(hd, slot, kstart, nkeys, rows, row_shift):
        vb = v_ref[0, kstart:kstart + nkeys, hcols[hd]]
        for mp in range(2):
            m_old = m_ref[hd, mp, rows]
            mb = mb_ref[hd, slot, mp, rows]
            m_new = jnp.maximum(m_old, mb if row_shift is None else mb + row_shift)
            alpha = jnp.exp2(m_old - m_new)
            mrel = m_new if row_shift is None else m_new - row_shift
            ps = [jnp.exp2(s_ref[hd, slot, mp, rows, c * LANES:(c + 1) * LANES] - mrel)
                  for c in range(nkeys // LANES)]
            psum = ps[0]
            for pc in ps[1:]:
                psum = psum + pc
            l_ref[hd, mp, rows] = alpha * l_ref[hd, mp, rows] + psum
            p = jnp.concatenate([pc.astype(BF16) for pc in ps], axis=1)
            acc_ref[hd, mp, rows] = alpha * acc_ref[hd, mp, rows] + _dot(p, vb)
            m_ref[hd, mp, rows] = m_new

    def issue(hd, i):
        if i < nb:
            scores(hd, i % 2, i * TQ, TQ, full, False)
        else:
            for rows, nkeys in diag_parts:
                scores(hd, i % 2, nb * TQ, nkeys, rows, True)

    def consume(hd, i):
        if i < nb:
            softmax_pv(hd, i % 2, i * TQ, TQ, full, -slope2[hd] * (row + float((nb - i) * TQ)))
        else:
            for rows, nkeys in diag_parts:
                softmax_pv(hd, i % 2, nb * TQ, nkeys, rows, None)

    lam = (jnp.exp(jnp.sum(lq1_ref[...] * lk1_ref[...]))
           - jnp.exp(jnp.sum(lq2_ref[...] * lk2_ref[...])) + LAMBDA_INIT)

    def normalise(hd):
        l1 = jnp.sum(l_ref[hd, 0], axis=-1, keepdims=True)
        l2 = jnp.sum(l_ref[hd, 1], axis=-1, keepdims=True)
        out = acc_ref[hd, 0] * (1.0 / l1) - lam * (acc_ref[hd, 1] * (1.0 / l2))
        ms = jnp.mean(out * out, axis=-1, keepdims=True)
        out = out * lax.rsqrt(ms + RMS_EPS) * g_ref[...] * (1.0 - LAMBDA_INIT)
        o_ref[0, :, hcols[hd]] = out.astype(o_ref.dtype)

    for hd in heads:
        issue(hd, 0)
    for i in range(nb + 1):
        if i < nb:
            for hd in heads:
                issue(hd, i + 1)
        for hd in heads:
            consume(hd, i)
            if i == nb:
                normalise(hd)


def _diff_attn(proj, slopes, lq1, lk1, lq2, lk2, feat1, feat2, diagb, g):
    b, s, _ = proj.shape
    nq = s // TQ
    groups = N_DIFF_HEADS // DIFF_HP
    wide = DIFF_HP * LANES
    lam_spec = pl.BlockSpec((1, HEAD_DIM), lambda bi, hi, qi: (0, 0))
    common_in = [
        pl.BlockSpec(memory_space=pltpu.SMEM),
        lam_spec, lam_spec, lam_spec, lam_spec,
        pl.BlockSpec((1, TQ, wide), lambda bi, hi, qi: (bi, qi, QD_BLK // DIFF_HP + hi)),
        pl.BlockSpec((1, s, wide), lambda bi, hi, qi: (bi, 0, KD_BLK // DIFF_HP + hi)),
        pl.BlockSpec((1, s, wide), lambda bi, hi, qi: (bi, 0, VD_BLK // DIFF_HP + hi)),
    ]
    table_in = [pl.BlockSpec((DIFF_HP, TQ, TQ), lambda bi, hi, qi: (hi, 0, 0)),
                pl.BlockSpec((1, DIFF_V_DIM), lambda bi, hi, qi: (0, 0))]
    feat_in = [pl.BlockSpec((DIFF_HP, TQ, LANES), lambda bi, hi, qi: (hi, 0, 0))] * 2
    out_spec = pl.BlockSpec((1, TQ, wide), lambda bi, hi, qi: (bi, qi, hi))
    out_shape = jax.ShapeDtypeStruct((b, s, N_DIFF_HEADS * DIFF_V_DIM), BF16)
    params = pltpu.CompilerParams(dimension_semantics=("arbitrary", "arbitrary", "arbitrary"),
                                  vmem_limit_bytes=VMEM_LIMIT)
    y_fast, flags = pl.pallas_call(
        _diff_attn_kernel,
        grid=(b, groups, nq),
        in_specs=common_in + feat_in + table_in,
        out_specs=[out_spec, pl.BlockSpec((1, 1, 1, 8, LANES), lambda bi, hi, qi: (bi, hi, qi, 0, 0))],
        out_shape=[out_shape, jax.ShapeDtypeStruct((b, groups, nq, 8, LANES), F32)],
        scratch_shapes=[pltpu.VMEM((DIFF_HP, 2, s, LANES), BF16),
                        pltpu.VMEM((DIFF_HP, s, 2 * LANES), BF16),
                        pltpu.VMEM((DIFF_HP, 2, TQ, 2 * LANES), F32)],
        compiler_params=params,
        name="diff_attn",
    )(slopes, lq1, lk1, lq2, lk2, proj, proj, proj, feat1, feat2, diagb, g)

    def running_max_version():
        return pl.pallas_call(
            _diff_robust_kernel,
            grid=(b, groups, nq),
            in_specs=common_in + table_in,
            out_specs=out_spec,
            out_shape=out_shape,
            scratch_shapes=[pltpu.VMEM((DIFF_HP, 2, TQ, LANES), F32),
                            pltpu.VMEM((DIFF_HP, 2, TQ, LANES), F32),
                            pltpu.VMEM((DIFF_HP, 2, TQ, LANES), F32),
                            pltpu.VMEM((DIFF_HP, 2, 2, TQ, TQ), F32),
                            pltpu.VMEM((DIFF_HP, 2, 2, TQ, LANES), F32)],
            compiler_params=params,
            name="diff_attn_running_max",
        )(slopes, lq1, lk1, lq2, lk2, proj, proj, proj, diagb, g)

    return lax.cond(jnp.max(flags) > 0.0, running_max_version, lambda: y_fast)


def _sb_attn_kernel(q_ref, k_ref, v_ref, tri_ref, g_ref, o_ref, acc_ref, carry_ref, reach_ref):
    qi = pl.program_id(2)
    pairs = range(SB_PP)
    pcols = [slice(pp * LANES, (pp + 1) * LANES) for pp in pairs]
    lane = lax.broadcasted_iota(jnp.int32, (TQ, LANES), 1)
    lo_half = lane < HEAD_DIM
    row = lax.broadcasted_iota(jnp.int32, (TQ, SB_W), 0)
    colk = lax.broadcasted_iota(jnp.int32, (TQ, SB_W), 1)
    strict = colk < row
    t0 = pl.multiple_of(qi * TQ, TQ)
    tri = tri_ref[...]

    def masked_queries():
        qm = []
        for pp in pairs:
            qf = q_ref[0, :, pcols[pp]].astype(F32) * (HEAD_DIM ** -0.5 * LOG2E)
            qm.append((jnp.where(lo_half, qf, 0.0).astype(BF16),
                       jnp.where(lo_half, 0.0, qf).astype(BF16)))
        return qm

    def scores(qm, pp, kstart, nkeys, rows=slice(0, TQ)):
        kb = k_ref[0, pl.ds(kstart, nkeys), pcols[pp]]
        return [_dot_nt(qm[pp][hh][rows], kb) for hh in range(2)]

    def softplus_split(zs, nsub, own):
        z, x = {}, {}
        for hh in range(2):
            for u in reversed(range(nsub)):
                r0 = u * SB_W if own else 0
                zc = zs[hh][r0:, u * SB_W:(u + 1) * SB_W]
                sp = jnp.where(zc > SB_LINEAR, zc, jnp.log2(1.0 + jnp.exp2(zc)))
                if own:
                    sp = jnp.where(strict[:TQ - r0], sp, 0.0)
                z[hh, u] = zc
                x[hh, u] = _hi_lo(sp)
        return z, x

    def suffix_sums(x):
        return {key: _dot(val, tri) for key, val in x.items()}

    def add_rows(full, lo, hi, delta):
        parts = [full[:lo]] if lo else []
        parts.append(full[lo:hi] + delta)
        if hi < full.shape[0]:
            parts.append(full[hi:])
        return parts[0] if len(parts) == 1 else jnp.concatenate(parts, axis=0)

    def weights_pv(pp, z, r, nsub, own, kstart, carry, acc, rows=slice(0, TQ)):
        vb = v_ref[0, pl.ds(kstart, nsub * SB_W), pcols[pp]]
        lo, hi = rows.start, rows.stop
        for hh in range(2):
            a = {}
            for u in reversed(range(nsub)):
                r0 = u * SB_W if own else lo
                av = jnp.exp2(z[hh, u] - r[hh, u][:, :SB_W] - carry[hh][r0:hi])
                if own:
                    av = jnp.where(strict[:TQ - r0], av, 0.0)
                avb = av.astype(BF16)
                if own and r0:
                    avb = jnp.concatenate([jnp.zeros((r0, SB_W), BF16), avb], axis=0)
                a[u] = avb
                carry[hh] = add_rows(carry[hh], r0, hi, r[hh, u][:, SB_W:])
            pv = _dot(jnp.concatenate([a[u] for u in range(nsub)], axis=1), vb)
            acc[hh] = add_rows(acc[hh], lo, hi, pv)

    def park(carry, acc):
        reach = None
        for pp in pairs:
            for hh in range(2):
                carry_ref[pp, hh] = carry[pp][hh]
                acc_ref[pp, hh] = acc[pp][hh]
                reach = carry[pp][hh] if reach is None else jnp.minimum(reach, carry[pp][hh])
        return reach

    def reload():
        return ([[carry_ref[pp, 0], carry_ref[pp, 1]] for pp in pairs],
                [[acc_ref[pp, 0], acc_ref[pp, 1]] for pp in pairs])

    def normalise(pp, acc):
        out = jnp.where(lo_half, acc[0], acc[1])
        sq = out * out
        ss_lo = jnp.sum(jnp.where(lo_half, sq, 0.0), axis=-1, keepdims=True)
        ss_hi = jnp.sum(jnp.where(lo_half, 0.0, sq), axis=-1, keepdims=True)
        ms = jnp.where(lo_half, ss_lo, ss_hi) * (1.0 / HEAD_DIM)
        o_ref[0, :, pcols[pp]] = (out * lax.rsqrt(ms + RMS_EPS) * g_ref[pp]).astype(o_ref.dtype)

    def near_keys(with_previous):
        qm = masked_queries()
        nsub1 = SB_NEAR // SB_W
        top = slice(0, SB_TOP)
        prev = pl.multiple_of(t0 - SB_NEAR, SB_NEAR)
        carry = [[jnp.zeros((TQ, SB_W), F32), jnp.zeros((TQ, SB_W), F32)] for _ in pairs]
        acc = [[jnp.zeros((TQ, LANES), F32), jnp.zeros((TQ, LANES), F32)] for _ in pairs]
        zx0 = [softplus_split(scores(qm, pp, t0, TQ), SB_U, True) for pp in pairs]
        if with_previous:
            zs1 = [scores(qm, pp, prev, SB_NEAR, top) for pp in pairs]
        r0 = [suffix_sums(zx0[pp][1]) for pp in pairs]
        if with_previous:
            zx1 = [softplus_split(zs1[pp], nsub1, False) for pp in pairs]
        for pp in pairs:
            weights_pv(pp, zx0[pp][0], r0[pp], SB_U, True, t0, carry[pp], acc[pp])
        if with_previous:
            r1 = [suffix_sums(zx1[pp][1]) for pp in pairs]
            for pp in pairs:
                weights_pv(pp, zx1[pp][0], r1[pp], nsub1, False, prev, carry[pp], acc[pp], top)
        reach = park(carry, acc)
        reach_ref[0] = jnp.min(reach)
        reach_ref[1] = jnp.min(reach[SB_TOP:])
        reach_ref[2] = 0.0
        for pp in pairs:
            normalise(pp, acc[pp])

    pl.when(qi == 0)(lambda: near_keys(False))
    pl.when(qi > 0)(lambda: near_keys(True))

    def chunk(kstart, rows):
        qm = masked_queries()
        nsub = SB_NEAR // SB_W
        zx = [softplus_split(scores(qm, pp, kstart, SB_NEAR, rows), nsub, False) for pp in pairs]
        r = [suffix_sums(zx[pp][1]) for pp in pairs]
        carry, acc = reload()
        for pp in pairs:
            weights_pv(pp, zx[pp][0], r[pp], nsub, False, kstart, carry[pp], acc[pp], rows)
        reach_ref[0] = jnp.min(park(carry, acc))
        reach_ref[2] = 1.0

    @pl.when(jnp.logical_and(qi > 0, reach_ref[1] < SB_EXIT))
    def _():
        chunk(pl.multiple_of(t0 - SB_NEAR, SB_NEAR), slice(SB_TOP, TQ))

    n_far = qi * (TQ // SB_NEAR) - 1

    def far_chunk(c):
        chunk(pl.multiple_of(t0 - (c + 2) * SB_NEAR, SB_NEAR), slice(0, TQ))
        return c + 1

    lax.while_loop(lambda c: jnp.logical_and(c < n_far, reach_ref[0] < SB_EXIT), far_chunk, 0)

    @pl.when(reach_ref[2] > 0.5)
    def _():
        for pp in pairs:
            normalise(pp, [acc_ref[pp, 0], acc_ref[pp, 1]])


def _sb_attn(proj, tri, g):
    b, s, _ = proj.shape
    nq = s // TQ
    npair = N_SB_HEADS // 2
    wide = SB_PP * LANES
    return pl.pallas_call(
        _sb_attn_kernel,
        grid=(b, npair // SB_PP, nq),
        in_specs=[
            pl.BlockSpec((1, TQ, wide), lambda bi, pi, qi: (bi, qi, QS_BLK // SB_PP + pi)),
            pl.BlockSpec((1, s, wide), lambda bi, pi, qi: (bi, 0, KS_BLK // SB_PP + pi)),
            pl.BlockSpec((1, s, wide), lambda bi, pi, qi: (bi, 0, VS_BLK // SB_PP + pi)),
            pl.BlockSpec((2 * SB_W, 2 * SB_W), lambda bi, pi, qi: (0, 0)),
            pl.BlockSpec((SB_PP, 1, LANES), lambda bi, pi, qi: (pi, 0, 0)),
        ],
        out_specs=pl.BlockSpec((1, TQ, wide), lambda bi, pi, qi: (bi, qi, pi)),
        out_shape=jax.ShapeDtypeStruct((b, s, N_SB_HEADS * HEAD_DIM), BF16),
        scratch_shapes=[pltpu.VMEM((SB_PP, 2, TQ, LANES), F32), pltpu.VMEM((SB_PP, 2, TQ, SB_W), F32),
                        pltpu.SMEM((3,), F32)],
        compiler_params=pltpu.CompilerParams(
            dimension_semantics=("arbitrary", "arbitrary", "arbitrary"),
            vmem_limit_bytes=VMEM_LIMIT),
        name="sb_attn",
    )(proj, proj, proj, tri, g)


def _xkv_kernel(mem_ref, w_ref, o_ref):
    nb, n, d = mem_ref.shape
    kv = _dot(mem_ref[...].reshape(nb * n, d), w_ref[...])
    o_ref[...] = kv.reshape(nb, n, kv.shape[-1]).astype(o_ref.dtype)


def _xkv_proj(mem, w):
    b, n, d = mem.shape
    nb = TM_PROJ // n
    return pl.pallas_call(
        _xkv_kernel,
        grid=(b // nb,),
        in_specs=[pl.BlockSpec((nb, n, d), lambda i: (i, 0, 0)),
                  pl.BlockSpec((d, 2 * d), lambda i: (0, 0), pipeline_mode=pl.Buffered(1))],
        out_specs=pl.BlockSpec((nb, n, 2 * d), lambda i: (i, 0, 0)),
        out_shape=jax.ShapeDtypeStruct((b, n, 2 * d), BF16),
        compiler_params=pltpu.CompilerParams(dimension_semantics=("arbitrary",),
                                             vmem_limit_bytes=VMEM_LIMIT),
        name="xkv_proj",
    )(mem, w)


def _tail_kernel(yd_ref, ys_ref, x_ref, kv_ref, wo_ref, g1_ref, b1_ref, wq_ref, wxo_ref,
                 g2_ref, b2_ref, wg_ref, wu_ref, wd_ref, g3_ref, b3_ref, o_ref):
    subs = [slice(t * TM_ROW, (t + 1) * TM_ROW) for t in range(N_SUB)]
    hcols = [slice(h * XHEAD_DIM, (h + 1) * XHEAD_DIM) for h in range(N_XHEADS)]

    def query(x1):
        return (_dot(x1.astype(BF16), wq_ref[...]) * (XHEAD_DIM ** -0.5)).astype(BF16)

    def scores(q):
        return [_dot_nt(q[:, hc], kv_ref[0, :, hc]) for hc in hcols]

    def attend(sc):
        ps = [jnp.exp(sh - jnp.max(sh, axis=-1, keepdims=True)) for sh in sc]
        inv = [1.0 / jnp.sum(p, axis=-1, keepdims=True) for p in ps]
        pv = [_dot(p.astype(BF16), kv_ref[0, :, D_MODEL + h * XHEAD_DIM:D_MODEL + (h + 1) * XHEAD_DIM])
              for h, p in enumerate(ps)]
        return jnp.concatenate([(o * i).astype(BF16) for o, i in zip(pv, inv)], axis=1)

    def out_proj(x1, heads):
        return _layer_norm(ALPHA * x1 + _dot(heads, wxo_ref[...]), g2_ref[...], b2_ref[...])

    def gate_up(x2):
        xb = x2.astype(BF16)
        gate, up = _dot(xb, wg_ref[...]), _dot(xb, wu_ref[...])
        return (gate * jax.nn.sigmoid(gate) * up).astype(BF16)

    a, b = subs
    half = yd_ref.shape[1]
    mix = [_dot(yd_ref[r], wo_ref[:half, :]) + _dot(ys_ref[r], wo_ref[half:, :]) for r in subs]
    x1 = [_layer_norm(ALPHA * x_ref[r] + m, g1_ref[...], b1_ref[...]) for r, m in zip(subs, mix)]
    q_a = query(x1[0])
    sc_a = scores(q_a)
    q_b = query(x1[1])
    heads_a = attend(sc_a)
    sc_b = scores(q_b)
    x2_a = out_proj(x1[0], heads_a)
    heads_b = attend(sc_b)
    x2_b = out_proj(x1[1], heads_b)
    hidden_a = gate_up(x2_a)
    hidden_b = gate_up(x2_b)
    o_ref[a] = _layer_norm(ALPHA * x2_a + _dot(hidden_a, wd_ref[...]), g3_ref[...], b3_ref[...])
    o_ref[b] = _layer_norm(ALPHA * x2_b + _dot(hidden_b, wd_ref[...]), g3_ref[...], b3_ref[...])


def _tail(yd, ys, x2d, kv, wo, g1, b1, wq, wxo, g2, b2, wg, wu, wd, g3, b3, seq):
    t, d = x2d.shape
    half = yd.shape[1]
    n = kv.shape[1]
    f = wg.shape[1]
    tm = N_SUB * TM_ROW
    per_batch = seq // tm
    row = lambda i: (i, 0)
    const = lambda i: (0, 0)
    resident = lambda shape: pl.BlockSpec(shape, const, pipeline_mode=pl.Buffered(1))
    return pl.pallas_call(
        _tail_kernel,
        grid=(t // tm,),
        in_specs=[pl.BlockSpec((tm, half), row), pl.BlockSpec((tm, half), row),
                  pl.BlockSpec((tm, d), row),
                  pl.BlockSpec((1, n, 2 * d), lambda i: (i // per_batch, 0, 0)),
                  resident((d, d)), resident((1, d)), resident((1, d)),
                  resident((d, d)), resident((d, d)), resident((1, d)), resident((1, d)),
                  resident((d, f)), resident((d, f)), resident((f, d)),
                  resident((1, d)), resident((1, d))],
        out_specs=pl.BlockSpec((tm, d), row),
        out_shape=jax.ShapeDtypeStruct((t, d), F32),
        compiler_params=pltpu.CompilerParams(dimension_semantics=("arbitrary",),
                                             vmem_limit_bytes=VMEM_LIMIT),
        name="tail",
    )(yd, ys, x2d, kv, wo, g1, b1, wq, wxo, g2, b2, wg, wu, wd, g3, b3)


def _alibi_tables():
    f32 = np.float32
    slopes = np.exp2(-8.0 * np.arange(1, N_DIFF_HEADS + 1, dtype=f32) / N_DIFF_HEADS).astype(f32)
    slope2 = (slopes * f32(LOG2E)).astype(f32)
    key_term = (slope2[:, None] * np.arange(TQ, dtype=f32)[None, :]).astype(f32)
    rest, terms = key_term, []
    for _ in range(N_FEAT):
        top = (rest.view(np.uint32) & np.uint32(0xFFFF0000)).view(f32)
        terms.append(top)
        rest = (rest - top).astype(f32)
    feat = np.stack(terms, axis=-1)
    feat1 = np.zeros((N_DIFF_HEADS, TQ, LANES), f32)
    feat2 = np.zeros((N_DIFF_HEADS, TQ, LANES), f32)
    feat1[:, :, HEAD_DIM:HEAD_DIM + N_FEAT] = feat
    feat2[:, :, :N_FEAT] = feat
    i = np.arange(TQ)[:, None]
    j = np.arange(TQ)[None, :]
    dist = np.abs(i - j).astype(f32)
    allowed = (j // CHUNK) <= (i // CHUNK)
    diag = np.where(allowed[None], -slope2[:, None, None] * dist[None], f32(NEG_INF)).astype(f32)
    return (jnp.asarray(slopes), jnp.asarray(feat1, BF16), jnp.asarray(feat2, BF16),
            jnp.asarray(diag - key_term[:, None, :]))


def _suffix_sum_matrix():
    j = np.arange(2 * SB_W)[:, None] % SB_W
    c = np.arange(2 * SB_W)[None, :]
    return jnp.asarray(np.where((c >= SB_W) | (j >= c), 1.0, 0.0), BF16)


def kernel(x, mem, w_in, diff_lambda_q1, diff_lambda_k1, diff_lambda_q2, diff_lambda_k2,
           diff_subln_g, sb_norm_g, w_o, ln1_g, ln1_b, w_xq, w_xkv, w_xo, ln2_g, ln2_b,
           w_gate, w_up, w_down, ln3_g, ln3_b):
    b, s, d = x.shape
    assert (b, s, d) == (8, SEQ, D_MODEL) and w_in.shape == (DEPTH, D_MODEL, D_IN)
    x2d = x.reshape(b * s, d)
    slopes, feat1, feat2, diagb = _alibi_tables()
    tri = _suffix_sum_matrix()
    vec = lambda a: a[0].reshape(1, -1)

    proj, (wo16, wq16, wxo16, wg16, wu16, wd16) = _in_proj(
        x2d, w_in[0], [w_o[0], w_xq[0], w_xo[0], w_gate[0], w_up[0], w_down[0]])
    proj = proj.reshape(b, s, D_IN)
    y_diff = _diff_attn(proj, slopes, vec(diff_lambda_q1), vec(diff_lambda_k1),
                        vec(diff_lambda_q2), vec(diff_lambda_k2), feat1, feat2, diagb,
                        vec(diff_subln_g))
    y_sb = _sb_attn(proj, tri, sb_norm_g[0].reshape(N_SB_HEADS // 2, 1, LANES))

    half = N_DIFF_HEADS * DIFF_V_DIM
    kv = _xkv_proj(mem, w_xkv[0])
    out = _tail(y_diff.reshape(b * s, half), y_sb.reshape(b * s, -1), x2d, kv,
                wo16, vec(ln1_g), vec(ln1_b), wq16, wxo16, vec(ln2_g), vec(ln2_b),
                wg16, wu16, wd16, vec(ln3_g), vec(ln3_b), s)
    return out.reshape(b, s, d)
```

```python
import math

import jax
import jax.numpy as jnp
import numpy as np
from jax import lax
from jax.experimental import pallas as pl
from jax.experimental.pallas import tpu as pltpu

D_MODEL = 1024
DEPTH = 1
CHUNK = 64
N_MEM = 256
HEAD_DIM = 64
N_DIFF_HEADS = 4
DIFF_V_DIM = 2 * HEAD_DIM
N_SB_HEADS = 8
N_XHEADS = 4
XHEAD_DIM = D_MODEL // N_XHEADS
D_FF = 2816
ALPHA = (2.0 * DEPTH) ** 0.25
LN_EPS = 1e-5
RMS_EPS = 1e-5
NEG_INF = -1e30
LAMBDA_INIT = 0.8 - 0.6 * math.exp(-0.3 * 0)
LOG2E = math.log2(math.e)

LANES = 128
BF16_SUBLANES = 16
QD_BLK, KD_BLK, VD_BLK, QS_BLK, KS_BLK, VS_BLK = 0, 4, 8, 12, 16, 20
D_IN = 24 * LANES

TM_PROJ = 1024
TM_ROW = 256
N_SUB = 2
TQ = 512
SEQ = 2048
NQ = SEQ // TQ
DIFF_HP = 2
DIAG_HALF = TQ // 2
FINITE_LIMIT = 3.0e38
N_FEAT = 3
SB_W = 128
SB_U = TQ // SB_W
SB_NEAR = 256
SB_EXIT = 150.0
SB_LINEAR = 64.0
SB_TOP = 256
SB_PP = 2
VMEM_LIMIT = 56 * 1024 * 1024

BF16 = jnp.bfloat16
F32 = jnp.float32


def _dot(a, b):
    return jnp.dot(a, b, preferred_element_type=F32)


def _dot_nt(a, b):
    return lax.dot_general(a, b, (((1,), (1,)), ((), ())), preferred_element_type=F32)


def _layer_norm(v, g, b):
    mu = jnp.mean(v, axis=-1, keepdims=True)
    d = v - mu
    var = jnp.mean(d * d, axis=-1, keepdims=True)
    return d * lax.rsqrt(var + LN_EPS) * g + b


def _hi_lo(v):
    hi = v.astype(BF16)
    lo = (v - hi.astype(F32)).astype(BF16)
    return jnp.concatenate([hi, lo], axis=1)


def _in_proj_kernel(x_ref, w_ref, *refs):
    n = len(refs) // 2
    o_ref = refs[n]
    o_ref[...] = _dot(x_ref[...], w_ref[...]).astype(o_ref.dtype)
    for src, dst in zip(refs[:n], refs[n + 1:]):
        dst[...] = src[...].astype(dst.dtype)


def _in_proj(x2d, w, tail_weights):
    t, d = x2d.shape
    n = w.shape[1]
    steps = t // TM_PROJ
    w_specs, w_shapes = [], []
    for tw in tail_weights:
        rows, cols = tw.shape
        if rows % (BF16_SUBLANES * steps) == 0:
            blk, index = rows // steps, (lambda i: (i, 0))
        else:
            assert rows % (BF16_SUBLANES * steps // 2) == 0
            blk, index = 2 * rows // steps, (lambda i: (jnp.minimum(i, steps // 2 - 1), 0))
        w_specs.append(pl.BlockSpec((blk, cols), index))
        w_shapes.append(jax.ShapeDtypeStruct(tw.shape, BF16))
    outs = pl.pallas_call(
        _in_proj_kernel,
        grid=(steps,),
        in_specs=[pl.BlockSpec((TM_PROJ, d), lambda i: (i, 0)),
                  pl.BlockSpec((d, n), lambda i: (0, 0), pipeline_mode=pl.Buffered(1))] + w_specs,
        out_specs=[pl.BlockSpec((TM_PROJ, n), lambda i: (i, 0))] + w_specs,
        out_shape=[jax.ShapeDtypeStruct((t, n), BF16)] + w_shapes,
        compiler_params=pltpu.CompilerParams(dimension_semantics=("arbitrary",),
                                             vmem_limit_bytes=VMEM_LIMIT),
        name="in_proj",
    )(x2d, w, *tail_weights)
    return outs[0], outs[1:]


def _diff_attn_kernel(slopes_ref, lq1_ref, lk1_ref, lq2_ref, lk2_ref, q_ref, k_ref, v_ref,
                      feat1_ref, feat2_ref, diagb_ref, g_ref, o_ref, flag_ref,
                      kaug_ref, vaug_ref, facc_ref):
    hp = pl.program_id(1)
    qi = pl.program_id(2)

    @pl.when(qi == 0)
    def _():
        lane = lax.broadcasted_iota(jnp.int32, (TQ, LANES), 1)
        first = lane < HEAD_DIM
        for hd in range(DIFF_HP):
            cols = slice(hd * LANES, (hd + 1) * LANES)
            f1 = feat1_ref[hd].astype(F32)
            f2 = feat2_ref[hd].astype(F32)
            for blk in range(NQ):
                rows = slice(blk * TQ, (blk + 1) * TQ)
                kblk = k_ref[0, rows, cols].astype(F32)
                kaug_ref[hd, 0, rows, :] = jnp.where(first, kblk, f1).astype(BF16)
                kaug_ref[hd, 1, rows, :] = jnp.where(first, f2, kblk).astype(BF16)
                vaug_ref[hd, rows, :LANES] = v_ref[0, rows, cols]
                vaug_ref[hd, rows, LANES:] = jnp.ones((TQ, LANES), BF16)

    for nb in range(NQ):
        pl.when(qi == nb)(lambda nb=nb: _diff_query_block_fast(
            nb, hp, slopes_ref, lq1_ref, lk1_ref, lq2_ref, lk2_ref, q_ref, k_ref,
            diagb_ref, g_ref, o_ref, flag_ref, kaug_ref, vaug_ref, facc_ref))


def _diff_query_block_fast(nb, hp, slopes_ref, lq1_ref, lk1_ref, lq2_ref, lk2_ref, q_ref, k_ref,
                           diagb_ref, g_ref, o_ref, flag_ref, kaug_ref, vaug_ref, facc_ref):
    heads = range(DIFF_HP)
    hcols = [slice(hd * LANES, (hd + 1) * LANES) for hd in heads]
    slope2 = [slopes_ref[hp * DIFF_HP + hd] * LOG2E for hd in heads]
    t0 = nb * TQ
    lane = lax.broadcasted_iota(jnp.int32, (TQ, LANES), 1)
    first = lane < HEAD_DIM
    ones1 = jnp.where(lane < HEAD_DIM + N_FEAT, 1.0, 0.0)
    ones2 = jnp.where(lane < N_FEAT, 1.0, 0.0)
    row = lax.broadcasted_iota(jnp.int32, (TQ, LANES), 0).astype(F32)
    halves = (slice(0, DIAG_HALF), slice(DIAG_HALF, TQ))

    qm, ref0 = [], []
    for hd in heads:
        qf = q_ref[0, :, hcols[hd]].astype(F32) * (HEAD_DIM ** -0.5 * LOG2E)
        qm.append((jnp.where(first, qf, ones1).astype(BF16), jnp.where(first, ones2, qf).astype(BF16)))
        own = qf * k_ref[0, t0:t0 + TQ, hcols[hd]].astype(F32)
        ref0.append((jnp.sum(jnp.where(first, own, 0.0), axis=-1, keepdims=True),
                     jnp.sum(jnp.where(first, 0.0, own), axis=-1, keepdims=True)))

    units = []
    for blk in range(nb, -1, -1):
        for hd in heads:
            for mp in range(2):
                for rows in halves:
                    nkeys = rows.stop if blk == nb else TQ
                    units.append((hd, mp, rows, blk * TQ, nkeys, blk))

    def qk(u):
        hd, mp, rows, kstart, nkeys, _ = u
        return _dot_nt(qm[hd][mp][rows], kaug_ref[hd, mp, kstart:kstart + nkeys, :])

    seen = set()

    def consume(u, s):
        hd, mp, rows, kstart, nkeys, blk = u
        if blk == nb:
            s = s + diagb_ref[hd, rows, :nkeys]
            shift = -ref0[hd][mp][rows]
        else:
            shift = -slope2[hd] * (row[rows] + float((nb - blk) * TQ)) - ref0[hd][mp][rows]
        p = jnp.concatenate([jnp.exp2(s[:, c * LANES:(c + 1) * LANES] + shift).astype(BF16)
                             for c in range(nkeys // LANES)], axis=1)
        pv = _dot(p, vaug_ref[hd, kstart:kstart + nkeys, :])
        key = (hd, mp, rows.start)
        if key in seen:
            facc_ref[hd, mp, rows] += pv
        else:
            seen.add(key)
            facc_ref[hd, mp, rows] = pv

    s_next = qk(units[0])
    for n, u in enumerate(units):
        s_cur = s_next
        if n + 1 < len(units):
            s_next = qk(units[n + 1])
        consume(u, s_cur)

    lam = (jnp.exp(jnp.sum(lq1_ref[...] * lk1_ref[...]))
           - jnp.exp(jnp.sum(lq2_ref[...] * lk2_ref[...])) + LAMBDA_INIT)
    bad = jnp.zeros((TQ, LANES), F32)
    for hd in heads:
        l1 = facc_ref[hd, 0, :, LANES:]
        l2 = facc_ref[hd, 1, :, LANES:]
        out = facc_ref[hd, 0, :, :LANES] * (1.0 / l1) - lam * (facc_ref[hd, 1, :, :LANES] * (1.0 / l2))
        ms = jnp.mean(out * out, axis=-1, keepdims=True)
        out = out * lax.rsqrt(ms + RMS_EPS) * g_ref[...] * (1.0 - LAMBDA_INIT)
        o_ref[0, :, hcols[hd]] = out.astype(o_ref.dtype)
        for chk in (out, l1 + l2):
            bad = jnp.maximum(bad, jnp.where(jnp.abs(chk) < FINITE_LIMIT, 0.0, 1.0))

    flag_ref[...] = jnp.broadcast_to(jnp.max(bad, axis=0, keepdims=True)[None, None, None],
                                     flag_ref.shape)


def _diff_robust_kernel(slopes_ref, lq1_ref, lk1_ref, lq2_ref, lk2_ref, q_ref, k_ref, v_ref,
                        diagb_ref, g_ref, stale_ref, o_ref, m_ref, l_ref, acc_ref, s_ref, mb_ref):
    del stale_ref
    hp = pl.program_id(1)
    qi = pl.program_id(2)
    for nb in range(NQ):
        pl.when(qi == nb)(lambda nb=nb: _diff_query_block(
            nb, hp, slopes_ref, lq1_ref, lk1_ref, lq2_ref, lk2_ref, q_ref, k_ref, v_ref,
            diagb_ref, g_ref, o_ref, m_ref, l_ref, acc_ref, s_ref, mb_ref))


def _diff_query_block(nb, hp, slopes_ref, lq1_ref, lk1_ref, lq2_ref, lk2_ref, q_ref, k_ref, v_ref,
                      diagb_ref, g_ref, o_ref, m_ref, l_ref, acc_ref, s_ref, mb_ref):
    heads = range(DIFF_HP)
    hcols = [slice(hd * LANES, (hd + 1) * LANES) for hd in heads]
    slope2 = [slopes_ref[hp * DIFF_HP + hd] * LOG2E for hd in heads]

    lane = lax.broadcasted_iota(jnp.int32, (TQ, LANES), 1)
    qm = []
    for hd in heads:
        qf = q_ref[0, :, hcols[hd]].astype(F32) * (HEAD_DIM ** -0.5 * LOG2E)
        qm.append((jnp.where(lane < HEAD_DIM, qf, 0.0).astype(BF16),
                   jnp.where(lane >= HEAD_DIM, qf, 0.0).astype(BF16)))

    m_ref[...] = jnp.full_like(m_ref, NEG_INF)
    l_ref[...] = jnp.zeros_like(l_ref)
    acc_ref[...] = jnp.zeros_like(acc_ref)

    col = lax.broadcasted_iota(jnp.int32, (1, TQ), 1).astype(F32)
    row = lax.broadcasted_iota(jnp.int32, (TQ, LANES), 0).astype(F32)

    full = slice(0, TQ)
    diag_parts = ((slice(0, DIAG_HALF), DIAG_HALF), (slice(DIAG_HALF, TQ), TQ))

    def scores(hd, slot, kstart, nkeys, rows, diagonal):
        kb = k_ref[0, kstart:kstart + nkeys, hcols[hd]]
        bias = slope2[hd] * col[:, :nkeys]
        if diagonal:
            bias = bias + diagb_ref[hd, rows, :nkeys]
        for mp in range(2):
            s = _dot_nt(qm[hd][mp][rows], kb) + bias
            s_ref[hd, slot, mp, rows, :nkeys] = s
            mb_ref[hd, slot, mp, rows] = jnp.broadcast_to(jnp.max(s, axis=-1, keepdims=True),
                                                          (rows.stop - rows.start, LANES))

    def softmax_pv(hd, slot, kstart, nkeys, rows, row_shift):
        vb = v_ref[0, kstart:kstart + nkeys, hcols[hd]]
        for mp in range(2):
            m_old = m_ref[hd, mp, rows]
            mb = mb_ref[hd, slot, mp, rows]
            m_new = jnp.maximum(m_old, mb if row_shift is None else mb + row_shift)
            alpha = jnp.exp2(m_old - m_new)
            mrel = m_new if row_shift is None else m_new - row_shift
            ps = [jnp.exp2(s_ref[hd, slot, mp, rows, c * LANES:(c + 1) * LANES] - mrel)
                  for c in range(nkeys // LANES)]
            psum = ps[0]
            for pc in ps[1:]:
                psum = psum + pc
            l_ref[hd, mp, rows] = alpha * l_ref[hd, mp, rows] + psum
            p = jnp.concatenate([pc.astype(BF16) for pc in ps], axis=1)
            acc_ref[hd, mp, rows] = alpha * acc_ref[hd, mp, rows] + _dot(p, vb)
            m_ref[hd, mp, rows] = m_new

    def issue(hd, i):
        if i < nb:
            scores(hd, i % 2, i * TQ, TQ, full, False)
        else:
            for rows, nkeys in diag_parts:
                scores(hd, i % 2, nb * TQ, nkeys, rows, True)

    def consume(hd, i):
        if i < nb:
            softmax_pv(hd, i % 2, i * TQ, TQ, full, -slope2[hd] * (row + float((nb - i) * TQ)))
        else:
            for rows, nkeys in diag_parts:
                softmax_pv(hd, i % 2, nb * TQ, nkeys, rows, None)

    lam = (jnp.exp(jnp.sum(lq1_ref[...] * lk1_ref[...]))
           - jnp.exp(jnp.sum(lq2_ref[...] * lk2_ref[...])) + LAMBDA_INIT)

    def normalise(hd):
        l1 = jnp.sum(l_ref[hd, 0], axis=-1, keepdims=True)
        l2 = jnp.sum(l_ref[hd, 1], axis=-1, keepdims=True)
        out = acc_ref[hd, 0] * (1.0 / l1) - lam * (acc_ref[hd, 1] * (1.0 / l2))
        ms = jnp.mean(out * out, axis=-1, keepdims=True)
        out = out * lax.rsqrt(ms + RMS_EPS) * g_ref[...] * (1.0 - LAMBDA_INIT)
        o_ref[0, :, hcols[hd]] = out.astype(o_ref.dtype)

    for hd in heads:
        issue(hd, 0)
    for i in range(nb + 1):
        if i < nb:
            for hd in heads:
                issue(hd, i + 1)
        for hd in heads:
            consume(hd, i)
            if i == nb:
                normalise(hd)


def _diff_attn(proj, slopes, lq1, lk1, lq2, lk2, feat1, feat2, diagb, g):
    b, s, _ = proj.shape
    nq = s // TQ
    groups = N_DIFF_HEADS // DIFF_HP
    wide = DIFF_HP * LANES
    lam_spec = pl.BlockSpec((1, HEAD_DIM), lambda bi, hi, qi: (0, 0))
    common_in = [
        pl.BlockSpec(memory_space=pltpu.SMEM),
        lam_spec, lam_spec, lam_spec, lam_spec,
        pl.BlockSpec((1, TQ, wide), lambda bi, hi, qi: (bi, qi, QD_BLK // DIFF_HP + hi)),
        pl.BlockSpec((1, s, wide), lambda bi, hi, qi: (bi, 0, KD_BLK // DIFF_HP + hi)),
        pl.BlockSpec((1, s, wide), lambda bi, hi, qi: (bi, 0, VD_BLK // DIFF_HP + hi)),
    ]
    table_in = [pl.BlockSpec((DIFF_HP, TQ, TQ), lambda bi, hi, qi: (hi, 0, 0)),
                pl.BlockSpec((1, DIFF_V_DIM), lambda bi, hi, qi: (0, 0))]
    feat_in = [pl.BlockSpec((DIFF_HP, TQ, LANES), lambda bi, hi, qi: (hi, 0, 0))] * 2
    out_spec = pl.BlockSpec((1, TQ, wide), lambda bi, hi, qi: (bi, qi, hi))
    out_shape = jax.ShapeDtypeStruct((b, s, N_DIFF_HEADS * DIFF_V_DIM), BF16)
    params = pltpu.CompilerParams(dimension_semantics=("arbitrary", "arbitrary", "arbitrary"),
                                  vmem_limit_bytes=VMEM_LIMIT)
    y_fast, flags = pl.pallas_call(
        _diff_attn_kernel,
        grid=(b, groups, nq),
        in_specs=common_in + feat_in + table_in,
        out_specs=[out_spec, pl.BlockSpec((1, 1, 1, 8, LANES), lambda bi, hi, qi: (bi, hi, qi, 0, 0))],
        out_shape=[out_shape, jax.ShapeDtypeStruct((b, groups, nq, 8, LANES), F32)],
        scratch_shapes=[pltpu.VMEM((DIFF_HP, 2, s, LANES), BF16),
                        pltpu.VMEM((DIFF_HP, s, 2 * LANES), BF16),
                        pltpu.VMEM((DIFF_HP, 2, TQ, 2 * LANES), F32)],
        compiler_params=params,
        name="diff_attn",
    )(slopes, lq1, lk1, lq2, lk2, proj, proj, proj, feat1, feat2, diagb, g)

    def running_max_version(stale):
        return pl.pallas_call(
            _diff_robust_kernel,
            grid=(b, groups, nq),
            in_specs=common_in + table_in + [pl.BlockSpec(memory_space=pl.ANY)],
            out_specs=out_spec,
            out_shape=out_shape,
            scratch_shapes=[pltpu.VMEM((DIFF_HP, 2, TQ, LANES), F32),
                            pltpu.VMEM((DIFF_HP, 2, TQ, LANES), F32),
                            pltpu.VMEM((DIFF_HP, 2, TQ, LANES), F32),
                            pltpu.VMEM((DIFF_HP, 2, 2, TQ, TQ), F32),
                            pltpu.VMEM((DIFF_HP, 2, 2, TQ, LANES), F32)],
            input_output_aliases={len(common_in) + len(table_in): 0},
            compiler_params=params,
            name="diff_attn_running_max",
        )(slopes, lq1, lk1, lq2, lk2, proj, proj, proj, diagb, g, stale)

    overflow = jnp.max(flags) > 0.0
    y, _ = lax.while_loop(lambda c: jnp.logical_and(overflow, c[1] == 0),
                          lambda c: (running_max_version(c[0]), c[1] + 1),
                          (y_fast, jnp.int32(0)))
    return y


def _sb_attn_kernel(q_ref, k_ref, v_ref, tri_ref, g_ref, o_ref, acc_ref, carry_ref, reach_ref):
    qi = pl.program_id(2)
    pairs = range(SB_PP)
    pcols = [slice(pp * LANES, (pp + 1) * LANES) for pp in pairs]
    lane = lax.broadcasted_iota(jnp.int32, (TQ, LANES), 1)
    lo_half = lane < HEAD_DIM
    row = lax.broadcasted_iota(jnp.int32, (TQ, SB_W), 0)
    colk = lax.broadcasted_iota(jnp.int32, (TQ, SB_W), 1)
    strict = colk < row
    t0 = pl.multiple_of(qi * TQ, TQ)
    tri = tri_ref[...]

    def masked_queries():
        qm = []
        for pp in pairs:
            qf = q_ref[0, :, pcols[pp]].astype(F32) * (HEAD_DIM ** -0.5 * LOG2E)
            qm.append((jnp.where(lo_half, qf, 0.0).astype(BF16),
                       jnp.where(lo_half, 0.0, qf).astype(BF16)))
        return qm

    def scores(qm, pp, kstart, nkeys, rows=slice(0, TQ)):
        kb = k_ref[0, pl.ds(kstart, nkeys), pcols[pp]]
        return [_dot_nt(qm[pp][hh][rows], kb) for hh in range(2)]

    def softplus_split(zs, nsub, own):
        z, x = {}, {}
        for hh in range(2):
            for u in reversed(range(nsub)):
                r0 = u * SB_W if own else 0
                zc = zs[hh][r0:, u * SB_W:(u + 1) * SB_W]
                sp = jnp.where(zc > SB_LINEAR, zc, jnp.log2(1.0 + jnp.exp2(zc)))
                if own:
                    sp = jnp.where(strict[:TQ - r0], sp, 0.0)
                z[hh, u] = zc
                x[hh, u] = _hi_lo(sp)
        return z, x

    def suffix_sums(x):
        return {key: _dot(val, tri) for key, val in x.items()}

    def add_rows(full, lo, hi, delta):
        parts = [full[:lo]] if lo else []
        parts.append(full[lo:hi] + delta)
        if hi < full.shape[0]:
            parts.append(full[hi:])
        return parts[0] if len(parts) == 1 else jnp.concatenate(parts, axis=0)

    def weights_pv(pp, z, r, nsub, own, kstart, carry, acc, rows=slice(0, TQ)):
        vb = v_ref[0, pl.ds(kstart, nsub * SB_W), pcols[pp]]
        lo, hi = rows.start, rows.stop
        for hh in range(2):
            a = {}
            for u in reversed(range(nsub)):
                r0 = u * SB_W if own else lo
                av = jnp.exp2(z[hh, u] - r[hh, u][:, :SB_W] - carry[hh][r0:hi])
                if own:
                    av = jnp.where(strict[:TQ - r0], av, 0.0)
                avb = av.astype(BF16)
                if own and r0:
                    avb = jnp.concatenate([jnp.zeros((r0, SB_W), BF16), avb], axis=0)
                a[u] = avb
                carry[hh] = add_rows(carry[hh], r0, hi, r[hh, u][:, SB_W:])
            pv = _dot(jnp.concatenate([a[u] for u in range(nsub)], axis=1), vb)
            acc[hh] = add_rows(acc[hh], lo, hi, pv)

    def park(carry, acc):
        reach = None
        for pp in pairs:
            for hh in range(2):
                carry_ref[pp, hh] = carry[pp][hh]
                acc_ref[pp, hh] = acc[pp][hh]
                reach = carry[pp][hh] if reach is None else jnp.minimum(reach, carry[pp][hh])
        return reach

    def reload():
        return ([[carry_ref[pp, 0], carry_ref[pp, 1]] for pp in pairs],
                [[acc_ref[pp, 0], acc_ref[pp, 1]] for pp in pairs])

    def normalise(pp, acc):
        out = jnp.where(lo_half, acc[0], acc[1])
        sq = out * out
        ss_lo = jnp.sum(jnp.where(lo_half, sq, 0.0), axis=-1, keepdims=True)
        ss_hi = jnp.sum(jnp.where(lo_half, 0.0, sq), axis=-1, keepdims=True)
        ms = jnp.where(lo_half, ss_lo, ss_hi) * (1.0 / HEAD_DIM)
        o_ref[0, :, pcols[pp]] = (out * lax.rsqrt(ms + RMS_EPS) * g_ref[pp]).astype(o_ref.dtype)

    def near_keys(with_previous):
        qm = masked_queries()
        nsub1 = SB_NEAR // SB_W
        top = slice(0, SB_TOP)
        prev = pl.multiple_of(t0 - SB_NEAR, SB_NEAR)
        carry = [[jnp.zeros((TQ, SB_W), F32), jnp.zeros((TQ, SB_W), F32)] for _ in pairs]
        acc = [[jnp.zeros((TQ, LANES), F32), jnp.zeros((TQ, LANES), F32)] for _ in pairs]
        zx0 = [softplus_split(scores(qm, pp, t0, TQ), SB_U, True) for pp in pairs]
        if with_previous:
            zs1 = [scores(qm, pp, prev, SB_NEAR, top) for pp in pairs]
        r0 = [suffix_sums(zx0[pp][1]) for pp in pairs]
        if with_previous:
            zx1 = [softplus_split(zs1[pp], nsub1, False) for pp in pairs]
        for pp in pairs:
            weights_pv(pp, zx0[pp][0], r0[pp], SB_U, True, t0, carry[pp], acc[pp])
        if with_previous:
            r1 = [suffix_sums(zx1[pp][1]) for pp in pairs]
            for pp in pairs:
                weights_pv(pp, zx1[pp][0], r1[pp], nsub1, False, prev, carry[pp], acc[pp], top)
        reach = park(carry, acc)
        reach_ref[0] = jnp.min(reach)
        reach_ref[1] = jnp.min(reach[SB_TOP:])
        reach_ref[2] = 0.0
        for pp in pairs:
            normalise(pp, acc[pp])

    pl.when(qi == 0)(lambda: near_keys(False))
    pl.when(qi > 0)(lambda: near_keys(True))

    def chunk(kstart, rows):
        qm = masked_queries()
        nsub = SB_NEAR // SB_W
        zx = [softplus_split(scores(qm, pp, kstart, SB_NEAR, rows), nsub, False) for pp in pairs]
        r = [suffix_sums(zx[pp][1]) for pp in pairs]
        carry, acc = reload()
        for pp in pairs:
            weights_pv(pp, zx[pp][0], r[pp], nsub, False, kstart, carry[pp], acc[pp], rows)
        reach_ref[0] = jnp.min(park(carry, acc))
        reach_ref[2] = 1.0

    @pl.when(jnp.logical_and(qi > 0, reach_ref[1] < SB_EXIT))
    def _():
        chunk(pl.multiple_of(t0 - SB_NEAR, SB_NEAR), slice(SB_TOP, TQ))

    n_far = qi * (TQ // SB_NEAR) - 1

    def far_chunk(c):
        chunk(pl.multiple_of(t0 - (c + 2) * SB_NEAR, SB_NEAR), slice(0, TQ))
        return c + 1

    lax.while_loop(lambda c: jnp.logical_and(c < n_far, reach_ref[0] < SB_EXIT), far_chunk, 0)

    @pl.when(reach_ref[2] > 0.5)
    def _():
        for pp in pairs:
            normalise(pp, [acc_ref[pp, 0], acc_ref[pp, 1]])


def _sb_attn(proj, tri, g):
    b, s, _ = proj.shape
    nq = s // TQ
    npair = N_SB_HEADS // 2
    wide = SB_PP * LANES
    return pl.pallas_call(
        _sb_attn_kernel,
        grid=(b, npair // SB_PP, nq),
        in_specs=[
            pl.BlockSpec((1, TQ, wide), lambda bi, pi, qi: (bi, qi, QS_BLK // SB_PP + pi)),
            pl.BlockSpec((1, s, wide), lambda bi, pi, qi: (bi, 0, KS_BLK // SB_PP + pi)),
            pl.BlockSpec((1, s, wide), lambda bi, pi, qi: (bi, 0, VS_BLK // SB_PP + pi)),
            pl.BlockSpec((2 * SB_W, 2 * SB_W), lambda bi, pi, qi: (0, 0)),
            pl.BlockSpec((SB_PP, 1, LANES), lambda bi, pi, qi: (pi, 0, 0)),
        ],
        out_specs=pl.BlockSpec((1, TQ, wide), lambda bi, pi, qi: (bi, qi, pi)),
        out_shape=jax.ShapeDtypeStruct((b, s, N_SB_HEADS * HEAD_DIM), BF16),
        scratch_shapes=[pltpu.VMEM((SB_PP, 2, TQ, LANES), F32), pltpu.VMEM((SB_PP, 2, TQ, SB_W), F32),
                        pltpu.SMEM((3,), F32)],
        compiler_params=pltpu.CompilerParams(
            dimension_semantics=("arbitrary", "arbitrary", "arbitrary"),
            vmem_limit_bytes=VMEM_LIMIT),
        name="sb_attn",
    )(proj, proj, proj, tri, g)


def _xkv_kernel(mem_ref, w_ref, o_ref):
    nb, n, d = mem_ref.shape
    kv = _dot(mem_ref[...].reshape(nb * n, d), w_ref[...])
    o_ref[...] = kv.reshape(nb, n, kv.shape[-1]).astype(o_ref.dtype)


def _xkv_proj(mem, w):
    b, n, d = mem.shape
    nb = TM_PROJ // n
    return pl.pallas_call(
        _xkv_kernel,
        grid=(b // nb,),
        in_specs=[pl.BlockSpec((nb, n, d), lambda i: (i, 0, 0)),
                  pl.BlockSpec((d, 2 * d), lambda i: (0, 0), pipeline_mode=pl.Buffered(1))],
        out_specs=pl.BlockSpec((nb, n, 2 * d), lambda i: (i, 0, 0)),
        out_shape=jax.ShapeDtypeStruct((b, n, 2 * d), BF16),
        compiler_params=pltpu.CompilerParams(dimension_semantics=("arbitrary",),
                                             vmem_limit_bytes=VMEM_LIMIT),
        name="xkv_proj",
    )(mem, w)


def _tail_kernel(yd_ref, ys_ref, x_ref, kv_ref, wo_ref, g1_ref, b1_ref, wq_ref, wxo_ref,
                 g2_ref, b2_ref, wg_ref, wu_ref, wd_ref, g3_ref, b3_ref, o_ref):
    subs = [slice(t * TM_ROW, (t + 1) * TM_ROW) for t in range(N_SUB)]
    hcols = [slice(h * XHEAD_DIM, (h + 1) * XHEAD_DIM) for h in range(N_XHEADS)]

    def query(x1):
        return (_dot(x1.astype(BF16), wq_ref[...]) * (XHEAD_DIM ** -0.5)).astype(BF16)

    def scores(q):
        return [_dot_nt(q[:, hc], kv_ref[0, :, hc]) for hc in hcols]

    def attend(sc):
        ps = [jnp.exp(sh - jnp.max(sh, axis=-1, keepdims=True)) for sh in sc]
        inv = [1.0 / jnp.sum(p, axis=-1, keepdims=True) for p in ps]
        pv = [_dot(p.astype(BF16), kv_ref[0, :, D_MODEL + h * XHEAD_DIM:D_MODEL + (h + 1) * XHEAD_DIM])
              for h, p in enumerate(ps)]
        return jnp.concatenate([(o * i).astype(BF16) for o, i in zip(pv, inv)], axis=1)

    def out_proj(x1, heads):
        return _layer_norm(ALPHA * x1 + _dot(heads, wxo_ref[...]), g2_ref[...], b2_ref[...])

    def gate_up(x2):
        xb = x2.astype(BF16)
        gate, up = _dot(xb, wg_ref[...]), _dot(xb, wu_ref[...])
        return (gate * jax.nn.sigmoid(gate) * up).astype(BF16)

    a, b = subs
    half = yd_ref.shape[1]
    mix = [_dot(yd_ref[r], wo_ref[:half, :]) + _dot(ys_ref[r], wo_ref[half:, :]) for r in subs]
    x1 = [_layer_norm(ALPHA * x_ref[r] + m, g1_ref[...], b1_ref[...]) for r, m in zip(subs, mix)]
    q_a = query(x1[0])
    sc_a = scores(q_a)
    q_b = query(x1[1])
    heads_a = attend(sc_a)
    sc_b = scores(q_b)
    x2_a = out_proj(x1[0], heads_a)
    heads_b = attend(sc_b)
    x2_b = out_proj(x1[1], heads_b)
    hidden_a = gate_up(x2_a)
    hidden_b = gate_up(x2_b)
    o_ref[a] = _layer_norm(ALPHA * x2_a + _dot(hidden_a, wd_ref[...]), g3_ref[...], b3_ref[...])
    o_ref[b] = _layer_norm(ALPHA * x2_b + _dot(hidden_b, wd_ref[...]), g3_ref[...], b3_ref[...])


def _tail(yd, ys, x2d, kv, wo, g1, b1, wq, wxo, g2, b2, wg, wu, wd, g3, b3, seq):
    t, d = x2d.shape
    half = yd.shape[1]
    n = kv.shape[1]
    f = wg.shape[1]
    tm = N_SUB * TM_ROW
    per_batch = seq // tm
    row = lambda i: (i, 0)
    const = lambda i: (0, 0)
    resident = lambda shape: pl.BlockSpec(shape, const, pipeline_mode=pl.Buffered(1))
    return pl.pallas_call(
        _tail_kernel,
        grid=(t // tm,),
        in_specs=[pl.BlockSpec((tm, half), row), pl.BlockSpec((tm, half), row),
                  pl.BlockSpec((tm, d), row),
                  pl.BlockSpec((1, n, 2 * d), lambda i: (i // per_batch, 0, 0)),
                  resident((d, d)), resident((1, d)), resident((1, d)),
                  resident((d, d)), resident((d, d)), resident((1, d)), resident((1, d)),
                  resident((d, f)), resident((d, f)), resident((f, d)),
                  resident((1, d)), resident((1, d))],
        out_specs=pl.BlockSpec((tm, d), row),
        out_shape=jax.ShapeDtypeStruct((t, d), F32),
        compiler_params=pltpu.CompilerParams(dimension_semantics=("arbitrary",),
                                             vmem_limit_bytes=VMEM_LIMIT),
        name="tail",
    )(yd, ys, x2d, kv, wo, g1, b1, wq, wxo, g2, b2, wg, wu, wd, g3, b3)


def _alibi_tables():
    f32 = np.float32
    slopes = np.exp2(-8.0 * np.arange(1, N_DIFF_HEADS + 1, dtype=f32) / N_DIFF_HEADS).astype(f32)
    slope2 = (slopes * f32(LOG2E)).astype(f32)
    key_term = (slope2[:, None] * np.arange(TQ, dtype=f32)[None, :]).astype(f32)
    rest, terms = key_term, []
    for _ in range(N_FEAT):
        top = (rest.view(np.uint32) & np.uint32(0xFFFF0000)).view(f32)
        terms.append(top)
        rest = (rest - top).astype(f32)
    feat = np.stack(terms, axis=-1)
    feat1 = np.zeros((N_DIFF_HEADS, TQ, LANES), f32)
    feat2 = np.zeros((N_DIFF_HEADS, TQ, LANES), f32)
    feat1[:, :, HEAD_DIM:HEAD_DIM + N_FEAT] = feat
    feat2[:, :, :N_FEAT] = feat
    i = np.arange(TQ)[:, None]
    j = np.arange(TQ)[None, :]
    dist = np.abs(i - j).astype(f32)
    allowed = (j // CHUNK) <= (i // CHUNK)
    diag = np.where(allowed[None], -slope2[:, None, None] * dist[None], f32(NEG_INF)).astype(f32)
    return (jnp.asarray(slopes), jnp.asarray(feat1, BF16), jnp.asarray(feat2, BF16),
            jnp.asarray(diag - key_term[:, None, :]))


def _suffix_sum_matrix():
    j = np.arange(2 * SB_W)[:, None] % SB_W
    c = np.arange(2 * SB_W)[None, :]
    return jnp.asarray(np.where((c >= SB_W) | (j >= c), 1.0, 0.0), BF16)


def kernel(x, mem, w_in, diff_lambda_q1, diff_lambda_k1, diff_lambda_q2, diff_lambda_k2,
           diff_subln_g, sb_norm_g, w_o, ln1_g, ln1_b, w_xq, w_xkv, w_xo, ln2_g, ln2_b,
           w_gate, w_up, w_down, ln3_g, ln3_b):
    b, s, d = x.shape
    assert (b, s, d) == (8, SEQ, D_MODEL) and w_in.shape == (DEPTH, D_MODEL, D_IN)
    x2d = x.reshape(b * s, d)
    slopes, feat1, feat2, diagb = _alibi_tables()
    tri = _suffix_sum_matrix()
    vec = lambda a: a[0].reshape(1, -1)

    proj, (wo16, wq16, wxo16, wg16, wu16, wd16) = _in_proj(
        x2d, w_in[0], [w_o[0], w_xq[0], w_xo[0], w_gate[0], w_up[0], w_down[0]])
    proj = proj.reshape(b, s, D_IN)
    y_diff = _diff_attn(proj, slopes, vec(diff_lambda_q1), vec(diff_lambda_k1),
                        vec(diff_lambda_q2), vec(diff_lambda_k2), feat1, feat2, diagb,
                        vec(diff_subln_g))
    y_sb = _sb_attn(proj, tri, sb_norm_g[0].reshape(N_SB_HEADS // 2, 1, LANES))

    half = N_DIFF_HEADS * DIFF_V_DIM
    kv = _xkv_proj(mem, w_xkv[0])
    out = _tail(y_diff.reshape(b * s, half), y_sb.reshape(b * s, -1), x2d, kv,
                wo16, vec(ln1_g), vec(ln1_b), wq16, wxo16, vec(ln2_g), vec(ln2_b),
                wg16, wu16, wd16, vec(ln3_g), vec(ln3_b), s)
    return out.reshape(b, s, d)
```

```python
import math

import jax
import jax.numpy as jnp
import numpy as np
from jax import lax
from jax.experimental import pallas as pl
from jax.experimental.pallas import tpu as pltpu

D_MODEL = 1024
DEPTH = 1
CHUNK = 64
N_MEM = 256
HEAD_DIM = 64
N_DIFF_HEADS = 4
DIFF_V_DIM = 2 * HEAD_DIM
N_SB_HEADS = 8
N_XHEADS = 4
XHEAD_DIM = D_MODEL // N_XHEADS
D_FF = 2816
ALPHA = (2.0 * DEPTH) ** 0.25
LN_EPS = 1e-5
RMS_EPS = 1e-5
NEG_INF = -1e30
LAMBDA_INIT = 0.8 - 0.6 * math.exp(-0.3 * 0)
LOG2E = math.log2(math.e)

LANES = 128
BF16_SUBLANES = 16
QD_BLK, KD_BLK, VD_BLK, QS_BLK, KS_BLK, VS_BLK = 0, 4, 8, 12, 16, 20
D_IN = 24 * LANES

TM_PROJ = 1024
X_SLOTS = 3
TM_ROW = 256
N_SUB = 2
TQ = 512
SEQ = 2048
NQ = SEQ // TQ
DIFF_HP = 2
DIAG_HALF = TQ // 2
FINITE_LIMIT = 3.0e38
N_FEAT = 3
SB_W = 128
SB_U = TQ // SB_W
SB_NEAR = 256
SB_EXIT = 150.0
SB_LINEAR = 64.0
SB_TOP = 256
SB_PP = 2
VMEM_LIMIT = 56 * 1024 * 1024

BF16 = jnp.bfloat16
F32 = jnp.float32


def _dot(a, b):
    return jnp.dot(a, b, preferred_element_type=F32)


def _dot_nt(a, b):
    return lax.dot_general(a, b, (((1,), (1,)), ((), ())), preferred_element_type=F32)


def _layer_norm(v, g, b):
    mu = jnp.mean(v, axis=-1, keepdims=True)
    d = v - mu
    var = jnp.mean(d * d, axis=-1, keepdims=True)
    return d * lax.rsqrt(var + LN_EPS) * g + b


def _hi_lo(v):
    hi = v.astype(BF16)
    lo = (v - hi.astype(F32)).astype(BF16)
    return jnp.concatenate([hi, lo], axis=1)


def _in_proj_kernel(x_hbm, w_ref, *refs):
    *refs, xbuf, sem = refs
    n = len(refs) // 2
    o_ref = refs[n]
    i = pl.program_id(0)
    steps = pl.num_programs(0)

    def x_copy(step):
        slot = step % X_SLOTS
        rows = pl.ds(pl.multiple_of(step * TM_PROJ, TM_PROJ), TM_PROJ)
        return pltpu.make_async_copy(x_hbm.at[rows], xbuf.at[slot], sem.at[slot])

    @pl.when(i == 0)
    def _():
        for s in range(X_SLOTS - 1):
            x_copy(s).start()

    @pl.when(i + X_SLOTS - 1 < steps)
    def _():
        x_copy(i + X_SLOTS - 1).start()

    x_copy(i).wait()
    o_ref[...] = _dot(xbuf[i % X_SLOTS], w_ref[...]).astype(o_ref.dtype)
    for src, dst in zip(refs[:n], refs[n + 1:]):
        dst[...] = src[...].astype(dst.dtype)


def _in_proj(x2d, w, tail_weights):
    t, d = x2d.shape
    n = w.shape[1]
    steps = t // TM_PROJ
    w_specs, w_shapes = [], []
    for tw in tail_weights:
        rows, cols = tw.shape
        if rows % (BF16_SUBLANES * steps) == 0:
            blk, index = rows // steps, (lambda i: (i, 0))
        else:
            assert rows % (BF16_SUBLANES * steps // 2) == 0
            blk, index = 2 * rows // steps, (lambda i: (jnp.minimum(i, steps // 2 - 1), 0))
        w_specs.append(pl.BlockSpec((blk, cols), index))
        w_shapes.append(jax.ShapeDtypeStruct(tw.shape, BF16))
    assert steps >= X_SLOTS
    outs = pl.pallas_call(
        _in_proj_kernel,
        grid=(steps,),
        in_specs=[pl.BlockSpec(memory_space=pl.ANY),
                  pl.BlockSpec((d, n), lambda i: (0, 0), pipeline_mode=pl.Buffered(1))] + w_specs,
        out_specs=[pl.BlockSpec((TM_PROJ, n), lambda i: (i, 0))] + w_specs,
        out_shape=[jax.ShapeDtypeStruct((t, n), BF16)] + w_shapes,
        scratch_shapes=[pltpu.VMEM((X_SLOTS, TM_PROJ, d), x2d.dtype),
                        pltpu.SemaphoreType.DMA((X_SLOTS,))],
        compiler_params=pltpu.CompilerParams(dimension_semantics=("arbitrary",),
                                             vmem_limit_bytes=VMEM_LIMIT),
        name="in_proj",
    )(x2d, w, *tail_weights)
    return outs[0], outs[1:]


def _diff_attn_kernel(slopes_ref, lq1_ref, lk1_ref, lq2_ref, lk2_ref, q_ref, k_ref, v_ref,
                      feat1_ref, feat2_ref, diagb_ref, g_ref, o_ref, flag_ref,
                      kaug_ref, vaug_ref, facc_ref):
    hp = pl.program_id(1)
    qi = pl.program_id(2)

    @pl.when(qi == 0)
    def _():
        lane = lax.broadcasted_iota(jnp.int32, (TQ, LANES), 1)
        first = lane < HEAD_DIM
        for hd in range(DIFF_HP):
            cols = slice(hd * LANES, (hd + 1) * LANES)
            f1 = feat1_ref[hd].astype(F32)
            f2 = feat2_ref[hd].astype(F32)
            for blk in range(NQ):
                rows = slice(blk * TQ, (blk + 1) * TQ)
                kblk = k_ref[0, rows, cols].astype(F32)
                kaug_ref[hd, 0, rows, :] = jnp.where(first, kblk, f1).astype(BF16)
                kaug_ref[hd, 1, rows, :] = jnp.where(first, f2, kblk).astype(BF16)
                vaug_ref[hd, rows, :LANES] = v_ref[0, rows, cols]
                vaug_ref[hd, rows, LANES:] = jnp.ones((TQ, LANES), BF16)

    for nb in range(NQ):
        pl.when(qi == nb)(lambda nb=nb: _diff_query_block_fast(
            nb, hp, slopes_ref, lq1_ref, lk1_ref, lq2_ref, lk2_ref, q_ref, k_ref,
            diagb_ref, g_ref, o_ref, flag_ref, kaug_ref, vaug_ref, facc_ref))


def _diff_query_block_fast(nb, hp, slopes_ref, lq1_ref, lk1_ref, lq2_ref, lk2_ref, q_ref, k_ref,
                           diagb_ref, g_ref, o_ref, flag_ref, kaug_ref, vaug_ref, facc_ref):
    heads = range(DIFF_HP)
    hcols = [slice(hd * LANES, (hd + 1) * LANES) for hd in heads]
    slope2 = [slopes_ref[hp * DIFF_HP + hd] * LOG2E for hd in heads]
    t0 = nb * TQ
    lane = lax.broadcasted_iota(jnp.int32, (TQ, LANES), 1)
    first = lane < HEAD_DIM
    ones1 = jnp.where(lane < HEAD_DIM + N_FEAT, 1.0, 0.0)
    ones2 = jnp.where(lane < N_FEAT, 1.0, 0.0)
    row = lax.broadcasted_iota(jnp.int32, (TQ, LANES), 0).astype(F32)
    halves = (slice(0, DIAG_HALF), slice(DIAG_HALF, TQ))

    qm, ref0 = [], []
    for hd in heads:
        qf = q_ref[0, :, hcols[hd]].astype(F32) * (HEAD_DIM ** -0.5 * LOG2E)
        qm.append((jnp.where(first, qf, ones1).astype(BF16), jnp.where(first, ones2, qf).astype(BF16)))
        own = qf * k_ref[0, t0:t0 + TQ, hcols[hd]].astype(F32)
        ref0.append((jnp.sum(jnp.where(first, own, 0.0), axis=-1, keepdims=True),
                     jnp.sum(jnp.where(first, 0.0, own), axis=-1, keepdims=True)))

    units = []
    for blk in range(nb, -1, -1):
        for hd in heads:
            for mp in range(2):
                for rows in halves:
                    nkeys = rows.stop if blk == nb else TQ
                    units.append((hd, mp, rows, blk * TQ, nkeys, blk))

    def qk(u):
        hd, mp, rows, kstart, nkeys, _ = u
        return _dot_nt(qm[hd][mp][rows], kaug_ref[hd, mp, kstart:kstart + nkeys, :])

    seen = set()

    def consume(u, s):
        hd, mp, rows, kstart, nkeys, blk = u
        if blk == nb:
            s = s + diagb_ref[hd, rows, :nkeys]
            shift = -ref0[hd][mp][rows]
        else:
            shift = -slope2[hd] * (row[rows] + float((nb - blk) * TQ)) - ref0[hd][mp][rows]
        p = jnp.concatenate([jnp.exp2(s[:, c * LANES:(c + 1) * LANES] + shift).astype(BF16)
                             for c in range(nkeys // LANES)], axis=1)
        pv = _dot(p, vaug_ref[hd, kstart:kstart + nkeys, :])
        key = (hd, mp, rows.start)
        if key in seen:
            facc_ref[hd, mp, rows] += pv
        else:
            seen.add(key)
            facc_ref[hd, mp, rows] = pv

    s_next = qk(units[0])
    for n, u in enumerate(units):
        s_cur = s_next
        if n + 1 < len(units):
            s_next = qk(units[n + 1])
        consume(u, s_cur)

    lam = (jnp.exp(jnp.sum(lq1_ref[...] * lk1_ref[...]))
           - jnp.exp(jnp.sum(lq2_ref[...] * lk2_ref[...])) + LAMBDA_INIT)
    bad = jnp.zeros((TQ, LANES), F32)
    for hd in heads:
        l1 = facc_ref[hd, 0, :, LANES:]
        l2 = facc_ref[hd, 1, :, LANES:]
        out = facc_ref[hd, 0, :, :LANES] * (1.0 / l1) - lam * (facc_ref[hd, 1, :, :LANES] * (1.0 / l2))
        ms = jnp.mean(out * out, axis=-1, keepdims=True)
        out = out * lax.rsqrt(ms + RMS_EPS) * g_ref[...] * (1.0 - LAMBDA_INIT)
        o_ref[0, :, hcols[hd]] = out.astype(o_ref.dtype)
        for chk in (out, l1 + l2):
            bad = jnp.maximum(bad, jnp.where(jnp.abs(chk) < FINITE_LIMIT, 0.0, 1.0))

    flag_ref[...] = jnp.broadcast_to(jnp.max(bad, axis=0, keepdims=True)[None, None, None],
                                     flag_ref.shape)


def _diff_robust_kernel(slopes_ref, lq1_ref, lk1_ref, lq2_ref, lk2_ref, q_ref, k_ref, v_ref,
                        diagb_ref, g_ref, o_ref, m_ref, l_ref, acc_ref, s_ref, mb_ref):
    hp = pl.program_id(1)
    qi = pl.program_id(2)
    for nb in range(NQ):
        pl.when(qi == nb)(lambda nb=nb: _diff_query_block(
            nb, hp, slopes_ref, lq1_ref, lk1_ref, lq2_ref, lk2_ref, q_ref, k_ref, v_ref,
            diagb_ref, g_ref, o_ref, m_ref, l_ref, acc_ref, s_ref, mb_ref))


def _diff_query_block(nb, hp, slopes_ref, lq1_ref, lk1_ref, lq2_ref, lk2_ref, q_ref, k_ref, v_ref,
                      diagb_ref, g_ref, o_ref, m_ref, l_ref, acc_ref, s_ref, mb_ref):
    heads = range(DIFF_HP)
    hcols = [slice(hd * LANES, (hd + 1) * LANES) for hd in heads]
    slope2 = [slopes_ref[hp * DIFF_HP + hd] * LOG2E for hd in heads]

    lane = lax.broadcasted_iota(jnp.int32, (TQ, LANES), 1)
    qm = []
    for hd in heads:
        qf = q_ref[0, :, hcols[hd]].astype(F32) * (HEAD_DIM ** -0.5 * LOG2E)
        qm.append((jnp.where(lane < HEAD_DIM, qf, 0.0).astype(BF16),
                   jnp.where(lane >= HEAD_DIM, qf, 0.0).astype(BF16)))

    m_ref[...] = jnp.full_like(m_ref, NEG_INF)
    l_ref[...] = jnp.zeros_like(l_ref)
    acc_ref[...] = jnp.zeros_like(acc_ref)

    col = lax.broadcasted_iota(jnp.int32, (1, TQ), 1).astype(F32)
    row = lax.broadcasted_iota(jnp.int32, (TQ, LANES), 0).astype(F32)

    full = slice(0, TQ)
    diag_parts = ((slice(0, DIAG_HALF), DIAG_HALF), (slice(DIAG_HALF, TQ), TQ))

    def scores(hd, slot, kstart, nkeys, rows, diagonal):
        kb = k_ref[0, kstart:kstart + nkeys, hcols[hd]]
        bias = slope2[hd] * col[:, :nkeys]
        if diagonal:
            bias = bias + diagb_ref[hd, rows, :nkeys]
        for mp in range(2):
            s = _dot_nt(qm[hd][mp][rows], kb) + bias
            s_ref[hd, slot, mp, rows, :nkeys] = s
            mb_ref[hd, slot, mp, rows] = jnp.broadcast_to(jnp.max(s, axis=-1, keepdims=True),
                                                          (rows.stop - rows.start, LANES))

    def softmax_pv(hd, slot, kstart, nkeys, rows, row_shift):
        vb = v_ref[0, kstart:kstart + nkeys, hcols[hd]]
        for mp in range(2):
            m_old = m_ref[hd, mp, rows]
            mb = mb_ref[hd, slot, mp, rows]
            m_new = jnp.maximum(m_old, mb if row_shift is None else mb + row_shift)
            alpha = jnp.exp2(m_old - m_new)
            mrel = m_new if row_shift is None else m_new - row_shift
            ps = [jnp.exp2(s_ref[hd, slot, mp, rows, c * LANES:(c + 1) * LANES] - mrel)
                  for c in range(nkeys // LANES)]
            psum = ps[0]
            for pc in ps[1:]:
                psum = psum + pc
            l_ref[hd, mp, rows] = alpha * l_ref[hd, mp, rows] + psum
            p = jnp.concatenate([pc.astype(BF16) for pc in ps], axis=1)
            acc_ref[hd, mp, rows] = alpha * acc_ref[hd, mp, rows] + _dot(p, vb)
            m_ref[hd, mp, rows] = m_new

    def issue(hd, i):
        if i < nb:
            scores(hd, i % 2, i * TQ, TQ, full, False)
        else:
            for rows, nkeys in diag_parts:
                scores(hd, i % 2, nb * TQ, nkeys, rows, True)

    def consume(hd, i):
        if i < nb:
            softmax_pv(hd, i % 2, i * TQ, TQ, full, -slope2[hd] * (row + float((nb - i) * TQ)))
        else:
            for rows, nkeys in diag_parts:
                softmax_pv(hd, i % 2, nb * TQ, nkeys, rows, None)

    lam = (jnp.exp(jnp.sum(lq1_ref[...] * lk1_ref[...]))
           - jnp.exp(jnp.sum(lq2_ref[...] * lk2_ref[...])) + LAMBDA_INIT)

    def normalise(hd):
        l1 = jnp.sum(l_ref[hd, 0], axis=-1, keepdims=True)
        l2 = jnp.sum(l_ref[hd, 1], axis=-1, keepdims=True)
        out = acc_ref[hd, 0] * (1.0 / l1) - lam * (acc_ref[hd, 1] * (1.0 / l2))
        ms = jnp.mean(out * out, axis=-1, keepdims=True)
        out = out * lax.rsqrt(ms + RMS_EPS) * g_ref[...] * (1.0 - LAMBDA_INIT)
        o_ref[0, :, hcols[hd]] = out.astype(o_ref.dtype)

    for hd in heads:
        issue(hd, 0)
    for i in range(nb + 1):
        if i < nb:
            for hd in heads:
                issue(hd, i + 1)
        for hd in heads:
            consume(hd, i)
            if i == nb:
                normalise(hd)


def _diff_attn(proj, slopes, lq1, lk1, lq2, lk2, feat1, feat2, diagb, g):
    b, s, _ = proj.shape
    nq = s // TQ
    groups = N_DIFF_HEADS // DIFF_HP
    wide = DIFF_HP * LANES
    lam_spec = pl.BlockSpec((1, HEAD_DIM), lambda bi, hi, qi: (0, 0))
    common_in = [
        pl.BlockSpec(memory_space=pltpu.SMEM),
        lam_spec, lam_spec, lam_spec, lam_spec,
        pl.BlockSpec((1, TQ, wide), lambda bi, hi, qi: (bi, qi, QD_BLK // DIFF_HP + hi)),
        pl.BlockSpec((1, s, wide), lambda bi, hi, qi: (bi, 0, KD_BLK // DIFF_HP + hi)),
        pl.BlockSpec((1, s, wide), lambda bi, hi, qi: (bi, 0, VD_BLK // DIFF_HP + hi)),
    ]
    table_in = [pl.BlockSpec((DIFF_HP, TQ, TQ), lambda bi, hi, qi: (hi, 0, 0)),
                pl.BlockSpec((1, DIFF_V_DIM), lambda bi, hi, qi: (0, 0))]
    feat_in = [pl.BlockSpec((DIFF_HP, TQ, LANES), lambda bi, hi, qi: (hi, 0, 0))] * 2
    out_spec = pl.BlockSpec((1, TQ, wide), lambda bi, hi, qi: (bi, qi, hi))
    out_shape = jax.ShapeDtypeStruct((b, s, N_DIFF_HEADS * DIFF_V_DIM), BF16)
    params = pltpu.CompilerParams(dimension_semantics=("arbitrary", "arbitrary", "arbitrary"),
                                  vmem_limit_bytes=VMEM_LIMIT)
    y_fast, flags = pl.pallas_call(
        _diff_attn_kernel,
        grid=(b, groups, nq),
        in_specs=common_in + feat_in + table_in,
        out_specs=[out_spec, pl.BlockSpec((1, 1, 1, 8, LANES), lambda bi, hi, qi: (bi, hi, qi, 0, 0))],
        out_shape=[out_shape, jax.ShapeDtypeStruct((b, groups, nq, 8, LANES), F32)],
        scratch_shapes=[pltpu.VMEM((DIFF_HP, 2, s, LANES), BF16),
                        pltpu.VMEM((DIFF_HP, s, 2 * LANES), BF16),
                        pltpu.VMEM((DIFF_HP, 2, TQ, 2 * LANES), F32)],
        compiler_params=params,
        name="diff_attn",
    )(slopes, lq1, lk1, lq2, lk2, proj, proj, proj, feat1, feat2, diagb, g)

    def running_max_version():
        return pl.pallas_call(
            _diff_robust_kernel,
            grid=(b, groups, nq),
            in_specs=common_in + table_in,
            out_specs=out_spec,
            out_shape=out_shape,
            scratch_shapes=[pltpu.VMEM((DIFF_HP, 2, TQ, LANES), F32),
                            pltpu.VMEM((DIFF_HP, 2, TQ, LANES), F32),
                            pltpu.VMEM((DIFF_HP, 2, TQ, LANES), F32),
                            pltpu.VMEM((DIFF_HP, 2, 2, TQ, TQ), F32),
                            pltpu.VMEM((DIFF_HP, 2, 2, TQ, LANES), F32)],
            compiler_params=params,
            name="diff_attn_running_max",
        )(slopes, lq1, lk1, lq2, lk2, proj, proj, proj, diagb, g)

    return lax.cond(jnp.max(flags) > 0.0, running_max_version, lambda: y_fast)


def _sb_attn_kernel(q_ref, k_ref, v_ref, tri_ref, g_ref, o_ref, acc_ref, carry_ref, reach_ref):
    qi = pl.program_id(2)
    pairs = range(SB_PP)
    pcols = [slice(pp * LANES, (pp + 1) * LANES) for pp in pairs]
    lane = lax.broadcasted_iota(jnp.int32, (TQ, LANES), 1)
    lo_half = lane < HEAD_DIM
    row = lax.broadcasted_iota(jnp.int32, (TQ, SB_W), 0)
    colk = lax.broadcasted_iota(jnp.int32, (TQ, SB_W), 1)
    strict = colk < row
    t0 = pl.multiple_of(qi * TQ, TQ)
    tri = tri_ref[...]

    def masked_queries():
        qm = []
        for pp in pairs:
            qf = q_ref[0, :, pcols[pp]].astype(F32) * (HEAD_DIM ** -0.5 * LOG2E)
            qm.append((jnp.where(lo_half, qf, 0.0).astype(BF16),
                       jnp.where(lo_half, 0.0, qf).astype(BF16)))
        return qm

    def scores(qm, pp, kstart, nkeys, rows=slice(0, TQ)):
        kb = k_ref[0, pl.ds(kstart, nkeys), pcols[pp]]
        return [_dot_nt(qm[pp][hh][rows], kb) for hh in range(2)]

    def softplus_split(zs, nsub, own):
        z, x = {}, {}
        for hh in range(2):
            for u in reversed(range(nsub)):
                r0 = u * SB_W if own else 0
                zc = zs[hh][r0:, u * SB_W:(u + 1) * SB_W]
                sp = jnp.where(zc > SB_LINEAR, zc, jnp.log2(1.0 + jnp.exp2(zc)))
                if own:
                    sp = jnp.where(strict[:TQ - r0], sp, 0.0)
                z[hh, u] = zc
                x[hh, u] = _hi_lo(sp)
        return z, x

    def suffix_sums(x):
        return {key: _dot(val, tri) for key, val in x.items()}

    def add_rows(full, lo, hi, delta):
        parts = [full[:lo]] if lo else []
        parts.append(full[lo:hi] + delta)
        if hi < full.shape[0]:
            parts.append(full[hi:])
        return parts[0] if len(parts) == 1 else jnp.concatenate(parts, axis=0)

    def weights_pv(pp, z, r, nsub, own, kstart, carry, acc, rows=slice(0, TQ)):
        vb = v_ref[0, pl.ds(kstart, nsub * SB_W), pcols[pp]]
        lo, hi = rows.start, rows.stop
        for hh in range(2):
            a = {}
            for u in reversed(range(nsub)):
                r0 = u * SB_W if own else lo
                av = jnp.exp2(z[hh, u] - r[hh, u][:, :SB_W] - carry[hh][r0:hi])
                if own:
                    av = jnp.where(strict[:TQ - r0], av, 0.0)
                avb = av.astype(BF16)
                if own and r0:
                    avb = jnp.concatenate([jnp.zeros((r0, SB_W), BF16), avb], axis=0)
                a[u] = avb
                carry[hh] = add_rows(carry[hh], r0, hi, r[hh, u][:, SB_W:])
            pv = _dot(jnp.concatenate([a[u] for u in range(nsub)], axis=1), vb)
            acc[hh] = add_rows(acc[hh], lo, hi, pv)

    def park(carry, acc):
        reach = None
        for pp in pairs:
            for hh in range(2):
                carry_ref[pp, hh] = carry[pp][hh]
                acc_ref[pp, hh] = acc[pp][hh]
                reach = carry[pp][hh] if reach is None else jnp.minimum(reach, carry[pp][hh])
        return reach

    def reload():
        return ([[carry_ref[pp, 0], carry_ref[pp, 1]] for pp in pairs],
                [[acc_ref[pp, 0], acc_ref[pp, 1]] for pp in pairs])

    def normalise(pp, acc):
        out = jnp.where(lo_half, acc[0], acc[1])
        sq = out * out
        ss_lo = jnp.sum(jnp.where(lo_half, sq, 0.0), axis=-1, keepdims=True)
        ss_hi = jnp.sum(jnp.where(lo_half, 0.0, sq), axis=-1, keepdims=True)
        ms = jnp.where(lo_half, ss_lo, ss_hi) * (1.0 / HEAD_DIM)
        o_ref[0, :, pcols[pp]] = (out * lax.rsqrt(ms + RMS_EPS) * g_ref[pp]).astype(o_ref.dtype)

    def near_keys(with_previous):
        qm = masked_queries()
        nsub1 = SB_NEAR // SB_W
        top = slice(0, SB_TOP)
        prev = pl.multiple_of(t0 - SB_NEAR, SB_NEAR)
        carry = [[jnp.zeros((TQ, SB_W), F32), jnp.zeros((TQ, SB_W), F32)] for _ in pairs]
        acc = [[jnp.zeros((TQ, LANES), F32), jnp.zeros((TQ, LANES), F32)] for _ in pairs]
        zx0 = [softplus_split(scores(qm, pp, t0, TQ), SB_U, True) for pp in pairs]
        if with_previous:
            zs1 = [scores(qm, pp, prev, SB_NEAR, top) for pp in pairs]
        r0 = [suffix_sums(zx0[pp][1]) for pp in pairs]
        if with_previous:
            zx1 = [softplus_split(zs1[pp], nsub1, False) for pp in pairs]
        for pp in pairs:
            weights_pv(pp, zx0[pp][0], r0[pp], SB_U, True, t0, carry[pp], acc[pp])
        if with_previous:
            r1 = [suffix_sums(zx1[pp][1]) for pp in pairs]
            for pp in pairs:
                weights_pv(pp, zx1[pp][0], r1[pp], nsub1, False, prev, carry[pp], acc[pp], top)
        reach = park(carry, acc)
        reach_ref[0] = jnp.min(reach)
        reach_ref[1] = jnp.min(reach[SB_TOP:])
        reach_ref[2] = 0.0
        for pp in pairs:
            normalise(pp, acc[pp])

    pl.when(qi == 0)(lambda: near_keys(False))
    pl.when(qi > 0)(lambda: near_keys(True))

    def chunk(kstart, rows):
        qm = masked_queries()
        nsub = SB_NEAR // SB_W
        zx = [softplus_split(scores(qm, pp, kstart, SB_NEAR, rows), nsub, False) for pp in pairs]
        r = [suffix_sums(zx[pp][1]) for pp in pairs]
        carry, acc = reload()
        for pp in pairs:
            weights_pv(pp, zx[pp][0], r[pp], nsub, False, kstart, carry[pp], acc[pp], rows)
        reach_ref[0] = jnp.min(park(carry, acc))
        reach_ref[2] = 1.0

    @pl.when(jnp.logical_and(qi > 0, reach_ref[1] < SB_EXIT))
    def _():
        chunk(pl.multiple_of(t0 - SB_NEAR, SB_NEAR), slice(SB_TOP, TQ))

    n_far = qi * (TQ // SB_NEAR) - 1

    def far_chunk(c):
        chunk(pl.multiple_of(t0 - (c + 2) * SB_NEAR, SB_NEAR), slice(0, TQ))
        return c + 1

    lax.while_loop(lambda c: jnp.logical_and(c < n_far, reach_ref[0] < SB_EXIT), far_chunk, 0)

    @pl.when(reach_ref[2] > 0.5)
    def _():
        for pp in pairs:
            normalise(pp, [acc_ref[pp, 0], acc_ref[pp, 1]])


def _sb_attn(proj, tri, g):
    b, s, _ = proj.shape
    nq = s // TQ
    npair = N_SB_HEADS // 2
    wide = SB_PP * LANES
    return pl.pallas_call(
        _sb_attn_kernel,
        grid=(b, npair // SB_PP, nq),
        in_specs=[
            pl.BlockSpec((1, TQ, wide), lambda bi, pi, qi: (bi, qi, QS_BLK // SB_PP + pi)),
            pl.BlockSpec((1, s, wide), lambda bi, pi, qi: (bi, 0, KS_BLK // SB_PP + pi)),
            pl.BlockSpec((1, s, wide), lambda bi, pi, qi: (bi, 0, VS_BLK // SB_PP + pi)),
            pl.BlockSpec((2 * SB_W, 2 * SB_W), lambda bi, pi, qi: (0, 0)),
            pl.BlockSpec((SB_PP, 1, LANES), lambda bi, pi, qi: (pi, 0, 0)),
        ],
        out_specs=pl.BlockSpec((1, TQ, wide), lambda bi, pi, qi: (bi, qi, pi)),
        out_shape=jax.ShapeDtypeStruct((b, s, N_SB_HEADS * HEAD_DIM), BF16),
        scratch_shapes=[pltpu.VMEM((SB_PP, 2, TQ, LANES), F32), pltpu.VMEM((SB_PP, 2, TQ, SB_W), F32),
                        pltpu.SMEM((3,), F32)],
        compiler_params=pltpu.CompilerParams(
            dimension_semantics=("arbitrary", "arbitrary", "arbitrary"),
            vmem_limit_bytes=VMEM_LIMIT),
        name="sb_attn",
    )(proj, proj, proj, tri, g)


def _xkv_kernel(mem_ref, w_ref, o_ref):
    nb, n, d = mem_ref.shape
    kv = _dot(mem_ref[...].reshape(nb * n, d), w_ref[...])
    o_ref[...] = kv.reshape(nb, n, kv.shape[-1]).astype(o_ref.dtype)


def _xkv_proj(mem, w):
    b, n, d = mem.shape
    nb = TM_PROJ // n
    return pl.pallas_call(
        _xkv_kernel,
        grid=(b // nb,),
        in_specs=[pl.BlockSpec((nb, n, d), lambda i: (i, 0, 0)),
                  pl.BlockSpec((d, 2 * d), lambda i: (0, 0), pipeline_mode=pl.Buffered(1))],
        out_specs=pl.BlockSpec((nb, n, 2 * d), lambda i: (i, 0, 0)),
        out_shape=jax.ShapeDtypeStruct((b, n, 2 * d), BF16),
        compiler_params=pltpu.CompilerParams(dimension_semantics=("arbitrary",),
                                             vmem_limit_bytes=VMEM_LIMIT),
        name="xkv_proj",
    )(mem, w)


def _tail_kernel(yd_ref, ys_ref, x_ref, kv_ref, wo_ref, g1_ref, b1_ref, wq_ref, wxo_ref,
                 g2_ref, b2_ref, wg_ref, wu_ref, wd_ref, g3_ref, b3_ref, o_ref):
    subs = [slice(t * TM_ROW, (t + 1) * TM_ROW) for t in range(N_SUB)]
    hcols = [slice(h * XHEAD_DIM, (h + 1) * XHEAD_DIM) for h in range(N_XHEADS)]

    def query(x1):
        return (_dot(x1.astype(BF16), wq_ref[...]) * (XHEAD_DIM ** -0.5)).astype(BF16)

    def scores(q):
        return [_dot_nt(q[:, hc], kv_ref[0, :, hc]) for hc in hcols]

    def attend(sc):
        ps = [jnp.exp(sh - jnp.max(sh, axis=-1, keepdims=True)) for sh in sc]
        inv = [1.0 / jnp.sum(p, axis=-1, keepdims=True) for p in ps]
        pv = [_dot(p.astype(BF16), kv_ref[0, :, D_MODEL + h * XHEAD_DIM:D_MODEL + (h + 1) * XHEAD_DIM])
              for h, p in enumerate(ps)]
        return jnp.concatenate([(o * i).astype(BF16) for o, i in zip(pv, inv)], axis=1)

    def out_proj(x1, heads):
        return _layer_norm(ALPHA * x1 + _dot(heads, wxo_ref[...]), g2_ref[...], b2_ref[...])

    def gate_up(x2):
        xb = x2.astype(BF16)
        gate, up = _dot(xb, wg_ref[...]), _dot(xb, wu_ref[...])
        return (gate * jax.nn.sigmoid(gate) * up).astype(BF16)

    a, b = subs
    half = yd_ref.shape[1]
    mix = [_dot(yd_ref[r], wo_ref[:half, :]) + _dot(ys_ref[r], wo_ref[half:, :]) for r in subs]
    x1 = [_layer_norm(ALPHA * x_ref[r] + m, g1_ref[...], b1_ref[...]) for r, m in zip(subs, mix)]
    q_a = query(x1[0])
    sc_a = scores(q_a)
    q_b = query(x1[1])
    heads_a = attend(sc_a)
    sc_b = scores(q_b)
    x2_a = out_proj(x1[0], heads_a)
    heads_b = attend(sc_b)
    x2_b = out_proj(x1[1], heads_b)
    hidden_a = gate_up(x2_a)
    hidden_b = gate_up(x2_b)
    o_ref[a] = _layer_norm(ALPHA * x2_a + _dot(hidden_a, wd_ref[...]), g3_ref[...], b3_ref[...])
    o_ref[b] = _layer_norm(ALPHA * x2_b + _dot(hidden_b, wd_ref[...]), g3_ref[...], b3_ref[...])


def _tail(yd, ys, x2d, kv, wo, g1, b1, wq, wxo, g2, b2, wg, wu, wd, g3, b3, seq):
    t, d = x2d.shape
    half = yd.shape[1]
    n = kv.shape[1]
    f = wg.shape[1]
    tm = N_SUB * TM_ROW
    per_batch = seq // tm
    row = lambda i: (i, 0)
    const = lambda i: (0, 0)
    resident = lambda shape: pl.BlockSpec(shape, const, pipeline_mode=pl.Buffered(1))
    return pl.pallas_call(
        _tail_kernel,
        grid=(t // tm,),
        in_specs=[pl.BlockSpec((tm, half), row), pl.BlockSpec((tm, half), row),
                  pl.BlockSpec((tm, d), row),
                  pl.BlockSpec((1, n, 2 * d), lambda i: (i // per_batch, 0, 0)),
                  resident((d, d)), resident((1, d)), resident((1, d)),
                  resident((d, d)), resident((d, d)), resident((1, d)), resident((1, d)),
                  resident((d, f)), resident((d, f)), resident((f, d)),
                  resident((1, d)), resident((1, d))],
        out_specs=pl.BlockSpec((tm, d), row),
        out_shape=jax.ShapeDtypeStruct((t, d), F32),
        compiler_params=pltpu.CompilerParams(dimension_semantics=("arbitrary",),
                                             vmem_limit_bytes=VMEM_LIMIT),
        name="tail",
    )(yd, ys, x2d, kv, wo, g1, b1, wq, wxo, g2, b2, wg, wu, wd, g3, b3)


def _alibi_tables():
    f32 = np.float32
    slopes = np.exp2(-8.0 * np.arange(1, N_DIFF_HEADS + 1, dtype=f32) / N_DIFF_HEADS).astype(f32)
    slope2 = (slopes * f32(LOG2E)).astype(f32)
    key_term = (slope2[:, None] * np.arange(TQ, dtype=f32)[None, :]).astype(f32)
    rest, terms = key_term, []
    for _ in range(N_FEAT):
        top = (rest.view(np.uint32) & np.uint32(0xFFFF0000)).view(f32)
        terms.append(top)
        rest = (rest - top).astype(f32)
    feat = np.stack(terms, axis=-1)
    feat1 = np.zeros((N_DIFF_HEADS, TQ, LANES), f32)
    feat2 = np.zeros((N_DIFF_HEADS, TQ, LANES), f32)
    feat1[:, :, HEAD_DIM:HEAD_DIM + N_FEAT] = feat
    feat2[:, :, :N_FEAT] = feat
    i = np.arange(TQ)[:, None]
    j = np.arange(TQ)[None, :]
    dist = np.abs(i - j).astype(f32)
    allowed = (j // CHUNK) <= (i // CHUNK)
    diag = np.where(allowed[None], -slope2[:, None, None] * dist[None], f32(NEG_INF)).astype(f32)
    return (jnp.asarray(slopes), jnp.asarray(feat1, BF16), jnp.asarray(feat2, BF16),
            jnp.asarray(diag - key_term[:, None, :]))


def _suffix_sum_matrix():
    j = np.arange(2 * SB_W)[:, None] % SB_W
    c = np.arange(2 * SB_W)[None, :]
    return jnp.asarray(np.where((c >= SB_W) | (j >= c), 1.0, 0.0), BF16)


def kernel(x, mem, w_in, diff_lambda_q1, diff_lambda_k1, diff_lambda_q2, diff_lambda_k2,
           diff_subln_g, sb_norm_g, w_o, ln1_g, ln1_b, w_xq, w_xkv, w_xo, ln2_g, ln2_b,
           w_gate, w_up, w_down, ln3_g, ln3_b):
    b, s, d = x.shape
    assert (b, s, d) == (8, SEQ, D_MODEL) and w_in.shape == (DEPTH, D_MODEL, D_IN)
    x2d = x.reshape(b * s, d)
    slopes, feat1, feat2, diagb = _alibi_tables()
    tri = _suffix_sum_matrix()
    vec = lambda a: a[0].reshape(1, -1)

    proj, (wo16, wq16, wxo16, wg16, wu16, wd16) = _in_proj(
        x2d, w_in[0], [w_o[0], w_xq[0], w_xo[0], w_gate[0], w_up[0], w_down[0]])
    proj = proj.reshape(b, s, D_IN)
    y_diff = _diff_attn(proj, slopes, vec(diff_lambda_q1), vec(diff_lambda_k1),
                        vec(diff_lambda_q2), vec(diff_lambda_k2), feat1, feat2, diagb,
                        vec(diff_subln_g))
    y_sb = _sb_attn(proj, tri, sb_norm_g[0].reshape(N_SB_HEADS // 2, 1, LANES))

    half = N_DIFF_HEADS * DIFF_V_DIM
    kv = _xkv_proj(mem, w_xkv[0])
    out = _tail(y_diff.reshape(b * s, half), y_sb.reshape(b * s, -1), x2d, kv,
                wo16, vec(ln1_g), vec(ln1_b), wq16, wxo16, vec(ln2_g), vec(ln2_b),
                wg16, wu16, wd16, vec(ln3_g), vec(ln3_b), s)
    return out.reshape(b, s, d)
```

```python
import math

import jax
import jax.numpy as jnp
import numpy as np
from jax import lax
from jax.experimental import pallas as pl
from jax.experimental.pallas import tpu as pltpu

D_MODEL = 1024
DEPTH = 1
CHUNK = 64
N_MEM = 256
HEAD_DIM = 64
N_DIFF_HEADS = 4
DIFF_V_DIM = 2 * HEAD_DIM
N_SB_HEADS = 8
N_XHEADS = 4
XHEAD_DIM = D_MODEL // N_XHEADS
D_FF = 2816
ALPHA = (2.0 * DEPTH) ** 0.25
LN_EPS = 1e-5
RMS_EPS = 1e-5
NEG_INF = -1e30
LAMBDA_INIT = 0.8 - 0.6 * math.exp(-0.3 * 0)
LOG2E = math.log2(math.e)

LANES = 128
BF16_SUBLANES = 16
QD_BLK, KD_BLK, VD_BLK, QS_BLK, KS_BLK, VS_BLK = 0, 4, 8, 12, 16, 20
D_IN = 24 * LANES

TM_PROJ = 1024
TM_ROW = 256
N_SUB = 2
TQ = 512
SEQ = 2048
NQ = SEQ // TQ
DIFF_HP = 2
DIAG_HALF = TQ // 2
FINITE_LIMIT = 3.0e38
N_FEAT = 3
SB_W = 128
SB_U = TQ // SB_W
SB_NEAR = 256
SB_EXIT = 150.0
SB_LINEAR = 64.0
SB_TOP = 256
SB_PP = 2
VMEM_LIMIT = 56 * 1024 * 1024

BF16 = jnp.bfloat16
F32 = jnp.float32


def _dot(a, b):
    return jnp.dot(a, b, preferred_element_type=F32)


def _dot_nt(a, b):
    return lax.dot_general(a, b, (((1,), (1,)), ((), ())), preferred_element_type=F32)


def _layer_norm(v, g, b):
    mu = jnp.mean(v, axis=-1, keepdims=True)
    d = v - mu
    var = jnp.mean(d * d, axis=-1, keepdims=True)
    return d * lax.rsqrt(var + LN_EPS) * g + b


def _hi_lo(v):
    hi = v.astype(BF16)
    lo = (v - hi.astype(F32)).astype(BF16)
    return jnp.concatenate([hi, lo], axis=1)


def _in_proj_kernel(x_ref, w_ref, *refs):
    n = len(refs) // 2
    o_ref = refs[n]
    o_ref[...] = _dot(x_ref[...], w_ref[...]).astype(o_ref.dtype)
    for src, dst in zip(refs[:n], refs[n + 1:]):
        dst[...] = src[...].astype(dst.dtype)


def _in_proj(x2d, w, tail_weights):
    t, d = x2d.shape
    n = w.shape[1]
    steps = t // TM_PROJ
    w_specs, w_shapes = [], []
    for tw in tail_weights:
        rows, cols = tw.shape
        assert rows % (BF16_SUBLANES * steps // 2) == 0
        blk, index = 2 * rows // steps, (lambda i: (jnp.minimum(i, steps // 2 - 1), 0))
        w_specs.append(pl.BlockSpec((blk, cols), index))
        w_shapes.append(jax.ShapeDtypeStruct(tw.shape, BF16))
    outs = pl.pallas_call(
        _in_proj_kernel,
        grid=(steps,),
        in_specs=[pl.BlockSpec((TM_PROJ, d), lambda i: (i, 0)),
                  pl.BlockSpec((d, n), lambda i: (0, 0), pipeline_mode=pl.Buffered(1))] + w_specs,
        out_specs=[pl.BlockSpec((TM_PROJ, n), lambda i: (i, 0))] + w_specs,
        out_shape=[jax.ShapeDtypeStruct((t, n), BF16)] + w_shapes,
        compiler_params=pltpu.CompilerParams(dimension_semantics=("arbitrary",),
                                             vmem_limit_bytes=VMEM_LIMIT),
        name="in_proj",
    )(x2d, w, *tail_weights)
    return outs[0], outs[1:]


def _diff_attn_kernel(slopes_ref, lq1_ref, lk1_ref, lq2_ref, lk2_ref, q_ref, k_ref, v_ref,
                      feat1_ref, feat2_ref, diagb_ref, g_ref, o_ref, flag_ref,
                      kaug_ref, vaug_ref, facc_ref):
    hp = pl.program_id(1)
    qi = pl.program_id(2)

    @pl.when(qi == 0)
    def _():
        lane = lax.broadcasted_iota(jnp.int32, (TQ, LANES), 1)
        first = lane < HEAD_DIM
        for hd in range(DIFF_HP):
            cols = slice(hd * LANES, (hd + 1) * LANES)
            f1 = feat1_ref[hd].astype(F32)
            f2 = feat2_ref[hd].astype(F32)
            for blk in range(NQ):
                rows = slice(blk * TQ, (blk + 1) * TQ)
                kblk = k_ref[0, rows, cols].astype(F32)
                kaug_ref[hd, 0, rows, :] = jnp.where(first, kblk, f1).astype(BF16)
                kaug_ref[hd, 1, rows, :] = jnp.where(first, f2, kblk).astype(BF16)
                vaug_ref[hd, rows, :LANES] = v_ref[0, rows, cols]
                vaug_ref[hd, rows, LANES:] = jnp.ones((TQ, LANES), BF16)

    for nb in range(NQ):
        pl.when(qi == nb)(lambda nb=nb: _diff_query_block_fast(
            nb, hp, slopes_ref, lq1_ref, lk1_ref, lq2_ref, lk2_ref, q_ref, k_ref,
            diagb_ref, g_ref, o_ref, flag_ref, kaug_ref, vaug_ref, facc_ref))


def _diff_query_block_fast(nb, hp, slopes_ref, lq1_ref, lk1_ref, lq2_ref, lk2_ref, q_ref, k_ref,
                           diagb_ref, g_ref, o_ref, flag_ref, kaug_ref, vaug_ref, facc_ref):
    heads = range(DIFF_HP)
    hcols = [slice(hd * LANES, (hd + 1) * LANES) for hd in heads]
    slope2 = [slopes_ref[hp * DIFF_HP + hd] * LOG2E for hd in heads]
    t0 = nb * TQ
    lane = lax.broadcasted_iota(jnp.int32, (TQ, LANES), 1)
    first = lane < HEAD_DIM
    ones1 = jnp.where(lane < HEAD_DIM + N_FEAT, 1.0, 0.0)
    ones2 = jnp.where(lane < N_FEAT, 1.0, 0.0)
    row = lax.broadcasted_iota(jnp.int32, (TQ, LANES), 0).astype(F32)
    halves = (slice(0, DIAG_HALF), slice(DIAG_HALF, TQ))

    qm, ref0 = [], []
    for hd in heads:
        qf = q_ref[0, :, hcols[hd]].astype(F32) * (HEAD_DIM ** -0.5 * LOG2E)
        qm.append((jnp.where(first, qf, ones1).astype(BF16), jnp.where(first, ones2, qf).astype(BF16)))
        own = qf * k_ref[0, t0:t0 + TQ, hcols[hd]].astype(F32)
        ref0.append((jnp.sum(jnp.where(first, own, 0.0), axis=-1, keepdims=True),
                     jnp.sum(jnp.where(first, 0.0, own), axis=-1, keepdims=True)))

    units = []
    for blk in range(nb, -1, -1):
        for hd in heads:
            for mp in range(2):
                for rows in halves:
                    nkeys = rows.stop if blk == nb else TQ
                    units.append((hd, mp, rows, blk * TQ, nkeys, blk))

    def qk(u):
        hd, mp, rows, kstart, nkeys, _ = u
        return _dot_nt(qm[hd][mp][rows], kaug_ref[hd, mp, kstart:kstart + nkeys, :])

    seen = set()

    def consume(u, s):
        hd, mp, rows, kstart, nkeys, blk = u
        if blk == nb:
            s = s + diagb_ref[hd, rows, :nkeys]
            shift = -ref0[hd][mp][rows]
        else:
            shift = -slope2[hd] * (row[rows] + float((nb - blk) * TQ)) - ref0[hd][mp][rows]
        p = jnp.concatenate([jnp.exp2(s[:, c * LANES:(c + 1) * LANES] + shift).astype(BF16)
                             for c in range(nkeys // LANES)], axis=1)
        pv = _dot(p, vaug_ref[hd, kstart:kstart + nkeys, :])
        key = (hd, mp, rows.start)
        if key in seen:
            facc_ref[hd, mp, rows] += pv
        else:
            seen.add(key)
            facc_ref[hd, mp, rows] = pv

    s_next = qk(units[0])
    for n, u in enumerate(units):
        s_cur = s_next
        if n + 1 < len(units):
            s_next = qk(units[n + 1])
        consume(u, s_cur)

    lam = (jnp.exp(jnp.sum(lq1_ref[...] * lk1_ref[...]))
           - jnp.exp(jnp.sum(lq2_ref[...] * lk2_ref[...])) + LAMBDA_INIT)
    bad = jnp.zeros((TQ, LANES), F32)
    for hd in heads:
        l1 = facc_ref[hd, 0, :, LANES:]
        l2 = facc_ref[hd, 1, :, LANES:]
        out = facc_ref[hd, 0, :, :LANES] * (1.0 / l1) - lam * (facc_ref[hd, 1, :, :LANES] * (1.0 / l2))
        ms = jnp.mean(out * out, axis=-1, keepdims=True)
        out = out * lax.rsqrt(ms + RMS_EPS) * g_ref[...] * (1.0 - LAMBDA_INIT)
        o_ref[0, :, hcols[hd]] = out.astype(o_ref.dtype)
        for chk in (out, l1 + l2):
            bad = jnp.maximum(bad, jnp.where(jnp.abs(chk) < FINITE_LIMIT, 0.0, 1.0))

    flag_ref[...] = jnp.broadcast_to(jnp.max(bad, axis=0, keepdims=True)[None, None, None],
                                     flag_ref.shape)


def _diff_robust_kernel(slopes_ref, lq1_ref, lk1_ref, lq2_ref, lk2_ref, q_ref, k_ref, v_ref,
                        diagb_ref, g_ref, o_ref, m_ref, l_ref, acc_ref, s_ref, mb_ref):
    hp = pl.program_id(1)
    qi = pl.program_id(2)
    for nb in range(NQ):
        pl.when(qi == nb)(lambda nb=nb: _diff_query_block(
            nb, hp, slopes_ref, lq1_ref, lk1_ref, lq2_ref, lk2_ref, q_ref, k_ref, v_ref,
            diagb_ref, g_ref, o_ref, m_ref, l_ref, acc_ref, s_ref, mb_ref))


def _diff_query_block(nb, hp, slopes_ref, lq1_ref, lk1_ref, lq2_ref, lk2_ref, q_ref, k_ref, v_ref,
                      diagb_ref, g_ref, o_ref, m_ref, l_ref, acc_ref, s_ref, mb_ref):
    heads = range(DIFF_HP)
    hcols = [slice(hd * LANES, (hd + 1) * LANES) for hd in heads]
    slope2 = [slopes_ref[hp * DIFF_HP + hd] * LOG2E for hd in heads]

    lane = lax.broadcasted_iota(jnp.int32, (TQ, LANES), 1)
    qm = []
    for hd in heads:
        qf = q_ref[0, :, hcols[hd]].astype(F32) * (HEAD_DIM ** -0.5 * LOG2E)
        qm.append((jnp.where(lane < HEAD_DIM, qf, 0.0).astype(BF16),
                   jnp.where(lane >= HEAD_DIM, qf, 0.0).astype(BF16)))

    m_ref[...] = jnp.full_like(m_ref, NEG_INF)
    l_ref[...] = jnp.zeros_like(l_ref)
    acc_ref[...] = jnp.zeros_like(acc_ref)

    col = lax.broadcasted_iota(jnp.int32, (1, TQ), 1).astype(F32)
    row = lax.broadcasted_iota(jnp.int32, (TQ, LANES), 0).astype(F32)

    full = slice(0, TQ)
    diag_parts = ((slice(0, DIAG_HALF), DIAG_HALF), (slice(DIAG_HALF, TQ), TQ))

    def scores(hd, slot, kstart, nkeys, rows, diagonal):
        kb = k_ref[0, kstart:kstart + nkeys, hcols[hd]]
        bias = slope2[hd] * col[:, :nkeys]
        if diagonal:
            bias = bias + diagb_ref[hd, rows, :nkeys]
        for mp in range(2):
            s = _dot_nt(qm[hd][mp][rows], kb) + bias
            s_ref[hd, slot, mp, rows, :nkeys] = s
            mb_ref[hd, slot, mp, rows] = jnp.broadcast_to(jnp.max(s, axis=-1, keepdims=True),
                                                          (rows.stop - rows.start, LANES))

    def softmax_pv(hd, slot, kstart, nkeys, rows, row_shift):
        vb = v_ref[0, kstart:kstart + nkeys, hcols[hd]]
        for mp in range(2):
            m_old = m_ref[hd, mp, rows]
            mb = mb_ref[hd, slot, mp, rows]
            m_new = jnp.maximum(m_old, mb if row_shift is None else mb + row_shift)
            alpha = jnp.exp2(m_old - m_new)
            mrel = m_new if row_shift is None else m_new - row_shift
            ps = [jnp.exp2(s_ref[hd, slot, mp, rows, c * LANES:(c + 1) * LANES] - mrel)
                  for c in range(nkeys // LANES)]
            psum = ps[0]
            for pc in ps[1:]:
                psum = psum + pc
            l_ref[hd, mp, rows] = alpha * l_ref[hd, mp, rows] + psum
            p = jnp.concatenate([pc.astype(BF16) for pc in ps], axis=1)
            acc_ref[hd, mp, rows] = alpha * acc_ref[hd, mp, rows] + _dot(p, vb)
            m_ref[hd, mp, rows] = m_new

    def issue(hd, i):
        if i < nb:
            scores(hd, i % 2, i * TQ, TQ, full, False)
        else:
            for rows, nkeys in diag_parts:
                scores(hd, i % 2, nb * TQ, nkeys, rows, True)

    def consume(hd, i):
        if i < nb:
            softmax_pv(hd, i % 2, i * TQ, TQ, full, -slope2[hd] * (row + float((nb - i) * TQ)))
        else:
            for rows, nkeys in diag_parts:
                softmax_pv(hd, i % 2, nb * TQ, nkeys, rows, None)

    lam = (jnp.exp(jnp.sum(lq1_ref[...] * lk1_ref[...]))
           - jnp.exp(jnp.sum(lq2_ref[...] * lk2_ref[...])) + LAMBDA_INIT)

    def normalise(hd):
        l1 = jnp.sum(l_ref[hd, 0], axis=-1, keepdims=True)
        l2 = jnp.sum(l_ref[hd, 1], axis=-1, keepdims=True)
        out = acc_ref[hd, 0] * (1.0 / l1) - lam * (acc_ref[hd, 1] * (1.0 / l2))
        ms = jnp.mean(out * out, axis=-1, keepdims=True)
        out = out * lax.rsqrt(ms + RMS_EPS) * g_ref[...] * (1.0 - LAMBDA_INIT)
        o_ref[0, :, hcols[hd]] = out.astype(o_ref.dtype)

    for hd in heads:
        issue(hd, 0)
    for i in range(nb + 1):
        if i < nb:
            for hd in heads:
                issue(hd, i + 1)
        for hd in heads:
            consume(hd, i)
            if i == nb:
                normalise(hd)


def _diff_attn(proj, slopes, lq1, lk1, lq2, lk2, feat1, feat2, diagb, g):
    b, s, _ = proj.shape
    nq = s // TQ
    groups = N_DIFF_HEADS // DIFF_HP
    wide = DIFF_HP * LANES
    lam_spec = pl.BlockSpec((1, HEAD_DIM), lambda bi, hi, qi: (0, 0))
    common_in = [
        pl.BlockSpec(memory_space=pltpu.SMEM),
        lam_spec, lam_spec, lam_spec, lam_spec,
        pl.BlockSpec((1, TQ, wide), lambda bi, hi, qi: (bi, qi, QD_BLK // DIFF_HP + hi)),
        pl.BlockSpec((1, s, wide), lambda bi, hi, qi: (bi, 0, KD_BLK // DIFF_HP + hi)),
        pl.BlockSpec((1, s, wide), lambda bi, hi, qi: (bi, 0, VD_BLK // DIFF_HP + hi)),
    ]
    table_in = [pl.BlockSpec((DIFF_HP, TQ, TQ), lambda bi, hi, qi: (hi, 0, 0)),
                pl.BlockSpec((1, DIFF_V_DIM), lambda bi, hi, qi: (0, 0))]
    feat_in = [pl.BlockSpec((DIFF_HP, TQ, LANES), lambda bi, hi, qi: (hi, 0, 0))] * 2
    out_spec = pl.BlockSpec((1, TQ, wide), lambda bi, hi, qi: (bi, qi, hi))
    out_shape = jax.ShapeDtypeStruct((b, s, N_DIFF_HEADS * DIFF_V_DIM), BF16)
    params = pltpu.CompilerParams(dimension_semantics=("arbitrary", "arbitrary", "arbitrary"),
                                  vmem_limit_bytes=VMEM_LIMIT)
    y_fast, flags = pl.pallas_call(
        _diff_attn_kernel,
        grid=(b, groups, nq),
        in_specs=common_in + feat_in + table_in,
        out_specs=[out_spec, pl.BlockSpec((1, 1, 1, 8, LANES), lambda bi, hi, qi: (bi, hi, qi, 0, 0))],
        out_shape=[out_shape, jax.ShapeDtypeStruct((b, groups, nq, 8, LANES), F32)],
        scratch_shapes=[pltpu.VMEM((DIFF_HP, 2, s, LANES), BF16),
                        pltpu.VMEM((DIFF_HP, s, 2 * LANES), BF16),
                        pltpu.VMEM((DIFF_HP, 2, TQ, 2 * LANES), F32)],
        compiler_params=params,
        name="diff_attn",
    )(slopes, lq1, lk1, lq2, lk2, proj, proj, proj, feat1, feat2, diagb, g)

    def running_max_version():
        return pl.pallas_call(
            _diff_robust_kernel,
            grid=(b, groups, nq),
            in_specs=common_in + table_in,
            out_specs=out_spec,
            out_shape=out_shape,
            scratch_shapes=[pltpu.VMEM((DIFF_HP, 2, TQ, LANES), F32),
                            pltpu.VMEM((DIFF_HP, 2, TQ, LANES), F32),
                            pltpu.VMEM((DIFF_HP, 2, TQ, LANES), F32),
                            pltpu.VMEM((DIFF_HP, 2, 2, TQ, TQ), F32),
                            pltpu.VMEM((DIFF_HP, 2, 2, TQ, LANES), F32)],
            compiler_params=params,
            name="diff_attn_running_max",
        )(slopes, lq1, lk1, lq2, lk2, proj, proj, proj, diagb, g)

    return lax.cond(jnp.max(flags) > 0.0, running_max_version, lambda: y_fast)


def _sb_attn_kernel(q_ref, k_ref, v_ref, tri_ref, g_ref, o_ref, acc_ref, carry_ref, reach_ref):
    qi = pl.program_id(2)
    pairs = range(SB_PP)
    pcols = [slice(pp * LANES, (pp + 1) * LANES) for pp in pairs]
    lane = lax.broadcasted_iota(jnp.int32, (TQ, LANES), 1)
    lo_half = lane < HEAD_DIM
    row = lax.broadcasted_iota(jnp.int32, (TQ, SB_W), 0)
    colk = lax.broadcasted_iota(jnp.int32, (TQ, SB_W), 1)
    strict = colk < row
    t0 = pl.multiple_of(qi * TQ, TQ)
    tri = tri_ref[...]

    def masked_queries():
        qm = []
        for pp in pairs:
            qf = q_ref[0, :, pcols[pp]].astype(F32) * (HEAD_DIM ** -0.5 * LOG2E)
            qm.append((jnp.where(lo_half, qf, 0.0).astype(BF16),
                       jnp.where(lo_half, 0.0, qf).astype(BF16)))
        return qm

    def scores(qm, pp, kstart, nkeys, rows=slice(0, TQ)):
        kb = k_ref[0, pl.ds(kstart, nkeys), pcols[pp]]
        return [_dot_nt(qm[pp][hh][rows], kb) for hh in range(2)]

    def softplus_split(zs, nsub, own):
        z, x = {}, {}
        for hh in range(2):
            for u in reversed(range(nsub)):
                r0 = u * SB_W if own else 0
                zc = zs[hh][r0:, u * SB_W:(u + 1) * SB_W]
                sp = jnp.where(zc > SB_LINEAR, zc, jnp.log2(1.0 + jnp.exp2(zc)))
                if own:
                    sp = jnp.where(strict[:TQ - r0], sp, 0.0)
                z[hh, u] = zc
                x[hh, u] = _hi_lo(sp)
        return z, x

    def suffix_sums(x):
        return {key: _dot(val, tri) for key, val in x.items()}

    def add_rows(full, lo, hi, delta):
        parts = [full[:lo]] if lo else []
        parts.append(full[lo:hi] + delta)
        if hi < full.shape[0]:
            parts.append(full[hi:])
        return parts[0] if len(parts) == 1 else jnp.concatenate(parts, axis=0)

    def weights_pv(pp, z, r, nsub, own, kstart, carry, acc, rows=slice(0, TQ)):
        vb = v_ref[0, pl.ds(kstart, nsub * SB_W), pcols[pp]]
        lo, hi = rows.start, rows.stop
        for hh in range(2):
            a = {}
            for u in reversed(range(nsub)):
                r0 = u * SB_W if own else lo
                av = jnp.exp2(z[hh, u] - r[hh, u][:, :SB_W] - carry[hh][r0:hi])
                if own:
                    av = jnp.where(strict[:TQ - r0], av, 0.0)
                avb = av.astype(BF16)
                if own and r0:
                    avb = jnp.concatenate([jnp.zeros((r0, SB_W), BF16), avb], axis=0)
                a[u] = avb
                carry[hh] = add_rows(carry[hh], r0, hi, r[hh, u][:, SB_W:])
            pv = _dot(jnp.concatenate([a[u] for u in range(nsub)], axis=1), vb)
            acc[hh] = add_rows(acc[hh], lo, hi, pv)

    def park(carry, acc):
        reach = None
        for pp in pairs:
            for hh in range(2):
                carry_ref[pp, hh] = carry[pp][hh]
                acc_ref[pp, hh] = acc[pp][hh]
                reach = carry[pp][hh] if reach is None else jnp.minimum(reach, carry[pp][hh])
        return reach

    def reload():
        return ([[carry_ref[pp, 0], carry_ref[pp, 1]] for pp in pairs],
                [[acc_ref[pp, 0], acc_ref[pp, 1]] for pp in pairs])

    def normalise(pp, acc):
        out = jnp.where(lo_half, acc[0], acc[1])
        sq = out * out
        ss_lo = jnp.sum(jnp.where(lo_half, sq, 0.0), axis=-1, keepdims=True)
        ss_hi = jnp.sum(jnp.where(lo_half, 0.0, sq), axis=-1, keepdims=True)
        ms = jnp.where(lo_half, ss_lo, ss_hi) * (1.0 / HEAD_DIM)
        o_ref[0, :, pcols[pp]] = (out * lax.rsqrt(ms + RMS_EPS) * g_ref[pp]).astype(o_ref.dtype)

    def near_keys(with_previous):
        qm = masked_queries()
        nsub1 = SB_NEAR // SB_W
        top = slice(0, SB_TOP)
        prev = pl.multiple_of(t0 - SB_NEAR, SB_NEAR)
        carry = [[jnp.zeros((TQ, SB_W), F32), jnp.zeros((TQ, SB_W), F32)] for _ in pairs]
        acc = [[jnp.zeros((TQ, LANES), F32), jnp.zeros((TQ, LANES), F32)] for _ in pairs]
        zx0 = [softplus_split(scores(qm, pp, t0, TQ), SB_U, True) for pp in pairs]
        if with_previous:
            zs1 = [scores(qm, pp, prev, SB_NEAR, top) for pp in pairs]
        r0 = [suffix_sums(zx0[pp][1]) for pp in pairs]
        if with_previous:
            zx1 = [softplus_split(zs1[pp], nsub1, False) for pp in pairs]
        for pp in pairs:
            weights_pv(pp, zx0[pp][0], r0[pp], SB_U, True, t0, carry[pp], acc[pp])
        if with_previous:
            r1 = [suffix_sums(zx1[pp][1]) for pp in pairs]
            for pp in pairs:
                weights_pv(pp, zx1[pp][0], r1[pp], nsub1, False, prev, carry[pp], acc[pp], top)
        reach = park(carry, acc)
        reach_ref[0] = jnp.min(reach)
        reach_ref[1] = jnp.min(reach[SB_TOP:])
        reach_ref[2] = 0.0
        for pp in pairs:
            normalise(pp, acc[pp])

    pl.when(qi == 0)(lambda: near_keys(False))
    pl.when(qi > 0)(lambda: near_keys(True))

    def chunk(kstart, rows):
        qm = masked_queries()
        nsub = SB_NEAR // SB_W
        zx = [softplus_split(scores(qm, pp, kstart, SB_NEAR, rows), nsub, False) for pp in pairs]
        r = [suffix_sums(zx[pp][1]) for pp in pairs]
        carry, acc = reload()
        for pp in pairs:
            weights_pv(pp, zx[pp][0], r[pp], nsub, False, kstart, carry[pp], acc[pp], rows)
        reach_ref[0] = jnp.min(park(carry, acc))
        reach_ref[2] = 1.0

    @pl.when(jnp.logical_and(qi > 0, reach_ref[1] < SB_EXIT))
    def _():
        chunk(pl.multiple_of(t0 - SB_NEAR, SB_NEAR), slice(SB_TOP, TQ))

    n_far = qi * (TQ // SB_NEAR) - 1

    def far_chunk(c):
        chunk(pl.multiple_of(t0 - (c + 2) * SB_NEAR, SB_NEAR), slice(0, TQ))
        return c + 1

    lax.while_loop(lambda c: jnp.logical_and(c < n_far, reach_ref[0] < SB_EXIT), far_chunk, 0)

    @pl.when(reach_ref[2] > 0.5)
    def _():
        for pp in pairs:
            normalise(pp, [acc_ref[pp, 0], acc_ref[pp, 1]])


def _sb_attn(proj, tri, g):
    b, s, _ = proj.shape
    nq = s // TQ
    npair = N_SB_HEADS // 2
    wide = SB_PP * LANES
    return pl.pallas_call(
        _sb_attn_kernel,
        grid=(b, npair // SB_PP, nq),
        in_specs=[
            pl.BlockSpec((1, TQ, wide), lambda bi, pi, qi: (bi, qi, QS_BLK // SB_PP + pi)),
            pl.BlockSpec((1, s, wide), lambda bi, pi, qi: (bi, 0, KS_BLK // SB_PP + pi)),
            pl.BlockSpec((1, s, wide), lambda bi, pi, qi: (bi, 0, VS_BLK // SB_PP + pi)),
            pl.BlockSpec((2 * SB_W, 2 * SB_W), lambda bi, pi, qi: (0, 0)),
            pl.BlockSpec((SB_PP, 1, LANES), lambda bi, pi, qi: (pi, 0, 0)),
        ],
        out_specs=pl.BlockSpec((1, TQ, wide), lambda bi, pi, qi: (bi, qi, pi)),
        out_shape=jax.ShapeDtypeStruct((b, s, N_SB_HEADS * HEAD_DIM), BF16),
        scratch_shapes=[pltpu.VMEM((SB_PP, 2, TQ, LANES), F32), pltpu.VMEM((SB_PP, 2, TQ, SB_W), F32),
                        pltpu.SMEM((3,), F32)],
        compiler_params=pltpu.CompilerParams(
            dimension_semantics=("arbitrary", "arbitrary", "arbitrary"),
            vmem_limit_bytes=VMEM_LIMIT),
        name="sb_attn",
    )(proj, proj, proj, tri, g)


def _xkv_kernel(mem_ref, w_ref, o_ref):
    nb, n, d = mem_ref.shape
    kv = _dot(mem_ref[...].reshape(nb * n, d), w_ref[...])
    o_ref[...] = kv.reshape(nb, n, kv.shape[-1]).astype(o_ref.dtype)


def _xkv_proj(mem, w):
    b, n, d = mem.shape
    nb = TM_PROJ // n
    return pl.pallas_call(
        _xkv_kernel,
        grid=(b // nb,),
        in_specs=[pl.BlockSpec((nb, n, d), lambda i: (i, 0, 0)),
                  pl.BlockSpec((d, 2 * d), lambda i: (0, 0), pipeline_mode=pl.Buffered(1))],
        out_specs=pl.BlockSpec((nb, n, 2 * d), lambda i: (i, 0, 0)),
        out_shape=jax.ShapeDtypeStruct((b, n, 2 * d), BF16),
        compiler_params=pltpu.CompilerParams(dimension_semantics=("arbitrary",),
                                             vmem_limit_bytes=VMEM_LIMIT),
        name="xkv_proj",
    )(mem, w)


def _tail_kernel(yd_ref, ys_ref, x_ref, kv_ref, wo_ref, g1_ref, b1_ref, wq_ref, wxo_ref,
                 g2_ref, b2_ref, wg_ref, wu_ref, wd_ref, g3_ref, b3_ref, o_ref):
    subs = [slice(t * TM_ROW, (t + 1) * TM_ROW) for t in range(N_SUB)]
    hcols = [slice(h * XHEAD_DIM, (h + 1) * XHEAD_DIM) for h in range(N_XHEADS)]

    def query(x1):
        return (_dot(x1.astype(BF16), wq_ref[...]) * (XHEAD_DIM ** -0.5)).astype(BF16)

    def scores(q):
        return [_dot_nt(q[:, hc], kv_ref[0, :, hc]) for hc in hcols]

    def attend(sc):
        ps = [jnp.exp(sh - jnp.max(sh, axis=-1, keepdims=True)) for sh in sc]
        inv = [1.0 / jnp.sum(p, axis=-1, keepdims=True) for p in ps]
        pv = [_dot(p.astype(BF16), kv_ref[0, :, D_MODEL + h * XHEAD_DIM:D_MODEL + (h + 1) * XHEAD_DIM])
              for h, p in enumerate(ps)]
        return jnp.concatenate([(o * i).astype(BF16) for o, i in zip(pv, inv)], axis=1)

    def out_proj(x1, heads):
        return _layer_norm(ALPHA * x1 + _dot(heads, wxo_ref[...]), g2_ref[...], b2_ref[...])

    def gate_up(x2):
        xb = x2.astype(BF16)
        gate, up = _dot(xb, wg_ref[...]), _dot(xb, wu_ref[...])
        return (gate * jax.nn.sigmoid(gate) * up).astype(BF16)

    a, b = subs
    half = yd_ref.shape[1]
    mix = [_dot(yd_ref[r], wo_ref[:half, :]) + _dot(ys_ref[r], wo_ref[half:, :]) for r in subs]
    x1 = [_layer_norm(ALPHA * x_ref[r] + m, g1_ref[...], b1_ref[...]) for r, m in zip(subs, mix)]
    q_a = query(x1[0])
    sc_a = scores(q_a)
    q_b = query(x1[1])
    heads_a = attend(sc_a)
    sc_b = scores(q_b)
    x2_a = out_proj(x1[0], heads_a)
    heads_b = attend(sc_b)
    x2_b = out_proj(x1[1], heads_b)
    hidden_a = gate_up(x2_a)
    hidden_b = gate_up(x2_b)
    o_ref[a] = _layer_norm(ALPHA * x2_a + _dot(hidden_a, wd_ref[...]), g3_ref[...], b3_ref[...])
    o_ref[b] = _layer_norm(ALPHA * x2_b + _dot(hidden_b, wd_ref[...]), g3_ref[...], b3_ref[...])


def _tail(yd, ys, x2d, kv, wo, g1, b1, wq, wxo, g2, b2, wg, wu, wd, g3, b3, seq):
    t, d = x2d.shape
    half = yd.shape[1]
    n = kv.shape[1]
    f = wg.shape[1]
    tm = N_SUB * TM_ROW
    per_batch = seq // tm
    row = lambda i: (i, 0)
    const = lambda i: (0, 0)
    resident = lambda shape: pl.BlockSpec(shape, const, pipeline_mode=pl.Buffered(1))
    return pl.pallas_call(
        _tail_kernel,
        grid=(t // tm,),
        in_specs=[pl.BlockSpec((tm, half), row), pl.BlockSpec((tm, half), row),
                  pl.BlockSpec((tm, d), row),
                  pl.BlockSpec((1, n, 2 * d), lambda i: (i // per_batch, 0, 0)),
                  resident((d, d)), resident((1, d)), resident((1, d)),
                  resident((d, d)), resident((d, d)), resident((1, d)), resident((1, d)),
                  resident((d, f)), resident((d, f)), resident((f, d)),
                  resident((1, d)), resident((1, d))],
        out_specs=pl.BlockSpec((tm, d), row),
        out_shape=jax.ShapeDtypeStruct((t, d), F32),
        compiler_params=pltpu.CompilerParams(dimension_semantics=("arbitrary",),
                                             vmem_limit_bytes=VMEM_LIMIT),
        name="tail",
    )(yd, ys, x2d, kv, wo, g1, b1, wq, wxo, g2, b2, wg, wu, wd, g3, b3)


def _alibi_tables():
    f32 = np.float32
    slopes = np.exp2(-8.0 * np.arange(1, N_DIFF_HEADS + 1, dtype=f32) / N_DIFF_HEADS).astype(f32)
    slope2 = (slopes * f32(LOG2E)).astype(f32)
    key_term = (slope2[:, None] * np.arange(TQ, dtype=f32)[None, :]).astype(f32)
    rest, terms = key_term, []
    for _ in range(N_FEAT):
        top = (rest.view(np.uint32) & np.uint32(0xFFFF0000)).view(f32)
        terms.append(top)
        rest = (rest - top).astype(f32)
    feat = np.stack(terms, axis=-1)
    feat1 = np.zeros((N_DIFF_HEADS, TQ, LANES), f32)
    feat2 = np.zeros((N_DIFF_HEADS, TQ, LANES), f32)
    feat1[:, :, HEAD_DIM:HEAD_DIM + N_FEAT] = feat
    feat2[:, :, :N_FEAT] = feat
    i = np.arange(TQ)[:, None]
    j = np.arange(TQ)[None, :]
    dist = np.abs(i - j).astype(f32)
    allowed = (j // CHUNK) <= (i // CHUNK)
    diag = np.where(allowed[None], -slope2[:, None, None] * dist[None], f32(NEG_INF)).astype(f32)
    return (jnp.asarray(slopes), jnp.asarray(feat1, BF16), jnp.asarray(feat2, BF16),
            jnp.asarray(diag - key_term[:, None, :]))


def _suffix_sum_matrix():
    j = np.arange(2 * SB_W)[:, None] % SB_W
    c = np.arange(2 * SB_W)[None, :]
    return jnp.asarray(np.where((c >= SB_W) | (j >= c), 1.0, 0.0), BF16)


def kernel(x, mem, w_in, diff_lambda_q1, diff_lambda_k1, diff_lambda_q2, diff_lambda_k2,
           diff_subln_g, sb_norm_g, w_o, ln1_g, ln1_b, w_xq, w_xkv, w_xo, ln2_g, ln2_b,
           w_gate, w_up, w_down, ln3_g, ln3_b):
    b, s, d = x.shape
    assert (b, s, d) == (8, SEQ, D_MODEL) and w_in.shape == (DEPTH, D_MODEL, D_IN)
    x2d = x.reshape(b * s, d)
    slopes, feat1, feat2, diagb = _alibi_tables()
    tri = _suffix_sum_matrix()
    vec = lambda a: a[0].reshape(1, -1)

    proj, (wo16, wq16, wxo16, wg16, wu16, wd16) = _in_proj(
        x2d, w_in[0], [w_o[0], w_xq[0], w_xo[0], w_gate[0], w_up[0], w_down[0]])
    proj = proj.reshape(b, s, D_IN)
    y_diff = _diff_attn(proj, slopes, vec(diff_lambda_q1), vec(diff_lambda_k1),
                        vec(diff_lambda_q2), vec(diff_lambda_k2), feat1, feat2, diagb,
                        vec(diff_subln_g))
    y_sb = _sb_attn(proj, tri, sb_norm_g[0].reshape(N_SB_HEADS // 2, 1, LANES))

    half = N_DIFF_HEADS * DIFF_V_DIM
    kv = _xkv_proj(mem, w_xkv[0])
    out = _tail(y_diff.reshape(b * s, half), y_sb.reshape(b * s, -1), x2d, kv,
                wo16, vec(ln1_g), vec(ln1_b), wq16, wxo16, vec(ln2_g), vec(ln2_b),
                wg16, wu16, wd16, vec(ln3_g), vec(ln3_b), s)
    return out.reshape(b, s, d)
```
